```python
import jax, jax.numpy as jnp
from jax import lax
import numpy as np

D_MODEL = 1024
BATCH = 2
SEQ = 8192
DEPTH = 1

MEM_LEN = 256
RET_HEADS = 4
RET_QK_DIM = 64
RET_V_DIM = 128
RET_WIDTH = RET_HEADS * RET_V_DIM
RET_CHUNK = 128
ROPE_BASE = 10000.0
DIL_HEADS = 8
DIL_HEAD_DIM = 64
DIL_WIDTH = DIL_HEADS * DIL_HEAD_DIM
DIL_CONFIGS = ((128, 1), (512, 4), (2048, 16))
DIL_BLOCK = 128
MIX_WIDTH = RET_WIDTH + DIL_WIDTH
XATTN_HEADS = 4
XATTN_HEAD_DIM = D_MODEL // XATTN_HEADS
N_GROUPS = 4
EXPERTS_PER_GROUP = 8
N_EXPERTS = N_GROUPS * EXPERTS_PER_GROUP
TOP_K_INNER = 2
EXPERT_FF = 512
MOE_BLOCK = 128
LN_EPS = 1e-5
GN_EPS = 1e-6
ALPHA = (2.0 * DEPTH) ** 0.25
BETA = (8.0 * DEPTH) ** -0.25

kernel_name = 'hybrid_retention_dilated_hmoe_layer'


def layer_norm(x, g, b):
    xf = x.astype(jnp.float32)
    mu = jnp.mean(xf, axis=-1, keepdims=True)
    var = jnp.mean(jnp.square(xf - mu), axis=-1, keepdims=True)
    y = (xf - mu) * lax.rsqrt(var + LN_EPS) * g.astype(jnp.float32) + b.astype(jnp.float32)
    return y.astype(x.dtype)


def rotary(t, pos):
    half = t.shape[-1] // 2
    inv = 1.0 / (ROPE_BASE ** (jnp.arange(half, dtype=jnp.float32) / half))
    ang = pos.astype(jnp.float32)[:, None] * inv[None, :]
    cos = jnp.cos(ang)[None, :, None, :]
    sin = jnp.sin(ang)[None, :, None, :]
    tf = t.astype(jnp.float32)
    t1, t2 = tf[..., :half], tf[..., half:]
    return jnp.concatenate([t1 * cos - t2 * sin, t1 * sin + t2 * cos], axis=-1)


def chunkwise_retention(q, k, v):
    B, S, H, dk = q.shape
    dv = v.shape[-1]
    C = RET_CHUNK
    N = S // C
    pos = jnp.arange(S)
    q = rotary(q, pos)
    k = rotary(k, pos) * (dk ** -0.5)

    def chunks(t):
        return t.astype(jnp.float32).reshape(B, N, C, H, t.shape[-1]).transpose(0, 3, 1, 2, 4)

    qc, kc, vc = chunks(q), chunks(k), chunks(v)
    lg = jnp.log(1.0 - jnp.exp2(-5.0 - jnp.arange(H, dtype=jnp.float32)))
    idx = jnp.arange(C, dtype=jnp.float32)
    diff = idx[:, None] - idx[None, :]
    decay = jnp.where(diff >= 0, jnp.exp(lg[:, None, None] * jnp.maximum(diff, 0.0)), 0.0)
    scores = jnp.einsum('bhnid,bhnjd->bhnij', qc, kc) * decay[None, :, None]
    y_intra = jnp.einsum('bhnij,bhnje->bhnie', scores, vc)
    zeta = jnp.exp(lg[:, None] * (C - 1.0 - idx))[None, :, None, :, None]
    kv = jnp.einsum('bhnjd,bhnje->nbhde', kc * zeta, vc)
    g_chunk = jnp.exp(lg * C)[None, :, None, None]

    def step(state, kv_n):
        return g_chunk * state + kv_n, state

    _, state_prev = lax.scan(step, jnp.zeros((B, H, dk, dv), jnp.float32), kv)
    xi = jnp.exp(lg[:, None] * (idx + 1.0))[None, :, None, :, None]
    y_cross = jnp.einsum('bhnid,nbhde->bhnie', qc, state_prev) * xi
    y = y_intra + y_cross
    mu = jnp.mean(y, axis=-1, keepdims=True)
    var = jnp.mean(jnp.square(y - mu), axis=-1, keepdims=True)
    y = (y - mu) * lax.rsqrt(var + GN_EPS)
    return y.transpose(0, 2, 3, 1, 4).reshape(B, S, H * dv)


def dilated_branch(q, k, v, window, dilation):
    B, S, H, dh = q.shape
    L = S // dilation
    Q = DIL_BLOCK
    nb = -(-L // Q)
    Lp = nb * Q
    w_sub = window // dilation

    def sub(t):
        t = t.reshape(B, L, dilation, H, dh).transpose(0, 2, 3, 1, 4)
        t = jnp.pad(t, ((0, 0), (0, 0), (0, 0), (0, Lp - L), (0, 0)))
        return t.reshape(B, dilation, H, nb, Q, dh)

    def with_prev(t):
        prev = jnp.pad(t[:, :, :, :-1], ((0, 0), (0, 0), (0, 0), (1, 0), (0, 0), (0, 0)))
        return jnp.concatenate([prev, t], axis=-2)

    qb = sub(q)
    kk = with_prev(sub(k))
    vv = with_prev(sub(v))
    s = jnp.einsum('brhnid,brhnjd->brhnij', qb, kk).astype(jnp.float32) * (dh ** -0.5)
    i = jnp.arange(Q)[:, None]
    j = jnp.arange(2 * Q)[None, :]
    dist = i + Q - j
    blk = jnp.arange(nb)[:, None, None]
    valid = (dist >= 0) & (dist <= w_sub) & ((blk > 0) | (j >= Q))
    s = jnp.where(valid, s, -jnp.inf)
    m = jnp.max(s, axis=-1, keepdims=True)
    p = jnp.exp(s - m)
    den = jnp.sum(p, axis=-1, keepdims=True)
    o = jnp.einsum('brhnij,brhnjd->brhnid', p, vv.astype(jnp.float32)) / den
    lse = (m + jnp.log(den))[..., 0]
    o = o.reshape(B, dilation, H, Lp, dh)[:, :, :, :L].transpose(0, 3, 1, 2, 4).reshape(B, S, H, dh)
    lse = lse.reshape(B, dilation, H, Lp)[:, :, :, :L].transpose(0, 3, 1, 2).reshape(B, S, H)
    return o, lse


def dilated_attention(q, k, v):
    outs = []
    lses = []
    for window, dilation in DIL_CONFIGS:
        o, lse = dilated_branch(q, k, v, window, dilation)
        outs.append(o)
        lses.append(lse)
    wts = jax.nn.softmax(jnp.stack(lses, axis=0), axis=0)
    o = jnp.sum(wts[..., None] * jnp.stack(outs, axis=0), axis=0)
    B, S, H, dh = o.shape
    return o.reshape(B, S, H * dh)


def hybrid_mixer(x, w_in, w_out):
    B, S, _ = x.shape
    rq = RET_HEADS * RET_QK_DIM
    h = x @ w_in
    cuts = [rq, 2 * rq, 2 * rq + RET_WIDTH, 2 * rq + 2 * RET_WIDTH,
            2 * rq + 2 * RET_WIDTH + DIL_WIDTH, 2 * rq + 2 * RET_WIDTH + 2 * DIL_WIDTH]
    r_q, r_k, r_v, r_g, d_q, d_k, d_v = jnp.split(h, cuts, axis=-1)
    y_ret = chunkwise_retention(r_q.reshape(B, S, RET_HEADS, RET_QK_DIM),
                                r_k.reshape(B, S, RET_HEADS, RET_QK_DIM),
                                r_v.reshape(B, S, RET_HEADS, RET_V_DIM))
    y_ret = jax.nn.silu(r_g.astype(jnp.float32)) * y_ret
    y_dil = dilated_attention(d_q.reshape(B, S, DIL_HEADS, DIL_HEAD_DIM),
                              d_k.reshape(B, S, DIL_HEADS, DIL_HEAD_DIM),
                              d_v.reshape(B, S, DIL_HEADS, DIL_HEAD_DIM))
    y = jnp.concatenate([y_ret, y_dil], axis=-1).astype(x.dtype)
    return y @ w_out


def memory_cross_attention(x, mem, w_q, w_kv, w_o):
    B, S, D = x.shape
    M = mem.shape[1]
    q = (x @ w_q).reshape(B, S, XATTN_HEADS, XATTN_HEAD_DIM)
    kv = mem @ w_kv
    k = kv[..., :D].reshape(B, M, XATTN_HEADS, XATTN_HEAD_DIM)
    v = kv[..., D:].reshape(B, M, XATTN_HEADS, XATTN_HEAD_DIM)
    s = jnp.einsum('bshd,bmhd->bhsm', q, k).astype(jnp.float32) * (XATTN_HEAD_DIM ** -0.5)
    p = jax.nn.softmax(s, axis=-1)
    o = jnp.einsum('bhsm,bmhd->bshd', p, v.astype(jnp.float32)).reshape(B, S, D).astype(x.dtype)
    return o @ w_o


def hierarchical_moe(x, w_rg, w_re, w_g, w_u, w_d):
    B, S, D = x.shape
    T = B * S
    A = T * TOP_K_INNER
    xt = x.reshape(T, D)
    p_group = jax.nn.softmax((xt @ w_rg).astype(jnp.float32), axis=-1)
    g1, gi = lax.top_k(p_group, 1)
    lg_exp = (xt @ w_re).astype(jnp.float32).reshape(T, N_GROUPS, EXPERTS_PER_GROUP)
    lg_sel = jnp.take_along_axis(lg_exp, gi[:, :, None], axis=1)[:, 0]
    v2, li = lax.top_k(lg_sel, TOP_K_INNER)
    g2 = jax.nn.softmax(v2, axis=-1)
    gates = (g1 * g2).reshape(-1)
    eid = (gi * EXPERTS_PER_GROUP + li).reshape(-1)
    tok = jnp.repeat(jnp.arange(T, dtype=jnp.int32), TOP_K_INNER)
    order = jnp.argsort(eid)
    e_s, tok_s, g_s = eid[order], tok[order], gates[order]
    counts = jnp.bincount(eid, length=N_EXPERTS)
    padded = ((counts + MOE_BLOCK - 1) // MOE_BLOCK) * MOE_BLOCK
    off = jnp.cumsum(counts) - counts
    pend = jnp.cumsum(padded)
    poff = pend - padded
    dest = poff[e_s] + (jnp.arange(A) - off[e_s])
    R = A + N_EXPERTS * MOE_BLOCK
    nblk = R // MOE_BLOCK
    row_tok = jnp.zeros((R,), jnp.int32).at[dest].set(tok_s)
    row_gate = jnp.zeros((R,), jnp.float32).at[dest].set(g_s)
    blk_exp = jnp.minimum(jnp.searchsorted(pend, jnp.arange(nblk) * MOE_BLOCK, side='right'), N_EXPERTS - 1)
    x_rows = xt[row_tok].reshape(nblk, MOE_BLOCK, D)

    def expert_block(args):
        xb, e = args
        return (jax.nn.silu(xb @ w_g[e]) * (xb @ w_u[e])) @ w_d[e]

    y_rows = lax.map(expert_block, (x_rows, blk_exp)).reshape(R, D)
    out = jnp.zeros((T, D), jnp.float32).at[row_tok].add(y_rows.astype(jnp.float32) * row_gate[:, None])
    return out.astype(x.dtype).reshape(B, S, D)


def setup_inputs(seed: int = 0) -> dict:
    key = jax.random.key(seed)
    ks = jax.random.split(key, 24)
    D = D_MODEL
    sd = D ** -0.5
    rq = RET_HEADS * RET_QK_DIM

    def nrm(k, shape, scale):
        return jax.random.normal(k, shape, jnp.float32) * scale

    x = nrm(ks[0], (BATCH, SEQ, D), 1.0)
    mem = nrm(ks[1], (BATCH, MEM_LEN, D), 1.0)
    w_in = jnp.concatenate([
        nrm(ks[2], (DEPTH, D, 2 * rq), sd),
        nrm(ks[3], (DEPTH, D, RET_WIDTH), sd * BETA),
        nrm(ks[4], (DEPTH, D, RET_WIDTH), sd),
        nrm(ks[5], (DEPTH, D, 2 * DIL_WIDTH), sd),
        nrm(ks[6], (DEPTH, D, DIL_WIDTH), sd * BETA),
    ], axis=-1)
    w_out = nrm(ks[7], (DEPTH, MIX_WIDTH, D), (MIX_WIDTH ** -0.5) * BETA)
    ln_mix_g = 1.0 + nrm(ks[8], (DEPTH, D), 0.02)
    ln_mix_b = nrm(ks[9], (DEPTH, D), 0.02)
    w_xq = nrm(ks[10], (DEPTH, D, D), sd)
    w_xkv = jnp.concatenate([nrm(ks[11], (DEPTH, D, D), sd), nrm(ks[12], (DEPTH, D, D), sd * BETA)], axis=-1)
    w_xo = nrm(ks[13], (DEPTH, D, D), sd * BETA)
    ln_x_g = 1.0 + nrm(ks[14], (DEPTH, D), 0.02)
    ln_x_b = nrm(ks[15], (DEPTH, D), 0.02)
    w_route_group = nrm(ks[16], (DEPTH, D, N_GROUPS), sd)
    w_route_expert = nrm(ks[17], (DEPTH, D, N_EXPERTS), sd)
    w_exp_gate = nrm(ks[18], (DEPTH, N_EXPERTS, D, EXPERT_FF), sd * BETA)
    w_exp_up = nrm(ks[19], (DEPTH, N_EXPERTS, D, EXPERT_FF), sd * BETA)
    w_exp_down = nrm(ks[20], (DEPTH, N_EXPERTS, EXPERT_FF, D), (EXPERT_FF ** -0.5) * BETA)
    ln_moe_g = 1.0 + nrm(ks[21], (DEPTH, D), 0.02)
    ln_moe_b = nrm(ks[22], (DEPTH, D), 0.02)
    return {'x': x, 'mem': mem, 'w_in': w_in, 'w_out': w_out, 'ln_mix_g': ln_mix_g, 'ln_mix_b': ln_mix_b,
            'w_xq': w_xq, 'w_xkv': w_xkv, 'w_xo': w_xo, 'ln_x_g': ln_x_g, 'ln_x_b': ln_x_b,
            'w_route_group': w_route_group, 'w_route_expert': w_route_expert,
            'w_exp_gate': w_exp_gate, 'w_exp_up': w_exp_up, 'w_exp_down': w_exp_down,
            'ln_moe_g': ln_moe_g, 'ln_moe_b': ln_moe_b}


def reference(x, mem, w_in, w_out, ln_mix_g, ln_mix_b, w_xq, w_xkv, w_xo, ln_x_g, ln_x_b,
              w_route_group, w_route_expert, w_exp_gate, w_exp_up, w_exp_down, ln_moe_g, ln_moe_b):
    for l in range(DEPTH):
        x = layer_norm(ALPHA * x + hybrid_mixer(x, w_in[l], w_out[l]), ln_mix_g[l], ln_mix_b[l])
        x = layer_norm(ALPHA * x + memory_cross_attention(x, mem, w_xq[l], w_xkv[l], w_xo[l]),
                       ln_x_g[l], ln_x_b[l])
        x = layer_norm(ALPHA * x + hierarchical_moe(x, w_route_group[l], w_route_expert[l],
                                                    w_exp_gate[l], w_exp_up[l], w_exp_down[l]),
                       ln_moe_g[l], ln_moe_b[l])
    return x
```

```python
import functools

import jax
import jax.numpy as jnp
from jax import lax
from jax.experimental import pallas as pl
from jax.experimental.pallas import tpu as pltpu

BF16 = jnp.bfloat16
F32 = jnp.float32

LANE = 128
SUBLANE = 8
VMEM_LIMIT = 56 * 1024 * 1024

RET_HEADS = 4
RET_QK_DIM = 64
RET_V_DIM = 128
RET_CHUNK = 128
ROPE_BASE = 10000.0
DIL_HEADS = 8
DIL_HEAD_DIM = 64
DIL_DILATIONS = (1, 4, 16)
DIL_BLOCK = 128
DIL_SUPER = DIL_BLOCK * max(DIL_DILATIONS)
XATTN_HEADS = 4
N_GROUPS = 4
EXPERTS_PER_GROUP = 8
N_EXPERTS = N_GROUPS * EXPERTS_PER_GROUP
EXPERT_ROWS = 256
LN_EPS = 1e-5
GN_EPS = 1e-6
NEG = -1e30


def _params(*sem):
    return pltpu.CompilerParams(dimension_semantics=sem, vmem_limit_bytes=VMEM_LIMIT)


def _layer_norm(z, g, b):
    mu = jnp.mean(z, axis=-1, keepdims=True)
    zc = z - mu
    var = jnp.mean(zc * zc, axis=-1, keepdims=True)
    return zc * lax.rsqrt(var + LN_EPS) * g + b


def _dot(a, b):
    return jnp.dot(a.astype(BF16), b.astype(BF16), preferred_element_type=F32)


def _dot_nt(a, b):
    return lax.dot_general(a.astype(BF16), b.astype(BF16), (((1,), (1,)), ((), ())),
                           preferred_element_type=F32)


def _dot_tn(a, b):
    return lax.dot_general(a.astype(BF16), b.astype(BF16), (((0,), (0,)), ((), ())),
                           preferred_element_type=F32)


def _proj_in_kernel(x_ref, w_ref, o_ref):
    o_ref[...] = _dot(x_ref[...], w_ref[...])


def _proj_in(x2d, w_bf16, tm=512):
    t, d = x2d.shape
    n = w_bf16.shape[1]
    return pl.pallas_call(
        _proj_in_kernel,
        grid=(t // tm,),
        in_specs=[pl.BlockSpec((tm, d), lambda i: (i, 0)),
                  pl.BlockSpec((d, n), lambda i: (0, 0))],
        out_specs=pl.BlockSpec((tm, n), lambda i: (i, 0)),
        out_shape=jax.ShapeDtypeStruct((t, n), F32),
        compiler_params=_params("parallel"),
        name="proj_in",
    )(x2d, w_bf16)


def _retention_kernel(qk_ref, v_ref, g_ref, cos_ref, sin_ref, decay_ref, zeta_ref, xi_ref, gam_ref,
                      o_ref, state_ref):
    n = pl.program_id(1)

    @pl.when(n == 0)
    def _():
        state_ref[...] = jnp.zeros_like(state_ref)

    c = qk_ref.shape[1]
    lane = lax.broadcasted_iota(jnp.int32, (c, LANE), 1)
    first_half = (lane % RET_QK_DIM) < (RET_QK_DIM // 2)
    n_pairs = RET_HEADS * RET_QK_DIM // LANE

    def rot(col):
        t = qk_ref[0, :, col * LANE:(col + 1) * LANE]
        swapped = jnp.where(first_half,
                            pltpu.roll(t, LANE - RET_QK_DIM // 2, axis=1),
                            pltpu.roll(t, RET_QK_DIM // 2, axis=1))
        return (t * cos_ref[:, col * LANE:(col + 1) * LANE]
                + swapped * sin_ref[:, col * LANE:(col + 1) * LANE])

    q_pairs = [rot(p) for p in range(n_pairs)]
    k_pairs = [rot(n_pairs + p) for p in range(n_pairs)]

    for h in range(RET_HEADS):
        p = (h * RET_QK_DIM) // LANE
        lo = (h * RET_QK_DIM) % LANE
        mine = (lane >= lo) & (lane < lo + RET_QK_DIM)
        qm = jnp.where(mine, q_pairs[p], 0.0)
        km = jnp.where(mine, k_pairs[p], 0.0)
        v = v_ref[0, :, h * RET_V_DIM:(h + 1) * RET_V_DIM]
        s = _dot_nt(qm, k_pairs[p]) * decay_ref[h]
        y = _dot(s, v)
        st = state_ref[h]
        y = y + _dot(qm, st) * xi_ref[h]
        kv = _dot_tn(km * zeta_ref[h], v)
        state_ref[h] = gam_ref[h] * st + kv
        mu = jnp.mean(y, axis=-1, keepdims=True)
        yc = y - mu
        var = jnp.mean(yc * yc, axis=-1, keepdims=True)
        yn = yc * lax.rsqrt(var + GN_EPS)
        gate = g_ref[0, :, h * RET_V_DIM:(h + 1) * RET_V_DIM]
        o_ref[0, :, h * RET_V_DIM:(h + 1) * RET_V_DIM] = gate * jax.nn.sigmoid(gate) * yn


def _retention_tables(s):
    half = RET_QK_DIM // 2
    inv = 1.0 / (ROPE_BASE ** (jnp.arange(half, dtype=F32) / half))
    ang = jnp.arange(s, dtype=F32)[:, None] * inv[None, :]
    cos = jnp.cos(ang)
    sin = jnp.sin(ang)
    cos_h = jnp.concatenate([cos, cos], axis=-1)
    sin_h = jnp.concatenate([-sin, sin], axis=-1)
    kscale = RET_QK_DIM ** -0.5
    cos_t = jnp.concatenate([jnp.tile(cos_h, (1, RET_HEADS)), jnp.tile(cos_h, (1, RET_HEADS)) * kscale], axis=-1)
    sin_t = jnp.concatenate([jnp.tile(sin_h, (1, RET_HEADS)), jnp.tile(sin_h, (1, RET_HEADS)) * kscale], axis=-1)
    c = RET_CHUNK
    lg = jnp.log(1.0 - jnp.exp2(-5.0 - jnp.arange(RET_HEADS, dtype=F32)))
    idx = jnp.arange(c, dtype=F32)
    diff = idx[:, None] - idx[None, :]
    decay = jnp.where(diff >= 0, jnp.exp(lg[:, None, None] * jnp.maximum(diff, 0.0)), 0.0)
    zeta = jnp.exp(lg[:, None] * (c - 1.0 - idx))[:, :, None]
    xi = jnp.exp(lg[:, None] * (idx + 1.0))[:, :, None]
    gam = jnp.broadcast_to(jnp.exp(lg * c)[:, None, None], (RET_HEADS, 1, LANE))
    return cos_t, sin_t, decay, zeta, xi, gam


def _retention(h3):
    b, s, _ = h3.shape
    c = RET_CHUNK
    qk_w = 2 * RET_HEADS * RET_QK_DIM
    v_w = RET_HEADS * RET_V_DIM
    assert qk_w == v_w
    cos_t, sin_t, decay, zeta, xi, gam = _retention_tables(s)
    const3 = lambda bi, n: (0, 0, 0)
    return pl.pallas_call(
        _retention_kernel,
        grid=(b, s // c),
        in_specs=[pl.BlockSpec((1, c, qk_w), lambda bi, n: (bi, n, 0)),
                  pl.BlockSpec((1, c, v_w), lambda bi, n: (bi, n, 1)),
                  pl.BlockSpec((1, c, v_w), lambda bi, n: (bi, n, 2)),
                  pl.BlockSpec((c, qk_w), lambda bi, n: (n, 0)),
                  pl.BlockSpec((c, qk_w), lambda bi, n: (n, 0)),
                  pl.BlockSpec((RET_HEADS, c, c), const3),
                  pl.BlockSpec((RET_HEADS, c, 1), const3),
                  pl.BlockSpec((RET_HEADS, c, 1), const3),
                  pl.BlockSpec((RET_HEADS, 1, LANE), const3)],
        out_specs=pl.BlockSpec((1, c, v_w), lambda bi, n: (bi, n, 0)),
        out_shape=jax.ShapeDtypeStruct((b, s, v_w), F32),
        scratch_shapes=[pltpu.VMEM((RET_HEADS, LANE, RET_V_DIM), F32)],
        compiler_params=_params("parallel", "arbitrary"),
        name="retention",
    )(h3, h3, h3, cos_t, sin_t, decay, zeta, xi, gam)


def _dilated_kernel(q_ref, kp_ref, kc_ref, vp_ref, vc_ref, o_ref, kk_ref, vv_ref, acc_ref, m_ref, l_ref):
    j = pl.program_id(2)
    sup = DIL_SUPER
    q_blk = DIL_BLOCK
    kk_ref[0:sup, :] = kp_ref[0]
    kk_ref[sup:2 * sup, :] = kc_ref[0]
    vv_ref[0:sup, :] = vp_ref[0]
    vv_ref[sup:2 * sup, :] = vc_ref[0]

    scale = DIL_HEAD_DIM ** -0.5

    for bi, d in enumerate(DIL_DILATIONS):
        n_per_r = sup // (q_blk * d)

        def block(t, carry, d=d, n_per_r=n_per_r, first_branch=(bi == 0)):
            r = t // n_per_r
            n = t % n_per_r
            q_start = n * (q_blk * d) + r
            k_start = sup + (n - 1) * (q_blk * d) + r
            q = q_ref[0, pl.ds(q_start, q_blk, stride=d), :]
            kb = kk_ref[pl.ds(k_start, 2 * q_blk, stride=d), :]
            vb = vv_ref[pl.ds(k_start, 2 * q_blk, stride=d), :]
            lane = lax.broadcasted_iota(jnp.int32, (q_blk, LANE), 1)
            head0 = lane < DIL_HEAD_DIM
            q2 = jnp.concatenate([jnp.where(head0, q, 0.0), jnp.where(head0, 0.0, q)], axis=0)
            s = _dot_nt(q2, kb) * scale
            qi = lax.broadcasted_iota(jnp.int32, (2 * q_blk, 2 * q_blk), 0) % q_blk
            kj = lax.broadcasted_iota(jnp.int32, (2 * q_blk, 2 * q_blk), 1)
            k_min = jnp.where((j > 0) | (n > 0), 0, q_blk)
            valid = (kj >= qi) & (kj <= qi + q_blk) & (kj >= k_min)
            s = jnp.where(valid, s, NEG)
            m2 = jnp.max(s, axis=-1, keepdims=True)
            p = jnp.exp(s - m2)
            l2 = jnp.sum(p, axis=-1, keepdims=True)
            o2 = _dot(p, vb)
            o = jnp.where(head0, o2[:q_blk], o2[q_blk:])
            m = jnp.where(head0, m2[:q_blk], m2[q_blk:])
            l = jnp.where(head0, l2[:q_blk], l2[q_blk:])
            rows = pl.ds(q_start, q_blk, stride=d)
            if first_branch:
                acc_ref[rows, :] = o
                m_ref[rows, :] = m
                l_ref[rows, :] = l
            else:
                m_old = m_ref[rows, :]
                m_new = jnp.maximum(m_old, m)
                a_old = jnp.exp(m_old - m_new)
                a_new = jnp.exp(m - m_new)
                acc_ref[rows, :] = a_old * acc_ref[rows, :] + a_new * o
                l_ref[rows, :] = a_old * l_ref[rows, :] + a_new * l
                m_ref[rows, :] = m_new
            return carry

        lax.fori_loop(0, sup // q_blk, block, 0)

    o_ref[0] = acc_ref[...] / l_ref[...]


def _dilated(h3, col0):
    b, s, _ = h3.shape
    sup = DIL_SUPER
    n_pairs = DIL_HEADS * DIL_HEAD_DIM // LANE
    cq, ck, cv = col0 * n_pairs, (col0 + 1) * n_pairs, (col0 + 2) * n_pairs
    cur = lambda c: (lambda bi, p, j: (bi, j, c + p))
    prev = lambda c: (lambda bi, p, j: (bi, jnp.maximum(j - 1, 0), c + p))
    blk = (1, sup, LANE)
    return pl.pallas_call(
        _dilated_kernel,
        grid=(b, n_pairs, s // sup),
        in_specs=[pl.BlockSpec(blk, cur(cq)),
                  pl.BlockSpec(blk, prev(ck)), pl.BlockSpec(blk, cur(ck)),
                  pl.BlockSpec(blk, prev(cv)), pl.BlockSpec(blk, cur(cv))],
        out_specs=pl.BlockSpec(blk, lambda bi, p, j: (bi, j, p)),
        out_shape=jax.ShapeDtypeStruct((b, s, n_pairs * LANE), F32),
        scratch_shapes=[pltpu.VMEM((2 * sup, LANE), F32), pltpu.VMEM((2 * sup, LANE), F32),
                        pltpu.VMEM((sup, LANE), F32), pltpu.VMEM((sup, LANE), F32), pltpu.VMEM((sup, LANE), F32)],
        compiler_params=_params("parallel", "parallel", "arbitrary"),
        name="dilated",
    )(h3, h3, h3, h3, h3)


def _mix_out_kernel(alpha, x_ref, yr_ref, yd_ref, w_ref, g_ref, b_ref, o_ref):
    wr = yr_ref.shape[1]
    y = _dot(yr_ref[...], w_ref[0:wr, :]) + _dot(yd_ref[...], w_ref[wr:, :])
    o_ref[...] = _layer_norm(alpha * x_ref[...] + y, g_ref[...], b_ref[...])


def _mix_out(x2d, y_ret, y_dil, w_bf16, g, bb, alpha, tm=512):
    t, d = x2d.shape
    row = lambda i: (i, 0)
    const = lambda i: (0, 0)
    return pl.pallas_call(
        functools.partial(_mix_out_kernel, alpha),
        grid=(t // tm,),
        in_specs=[pl.BlockSpec((tm, d), row),
                  pl.BlockSpec((tm, y_ret.shape[1]), row),
                  pl.BlockSpec((tm, y_dil.shape[1]), row),
                  pl.BlockSpec(w_bf16.shape, const),
                  pl.BlockSpec((1, d), const), pl.BlockSpec((1, d), const)],
        out_specs=pl.BlockSpec((tm, d), row),
        out_shape=jax.ShapeDtypeStruct((t, d), F32),
        compiler_params=_params("parallel"),
        name="mix_out",
    )(x2d, y_ret, y_dil, w_bf16, g, bb)


def _mem_kv(mem2d, w_bf16, tn=512):
    m, d = mem2d.shape
    n = w_bf16.shape[1]
    return pl.pallas_call(
        _proj_in_kernel,
        grid=(n // tn,),
        in_specs=[pl.BlockSpec((m, d), lambda i: (0, 0)),
                  pl.BlockSpec((d, tn), lambda i: (0, i))],
        out_specs=pl.BlockSpec((m, tn), lambda i: (0, i)),
        out_shape=jax.ShapeDtypeStruct((m, n), F32),
        compiler_params=_params("parallel"),
        name="mem_kv",
    )(mem2d, w_bf16)


def _split_bf16(a):
    hi = a.astype(BF16)
    lo = (a - hi.astype(F32)).astype(BF16)
    return hi, lo


def _xattn_kernel(alpha, x_ref, k_ref, v_ref, wq_ref, wo_ref, g_ref, b_ref, wrh_ref, wrl_ref,
                  o_ref, ot_ref, lg_ref, att_ref):
    tm, d = x_ref.shape
    dh = d // XATTN_HEADS
    x = x_ref[...]
    q = _dot(x, wq_ref[...])
    scale = dh ** -0.5
    for h in range(XATTN_HEADS):
        sl = slice(h * dh, (h + 1) * dh)
        s = _dot_nt(q[:, sl], k_ref[0, :, sl]) * scale
        m = jnp.max(s, axis=-1, keepdims=True)
        e = jnp.exp(s - m)
        p = e / jnp.sum(e, axis=-1, keepdims=True)
        att_ref[:, sl] = _dot(p, v_ref[0, :, sl])
    y = _dot(att_ref[...], wo_ref[...])
    x2 = _layer_norm(alpha * x + y, g_ref[...], b_ref[...])
    o_ref[...] = x2
    for c in range(d // LANE):
        ot_ref[pl.ds(c, tm, stride=SUBLANE), :] = x2[:, c * LANE:(c + 1) * LANE]
    xh, xl = _split_bf16(x2)
    lg_ref[...] = (jnp.dot(xh, wrh_ref[...], preferred_element_type=F32)
                   + (jnp.dot(xl, wrh_ref[...], preferred_element_type=F32)
                      + jnp.dot(xh, wrl_ref[...], preferred_element_type=F32)))


def _xattn(x1, kv, wq, wo, g, bb, wr_hi, wr_lo, alpha, seq, tm=256):
    t, d = x1.shape
    mlen = kv.shape[1]
    tiles_per_seq = seq // tm
    row = lambda i: (i, 0)
    const = lambda i: (0, 0)
    return pl.pallas_call(
        functools.partial(_xattn_kernel, alpha),
        grid=(t // tm,),
        in_specs=[pl.BlockSpec((tm, d), row),
                  pl.BlockSpec((1, mlen, d), lambda i: (i // tiles_per_seq, 0, 0)),
                  pl.BlockSpec((1, mlen, d), lambda i: (i // tiles_per_seq, 0, 1)),
                  pl.BlockSpec((d, d), const), pl.BlockSpec((d, d), const),
                  pl.BlockSpec((1, d), const), pl.BlockSpec((1, d), const),
                  pl.BlockSpec((d, LANE), const), pl.BlockSpec((d, LANE), const)],
        out_specs=[pl.BlockSpec((tm, d), row),
                   pl.BlockSpec((tm * SUBLANE, LANE), row),
                   pl.BlockSpec((tm, LANE), row)],
        out_shape=[jax.ShapeDtypeStruct((t, d), F32),
                   jax.ShapeDtypeStruct((t * SUBLANE, LANE), F32),
                   jax.ShapeDtypeStruct((t, LANE), F32)],
        scratch_shapes=[pltpu.VMEM((tm, d), F32)],
        compiler_params=_params("parallel"),
        name="xattn",
    )(x1, kv, kv, wq, wo, g, bb, wr_hi, wr_lo)


def _route_kernel(lg_ref, meta_ref, cnt_ref, carry_ref):
    i = pl.program_id(0)

    @pl.when(i == 0)
    def _():
        carry_ref[...] = jnp.zeros_like(carry_ref)

    lg = lg_ref[...]
    tt = lg.shape[0]
    lane = lax.broadcasted_iota(jnp.int32, (tt, LANE), 1)
    is_group = lane < N_GROUPS
    mg = jnp.max(jnp.where(is_group, lg, NEG), axis=-1, keepdims=True)
    eg = jnp.where(is_group, jnp.exp(lg - mg), 0.0)
    pg = eg / jnp.sum(eg, axis=-1, keepdims=True)
    g1 = jnp.max(pg, axis=-1, keepdims=True)
    gi = jnp.min(jnp.where(is_group & (pg == g1), lane, LANE), axis=-1, keepdims=True)
    lo = N_GROUPS + gi * EXPERTS_PER_GROUP
    in_grp = (lane >= lo) & (lane < lo + EXPERTS_PER_GROUP)
    v1 = jnp.max(jnp.where(in_grp, lg, NEG), axis=-1, keepdims=True)
    i1 = jnp.min(jnp.where(in_grp & (lg == v1), lane, LANE), axis=-1, keepdims=True)
    rest = in_grp & (lane != i1)
    v2 = jnp.max(jnp.where(rest, lg, NEG), axis=-1, keepdims=True)
    i2 = jnp.min(jnp.where(rest & (lg == v2), lane, LANE), axis=-1, keepdims=True)
    e2 = jnp.exp(v2 - v1)
    den = 1.0 + e2
    gate1 = g1 * (1.0 / den)
    gate2 = g1 * (e2 / den)
    sel1 = lane == i1
    sel2 = lane == i2
    onehot = jnp.where(sel1 | sel2, 1.0, 0.0)
    ri = lax.broadcasted_iota(jnp.int32, (tt, tt), 0)
    ci = lax.broadcasted_iota(jnp.int32, (tt, tt), 1)
    before = jnp.where(ci < ri, 1.0, 0.0)
    rank = _dot(before, onehot) + carry_ref[0:1, :]
    r1 = jnp.sum(jnp.where(sel1, rank, 0.0), axis=-1, keepdims=True)
    r2 = jnp.sum(jnp.where(sel2, rank, 0.0), axis=-1, keepdims=True)
    total = carry_ref[0:1, :] + jnp.sum(onehot, axis=0, keepdims=True)
    carry_ref[0:1, :] = total
    cnt_ref[...] = jnp.broadcast_to(total, cnt_ref.shape)
    e1f = (i1 - N_GROUPS).astype(F32)
    e2f = (i2 - N_GROUPS).astype(F32)
    meta = jnp.where(lane == 0, e1f, 0.0)
    meta = jnp.where(lane == 1, e2f, meta)
    meta = jnp.where(lane == 2, r1, meta)
    meta = jnp.where(lane == 3, r2, meta)
    meta = jnp.where(lane == 4, gate1, meta)
    meta = jnp.where(lane == 5, gate2, meta)
    meta_ref[...] = meta


def _route(logits, tt=256):
    t = logits.shape[0]
    return pl.pallas_call(
        _route_kernel,
        grid=(t // tt,),
        in_specs=[pl.BlockSpec((tt, LANE), lambda i: (i, 0))],
        out_specs=[pl.BlockSpec((tt, LANE), lambda i: (i, 0)),
                   pl.BlockSpec((SUBLANE, LANE), lambda i: (0, 0))],
        out_shape=[jax.ShapeDtypeStruct((t, LANE), F32),
                   jax.ShapeDtypeStruct((SUBLANE, LANE), F32)],
        scratch_shapes=[pltpu.VMEM((SUBLANE, LANE), F32)],
        compiler_params=_params("arbitrary"),
        name="route",
    )(logits)


def _row_copy(src, dst, s_row, d_row, sem):
    return pltpu.make_async_copy(src.at[pl.ds(pl.multiple_of(s_row * SUBLANE, SUBLANE), SUBLANE), :],
                                 dst.at[pl.ds(pl.multiple_of(d_row * SUBLANE, SUBLANE), SUBLANE), :], sem)


def _dispatch_kernel(dest_ref, xt_ref, xs_ref, sem):
    i = pl.program_id(0)
    tq = dest_ref.shape[2] // 2
    base = i * tq

    def issue(t, c):
        _row_copy(xt_ref, xs_ref, base + t, dest_ref[0, 0, 2 * t], sem).start()
        _row_copy(xt_ref, xs_ref, base + t, dest_ref[0, 0, 2 * t + 1], sem).start()
        return c

    lax.fori_loop(0, tq, issue, 0)

    def drain(t, c):
        _row_copy(xt_ref, xs_ref, 0, 0, sem).wait()
        return c

    lax.fori_loop(0, 2 * tq, drain, 0)


def _dispatch(dest, x_tiled, n_rows, tq=256):
    t = dest.shape[0]
    dest3 = dest.reshape(t // tq, 1, 2 * tq)
    return pl.pallas_call(
        _dispatch_kernel,
        grid=(t // tq,),
        in_specs=[pl.BlockSpec((1, 1, 2 * tq), lambda i: (i, 0, 0), memory_space=pltpu.SMEM),
                  pl.BlockSpec(memory_space=pl.ANY)],
        out_specs=pl.BlockSpec(memory_space=pl.ANY),
        out_shape=jax.ShapeDtypeStruct((n_rows * SUBLANE, LANE), F32),
        scratch_shapes=[pltpu.SemaphoreType.DMA(())],
        compiler_params=_params("arbitrary"),
        name="dispatch",
    )(dest3, x_tiled)


def _experts_kernel(blk_exp_ref, blk_rows_ref, n_used_ref, xs_ref, wg_ref, wu_ref, wd_ref, ys_ref, x_scr, y_scr):
    b = pl.program_id(0)

    @pl.when(b < n_used_ref[0])
    def _():
        rows, d = x_scr.shape
        for c in range(d // LANE):
            x_scr[:, c * LANE:(c + 1) * LANE] = xs_ref[pl.ds(c, rows, stride=SUBLANE), :]
        ri = lax.broadcasted_iota(jnp.int32, (rows, 1), 0)
        x = jnp.where(ri < blk_rows_ref[b], x_scr[...], 0.0)
        hg = _dot(x, wg_ref[0])
        hu = _dot(x, wu_ref[0])
        y_scr[...] = _dot(hg * jax.nn.sigmoid(hg) * hu, wd_ref[0])
        for c in range(d // LANE):
            ys_ref[pl.ds(c, rows, stride=SUBLANE), :] = y_scr[:, c * LANE:(c + 1) * LANE]


def _experts(blk_exp, blk_rows, n_used, xs_tiled, w_g, w_u, w_d):
    n_rows = xs_tiled.shape[0] // SUBLANE
    rows = EXPERT_ROWS
    nblk = n_rows // rows
    _, d, ff = w_g.shape
    used = lambda b, be, br, nu: jnp.minimum(b, nu[0] - 1)
    grid_spec = pltpu.PrefetchScalarGridSpec(
        num_scalar_prefetch=3,
        grid=(nblk,),
        in_specs=[pl.BlockSpec((rows * SUBLANE, LANE), lambda b, be, br, nu: (used(b, be, br, nu), 0)),
                  pl.BlockSpec((1, d, ff), lambda b, be, br, nu: (be[used(b, be, br, nu)], 0, 0)),
                  pl.BlockSpec((1, d, ff), lambda b, be, br, nu: (be[used(b, be, br, nu)], 0, 0)),
                  pl.BlockSpec((1, ff, d), lambda b, be, br, nu: (be[used(b, be, br, nu)], 0, 0))],
        out_specs=pl.BlockSpec((rows * SUBLANE, LANE), lambda b, be, br, nu: (used(b, be, br, nu), 0)),
        scratch_shapes=[pltpu.VMEM((rows, d), F32), pltpu.VMEM((rows, d), F32)],
    )
    return pl.pallas_call(
        _experts_kernel,
        grid_spec=grid_spec,
        out_shape=jax.ShapeDtypeStruct((n_rows * SUBLANE, LANE), F32),
        compiler_params=_params("arbitrary"),
        name="experts",
    )(blk_exp, blk_rows, n_used, xs_tiled, w_g, w_u, w_d)


def _combine_kernel(alpha, dest_ref, meta_ref, x_ref, g_ref, b_ref, ys_ref, o_ref, buf_ref, sem):
    tq, d = x_ref.shape

    def issue(t, c):
        _row_copy(ys_ref, buf_ref.at[0], dest_ref[0, 0, 2 * t], t, sem).start()
        _row_copy(ys_ref, buf_ref.at[1], dest_ref[0, 0, 2 * t + 1], t, sem).start()
        return c

    lax.fori_loop(0, tq, issue, 0)

    def drain(t, c):
        _row_copy(ys_ref, buf_ref.at[0], 0, 0, sem).wait()
        return c

    lax.fori_loop(0, 2 * tq, drain, 0)

    gate1 = meta_ref[:, 4:5]
    gate2 = meta_ref[:, 5:6]
    for c in range(d // LANE):
        sl = slice(c * LANE, (c + 1) * LANE)
        y1 = buf_ref[0, pl.ds(c, tq, stride=SUBLANE), :]
        y2 = buf_ref[1, pl.ds(c, tq, stride=SUBLANE), :]
        o_ref[:, sl] = alpha * x_ref[:, sl] + (y1 * gate1 + y2 * gate2)
    o_ref[...] = _layer_norm(o_ref[...], g_ref[...], b_ref[...])


def _combine(dest, meta, x2, g, bb, ys_tiled, alpha, tq=256):
    t, d = x2.shape
    dest3 = dest.reshape(t // tq, 1, 2 * tq)
    row = lambda i: (i, 0)
    const = lambda i: (0, 0)
    return pl.pallas_call(
        functools.partial(_combine_kernel, alpha),
        grid=(t // tq,),
        in_specs=[pl.BlockSpec((1, 1, 2 * tq), lambda i: (i, 0, 0), memory_space=pltpu.SMEM),
                  pl.BlockSpec((tq, LANE), row),
                  pl.BlockSpec((tq, d), row),
                  pl.BlockSpec((1, d), const), pl.BlockSpec((1, d), const),
                  pl.BlockSpec(memory_space=pl.ANY)],
        out_specs=pl.BlockSpec((tq, d), row),
        out_shape=jax.ShapeDtypeStruct((t, d), F32),
        scratch_shapes=[pltpu.VMEM((2, tq * SUBLANE, LANE), F32), pltpu.SemaphoreType.DMA(())],
        compiler_params=_params("arbitrary"),
        name="combine",
    )(dest3, meta, x2, g, bb, ys_tiled)


def _moe(x2, x2_tiled, logits, w_g, w_u, w_d, g, bb, alpha):
    t, d = x2.shape
    meta, cnt = _route(logits)
    eid = meta[:, 0:2].astype(jnp.int32)
    rank = meta[:, 2:4].astype(jnp.int32)
    counts = cnt[0, N_GROUPS:N_GROUPS + N_EXPERTS].astype(jnp.int32)
    padded = ((counts + EXPERT_ROWS - 1) // EXPERT_ROWS) * EXPERT_ROWS
    pend = jnp.cumsum(padded)
    poff = pend - padded
    dest = poff[eid] + rank
    n_rows = t * 2 + N_EXPERTS * EXPERT_ROWS
    nblk = n_rows // EXPERT_ROWS
    blk_start = jnp.arange(nblk, dtype=jnp.int32) * EXPERT_ROWS
    blk_exp = jnp.sum((pend[None, :] <= blk_start[:, None]).astype(jnp.int32), axis=1)
    blk_exp = jnp.minimum(blk_exp, N_EXPERTS - 1)
    blk_rows = jnp.clip(poff[blk_exp] + counts[blk_exp] - blk_start, 0, EXPERT_ROWS).astype(jnp.int32)
    n_used = (pend[-1:] // EXPERT_ROWS).astype(jnp.int32)
    xs_tiled = _dispatch(dest, x2_tiled, n_rows)
    ys_tiled = _experts(blk_exp, blk_rows, n_used, xs_tiled, w_g, w_u, w_d)
    return _combine(dest, meta, x2, g, bb, ys_tiled, alpha)


def kernel(x, mem, w_in, w_out, ln_mix_g, ln_mix_b, w_xq, w_xkv, w_xo, ln_x_g, ln_x_b, w_route_group,
           w_route_expert, w_exp_gate, w_exp_up, w_exp_down, ln_moe_g, ln_moe_b):
    b, s, d = x.shape
    depth = w_in.shape[0]
    alpha = (2.0 * depth) ** 0.25
    t = b * s
    ret_cols = (2 * RET_HEADS * RET_QK_DIM + 2 * RET_HEADS * RET_V_DIM)
    assert ret_cols % (DIL_HEADS * DIL_HEAD_DIM) == 0
    dil_col0 = ret_cols // (DIL_HEADS * DIL_HEAD_DIM)
    xc = x.reshape(t, d)
    for l in range(depth):
        h = _proj_in(xc, w_in[l].astype(BF16))
        h3 = h.reshape(b, s, h.shape[1])
        y_ret = _retention(h3).reshape(t, -1)
        y_dil = _dilated(h3, dil_col0).reshape(t, -1)
        x1 = _mix_out(xc, y_ret, y_dil, w_out[l].astype(BF16), ln_mix_g[l][None], ln_mix_b[l][None], alpha)
        kv = _mem_kv(mem.reshape(b * mem.shape[1], d), w_xkv[l].astype(BF16)).reshape(b, mem.shape[1], 2 * d)
        w_r = jnp.concatenate([w_route_group[l], w_route_expert[l]], axis=-1)
        w_r = jnp.pad(w_r, ((0, 0), (0, LANE - w_r.shape[1])))
        wr_hi, wr_lo = _split_bf16(w_r)
        x2, x2_tiled, logits = _xattn(x1, kv, w_xq[l].astype(BF16), w_xo[l].astype(BF16),
                                      ln_x_g[l][None], ln_x_b[l][None], wr_hi, wr_lo, alpha, s)
        xc = _moe(x2, x2_tiled, logits, w_exp_gate[l], w_exp_up[l], w_exp_down[l],
                  ln_moe_g[l][None], ln_moe_b[l][None], alpha)
    return xc.reshape(b, s, d)
```

```python
import functools

import jax
import jax.numpy as jnp
from jax import lax
from jax.experimental import pallas as pl
from jax.experimental.pallas import tpu as pltpu

BF16 = jnp.bfloat16
F32 = jnp.float32

LANE = 128
SUBLANE = 8
VMEM_LIMIT = 56 * 1024 * 1024

RET_HEADS = 4
RET_QK_DIM = 64
RET_V_DIM = 128
RET_CHUNK = 128
ROPE_BASE = 10000.0
DIL_HEADS = 8
DIL_HEAD_DIM = 64
DIL_DILATIONS = (1, 4, 16)
DIL_BLOCK = 128
DIL_SUPER = DIL_BLOCK * max(DIL_DILATIONS)
DIL_GROUP = 4
XATTN_HEADS = 4
N_GROUPS = 4
EXPERTS_PER_GROUP = 8
N_EXPERTS = N_GROUPS * EXPERTS_PER_GROUP
EXPERT_ROWS = 256
LN_EPS = 1e-5
GN_EPS = 1e-6
NEG = -1e30


def _params(*sem):
    return pltpu.CompilerParams(dimension_semantics=sem, vmem_limit_bytes=VMEM_LIMIT)


def _layer_norm(z, g, b):
    mu = jnp.mean(z, axis=-1, keepdims=True)
    zc = z - mu
    var = jnp.mean(zc * zc, axis=-1, keepdims=True)
    return zc * lax.rsqrt(var + LN_EPS) * g + b


def _dot(a, b):
    return jnp.dot(a.astype(BF16), b.astype(BF16), preferred_element_type=F32)


def _dot_nt(a, b):
    return lax.dot_general(a.astype(BF16), b.astype(BF16), (((1,), (1,)), ((), ())),
                           preferred_element_type=F32)


def _dot_tn(a, b):
    return lax.dot_general(a.astype(BF16), b.astype(BF16), (((0,), (0,)), ((), ())),
                           preferred_element_type=F32)


def _proj_in_kernel(x_ref, w_ref, o_ref):
    o_ref[...] = _dot(x_ref[...], w_ref[...])


def _proj_in(x2d, w_bf16, tm=512):
    t, d = x2d.shape
    n = w_bf16.shape[1]
    return pl.pallas_call(
        _proj_in_kernel,
        grid=(t // tm,),
        in_specs=[pl.BlockSpec((tm, d), lambda i: (i, 0)),
                  pl.BlockSpec((d, n), lambda i: (0, 0))],
        out_specs=pl.BlockSpec((tm, n), lambda i: (i, 0)),
        out_shape=jax.ShapeDtypeStruct((t, n), F32),
        compiler_params=_params("parallel"),
        name="proj_in",
    )(x2d, w_bf16)


def _retention_kernel(qk_ref, v_ref, g_ref, cos_ref, sin_ref, decay_ref, zeta_ref, xi_ref, gam_ref,
                      o_ref, state_ref):
    n = pl.program_id(1)

    @pl.when(n == 0)
    def _():
        state_ref[...] = jnp.zeros_like(state_ref)

    c = qk_ref.shape[1]
    lane = lax.broadcasted_iota(jnp.int32, (c, LANE), 1)
    first_half = (lane % RET_QK_DIM) < (RET_QK_DIM // 2)
    n_pairs = RET_HEADS * RET_QK_DIM // LANE

    def rot(col):
        t = qk_ref[0, :, col * LANE:(col + 1) * LANE]
        swapped = jnp.where(first_half,
                            pltpu.roll(t, LANE - RET_QK_DIM // 2, axis=1),
                            pltpu.roll(t, RET_QK_DIM // 2, axis=1))
        return t * cos_ref[...] + swapped * sin_ref[...]

    q_pairs = [rot(p) for p in range(n_pairs)]
    k_pairs = [rot(n_pairs + p) * (RET_QK_DIM ** -0.5) for p in range(n_pairs)]

    for h in range(RET_HEADS):
        p = (h * RET_QK_DIM) // LANE
        lo = (h * RET_QK_DIM) % LANE
        mine = (lane >= lo) & (lane < lo + RET_QK_DIM)
        qm = jnp.where(mine, q_pairs[p], 0.0)
        km = jnp.where(mine, k_pairs[p], 0.0)
        v = v_ref[0, :, h * RET_V_DIM:(h + 1) * RET_V_DIM]
        s = _dot_nt(qm, k_pairs[p]) * decay_ref[h]
        y = _dot(s, v)
        st = state_ref[h]
        y = y + _dot(qm, st) * xi_ref[h]
        kv = _dot_tn(km * zeta_ref[h], v)
        state_ref[h] = gam_ref[h] * st + kv
        mu = jnp.mean(y, axis=-1, keepdims=True)
        yc = y - mu
        var = jnp.mean(yc * yc, axis=-1, keepdims=True)
        yn = yc * lax.rsqrt(var + GN_EPS)
        gate = g_ref[0, :, h * RET_V_DIM:(h + 1) * RET_V_DIM]
        o_ref[0, :, h * RET_V_DIM:(h + 1) * RET_V_DIM] = gate * jax.nn.sigmoid(gate) * yn


def _retention_tables(s):
    half = RET_QK_DIM // 2
    inv = 1.0 / (ROPE_BASE ** (jnp.arange(half, dtype=F32) / half))
    ang = jnp.arange(s, dtype=F32)[:, None] * inv[None, :]
    cos = jnp.cos(ang)
    sin = jnp.sin(ang)
    cos_t = jnp.tile(jnp.concatenate([cos, cos], axis=-1), (1, LANE // RET_QK_DIM))
    sin_t = jnp.tile(jnp.concatenate([-sin, sin], axis=-1), (1, LANE // RET_QK_DIM))
    c = RET_CHUNK
    lg = jnp.log(1.0 - jnp.exp2(-5.0 - jnp.arange(RET_HEADS, dtype=F32)))
    idx = jnp.arange(c, dtype=F32)
    diff = idx[:, None] - idx[None, :]
    decay = jnp.where(diff >= 0, jnp.exp(lg[:, None, None] * jnp.maximum(diff, 0.0)), 0.0)
    zeta = jnp.exp(lg[:, None] * (c - 1.0 - idx))[:, :, None]
    xi = jnp.exp(lg[:, None] * (idx + 1.0))[:, :, None]
    gam = jnp.broadcast_to(jnp.exp(lg * c)[:, None, None], (RET_HEADS, 1, LANE))
    return cos_t, sin_t, decay, zeta, xi, gam


def _retention(h3):
    b, s, _ = h3.shape
    c = RET_CHUNK
    qk_w = 2 * RET_HEADS * RET_QK_DIM
    v_w = RET_HEADS * RET_V_DIM
    assert qk_w == v_w
    cos_t, sin_t, decay, zeta, xi, gam = _retention_tables(s)
    const3 = lambda bi, n: (0, 0, 0)
    return pl.pallas_call(
        _retention_kernel,
        grid=(b, s // c),
        in_specs=[pl.BlockSpec((1, c, qk_w), lambda bi, n: (bi, n, 0)),
                  pl.BlockSpec((1, c, v_w), lambda bi, n: (bi, n, 1)),
                  pl.BlockSpec((1, c, v_w), lambda bi, n: (bi, n, 2)),
                  pl.BlockSpec((c, LANE), lambda bi, n: (n, 0)),
                  pl.BlockSpec((c, LANE), lambda bi, n: (n, 0)),
                  pl.BlockSpec((RET_HEADS, c, c), const3),
                  pl.BlockSpec((RET_HEADS, c, 1), const3),
                  pl.BlockSpec((RET_HEADS, c, 1), const3),
                  pl.BlockSpec((RET_HEADS, 1, LANE), const3)],
        out_specs=pl.BlockSpec((1, c, v_w), lambda bi, n: (bi, n, 0)),
        out_shape=jax.ShapeDtypeStruct((b, s, v_w), F32),
        scratch_shapes=[pltpu.VMEM((RET_HEADS, LANE, RET_V_DIM), F32)],
        compiler_params=_params("parallel", "arbitrary"),
        name="retention",
    )(h3, h3, h3, cos_t, sin_t, decay, zeta, xi, gam)


def _dilated_kernel(q_ref, kp_ref, kc_ref, vp_ref, vc_ref, bias_ref, o_ref, kk_ref, vv_ref, acc_ref, m_ref, l_ref,
                    s_ref, p_ref):
    j = pl.program_id(2)
    sup = DIL_SUPER
    q_blk = DIL_BLOCK
    kk_ref[0:sup, :] = kp_ref[0]
    kk_ref[sup:2 * sup, :] = kc_ref[0]
    vv_ref[0:sup, :] = vp_ref[0]
    vv_ref[sup:2 * sup, :] = vc_ref[0]
    scale = DIL_HEAD_DIM ** -0.5
    n_blocks = sup // q_blk

    for bi, d in enumerate(DIL_DILATIONS):
        n_per_r = sup // (q_blk * d)

        def group(t0, carry, bi=bi, d=d, n_per_r=n_per_r):
            lane = lax.broadcasted_iota(jnp.int32, (q_blk, LANE), 1)
            head0 = lane < DIL_HEAD_DIM
            starts = []
            for g in range(DIL_GROUP):
                t = t0 + g * (n_blocks // DIL_GROUP)
                r = t // n_per_r
                n = t % n_per_r
                q_start = n * (q_blk * d) + r
                k_start = sup + (n - 1) * (q_blk * d) + r
                starts.append((q_start, k_start))
                q = q_ref[0, pl.ds(q_start, q_blk, stride=d), :] * scale
                kb = kk_ref[pl.ds(k_start, 2 * q_blk, stride=d), :]
                q2 = jnp.concatenate([jnp.where(head0, q, 0.0), jnp.where(head0, 0.0, q)], axis=0)
                first = jnp.where((j == 0) & (n == 0), 1, 0)
                s_ref[g] = _dot_nt(q2, kb) + bias_ref[first]
            for g in range(DIL_GROUP):
                s = s_ref[g]
                m2 = jnp.max(s, axis=-1, keepdims=True)
                p_ref[g] = jnp.exp(s - m2).astype(BF16)
                m_ref[bi, pl.ds(starts[g][0], q_blk, stride=d), :] = jnp.where(head0, m2[:q_blk], m2[q_blk:])
            for g in range(DIL_GROUP):
                q_start, k_start = starts[g]
                vb = vv_ref[pl.ds(k_start, 2 * q_blk, stride=d), :].astype(BF16)
                o2 = jnp.dot(p_ref[g], jnp.concatenate([vb, jnp.ones_like(vb)], axis=1),
                             preferred_element_type=F32)
                rows = pl.ds(q_start, q_blk, stride=d)
                acc_ref[bi, rows, :] = jnp.where(head0, o2[:q_blk, :LANE], o2[q_blk:, :LANE])
                l_ref[bi, rows, :] = jnp.where(head0, o2[:q_blk, LANE:], o2[q_blk:, LANE:])
            return carry

        lax.fori_loop(0, n_blocks // DIL_GROUP, group, 0)

    def merge(c, carry):
        rows = pl.ds(pl.multiple_of(c * q_blk, q_blk), q_blk)
        ms = [m_ref[bi, rows, :] for bi in range(len(DIL_DILATIONS))]
        m_all = functools.reduce(jnp.maximum, ms)
        ws = [jnp.exp(m - m_all) for m in ms]
        num = functools.reduce(lambda a, b: a + b, [w * acc_ref[bi, rows, :] for bi, w in enumerate(ws)])
        den = functools.reduce(lambda a, b: a + b, [w * l_ref[bi, rows, :] for bi, w in enumerate(ws)])
        o_ref[0, rows, :] = num / den
        return carry

    lax.fori_loop(0, n_blocks, merge, 0)


def _dilated_bias():
    q_blk = DIL_BLOCK
    qi = jnp.arange(2 * q_blk)[:, None] % q_blk
    kj = jnp.arange(2 * q_blk)[None, :]
    band = (kj >= qi) & (kj <= qi + q_blk)
    return jnp.stack([jnp.where(band, 0.0, NEG), jnp.where(band & (kj >= q_blk), 0.0, NEG)]).astype(F32)


def _dilated(h3, col0):
    b, s, _ = h3.shape
    sup = DIL_SUPER
    n_pairs = DIL_HEADS * DIL_HEAD_DIM // LANE
    n_br = len(DIL_DILATIONS)
    cq, ck, cv = col0 * n_pairs, (col0 + 1) * n_pairs, (col0 + 2) * n_pairs
    cur = lambda c: (lambda bi, p, j: (bi, j, c + p))
    prev = lambda c: (lambda bi, p, j: (bi, jnp.maximum(j - 1, 0), c + p))
    blk = (1, sup, LANE)
    return pl.pallas_call(
        _dilated_kernel,
        grid=(b, n_pairs, s // sup),
        in_specs=[pl.BlockSpec(blk, cur(cq)),
                  pl.BlockSpec(blk, prev(ck)), pl.BlockSpec(blk, cur(ck)),
                  pl.BlockSpec(blk, prev(cv)), pl.BlockSpec(blk, cur(cv)),
                  pl.BlockSpec((2, 2 * DIL_BLOCK, 2 * DIL_BLOCK), lambda bi, p, j: (0, 0, 0))],
        out_specs=pl.BlockSpec(blk, lambda bi, p, j: (bi, j, p)),
        out_shape=jax.ShapeDtypeStruct((b, s, n_pairs * LANE), F32),
        scratch_shapes=[pltpu.VMEM((2 * sup, LANE), F32), pltpu.VMEM((2 * sup, LANE), F32),
                        pltpu.VMEM((n_br, sup, LANE), F32), pltpu.VMEM((n_br, sup, LANE), F32),
                        pltpu.VMEM((n_br, sup, LANE), F32),
                        pltpu.VMEM((DIL_GROUP, 2 * DIL_BLOCK, 2 * DIL_BLOCK), F32),
                        pltpu.VMEM((DIL_GROUP, 2 * DIL_BLOCK, 2 * DIL_BLOCK), BF16)],
        compiler_params=_params("parallel", "parallel", "arbitrary"),
        name="dilated",
    )(h3, h3, h3, h3, h3, _dilated_bias())


def _mix_out_kernel(alpha, x_ref, yr_ref, yd_ref, w_ref, g_ref, b_ref, o_ref):
    wr = yr_ref.shape[1]
    y = _dot(yr_ref[...], w_ref[0:wr, :]) + _dot(yd_ref[...], w_ref[wr:, :])
    o_ref[...] = _layer_norm(alpha * x_ref[...] + y, g_ref[...], b_ref[...])


def _mix_out(x2d, y_ret, y_dil, w_bf16, g, bb, alpha, tm=512):
    t, d = x2d.shape
    row = lambda i: (i, 0)
    const = lambda i: (0, 0)
    return pl.pallas_call(
        functools.partial(_mix_out_kernel, alpha),
        grid=(t // tm,),
        in_specs=[pl.BlockSpec((tm, d), row),
                  pl.BlockSpec((tm, y_ret.shape[1]), row),
                  pl.BlockSpec((tm, y_dil.shape[1]), row),
                  pl.BlockSpec(w_bf16.shape, const),
                  pl.BlockSpec((1, d), const), pl.BlockSpec((1, d), const)],
        out_specs=pl.BlockSpec((tm, d), row),
        out_shape=jax.ShapeDtypeStruct((t, d), F32),
        compiler_params=_params("parallel"),
        name="mix_out",
    )(x2d, y_ret, y_dil, w_bf16, g, bb)


def _mem_kv(mem2d, w_bf16, tn=512):
    m, d = mem2d.shape
    n = w_bf16.shape[1]
    return pl.pallas_call(
        _proj_in_kernel,
        grid=(n // tn,),
        in_specs=[pl.BlockSpec((m, d), lambda i: (0, 0)),
                  pl.BlockSpec((d, tn), lambda i: (0, i))],
        out_specs=pl.BlockSpec((m, tn), lambda i: (0, i)),
        out_shape=jax.ShapeDtypeStruct((m, n), F32),
        compiler_params=_params("parallel"),
        name="mem_kv",
    )(mem2d, w_bf16)


def _split_bf16(a):
    hi = a.astype(BF16)
    lo = (a - hi.astype(F32)).astype(BF16)
    return hi, lo


def _xattn_kernel(alpha, x_ref, k_ref, v_ref, wq_ref, wo_ref, g_ref, b_ref, wrh_ref, wrl_ref,
                  o_ref, ot_ref, lg_ref, att_ref):
    tm, d = x_ref.shape
    dh = d // XATTN_HEADS
    x = x_ref[...]
    q = _dot(x, wq_ref[...])
    scale = dh ** -0.5
    for h in range(XATTN_HEADS):
        sl = slice(h * dh, (h + 1) * dh)
        s = _dot_nt(q[:, sl], k_ref[0, :, sl]) * scale
        m = jnp.max(s, axis=-1, keepdims=True)
        e = jnp.exp(s - m)
        p = e / jnp.sum(e, axis=-1, keepdims=True)
        att_ref[:, sl] = _dot(p, v_ref[0, :, sl])
    y = _dot(att_ref[...], wo_ref[...])
    x2 = _layer_norm(alpha * x + y, g_ref[...], b_ref[...])
    o_ref[...] = x2
    for c in range(d // LANE):
        ot_ref[pl.ds(c, tm, stride=SUBLANE), :] = x2[:, c * LANE:(c + 1) * LANE]
    xh, xl = _split_bf16(x2)
    lg_ref[...] = (jnp.dot(xh, wrh_ref[...], preferred_element_type=F32)
                   + (jnp.dot(xl, wrh_ref[...], preferred_element_type=F32)
                      + jnp.dot(xh, wrl_ref[...], preferred_element_type=F32)))


def _xattn(x1, kv, wq, wo, g, bb, wr_hi, wr_lo, alpha, seq, tm=256):
    t, d = x1.shape
    mlen = kv.shape[1]
    tiles_per_seq = seq // tm
    row = lambda i: (i, 0)
    const = lambda i: (0, 0)
    return pl.pallas_call(
        functools.partial(_xattn_kernel, alpha),
        grid=(t // tm,),
        in_specs=[pl.BlockSpec((tm, d), row),
                  pl.BlockSpec((1, mlen, d), lambda i: (i // tiles_per_seq, 0, 0)),
                  pl.BlockSpec((1, mlen, d), lambda i: (i // tiles_per_seq, 0, 1)),
                  pl.BlockSpec((d, d), const), pl.BlockSpec((d, d), const),
                  pl.BlockSpec((1, d), const), pl.BlockSpec((1, d), const),
                  pl.BlockSpec((d, LANE), const), pl.BlockSpec((d, LANE), const)],
        out_specs=[pl.BlockSpec((tm, d), row),
                   pl.BlockSpec((tm * SUBLANE, LANE), row),
                   pl.BlockSpec((tm, LANE), row)],
        out_shape=[jax.ShapeDtypeStruct((t, d), F32),
                   jax.ShapeDtypeStruct((t * SUBLANE, LANE), F32),
                   jax.ShapeDtypeStruct((t, LANE), F32)],
        scratch_shapes=[pltpu.VMEM((tm, d), F32)],
        compiler_params=_params("parallel"),
        name="xattn",
    )(x1, kv, kv, wq, wo, g, bb, wr_hi, wr_lo)


def _route_kernel(lg_ref, meta_ref, cnt_ref, carry_ref):
    i = pl.program_id(0)

    @pl.when(i == 0)
    def _():
        carry_ref[...] = jnp.zeros_like(carry_ref)

    lg = lg_ref[...]
    tt = lg.shape[0]
    lane = lax.broadcasted_iota(jnp.int32, (tt, LANE), 1)
    is_group = lane < N_GROUPS
    mg = jnp.max(jnp.where(is_group, lg, NEG), axis=-1, keepdims=True)
    eg = jnp.where(is_group, jnp.exp(lg - mg), 0.0)
    pg = eg / jnp.sum(eg, axis=-1, keepdims=True)
    g1 = jnp.max(pg, axis=-1, keepdims=True)
    gi = jnp.min(jnp.where(is_group & (pg == g1), lane, LANE), axis=-1, keepdims=True)
    lo = N_GROUPS + gi * EXPERTS_PER_GROUP
    in_grp = (lane >= lo) & (lane < lo + EXPERTS_PER_GROUP)
    v1 = jnp.max(jnp.where(in_grp, lg, NEG), axis=-1, keepdims=True)
    i1 = jnp.min(jnp.where(in_grp & (lg == v1), lane, LANE), axis=-1, keepdims=True)
    rest = in_grp & (lane != i1)
    v2 = jnp.max(jnp.where(rest, lg, NEG), axis=-1, keepdims=True)
    i2 = jnp.min(jnp.where(rest & (lg == v2), lane, LANE), axis=-1, keepdims=True)
    e2 = jnp.exp(v2 - v1)
    den = 1.0 + e2
    gate1 = g1 * (1.0 / den)
    gate2 = g1 * (e2 / den)
    sel1 = lane == i1
    sel2 = lane == i2
    onehot = jnp.where(sel1 | sel2, 1.0, 0.0)
    ri = lax.broadcasted_iota(jnp.int32, (tt, tt), 0)
    ci = lax.broadcasted_iota(jnp.int32, (tt, tt), 1)
    before = jnp.where(ci < ri, 1.0, 0.0)
    rank = _dot(before, onehot) + carry_ref[0:1, :]
    r1 = jnp.sum(jnp.where(sel1, rank, 0.0), axis=-1, keepdims=True)
    r2 = jnp.sum(jnp.where(sel2, rank, 0.0), axis=-1, keepdims=True)
    total = carry_ref[0:1, :] + jnp.sum(onehot, axis=0, keepdims=True)
    carry_ref[0:1, :] = total
    cnt_ref[...] = jnp.broadcast_to(total, cnt_ref.shape)
    e1f = (i1 - N_GROUPS).astype(F32)
    e2f = (i2 - N_GROUPS).astype(F32)
    meta = jnp.where(lane == 0, e1f, 0.0)
    meta = jnp.where(lane == 1, e2f, meta)
    meta = jnp.where(lane == 2, r1, meta)
    meta = jnp.where(lane == 3, r2, meta)
    meta = jnp.where(lane == 4, gate1, meta)
    meta = jnp.where(lane == 5, gate2, meta)
    meta_ref[...] = meta


def _route(logits, tt=256):
    t = logits.shape[0]
    return pl.pallas_call(
        _route_kernel,
        grid=(t // tt,),
        in_specs=[pl.BlockSpec((tt, LANE), lambda i: (i, 0))],
        out_specs=[pl.BlockSpec((tt, LANE), lambda i: (i, 0)),
                   pl.BlockSpec((SUBLANE, LANE), lambda i: (0, 0))],
        out_shape=[jax.ShapeDtypeStruct((t, LANE), F32),
                   jax.ShapeDtypeStruct((SUBLANE, LANE), F32)],
        scratch_shapes=[pltpu.VMEM((SUBLANE, LANE), F32)],
        compiler_params=_params("arbitrary"),
        name="route",
    )(logits)


def _row_copy(src, dst, s_row, d_row, sem):
    return pltpu.make_async_copy(src.at[pl.ds(pl.multiple_of(s_row * SUBLANE, SUBLANE), SUBLANE), :],
                                 dst.at[pl.ds(pl.multiple_of(d_row * SUBLANE, SUBLANE), SUBLANE), :], sem)


def _dispatch_kernel(dest_ref, xt_ref, xs_ref, sem):
    tq = dest_ref.shape[2] // 2

    def issue(t, c):
        _row_copy(xt_ref, xs_ref, t, dest_ref[0, 0, 2 * t], sem).start(priority=0)
        _row_copy(xt_ref, xs_ref, t, dest_ref[0, 0, 2 * t + 1], sem).start(priority=1)
        return c

    lax.fori_loop(0, tq, issue, 0, unroll=4)
    for _ in range(2):
        pltpu.make_async_copy(xt_ref, xs_ref.at[pl.ds(0, tq * SUBLANE), :], sem).wait()


def _dispatch(dest, x_tiled, n_rows, tq=1024):
    t = dest.shape[0]
    dest3 = dest.reshape(t // tq, 1, 2 * tq)
    return pl.pallas_call(
        _dispatch_kernel,
        grid=(t // tq,),
        in_specs=[pl.BlockSpec((1, 1, 2 * tq), lambda i: (i, 0, 0), memory_space=pltpu.SMEM),
                  pl.BlockSpec((tq * SUBLANE, LANE), lambda i: (i, 0))],
        out_specs=pl.BlockSpec(memory_space=pl.ANY),
        out_shape=jax.ShapeDtypeStruct((n_rows * SUBLANE, LANE), F32),
        scratch_shapes=[pltpu.SemaphoreType.DMA(())],
        compiler_params=_params("arbitrary"),
        name="dispatch",
    )(dest3, x_tiled)


def _experts_kernel(blk_exp_ref, blk_rows_ref, n_used_ref, blk_first_ref, blk_slot_ref, blk_next_ref,
                    xs_ref, wg_hbm, wu_hbm, wd_hbm, ys_ref, x_scr, y_scr, wg_buf, wu_buf, wd_buf, sems):
    b = pl.program_id(0)

    def weight_copies(e, slot):
        return [pltpu.make_async_copy(hbm.at[e], buf.at[slot], sems.at[k, slot])
                for k, (hbm, buf) in enumerate(((wg_hbm, wg_buf), (wu_hbm, wu_buf), (wd_hbm, wd_buf)))]

    @pl.when(b < n_used_ref[0])
    def _():
        slot = blk_slot_ref[b]

        @pl.when(b == 0)
        def _():
            for cp in weight_copies(blk_exp_ref[0], slot):
                cp.start()

        @pl.when(blk_first_ref[b] == 1)
        def _():
            @pl.when(blk_next_ref[b] >= 0)
            def _():
                for cp in weight_copies(blk_next_ref[b], 1 - slot):
                    cp.start()

            for cp in weight_copies(blk_exp_ref[b], slot):
                cp.wait()

        rows, d = x_scr.shape
        for c in range(d // LANE):
            x_scr[:, c * LANE:(c + 1) * LANE] = xs_ref[pl.ds(c, rows, stride=SUBLANE), :]
        ri = lax.broadcasted_iota(jnp.int32, (rows, 1), 0)
        x = jnp.where(ri < blk_rows_ref[b], x_scr[...], 0.0)
        hg = _dot(x, wg_buf[slot])
        hu = _dot(x, wu_buf[slot])
        y_scr[...] = _dot(hg * jax.nn.sigmoid(hg) * hu, wd_buf[slot])
        for c in range(d // LANE):
            ys_ref[pl.ds(c, rows, stride=SUBLANE), :] = y_scr[:, c * LANE:(c + 1) * LANE]


def _experts(blk_exp, blk_rows, n_used, blk_first, blk_slot, blk_next, xs_tiled, w_g, w_u, w_d):
    n_rows = xs_tiled.shape[0] // SUBLANE
    rows = EXPERT_ROWS
    nblk = n_rows // rows
    _, d, ff = w_g.shape
    used = lambda b, be, br, nu, *_: (jnp.minimum(b, nu[0] - 1), 0)
    grid_spec = pltpu.PrefetchScalarGridSpec(
        num_scalar_prefetch=6,
        grid=(nblk,),
        in_specs=[pl.BlockSpec((rows * SUBLANE, LANE), used),
                  pl.BlockSpec(memory_space=pl.ANY), pl.BlockSpec(memory_space=pl.ANY),
                  pl.BlockSpec(memory_space=pl.ANY)],
        out_specs=pl.BlockSpec((rows * SUBLANE, LANE), used),
        scratch_shapes=[pltpu.VMEM((rows, d), F32), pltpu.VMEM((rows, d), F32),
                        pltpu.VMEM((2, d, ff), F32), pltpu.VMEM((2, d, ff), F32), pltpu.VMEM((2, ff, d), F32),
                        pltpu.SemaphoreType.DMA((3, 2))],
    )
    return pl.pallas_call(
        _experts_kernel,
        grid_spec=grid_spec,
        out_shape=jax.ShapeDtypeStruct((n_rows * SUBLANE, LANE), F32),
        compiler_params=_params("arbitrary"),
        name="experts",
    )(blk_exp, blk_rows, n_used, blk_first, blk_slot, blk_next, xs_tiled, w_g, w_u, w_d)


def _combine_kernel(alpha, dcur_ref, dnext_ref, meta_ref, x_ref, g_ref, b_ref, ys_ref, o_ref, buf_ref, sems):
    i = pl.program_id(0)
    tq, d = x_ref.shape
    slot_rows = 2 * tq * SUBLANE

    def issue_all(d_ref, slot):
        def issue(t, c):
            _row_copy(ys_ref, buf_ref, d_ref[0, 0, 2 * t], slot * (2 * tq) + t, sems.at[slot]).start(priority=0)
            _row_copy(ys_ref, buf_ref, d_ref[0, 0, 2 * t + 1], slot * (2 * tq) + tq + t, sems.at[slot]).start(priority=1)
            return c

        lax.fori_loop(0, tq, issue, 0, unroll=4)

    slot = i % 2

    @pl.when(i == 0)
    def _():
        issue_all(dcur_ref, 0)

    @pl.when(i + 1 < pl.num_programs(0))
    def _():
        issue_all(dnext_ref, 1 - slot)

    off = pl.multiple_of(slot * slot_rows, slot_rows)
    pltpu.make_async_copy(ys_ref.at[pl.ds(0, slot_rows), :], buf_ref.at[pl.ds(off, slot_rows), :],
                          sems.at[slot]).wait()

    gate1 = meta_ref[:, 4:5]
    gate2 = meta_ref[:, 5:6]
    for c in range(d // LANE):
        sl = slice(c * LANE, (c + 1) * LANE)
        y1 = buf_ref[pl.ds(off + c, tq, stride=SUBLANE), :]
        y2 = buf_ref[pl.ds(off + tq * SUBLANE + c, tq, stride=SUBLANE), :]
        o_ref[:, sl] = alpha * x_ref[:, sl] + (y1 * gate1 + y2 * gate2)
    o_ref[...] = _layer_norm(o_ref[...], g_ref[...], b_ref[...])


def _combine(dest, meta, x2, g, bb, ys_tiled, alpha, tq=256):
    t, d = x2.shape
    n = t // tq
    dest3 = dest.reshape(n, 1, 2 * tq)
    row = lambda i: (i, 0)
    const = lambda i: (0, 0)
    return pl.pallas_call(
        functools.partial(_combine_kernel, alpha),
        grid=(n,),
        in_specs=[pl.BlockSpec((1, 1, 2 * tq), lambda i: (i, 0, 0), memory_space=pltpu.SMEM),
                  pl.BlockSpec((1, 1, 2 * tq), lambda i: (jnp.minimum(i + 1, n - 1), 0, 0), memory_space=pltpu.SMEM),
                  pl.BlockSpec((tq, LANE), row),
                  pl.BlockSpec((tq, d), row),
                  pl.BlockSpec((1, d), const), pl.BlockSpec((1, d), const),
                  pl.BlockSpec(memory_space=pl.ANY)],
        out_specs=pl.BlockSpec((tq, d), row),
        out_shape=jax.ShapeDtypeStruct((t, d), F32),
        scratch_shapes=[pltpu.VMEM((2 * 2 * tq * SUBLANE, LANE), F32), pltpu.SemaphoreType.DMA((2,))],
        compiler_params=_params("arbitrary"),
        name="combine",
    )(dest3, dest3, meta, x2, g, bb, ys_tiled)


def _moe(x2, x2_tiled, logits, w_g, w_u, w_d, g, bb, alpha):
    t, d = x2.shape
    meta, cnt = _route(logits)
    eid = meta[:, 0:2].astype(jnp.int32)
    rank = meta[:, 2:4].astype(jnp.int32)
    counts = cnt[0, N_GROUPS:N_GROUPS + N_EXPERTS].astype(jnp.int32)
    padded = ((counts + EXPERT_ROWS - 1) // EXPERT_ROWS) * EXPERT_ROWS
    pend = jnp.cumsum(padded)
    poff = pend - padded
    dest = poff[eid] + rank
    n_rows = t * 2 + N_EXPERTS * EXPERT_ROWS
    nblk = n_rows // EXPERT_ROWS
    blk_start = jnp.arange(nblk, dtype=jnp.int32) * EXPERT_ROWS
    blk_exp = jnp.sum((pend[None, :] <= blk_start[:, None]).astype(jnp.int32), axis=1)
    blk_exp = jnp.minimum(blk_exp, N_EXPERTS - 1)
    blk_rows = jnp.clip(poff[blk_exp] + counts[blk_exp] - blk_start, 0, EXPERT_ROWS).astype(jnp.int32)
    n_used = (pend[-1:] // EXPERT_ROWS).astype(jnp.int32)
    e_idx = jnp.arange(N_EXPERTS, dtype=jnp.int32)
    has_rows = counts > 0
    ordinal = jnp.cumsum(has_rows.astype(jnp.int32)) - 1
    later = has_rows[None, :] & (e_idx[None, :] > e_idx[:, None])
    next_used = jnp.min(jnp.where(later, e_idx[None, :], N_EXPERTS), axis=1)
    next_used = jnp.where(next_used == N_EXPERTS, -1, next_used).astype(jnp.int32)
    blk_first = (blk_start == poff[blk_exp]).astype(jnp.int32)
    blk_slot = (ordinal[blk_exp] % 2).astype(jnp.int32)
    blk_next = next_used[blk_exp]
    xs_tiled = _dispatch(dest, x2_tiled, n_rows)
    ys_tiled = _experts(blk_exp, blk_rows, n_used, blk_first, blk_slot, blk_next, xs_tiled, w_g, w_u, w_d)
    return _combine(dest, meta, x2, g, bb, ys_tiled, alpha)


def kernel(x, mem, w_in, w_out, ln_mix_g, ln_mix_b, w_xq, w_xkv, w_xo, ln_x_g, ln_x_b, w_route_group,
           w_route_expert, w_exp_gate, w_exp_up, w_exp_down, ln_moe_g, ln_moe_b):
    b, s, d = x.shape
    depth = w_in.shape[0]
    alpha = (2.0 * depth) ** 0.25
    t = b * s
    ret_cols = (2 * RET_HEADS * RET_QK_DIM + 2 * RET_HEADS * RET_V_DIM)
    assert ret_cols % (DIL_HEADS * DIL_HEAD_DIM) == 0
    dil_col0 = ret_cols // (DIL_HEADS * DIL_HEAD_DIM)
    xc = x.reshape(t, d)
    for l in range(depth):
        h = _proj_in(xc, w_in[l].astype(BF16))
        h3 = h.reshape(b, s, h.shape[1])
        y_ret = _retention(h3).reshape(t, -1)
        y_dil = _dilated(h3, dil_col0).reshape(t, -1)
        x1 = _mix_out(xc, y_ret, y_dil, w_out[l].astype(BF16), ln_mix_g[l][None], ln_mix_b[l][None], alpha)
        kv = _mem_kv(mem.reshape(b * mem.shape[1], d), w_xkv[l].astype(BF16)).reshape(b, mem.shape[1], 2 * d)
        w_r = jnp.concatenate([w_route_group[l], w_route_expert[l]], axis=-1)
        w_r = jnp.pad(w_r, ((0, 0), (0, LANE - w_r.shape[1])))
        wr_hi, wr_lo = _split_bf16(w_r)
        x2, x2_tiled, logits = _xattn(x1, kv, w_xq[l].astype(BF16), w_xo[l].astype(BF16),
                                      ln_x_g[l][None], ln_x_b[l][None], wr_hi, wr_lo, alpha, s)
        xc = _moe(x2, x2_tiled, logits, w_exp_gate[l], w_exp_up[l], w_exp_down[l],
                  ln_moe_g[l][None], ln_moe_b[l][None], alpha)
    return xc.reshape(b, s, d)
```

```python
import functools

import jax
import jax.numpy as jnp
from jax import lax
from jax.experimental import pallas as pl
from jax.experimental.pallas import tpu as pltpu

BF16 = jnp.bfloat16
F32 = jnp.float32

LANE = 128
SUBLANE = 8
VMEM_LIMIT = 56 * 1024 * 1024

RET_HEADS = 4
RET_QK_DIM = 64
RET_V_DIM = 128
RET_CHUNK = 128
ROPE_BASE = 10000.0
DIL_HEADS = 8
DIL_HEAD_DIM = 64
DIL_DILATIONS = (1, 4, 16)
DIL_BLOCK = 128
DIL_SUPER = DIL_BLOCK * max(DIL_DILATIONS)
DIL_GROUP = 4
XATTN_HEADS = 4
N_GROUPS = 4
EXPERTS_PER_GROUP = 8
N_EXPERTS = N_GROUPS * EXPERTS_PER_GROUP
EXPERT_ROWS = 256
ROUTE_TILE = 256
LN_EPS = 1e-5
GN_EPS = 1e-6
NEG = -1e30


def _params(*sem):
    return pltpu.CompilerParams(dimension_semantics=sem, vmem_limit_bytes=VMEM_LIMIT)


def _layer_norm(z, g, b):
    mu = jnp.mean(z, axis=-1, keepdims=True)
    zc = z - mu
    var = jnp.mean(zc * zc, axis=-1, keepdims=True)
    return zc * lax.rsqrt(var + LN_EPS) * g + b


def _dot(a, b):
    return jnp.dot(a.astype(BF16), b.astype(BF16), preferred_element_type=F32)


def _dot_nt(a, b):
    return lax.dot_general(a.astype(BF16), b.astype(BF16), (((1,), (1,)), ((), ())),
                           preferred_element_type=F32)


def _dot_tn(a, b):
    return lax.dot_general(a.astype(BF16), b.astype(BF16), (((0,), (0,)), ((), ())),
                           preferred_element_type=F32)


def _proj_in_kernel(x_ref, w_ref, o_ref):
    o_ref[...] = _dot(x_ref[...], w_ref[...])


def _proj_in(x2d, w_bf16, tm=512):
    t, d = x2d.shape
    n = w_bf16.shape[1]
    return pl.pallas_call(
        _proj_in_kernel,
        grid=(t // tm,),
        in_specs=[pl.BlockSpec((tm, d), lambda i: (i, 0)),
                  pl.BlockSpec((d, n), lambda i: (0, 0))],
        out_specs=pl.BlockSpec((tm, n), lambda i: (i, 0)),
        out_shape=jax.ShapeDtypeStruct((t, n), F32),
        compiler_params=_params("parallel"),
        name="proj_in",
    )(x2d, w_bf16)


def _retention_kernel(qk_ref, v_ref, g_ref, cos_ref, sin_ref, decay_ref, zeta_ref, xi_ref, gam_ref,
                      o_ref, state_ref):
    n = pl.program_id(1)

    @pl.when(n == 0)
    def _():
        state_ref[...] = jnp.zeros_like(state_ref)

    c = qk_ref.shape[1]
    lane = lax.broadcasted_iota(jnp.int32, (c, LANE), 1)
    first_half = (lane % RET_QK_DIM) < (RET_QK_DIM // 2)
    n_pairs = RET_HEADS * RET_QK_DIM // LANE

    def rot(col):
        t = qk_ref[0, :, col * LANE:(col + 1) * LANE]
        swapped = jnp.where(first_half,
                            pltpu.roll(t, LANE - RET_QK_DIM // 2, axis=1),
                            pltpu.roll(t, RET_QK_DIM // 2, axis=1))
        return t * cos_ref[...] + swapped * sin_ref[...]

    q_pairs = [rot(p) for p in range(n_pairs)]
    k_pairs = [rot(n_pairs + p) * (RET_QK_DIM ** -0.5) for p in range(n_pairs)]

    for h in range(RET_HEADS):
        p = (h * RET_QK_DIM) // LANE
        lo = (h * RET_QK_DIM) % LANE
        mine = (lane >= lo) & (lane < lo + RET_QK_DIM)
        qm = jnp.where(mine, q_pairs[p], 0.0)
        km = jnp.where(mine, k_pairs[p], 0.0)
        v = v_ref[0, :, h * RET_V_DIM:(h + 1) * RET_V_DIM]
        s = _dot_nt(qm, k_pairs[p]) * decay_ref[h]
        y = _dot(s, v)
        st = state_ref[h]
        y = y + _dot(qm, st) * xi_ref[h]
        kv = _dot_tn(km * zeta_ref[h], v)
        state_ref[h] = gam_ref[h] * st + kv
        mu = jnp.mean(y, axis=-1, keepdims=True)
        yc = y - mu
        var = jnp.mean(yc * yc, axis=-1, keepdims=True)
        yn = yc * lax.rsqrt(var + GN_EPS)
        gate = g_ref[0, :, h * RET_V_DIM:(h + 1) * RET_V_DIM]
        o_ref[0, :, h * RET_V_DIM:(h + 1) * RET_V_DIM] = gate * jax.nn.sigmoid(gate) * yn


def _retention_tables(s):
    half = RET_QK_DIM // 2
    inv = 1.0 / (ROPE_BASE ** (jnp.arange(half, dtype=F32) / half))
    ang = jnp.arange(s, dtype=F32)[:, None] * inv[None, :]
    cos = jnp.cos(ang)
    sin = jnp.sin(ang)
    cos_t = jnp.tile(jnp.concatenate([cos, cos], axis=-1), (1, LANE // RET_QK_DIM))
    sin_t = jnp.tile(jnp.concatenate([-sin, sin], axis=-1), (1, LANE // RET_QK_DIM))
    c = RET_CHUNK
    lg = jnp.log(1.0 - jnp.exp2(-5.0 - jnp.arange(RET_HEADS, dtype=F32)))
    idx = jnp.arange(c, dtype=F32)
    diff = idx[:, None] - idx[None, :]
    decay = jnp.where(diff >= 0, jnp.exp(lg[:, None, None] * jnp.maximum(diff, 0.0)), 0.0)
    zeta = jnp.exp(lg[:, None] * (c - 1.0 - idx))[:, :, None]
    xi = jnp.exp(lg[:, None] * (idx + 1.0))[:, :, None]
    gam = jnp.broadcast_to(jnp.exp(lg * c)[:, None, None], (RET_HEADS, 1, LANE))
    return cos_t, sin_t, decay, zeta, xi, gam


def _retention(h3):
    b, s, _ = h3.shape
    c = RET_CHUNK
    qk_w = 2 * RET_HEADS * RET_QK_DIM
    v_w = RET_HEADS * RET_V_DIM
    assert qk_w == v_w
    cos_t, sin_t, decay, zeta, xi, gam = _retention_tables(s)
    const3 = lambda bi, n: (0, 0, 0)
    return pl.pallas_call(
        _retention_kernel,
        grid=(b, s // c),
        in_specs=[pl.BlockSpec((1, c, qk_w), lambda bi, n: (bi, n, 0)),
                  pl.BlockSpec((1, c, v_w), lambda bi, n: (bi, n, 1)),
                  pl.BlockSpec((1, c, v_w), lambda bi, n: (bi, n, 2)),
                  pl.BlockSpec((c, LANE), lambda bi, n: (n, 0)),
                  pl.BlockSpec((c, LANE), lambda bi, n: (n, 0)),
                  pl.BlockSpec((RET_HEADS, c, c), const3),
                  pl.BlockSpec((RET_HEADS, c, 1), const3),
                  pl.BlockSpec((RET_HEADS, c, 1), const3),
                  pl.BlockSpec((RET_HEADS, 1, LANE), const3)],
        out_specs=pl.BlockSpec((1, c, v_w), lambda bi, n: (bi, n, 0)),
        out_shape=jax.ShapeDtypeStruct((b, s, v_w), F32),
        scratch_shapes=[pltpu.VMEM((RET_HEADS, LANE, RET_V_DIM), F32)],
        compiler_params=_params("parallel", "arbitrary"),
        name="retention",
    )(h3, h3, h3, cos_t, sin_t, decay, zeta, xi, gam)


def _dilated_kernel(q_ref, kp_ref, kc_ref, vp_ref, vc_ref, bias_ref, o_ref, kk_ref, vv_ref, acc_ref, m_ref, l_ref,
                    s_ref, p_ref):
    j = pl.program_id(2)
    sup = DIL_SUPER
    q_blk = DIL_BLOCK
    kk_ref[0:sup, :] = kp_ref[0]
    kk_ref[sup:2 * sup, :] = kc_ref[0]
    vv_ref[0:sup, :] = vp_ref[0]
    vv_ref[sup:2 * sup, :] = vc_ref[0]
    scale = DIL_HEAD_DIM ** -0.5
    n_blocks = sup // q_blk

    for bi, d in enumerate(DIL_DILATIONS):
        n_per_r = sup // (q_blk * d)

        def group(t0, carry, bi=bi, d=d, n_per_r=n_per_r):
            lane = lax.broadcasted_iota(jnp.int32, (q_blk, LANE), 1)
            head0 = lane < DIL_HEAD_DIM
            starts = []
            for g in range(DIL_GROUP):
                t = t0 + g * (n_blocks // DIL_GROUP)
                r = t // n_per_r
                n = t % n_per_r
                q_start = n * (q_blk * d) + r
                k_start = sup + (n - 1) * (q_blk * d) + r
                starts.append((q_start, k_start))
                q = q_ref[0, pl.ds(q_start, q_blk, stride=d), :] * scale
                kb = kk_ref[pl.ds(k_start, 2 * q_blk, stride=d), :]
                q2 = jnp.concatenate([jnp.where(head0, q, 0.0), jnp.where(head0, 0.0, q)], axis=0)
                first = jnp.where((j == 0) & (n == 0), 1, 0)
                s_ref[g] = _dot_nt(q2, kb) + bias_ref[first]
            for g in range(DIL_GROUP):
                s = s_ref[g]
                m2 = jnp.max(s, axis=-1, keepdims=True)
                p_ref[g] = jnp.exp(s - m2).astype(BF16)
                m_ref[bi, pl.ds(starts[g][0], q_blk, stride=d), :] = jnp.where(head0, m2[:q_blk], m2[q_blk:])
            for g in range(DIL_GROUP):
                q_start, k_start = starts[g]
                vb = vv_ref[pl.ds(k_start, 2 * q_blk, stride=d), :].astype(BF16)
                o2 = jnp.dot(p_ref[g], jnp.concatenate([vb, jnp.ones_like(vb)], axis=1),
                             preferred_element_type=F32)
                rows = pl.ds(q_start, q_blk, stride=d)
                acc_ref[bi, rows, :] = jnp.where(head0, o2[:q_blk, :LANE], o2[q_blk:, :LANE])
                l_ref[bi, rows, :] = jnp.where(head0, o2[:q_blk, LANE:], o2[q_blk:, LANE:])
            return carry

        lax.fori_loop(0, n_blocks // DIL_GROUP, group, 0)

    def merge(c, carry):
        rows = pl.ds(pl.multiple_of(c * q_blk, q_blk), q_blk)
        ms = [m_ref[bi, rows, :] for bi in range(len(DIL_DILATIONS))]
        m_all = functools.reduce(jnp.maximum, ms)
        ws = [jnp.exp(m - m_all) for m in ms]
        num = functools.reduce(lambda a, b: a + b, [w * acc_ref[bi, rows, :] for bi, w in enumerate(ws)])
        den = functools.reduce(lambda a, b: a + b, [w * l_ref[bi, rows, :] for bi, w in enumerate(ws)])
        o_ref[0, rows, :] = num / den
        return carry

    lax.fori_loop(0, n_blocks, merge, 0)


def _dilated_bias():
    q_blk = DIL_BLOCK
    qi = jnp.arange(2 * q_blk)[:, None] % q_blk
    kj = jnp.arange(2 * q_blk)[None, :]
    band = (kj >= qi) & (kj <= qi + q_blk)
    return jnp.stack([jnp.where(band, 0.0, NEG), jnp.where(band & (kj >= q_blk), 0.0, NEG)]).astype(F32)


def _dilated(h3, col0):
    b, s, _ = h3.shape
    sup = DIL_SUPER
    n_pairs = DIL_HEADS * DIL_HEAD_DIM // LANE
    n_br = len(DIL_DILATIONS)
    cq, ck, cv = col0 * n_pairs, (col0 + 1) * n_pairs, (col0 + 2) * n_pairs
    cur = lambda c: (lambda bi, p, j: (bi, j, c + p))
    prev = lambda c: (lambda bi, p, j: (bi, jnp.maximum(j - 1, 0), c + p))
    blk = (1, sup, LANE)
    return pl.pallas_call(
        _dilated_kernel,
        grid=(b, n_pairs, s // sup),
        in_specs=[pl.BlockSpec(blk, cur(cq)),
                  pl.BlockSpec(blk, prev(ck)), pl.BlockSpec(blk, cur(ck)),
                  pl.BlockSpec(blk, prev(cv)), pl.BlockSpec(blk, cur(cv)),
                  pl.BlockSpec((2, 2 * DIL_BLOCK, 2 * DIL_BLOCK), lambda bi, p, j: (0, 0, 0))],
        out_specs=pl.BlockSpec(blk, lambda bi, p, j: (bi, j, p)),
        out_shape=jax.ShapeDtypeStruct((b, s, n_pairs * LANE), F32),
        scratch_shapes=[pltpu.VMEM((2 * sup, LANE), F32), pltpu.VMEM((2 * sup, LANE), F32),
                        pltpu.VMEM((n_br, sup, LANE), F32), pltpu.VMEM((n_br, sup, LANE), F32),
                        pltpu.VMEM((n_br, sup, LANE), F32),
                        pltpu.VMEM((DIL_GROUP, 2 * DIL_BLOCK, 2 * DIL_BLOCK), F32),
                        pltpu.VMEM((DIL_GROUP, 2 * DIL_BLOCK, 2 * DIL_BLOCK), BF16)],
        compiler_params=_params("parallel", "parallel", "arbitrary"),
        name="dilated",
    )(h3, h3, h3, h3, h3, _dilated_bias())


def _mix_out_kernel(alpha, x_ref, yr_ref, yd_ref, w_ref, g_ref, b_ref, o_ref):
    wr = yr_ref.shape[1]
    y = _dot(yr_ref[...], w_ref[0:wr, :]) + _dot(yd_ref[...], w_ref[wr:, :])
    o_ref[...] = _layer_norm(alpha * x_ref[...] + y, g_ref[...], b_ref[...])


def _mix_out(x2d, y_ret, y_dil, w_bf16, g, bb, alpha, tm=512):
    t, d = x2d.shape
    row = lambda i: (i, 0)
    const = lambda i: (0, 0)
    return pl.pallas_call(
        functools.partial(_mix_out_kernel, alpha),
        grid=(t // tm,),
        in_specs=[pl.BlockSpec((tm, d), row),
                  pl.BlockSpec((tm, y_ret.shape[1]), row),
                  pl.BlockSpec((tm, y_dil.shape[1]), row),
                  pl.BlockSpec(w_bf16.shape, const),
                  pl.BlockSpec((1, d), const), pl.BlockSpec((1, d), const)],
        out_specs=pl.BlockSpec((tm, d), row),
        out_shape=jax.ShapeDtypeStruct((t, d), F32),
        compiler_params=_params("parallel"),
        name="mix_out",
    )(x2d, y_ret, y_dil, w_bf16, g, bb)


def _mem_kv(mem2d, w_bf16, tn=512):
    m, d = mem2d.shape
    n = w_bf16.shape[1]
    return pl.pallas_call(
        _proj_in_kernel,
        grid=(n // tn,),
        in_specs=[pl.BlockSpec((m, d), lambda i: (0, 0)),
                  pl.BlockSpec((d, tn), lambda i: (0, i))],
        out_specs=pl.BlockSpec((m, tn), lambda i: (0, i)),
        out_shape=jax.ShapeDtypeStruct((m, n), F32),
        compiler_params=_params("parallel"),
        name="mem_kv",
    )(mem2d, w_bf16)


def _xattn_kernel(alpha, x_ref, k_ref, v_ref, wq_ref, wo_ref, g_ref, b_ref, wr_ref,
                  o_ref, ot_ref, lg_ref, att_ref):
    tm, d = x_ref.shape
    dh = d // XATTN_HEADS
    x = x_ref[...]
    q = _dot(x, wq_ref[...])
    scale = dh ** -0.5
    for h in range(XATTN_HEADS):
        sl = slice(h * dh, (h + 1) * dh)
        s = _dot_nt(q[:, sl], k_ref[0, :, sl]) * scale
        m = jnp.max(s, axis=-1, keepdims=True)
        e = jnp.exp(s - m)
        p = e / jnp.sum(e, axis=-1, keepdims=True)
        att_ref[:, sl] = _dot(p, v_ref[0, :, sl])
    y = _dot(att_ref[...], wo_ref[...])
    x2 = _layer_norm(alpha * x + y, g_ref[...], b_ref[...])
    o_ref[...] = x2
    for c in range(d // LANE):
        ot_ref[pl.ds(c, tm, stride=SUBLANE), :] = x2[:, c * LANE:(c + 1) * LANE]
    lg_ref[...] = _dot(x2, wr_ref[...])


def _xattn(x1, kv, wq, wo, g, bb, w_r, alpha, seq, tm=256):
    t, d = x1.shape
    mlen = kv.shape[1]
    tiles_per_seq = seq // tm
    row = lambda i: (i, 0)
    const = lambda i: (0, 0)
    return pl.pallas_call(
        functools.partial(_xattn_kernel, alpha),
        grid=(t // tm,),
        in_specs=[pl.BlockSpec((tm, d), row),
                  pl.BlockSpec((1, mlen, d), lambda i: (i // tiles_per_seq, 0, 0)),
                  pl.BlockSpec((1, mlen, d), lambda i: (i // tiles_per_seq, 0, 1)),
                  pl.BlockSpec((d, d), const), pl.BlockSpec((d, d), const),
                  pl.BlockSpec((1, d), const), pl.BlockSpec((1, d), const),
                  pl.BlockSpec((d, LANE), const)],
        out_specs=[pl.BlockSpec((tm, d), row),
                   pl.BlockSpec((tm * SUBLANE, LANE), row),
                   pl.BlockSpec((tm, LANE), row)],
        out_shape=[jax.ShapeDtypeStruct((t, d), F32),
                   jax.ShapeDtypeStruct((t * SUBLANE, LANE), F32),
                   jax.ShapeDtypeStruct((t, LANE), F32)],
        scratch_shapes=[pltpu.VMEM((tm, d), F32)],
        compiler_params=_params("parallel"),
        name="xattn",
    )(x1, kv, kv, wq, wo, g, bb, w_r)


def _route_kernel(lg_ref, meta_ref, cnt_ref, carry_ref):
    i = pl.program_id(0)

    @pl.when(i == 0)
    def _():
        carry_ref[...] = jnp.zeros_like(carry_ref)

    lg = lg_ref[...]
    tt = lg.shape[0]
    lane = lax.broadcasted_iota(jnp.int32, (tt, LANE), 1)
    is_group = lane < N_GROUPS
    mg = jnp.max(jnp.where(is_group, lg, NEG), axis=-1, keepdims=True)
    eg = jnp.where(is_group, jnp.exp(lg - mg), 0.0)
    pg = eg / jnp.sum(eg, axis=-1, keepdims=True)
    g1 = jnp.max(pg, axis=-1, keepdims=True)
    gi = jnp.min(jnp.where(is_group & (pg == g1), lane, LANE), axis=-1, keepdims=True)
    lo = N_GROUPS + gi * EXPERTS_PER_GROUP
    in_grp = (lane >= lo) & (lane < lo + EXPERTS_PER_GROUP)
    v1 = jnp.max(jnp.where(in_grp, lg, NEG), axis=-1, keepdims=True)
    i1 = jnp.min(jnp.where(in_grp & (lg == v1), lane, LANE), axis=-1, keepdims=True)
    rest = in_grp & (lane != i1)
    v2 = jnp.max(jnp.where(rest, lg, NEG), axis=-1, keepdims=True)
    i2 = jnp.min(jnp.where(rest & (lg == v2), lane, LANE), axis=-1, keepdims=True)
    e2 = jnp.exp(v2 - v1)
    den = 1.0 + e2
    gate1 = g1 * (1.0 / den)
    gate2 = g1 * (e2 / den)
    sel1 = lane == i1
    sel2 = lane == i2
    onehot = jnp.where(sel1 | sel2, 1.0, 0.0)
    ri = lax.broadcasted_iota(jnp.int32, (tt, tt), 0)
    ci = lax.broadcasted_iota(jnp.int32, (tt, tt), 1)
    before = jnp.where(ci < ri, 1.0, 0.0)
    rank = _dot(before, onehot) + carry_ref[0:1, :]
    r1 = jnp.sum(jnp.where(sel1, rank, 0.0), axis=-1, keepdims=True)
    r2 = jnp.sum(jnp.where(sel2, rank, 0.0), axis=-1, keepdims=True)
    total = carry_ref[0:1, :] + jnp.sum(onehot, axis=0, keepdims=True)
    carry_ref[0:1, :] = total
    cnt_ref[...] = jnp.broadcast_to(total, cnt_ref.shape)
    e1f = (i1 - N_GROUPS).astype(F32)
    e2f = (i2 - N_GROUPS).astype(F32)
    meta = jnp.where(lane == 0, e1f, 0.0)
    meta = jnp.where(lane == 1, e2f, meta)
    meta = jnp.where(lane == 2, r1, meta)
    meta = jnp.where(lane == 3, r2, meta)
    meta = jnp.where(lane == 4, gate1, meta)
    meta = jnp.where(lane == 5, gate2, meta)
    meta_ref[...] = meta


def _route(logits, tt=256):
    t = logits.shape[0]
    return pl.pallas_call(
        _route_kernel,
        grid=(t // tt,),
        in_specs=[pl.BlockSpec((tt, LANE), lambda i: (i, 0))],
        out_specs=[pl.BlockSpec((tt, LANE), lambda i: (i, 0)),
                   pl.BlockSpec((SUBLANE, LANE), lambda i: (0, 0))],
        out_shape=[jax.ShapeDtypeStruct((t, LANE), F32),
                   jax.ShapeDtypeStruct((SUBLANE, LANE), F32)],
        scratch_shapes=[pltpu.VMEM((SUBLANE, LANE), F32)],
        compiler_params=_params("arbitrary"),
        name="route",
    )(logits)


def _plan_kernel(meta_ref, first_row_ref, dest_ref):
    tt = meta_ref.shape[0]
    first_row = first_row_ref[0:1, :]
    meta = meta_ref[...]
    lane = lax.broadcasted_iota(jnp.int32, (tt, LANE), 1)

    def dest_of(k):
        e_lane = meta[:, k:k + 1].astype(jnp.int32) + N_GROUPS
        return jnp.sum(jnp.where(lane == e_lane, first_row, 0.0), axis=-1, keepdims=True) + meta[:, 2 + k:3 + k]

    d = jnp.where(lane == 0, dest_of(0), jnp.where(lane == 1, dest_of(1), 0.0))
    dt = jnp.transpose(d)
    dest_ref[0] = jnp.concatenate([dt[0:1, :], dt[1:2, :]], axis=1).astype(jnp.int32)


def _plan(meta, first_row, tt):
    t = meta.shape[0]
    return pl.pallas_call(
        _plan_kernel,
        grid=(t // tt,),
        in_specs=[pl.BlockSpec((tt, LANE), lambda i: (i, 0)),
                  pl.BlockSpec((SUBLANE, LANE), lambda i: (0, 0))],
        out_specs=pl.BlockSpec((1, 1, 2 * tt), lambda i: (i, 0, 0)),
        out_shape=jax.ShapeDtypeStruct((t // tt, 1, 2 * tt), jnp.int32),
        compiler_params=_params("parallel"),
        name="plan",
    )(meta, first_row)


def _row_copy(src, dst, s_row, d_row, sem):
    return pltpu.make_async_copy(src.at[pl.ds(pl.multiple_of(s_row * SUBLANE, SUBLANE), SUBLANE), :],
                                 dst.at[pl.ds(pl.multiple_of(d_row * SUBLANE, SUBLANE), SUBLANE), :], sem)


def _dispatch_kernel(dest_ref, xt_ref, xs_ref, sem):
    n_tiles = dest_ref.shape[0]
    tt = dest_ref.shape[2] // 2

    for g in range(n_tiles):
        def issue(t, c, g=g):
            src = xt_ref.at[pl.ds(g * tt * SUBLANE, tt * SUBLANE), :]
            _row_copy(src, xs_ref, t, dest_ref[g, 0, t], sem).start(priority=0)
            _row_copy(src, xs_ref, t, dest_ref[g, 0, tt + t], sem).start(priority=1)
            return c

        lax.fori_loop(0, tt, issue, 0, unroll=4)
    for _ in range(2):
        pltpu.make_async_copy(xt_ref, xs_ref.at[pl.ds(0, n_tiles * tt * SUBLANE), :], sem).wait()


def _dispatch(dest3, x_tiled, n_rows, tiles_per_step=4):
    n_tiles, _, two_tt = dest3.shape
    tq = tiles_per_step * two_tt // 2
    return pl.pallas_call(
        _dispatch_kernel,
        grid=(n_tiles // tiles_per_step,),
        in_specs=[pl.BlockSpec((tiles_per_step, 1, two_tt), lambda i: (i, 0, 0), memory_space=pltpu.SMEM),
                  pl.BlockSpec((tq * SUBLANE, LANE), lambda i: (i, 0))],
        out_specs=pl.BlockSpec(memory_space=pl.ANY),
        out_shape=jax.ShapeDtypeStruct((n_rows * SUBLANE, LANE), F32),
        scratch_shapes=[pltpu.SemaphoreType.DMA(())],
        compiler_params=_params("arbitrary"),
        name="dispatch",
    )(dest3, x_tiled)


def _experts_kernel(blk_exp_ref, blk_rows_ref, n_used_ref, blk_first_ref, blk_slot_ref, blk_next_ref,
                    xs_ref, wg_hbm, wu_hbm, wd_hbm, ys_ref, x_scr, y_scr, wg_buf, wu_buf, wd_buf, sems):
    b = pl.program_id(0)

    def weight_copies(e, slot):
        return [pltpu.make_async_copy(hbm.at[e], buf.at[slot], sems.at[k, slot])
                for k, (hbm, buf) in enumerate(((wg_hbm, wg_buf), (wu_hbm, wu_buf), (wd_hbm, wd_buf)))]

    @pl.when(b < n_used_ref[0])
    def _():
        slot = blk_slot_ref[b]

        @pl.when(b == 0)
        def _():
            for cp in weight_copies(blk_exp_ref[0], slot):
                cp.start()

        @pl.when(blk_first_ref[b] == 1)
        def _():
            @pl.when(blk_next_ref[b] >= 0)
            def _():
                for cp in weight_copies(blk_next_ref[b], 1 - slot):
                    cp.start()

            for cp in weight_copies(blk_exp_ref[b], slot):
                cp.wait()

        rows, d = x_scr.shape
        for c in range(d // LANE):
            x_scr[:, c * LANE:(c + 1) * LANE] = xs_ref[pl.ds(c, rows, stride=SUBLANE), :]
        ri = lax.broadcasted_iota(jnp.int32, (rows, 1), 0)
        x = jnp.where(ri < blk_rows_ref[b], x_scr[...], 0.0)
        hg = _dot(x, wg_buf[slot])
        hu = _dot(x, wu_buf[slot])
        y_scr[...] = _dot(hg * jax.nn.sigmoid(hg) * hu, wd_buf[slot])
        for c in range(d // LANE):
            ys_ref[pl.ds(c, rows, stride=SUBLANE), :] = y_scr[:, c * LANE:(c + 1) * LANE]


def _experts(blk_exp, blk_rows, n_used, blk_first, blk_slot, blk_next, xs_tiled, w_g, w_u, w_d):
    n_rows = xs_tiled.shape[0] // SUBLANE
    rows = EXPERT_ROWS
    nblk = n_rows // rows
    _, d, ff = w_g.shape
    used = lambda b, be, br, nu, *_: (jnp.minimum(b, nu[0] - 1), 0)
    grid_spec = pltpu.PrefetchScalarGridSpec(
        num_scalar_prefetch=6,
        grid=(nblk,),
        in_specs=[pl.BlockSpec((rows * SUBLANE, LANE), used),
                  pl.BlockSpec(memory_space=pl.ANY), pl.BlockSpec(memory_space=pl.ANY),
                  pl.BlockSpec(memory_space=pl.ANY)],
        out_specs=pl.BlockSpec((rows * SUBLANE, LANE), used),
        scratch_shapes=[pltpu.VMEM((rows, d), F32), pltpu.VMEM((rows, d), F32),
                        pltpu.VMEM((2, d, ff), F32), pltpu.VMEM((2, d, ff), F32), pltpu.VMEM((2, ff, d), F32),
                        pltpu.SemaphoreType.DMA((3, 2))],
    )
    return pl.pallas_call(
        _experts_kernel,
        grid_spec=grid_spec,
        out_shape=jax.ShapeDtypeStruct((n_rows * SUBLANE, LANE), F32),
        compiler_params=_params("arbitrary"),
        name="experts",
    )(blk_exp, blk_rows, n_used, blk_first, blk_slot, blk_next, xs_tiled, w_g, w_u, w_d)


def _combine_kernel(alpha, dcur_ref, dnext_ref, meta_ref, x_ref, g_ref, b_ref, ys_ref, o_ref, buf_ref, sems):
    i = pl.program_id(0)
    tq, d = x_ref.shape
    slot_rows = 2 * tq * SUBLANE

    def issue_all(d_ref, slot):
        def issue(t, c):
            _row_copy(ys_ref, buf_ref, d_ref[0, 0, t], slot * (2 * tq) + t, sems.at[slot]).start(priority=0)
            _row_copy(ys_ref, buf_ref, d_ref[0, 0, tq + t], slot * (2 * tq) + tq + t, sems.at[slot]).start(priority=1)
            return c

        lax.fori_loop(0, tq, issue, 0, unroll=4)

    slot = i % 2

    @pl.when(i == 0)
    def _():
        issue_all(dcur_ref, 0)

    @pl.when(i + 1 < pl.num_programs(0))
    def _():
        issue_all(dnext_ref, 1 - slot)

    off = pl.multiple_of(slot * slot_rows, slot_rows)
    pltpu.make_async_copy(ys_ref.at[pl.ds(0, slot_rows), :], buf_ref.at[pl.ds(off, slot_rows), :],
                          sems.at[slot]).wait()

    gate1 = meta_ref[:, 4:5]
    gate2 = meta_ref[:, 5:6]
    for c in range(d // LANE):
        sl = slice(c * LANE, (c + 1) * LANE)
        y1 = buf_ref[pl.ds(off + c, tq, stride=SUBLANE), :]
        y2 = buf_ref[pl.ds(off + tq * SUBLANE + c, tq, stride=SUBLANE), :]
        o_ref[:, sl] = alpha * x_ref[:, sl] + (y1 * gate1 + y2 * gate2)
    o_ref[...] = _layer_norm(o_ref[...], g_ref[...], b_ref[...])


def _combine(dest3, meta, x2, g, bb, ys_tiled, alpha):
    t, d = x2.shape
    n = dest3.shape[0]
    tq = t // n
    row = lambda i: (i, 0)
    const = lambda i: (0, 0)
    return pl.pallas_call(
        functools.partial(_combine_kernel, alpha),
        grid=(n,),
        in_specs=[pl.BlockSpec((1, 1, 2 * tq), lambda i: (i, 0, 0), memory_space=pltpu.SMEM),
                  pl.BlockSpec((1, 1, 2 * tq), lambda i: (jnp.minimum(i + 1, n - 1), 0, 0), memory_space=pltpu.SMEM),
                  pl.BlockSpec((tq, LANE), row),
                  pl.BlockSpec((tq, d), row),
                  pl.BlockSpec((1, d), const), pl.BlockSpec((1, d), const),
                  pl.BlockSpec(memory_space=pl.ANY)],
        out_specs=pl.BlockSpec((tq, d), row),
        out_shape=jax.ShapeDtypeStruct((t, d), F32),
        scratch_shapes=[pltpu.VMEM((2 * 2 * tq * SUBLANE, LANE), F32), pltpu.SemaphoreType.DMA((2,))],
        compiler_params=_params("arbitrary"),
        name="combine",
    )(dest3, dest3, meta, x2, g, bb, ys_tiled)


def _moe(x2, x2_tiled, logits, w_g, w_u, w_d, g, bb, alpha):
    t, d = x2.shape
    meta, cnt = _route(logits)
    counts = cnt[0, N_GROUPS:N_GROUPS + N_EXPERTS].astype(jnp.int32)
    padded = ((counts + EXPERT_ROWS - 1) // EXPERT_ROWS) * EXPERT_ROWS
    pend = jnp.cumsum(padded)
    poff = pend - padded
    first_row = jnp.pad(poff.astype(F32), (N_GROUPS, LANE - N_GROUPS - N_EXPERTS))
    dest3 = _plan(meta, jnp.broadcast_to(first_row[None], (SUBLANE, LANE)), ROUTE_TILE)
    n_rows = t * 2 + N_EXPERTS * EXPERT_ROWS
    nblk = n_rows // EXPERT_ROWS
    blk_start = jnp.arange(nblk, dtype=jnp.int32) * EXPERT_ROWS
    blk_exp = jnp.sum((pend[None, :] <= blk_start[:, None]).astype(jnp.int32), axis=1)
    blk_exp = jnp.minimum(blk_exp, N_EXPERTS - 1)
    blk_rows = jnp.clip(poff[blk_exp] + counts[blk_exp] - blk_start, 0, EXPERT_ROWS).astype(jnp.int32)
    n_used = (pend[-1:] // EXPERT_ROWS).astype(jnp.int32)
    e_idx = jnp.arange(N_EXPERTS, dtype=jnp.int32)
    has_rows = counts > 0
    ordinal = jnp.cumsum(has_rows.astype(jnp.int32)) - 1
    later = has_rows[None, :] & (e_idx[None, :] > e_idx[:, None])
    next_used = jnp.min(jnp.where(later, e_idx[None, :], N_EXPERTS), axis=1)
    next_used = jnp.where(next_used == N_EXPERTS, -1, next_used).astype(jnp.int32)
    blk_first = (blk_start == poff[blk_exp]).astype(jnp.int32)
    blk_slot = (ordinal[blk_exp] % 2).astype(jnp.int32)
    blk_next = next_used[blk_exp]
    xs_tiled = _dispatch(dest3, x2_tiled, n_rows)
    ys_tiled = _experts(blk_exp, blk_rows, n_used, blk_first, blk_slot, blk_next, xs_tiled, w_g, w_u, w_d)
    return _combine(dest3, meta, x2, g, bb, ys_tiled, alpha)


def kernel(x, mem, w_in, w_out, ln_mix_g, ln_mix_b, w_xq, w_xkv, w_xo, ln_x_g, ln_x_b, w_route_group,
           w_route_expert, w_exp_gate, w_exp_up, w_exp_down, ln_moe_g, ln_moe_b):
    b, s, d = x.shape
    depth = w_in.shape[0]
    alpha = (2.0 * depth) ** 0.25
    t = b * s
    ret_cols = (2 * RET_HEADS * RET_QK_DIM + 2 * RET_HEADS * RET_V_DIM)
    assert ret_cols % (DIL_HEADS * DIL_HEAD_DIM) == 0
    dil_col0 = ret_cols // (DIL_HEADS * DIL_HEAD_DIM)
    xc = x.reshape(t, d)
    for l in range(depth):
        h = _proj_in(xc, w_in[l].astype(BF16))
        h3 = h.reshape(b, s, h.shape[1])
        y_ret = _retention(h3).reshape(t, -1)
        y_dil = _dilated(h3, dil_col0).reshape(t, -1)
        x1 = _mix_out(xc, y_ret, y_dil, w_out[l].astype(BF16), ln_mix_g[l][None], ln_mix_b[l][None], alpha)
        kv = _mem_kv(mem.reshape(b * mem.shape[1], d), w_xkv[l].astype(BF16)).reshape(b, mem.shape[1], 2 * d)
        w_r = jnp.concatenate([w_route_group[l], w_route_expert[l]], axis=-1)
        w_r = jnp.pad(w_r, ((0, 0), (0, LANE - w_r.shape[1]))).astype(BF16)
        x2, x2_tiled, logits = _xattn(x1, kv, w_xq[l].astype(BF16), w_xo[l].astype(BF16),
                                      ln_x_g[l][None], ln_x_b[l][None], w_r, alpha, s)
        xc = _moe(x2, x2_tiled, logits, w_exp_gate[l], w_exp_up[l], w_exp_down[l],
                  ln_moe_g[l][None], ln_moe_b[l][None], alpha)
    return xc.reshape(b, s, d)
```

```python
import functools

import jax
import jax.numpy as jnp
from jax import lax
from jax.experimental import pallas as pl
from jax.experimental.pallas import tpu as pltpu

BF16 = jnp.bfloat16
F32 = jnp.float32

LANE = 128
SUBLANE = 8
VMEM_LIMIT = 56 * 1024 * 1024

RET_HEADS = 4
RET_QK_DIM = 64
RET_V_DIM = 128
RET_CHUNK = 128
ROPE_BASE = 10000.0
DIL_HEADS = 8
DIL_HEAD_DIM = 64
DIL_DILATIONS = (1, 4, 16)
DIL_BLOCK = 128
DIL_SUPER = DIL_BLOCK * max(DIL_DILATIONS)
DIL_GROUP = 4
XATTN_HEADS = 4
N_GROUPS = 4
EXPERTS_PER_GROUP = 8
N_EXPERTS = N_GROUPS * EXPERTS_PER_GROUP
ROUTE_ROWS = -(-(N_GROUPS + N_EXPERTS) // SUBLANE) * SUBLANE
EXPERT_ROWS = 256
ROUTE_TILE = 256
LN_EPS = 1e-5
GN_EPS = 1e-6
NEG = -1e30


def _params(*sem):
    return pltpu.CompilerParams(dimension_semantics=sem, vmem_limit_bytes=VMEM_LIMIT)


def _layer_norm(z, g, b):
    mu = jnp.mean(z, axis=-1, keepdims=True)
    zc = z - mu
    var = jnp.mean(zc * zc, axis=-1, keepdims=True)
    return zc * lax.rsqrt(var + LN_EPS) * g + b


def _dot(a, b):
    return jnp.dot(a.astype(BF16), b.astype(BF16), preferred_element_type=F32)


def _dot_nt(a, b):
    return lax.dot_general(a.astype(BF16), b.astype(BF16), (((1,), (1,)), ((), ())),
                           preferred_element_type=F32)


def _dot_tn(a, b):
    return lax.dot_general(a.astype(BF16), b.astype(BF16), (((0,), (0,)), ((), ())),
                           preferred_element_type=F32)


def _proj_in_kernel(x_ref, w_ref, o_ref):
    o_ref[...] = _dot(x_ref[...], w_ref[...])


def _proj_in(x2d, w_bf16, tm=512):
    t, d = x2d.shape
    n = w_bf16.shape[1]
    return pl.pallas_call(
        _proj_in_kernel,
        grid=(t // tm,),
        in_specs=[pl.BlockSpec((tm, d), lambda i: (i, 0)),
                  pl.BlockSpec((d, n), lambda i: (0, 0))],
        out_specs=pl.BlockSpec((tm, n), lambda i: (i, 0)),
        out_shape=jax.ShapeDtypeStruct((t, n), F32),
        compiler_params=_params("parallel"),
        name="proj_in",
    )(x2d, w_bf16)


def _retention_kernel(qk_ref, v_ref, g_ref, cos_ref, sin_ref, decay_ref, zeta_ref, xi_ref, gam_ref,
                      o_ref, state_ref):
    n = pl.program_id(1)

    @pl.when(n == 0)
    def _():
        state_ref[...] = jnp.zeros_like(state_ref)

    c = qk_ref.shape[1]
    lane = lax.broadcasted_iota(jnp.int32, (c, LANE), 1)
    first_half = (lane % RET_QK_DIM) < (RET_QK_DIM // 2)
    n_pairs = RET_HEADS * RET_QK_DIM // LANE

    def rot(col):
        t = qk_ref[0, :, col * LANE:(col + 1) * LANE]
        swapped = jnp.where(first_half,
                            pltpu.roll(t, LANE - RET_QK_DIM // 2, axis=1),
                            pltpu.roll(t, RET_QK_DIM // 2, axis=1))
        return t * cos_ref[...] + swapped * sin_ref[...]

    q_pairs = [rot(p) for p in range(n_pairs)]
    k_pairs = [rot(n_pairs + p) * (RET_QK_DIM ** -0.5) for p in range(n_pairs)]

    for h in range(RET_HEADS):
        p = (h * RET_QK_DIM) // LANE
        lo = (h * RET_QK_DIM) % LANE
        mine = (lane >= lo) & (lane < lo + RET_QK_DIM)
        qm = jnp.where(mine, q_pairs[p], 0.0)
        km = jnp.where(mine, k_pairs[p], 0.0)
        v = v_ref[0, :, h * RET_V_DIM:(h + 1) * RET_V_DIM]
        s = _dot_nt(qm, k_pairs[p]) * decay_ref[h]
        y = _dot(s, v)
        st = state_ref[h]
        y = y + _dot(qm, st) * xi_ref[h]
        kv = _dot_tn(km * zeta_ref[h], v)
        state_ref[h] = gam_ref[h] * st + kv
        mu = jnp.mean(y, axis=-1, keepdims=True)
        yc = y - mu
        var = jnp.mean(yc * yc, axis=-1, keepdims=True)
        yn = yc * lax.rsqrt(var + GN_EPS)
        gate = g_ref[0, :, h * RET_V_DIM:(h + 1) * RET_V_DIM]
        o_ref[0, :, h * RET_V_DIM:(h + 1) * RET_V_DIM] = gate * jax.nn.sigmoid(gate) * yn


def _retention_tables(s):
    half = RET_QK_DIM // 2
    inv = 1.0 / (ROPE_BASE ** (jnp.arange(half, dtype=F32) / half))
    ang = jnp.arange(s, dtype=F32)[:, None] * inv[None, :]
    cos = jnp.cos(ang)
    sin = jnp.sin(ang)
    cos_t = jnp.tile(jnp.concatenate([cos, cos], axis=-1), (1, LANE // RET_QK_DIM))
    sin_t = jnp.tile(jnp.concatenate([-sin, sin], axis=-1), (1, LANE // RET_QK_DIM))
    c = RET_CHUNK
    lg = jnp.log(1.0 - jnp.exp2(-5.0 - jnp.arange(RET_HEADS, dtype=F32)))
    idx = jnp.arange(c, dtype=F32)
    diff = idx[:, None] - idx[None, :]
    decay = jnp.where(diff >= 0, jnp.exp(lg[:, None, None] * jnp.maximum(diff, 0.0)), 0.0)
    zeta = jnp.exp(lg[:, None] * (c - 1.0 - idx))[:, :, None]
    xi = jnp.exp(lg[:, None] * (idx + 1.0))[:, :, None]
    gam = jnp.broadcast_to(jnp.exp(lg * c)[:, None, None], (RET_HEADS, 1, LANE))
    return cos_t, sin_t, decay, zeta, xi, gam


def _retention(h3):
    b, s, _ = h3.shape
    c = RET_CHUNK
    qk_w = 2 * RET_HEADS * RET_QK_DIM
    v_w = RET_HEADS * RET_V_DIM
    assert qk_w == v_w
    cos_t, sin_t, decay, zeta, xi, gam = _retention_tables(s)
    const3 = lambda bi, n: (0, 0, 0)
    return pl.pallas_call(
        _retention_kernel,
        grid=(b, s // c),
        in_specs=[pl.BlockSpec((1, c, qk_w), lambda bi, n: (bi, n, 0)),
                  pl.BlockSpec((1, c, v_w), lambda bi, n: (bi, n, 1)),
                  pl.BlockSpec((1, c, v_w), lambda bi, n: (bi, n, 2)),
                  pl.BlockSpec((c, LANE), lambda bi, n: (n, 0)),
                  pl.BlockSpec((c, LANE), lambda bi, n: (n, 0)),
                  pl.BlockSpec((RET_HEADS, c, c), const3),
                  pl.BlockSpec((RET_HEADS, c, 1), const3),
                  pl.BlockSpec((RET_HEADS, c, 1), const3),
                  pl.BlockSpec((RET_HEADS, 1, LANE), const3)],
        out_specs=pl.BlockSpec((1, c, v_w), lambda bi, n: (bi, n, 0)),
        out_shape=jax.ShapeDtypeStruct((b, s, v_w), F32),
        scratch_shapes=[pltpu.VMEM((RET_HEADS, LANE, RET_V_DIM), F32)],
        compiler_params=_params("parallel", "arbitrary"),
        name="retention",
    )(h3, h3, h3, cos_t, sin_t, decay, zeta, xi, gam)


def _dilated_kernel(q_ref, kp_ref, kc_ref, vp_ref, vc_ref, bias_ref, o_ref, kk_ref, vv_ref, acc_ref, m_ref, l_ref,
                    s_ref, p_ref):
    j = pl.program_id(2)
    sup = DIL_SUPER
    q_blk = DIL_BLOCK
    kk_ref[0:sup, :] = kp_ref[0]
    kk_ref[sup:2 * sup, :] = kc_ref[0]
    vv_ref[0:sup, :] = vp_ref[0]
    vv_ref[sup:2 * sup, :] = vc_ref[0]
    scale = DIL_HEAD_DIM ** -0.5
    n_blocks = sup // q_blk

    for bi, d in enumerate(DIL_DILATIONS):
        n_per_r = sup // (q_blk * d)

        def group(t0, carry, bi=bi, d=d, n_per_r=n_per_r):
            lane = lax.broadcasted_iota(jnp.int32, (q_blk, LANE), 1)
            head0 = lane < DIL_HEAD_DIM
            starts = []
            for g in range(DIL_GROUP):
                t = t0 + g * (n_blocks // DIL_GROUP)
                r = t // n_per_r
                n = t % n_per_r
                q_start = n * (q_blk * d) + r
                k_start = sup + (n - 1) * (q_blk * d) + r
                starts.append((q_start, k_start))
                q = q_ref[0, pl.ds(q_start, q_blk, stride=d), :] * scale
                kb = kk_ref[pl.ds(k_start, 2 * q_blk, stride=d), :]
                q2 = jnp.concatenate([jnp.where(head0, q, 0.0), jnp.where(head0, 0.0, q)], axis=0)
                first = jnp.where((j == 0) & (n == 0), 1, 0)
                s_ref[g] = _dot_nt(q2, kb) + bias_ref[first]
            for g in range(DIL_GROUP):
                s = s_ref[g]
                m2 = jnp.max(s, axis=-1, keepdims=True)
                p_ref[g] = jnp.exp(s - m2).astype(BF16)
                m_ref[bi, pl.ds(starts[g][0], q_blk, stride=d), :] = jnp.where(head0, m2[:q_blk], m2[q_blk:])
            for g in range(DIL_GROUP):
                q_start, k_start = starts[g]
                vb = vv_ref[pl.ds(k_start, 2 * q_blk, stride=d), :].astype(BF16)
                o2 = jnp.dot(p_ref[g], jnp.concatenate([vb, jnp.ones_like(vb)], axis=1),
                             preferred_element_type=F32)
                rows = pl.ds(q_start, q_blk, stride=d)
                acc_ref[bi, rows, :] = jnp.where(head0, o2[:q_blk, :LANE], o2[q_blk:, :LANE])
                l_ref[bi, rows, :] = jnp.where(head0, o2[:q_blk, LANE:], o2[q_blk:, LANE:])
            return carry

        lax.fori_loop(0, n_blocks // DIL_GROUP, group, 0)

    def merge(c, carry):
        rows = pl.ds(pl.multiple_of(c * q_blk, q_blk), q_blk)
        ms = [m_ref[bi, rows, :] for bi in range(len(DIL_DILATIONS))]
        m_all = functools.reduce(jnp.maximum, ms)
        ws = [jnp.exp(m - m_all) for m in ms]
        num = functools.reduce(lambda a, b: a + b, [w * acc_ref[bi, rows, :] for bi, w in enumerate(ws)])
        den = functools.reduce(lambda a, b: a + b, [w * l_ref[bi, rows, :] for bi, w in enumerate(ws)])
        o_ref[0, rows, :] = num / den
        return carry

    lax.fori_loop(0, n_blocks, merge, 0)


def _dilated_bias():
    q_blk = DIL_BLOCK
    qi = jnp.arange(2 * q_blk)[:, None] % q_blk
    kj = jnp.arange(2 * q_blk)[None, :]
    band = (kj >= qi) & (kj <= qi + q_blk)
    return jnp.stack([jnp.where(band, 0.0, NEG), jnp.where(band & (kj >= q_blk), 0.0, NEG)]).astype(F32)


def _dilated(h3, col0):
    b, s, _ = h3.shape
    sup = DIL_SUPER
    n_pairs = DIL_HEADS * DIL_HEAD_DIM // LANE
    n_br = len(DIL_DILATIONS)
    cq, ck, cv = col0 * n_pairs, (col0 + 1) * n_pairs, (col0 + 2) * n_pairs
    cur = lambda c: (lambda bi, p, j: (bi, j, c + p))
    prev = lambda c: (lambda bi, p, j: (bi, jnp.maximum(j - 1, 0), c + p))
    blk = (1, sup, LANE)
    return pl.pallas_call(
        _dilated_kernel,
        grid=(b, n_pairs, s // sup),
        in_specs=[pl.BlockSpec(blk, cur(cq)),
                  pl.BlockSpec(blk, prev(ck)), pl.BlockSpec(blk, cur(ck)),
                  pl.BlockSpec(blk, prev(cv)), pl.BlockSpec(blk, cur(cv)),
                  pl.BlockSpec((2, 2 * DIL_BLOCK, 2 * DIL_BLOCK), lambda bi, p, j: (0, 0, 0))],
        out_specs=pl.BlockSpec(blk, lambda bi, p, j: (bi, j, p)),
        out_shape=jax.ShapeDtypeStruct((b, s, n_pairs * LANE), F32),
        scratch_shapes=[pltpu.VMEM((2 * sup, LANE), F32), pltpu.VMEM((2 * sup, LANE), F32),
                        pltpu.VMEM((n_br, sup, LANE), F32), pltpu.VMEM((n_br, sup, LANE), F32),
                        pltpu.VMEM((n_br, sup, LANE), F32),
                        pltpu.VMEM((DIL_GROUP, 2 * DIL_BLOCK, 2 * DIL_BLOCK), F32),
                        pltpu.VMEM((DIL_GROUP, 2 * DIL_BLOCK, 2 * DIL_BLOCK), BF16)],
        compiler_params=_params("parallel", "parallel", "arbitrary"),
        name="dilated",
    )(h3, h3, h3, h3, h3, _dilated_bias())


def _mix_out_kernel(alpha, x_ref, yr_ref, yd_ref, w_ref, g_ref, b_ref, o_ref):
    wr = yr_ref.shape[1]
    y = _dot(yr_ref[...], w_ref[0:wr, :]) + _dot(yd_ref[...], w_ref[wr:, :])
    o_ref[...] = _layer_norm(alpha * x_ref[...] + y, g_ref[...], b_ref[...])


def _mix_out(x2d, y_ret, y_dil, w_bf16, g, bb, alpha, tm=512):
    t, d = x2d.shape
    row = lambda i: (i, 0)
    const = lambda i: (0, 0)
    return pl.pallas_call(
        functools.partial(_mix_out_kernel, alpha),
        grid=(t // tm,),
        in_specs=[pl.BlockSpec((tm, d), row),
                  pl.BlockSpec((tm, y_ret.shape[1]), row),
                  pl.BlockSpec((tm, y_dil.shape[1]), row),
                  pl.BlockSpec(w_bf16.shape, const),
                  pl.BlockSpec((1, d), const), pl.BlockSpec((1, d), const)],
        out_specs=pl.BlockSpec((tm, d), row),
        out_shape=jax.ShapeDtypeStruct((t, d), F32),
        compiler_params=_params("parallel"),
        name="mix_out",
    )(x2d, y_ret, y_dil, w_bf16, g, bb)


def _mem_kv(mem2d, w_bf16, tn=512):
    m, d = mem2d.shape
    n = w_bf16.shape[1]
    return pl.pallas_call(
        _proj_in_kernel,
        grid=(n // tn,),
        in_specs=[pl.BlockSpec((m, d), lambda i: (0, 0)),
                  pl.BlockSpec((d, tn), lambda i: (0, i))],
        out_specs=pl.BlockSpec((m, tn), lambda i: (0, i)),
        out_shape=jax.ShapeDtypeStruct((m, n), F32),
        compiler_params=_params("parallel"),
        name="mem_kv",
    )(mem2d, w_bf16)


def _xattn_kernel(alpha, x_ref, k_ref, v_ref, wq_ref, wo_ref, g_ref, b_ref, wr_ref,
                  o_ref, ot_ref, lg_ref, att_ref):
    tm, d = x_ref.shape
    dh = d // XATTN_HEADS
    x = x_ref[...]
    q = _dot(x, wq_ref[...])
    scale = dh ** -0.5
    for h in range(XATTN_HEADS):
        sl = slice(h * dh, (h + 1) * dh)
        s = _dot_nt(q[:, sl], k_ref[0, :, sl]) * scale
        m = jnp.max(s, axis=-1, keepdims=True)
        e = jnp.exp(s - m)
        p = e / jnp.sum(e, axis=-1, keepdims=True)
        att_ref[:, sl] = _dot(p, v_ref[0, :, sl])
    y = _dot(att_ref[...], wo_ref[...])
    x2 = _layer_norm(alpha * x + y, g_ref[...], b_ref[...])
    o_ref[...] = x2
    for c in range(d // LANE):
        ot_ref[pl.ds(c, tm, stride=SUBLANE), :] = x2[:, c * LANE:(c + 1) * LANE]
    lg_ref[...] = _dot_nt(wr_ref[...], x2)


def _xattn(x1, kv, wq, wo, g, bb, w_r, alpha, seq, tm=256):
    t, d = x1.shape
    mlen = kv.shape[1]
    tiles_per_seq = seq // tm
    row = lambda i: (i, 0)
    const = lambda i: (0, 0)
    return pl.pallas_call(
        functools.partial(_xattn_kernel, alpha),
        grid=(t // tm,),
        in_specs=[pl.BlockSpec((tm, d), row),
                  pl.BlockSpec((1, mlen, d), lambda i: (i // tiles_per_seq, 0, 0)),
                  pl.BlockSpec((1, mlen, d), lambda i: (i // tiles_per_seq, 0, 1)),
                  pl.BlockSpec((d, d), const), pl.BlockSpec((d, d), const),
                  pl.BlockSpec((1, d), const), pl.BlockSpec((1, d), const),
                  pl.BlockSpec((ROUTE_ROWS, d), const)],
        out_specs=[pl.BlockSpec((tm, d), row),
                   pl.BlockSpec((tm * SUBLANE, LANE), row),
                   pl.BlockSpec((ROUTE_ROWS, tm), lambda i: (0, i))],
        out_shape=[jax.ShapeDtypeStruct((t, d), F32),
                   jax.ShapeDtypeStruct((t * SUBLANE, LANE), F32),
                   jax.ShapeDtypeStruct((ROUTE_ROWS, t), F32)],
        scratch_shapes=[pltpu.VMEM((tm, d), F32)],
        compiler_params=_params("parallel"),
        name="xattn",
    )(x1, kv, kv, wq, wo, g, bb, w_r)


def _route_kernel(lg_ref, before_ref, meta_ref, cnt_ref, carry_ref):
    i = pl.program_id(0)

    @pl.when(i == 0)
    def _():
        carry_ref[...] = jnp.zeros_like(carry_ref)

    rows, tt = lg_ref.shape
    sub = before_ref.shape[0]
    r = lax.broadcasted_iota(jnp.int32, (rows, sub), 0)
    r8 = lax.broadcasted_iota(jnp.int32, (SUBLANE, sub), 0)
    is_group = r < N_GROUPS

    def col_max(a):
        return jnp.max(a, axis=0, keepdims=True)

    def first_row_where(mask):
        return jnp.min(jnp.where(mask, r, rows), axis=0, keepdims=True)

    for c in range(tt // sub):
        lg = lg_ref[:, c * sub:(c + 1) * sub]
        mg = col_max(jnp.where(is_group, lg, NEG))
        eg = jnp.where(is_group, jnp.exp(lg - mg), 0.0)
        pg = eg / jnp.sum(eg, axis=0, keepdims=True)
        g1 = col_max(pg)
        gi = first_row_where(is_group & (pg == g1))
        lo = N_GROUPS + gi * EXPERTS_PER_GROUP
        in_grp = (r >= lo) & (r < lo + EXPERTS_PER_GROUP)
        v1 = col_max(jnp.where(in_grp, lg, NEG))
        i1 = first_row_where(in_grp & (lg == v1))
        rest = in_grp & (r != i1)
        v2 = col_max(jnp.where(rest, lg, NEG))
        i2 = first_row_where(rest & (lg == v2))
        e2 = jnp.exp(v2 - v1)
        den = 1.0 + e2
        gate1 = g1 * (1.0 / den)
        gate2 = g1 * (e2 / den)
        sel1 = r == i1
        sel2 = r == i2
        onehot = jnp.where(sel1 | sel2, 1.0, 0.0)
        rank = _dot(onehot, before_ref[...]) + carry_ref[:, 0:1]
        r1 = jnp.sum(jnp.where(sel1, rank, 0.0), axis=0, keepdims=True)
        r2 = jnp.sum(jnp.where(sel2, rank, 0.0), axis=0, keepdims=True)
        carry_ref[...] = carry_ref[...] + jnp.sum(onehot, axis=1, keepdims=True)
        meta = jnp.where(r8 == 0, (i1 - N_GROUPS).astype(F32), 0.0)
        meta = jnp.where(r8 == 1, (i2 - N_GROUPS).astype(F32), meta)
        meta = jnp.where(r8 == 2, r1, meta)
        meta = jnp.where(r8 == 3, r2, meta)
        meta = jnp.where(r8 == 4, gate1, meta)
        meta = jnp.where(r8 == 5, gate2, meta)
        meta_ref[:, c * sub:(c + 1) * sub] = meta
    cnt_ref[...] = carry_ref[...]


def _route(logits_t, tt=1024, sub=256):
    rows, t = logits_t.shape
    before = (jnp.arange(sub)[:, None] < jnp.arange(sub)[None, :]).astype(BF16)
    return pl.pallas_call(
        _route_kernel,
        grid=(t // tt,),
        in_specs=[pl.BlockSpec((rows, tt), lambda i: (0, i)),
                  pl.BlockSpec((sub, sub), lambda i: (0, 0))],
        out_specs=[pl.BlockSpec((SUBLANE, tt), lambda i: (0, i)),
                   pl.BlockSpec((rows, LANE), lambda i: (0, 0))],
        out_shape=[jax.ShapeDtypeStruct((SUBLANE, t), F32),
                   jax.ShapeDtypeStruct((rows, LANE), F32)],
        scratch_shapes=[pltpu.VMEM((rows, LANE), F32)],
        compiler_params=_params("arbitrary"),
        name="route",
    )(logits_t, before)


def _plan_kernel(meta_ref, first_row_ref, dest_ref):
    n_tiles = dest_ref.shape[0]
    tt = dest_ref.shape[2] // 2
    rows = first_row_ref.shape[0]
    first_row = first_row_ref[:, 0:1]
    r = lax.broadcasted_iota(jnp.int32, (rows, tt), 0)
    for g in range(n_tiles):
        m = meta_ref[:, g * tt:(g + 1) * tt]

        def dest_of(k):
            e_row = m[k:k + 1, :].astype(jnp.int32) + N_GROUPS
            return jnp.sum(jnp.where(r == e_row, first_row, 0.0), axis=0, keepdims=True) + m[2 + k:3 + k, :]

        dest_ref[g] = jnp.concatenate([dest_of(0), dest_of(1)], axis=1).astype(jnp.int32)


def _plan(meta_t, first_row, tt, tiles_per_step=4):
    t = meta_t.shape[1]
    rows = first_row.shape[0]
    return pl.pallas_call(
        _plan_kernel,
        grid=(t // (tt * tiles_per_step),),
        in_specs=[pl.BlockSpec((SUBLANE, tt * tiles_per_step), lambda i: (0, i)),
                  pl.BlockSpec((rows, LANE), lambda i: (0, 0))],
        out_specs=pl.BlockSpec((tiles_per_step, 1, 2 * tt), lambda i: (i, 0, 0)),
        out_shape=jax.ShapeDtypeStruct((t // tt, 1, 2 * tt), jnp.int32),
        compiler_params=_params("parallel"),
        name="plan",
    )(meta_t, first_row)


def _row_copy(src, dst, s_row, d_row, sem):
    return pltpu.make_async_copy(src.at[pl.ds(pl.multiple_of(s_row * SUBLANE, SUBLANE), SUBLANE), :],
                                 dst.at[pl.ds(pl.multiple_of(d_row * SUBLANE, SUBLANE), SUBLANE), :], sem)


def _dispatch_kernel(dest_ref, xt_ref, xs_ref, sem):
    tq = dest_ref.shape[2] // 2

    def issue(t, c):
        _row_copy(xt_ref, xs_ref, t, dest_ref[0, 0, t], sem).start(priority=0)
        _row_copy(xt_ref, xs_ref, t, dest_ref[0, 0, tq + t], sem).start(priority=1)
        return c

    lax.fori_loop(0, tq, issue, 0, unroll=8)
    for _ in range(2):
        pltpu.make_async_copy(xt_ref, xs_ref.at[pl.ds(0, tq * SUBLANE), :], sem).wait()


def _dispatch(dest3, x_tiled, n_rows, tiles_per_step=4):
    n_tiles, _, two_tt = dest3.shape
    tt = two_tt // 2
    tq = tiles_per_step * tt
    n_steps = n_tiles // tiles_per_step
    dest3 = dest3.reshape(n_steps, tiles_per_step, 2, tt).transpose(0, 2, 1, 3).reshape(n_steps, 1, 2 * tq)
    return pl.pallas_call(
        _dispatch_kernel,
        grid=(n_steps,),
        in_specs=[pl.BlockSpec((1, 1, 2 * tq), lambda i: (i, 0, 0), memory_space=pltpu.SMEM),
                  pl.BlockSpec((tq * SUBLANE, LANE), lambda i: (i, 0))],
        out_specs=pl.BlockSpec(memory_space=pl.ANY),
        out_shape=jax.ShapeDtypeStruct((n_rows * SUBLANE, LANE), F32),
        scratch_shapes=[pltpu.SemaphoreType.DMA(())],
        compiler_params=_params("arbitrary"),
        name="dispatch",
    )(dest3, x_tiled)


def _experts_kernel(blk_exp_ref, blk_rows_ref, n_used_ref, blk_first_ref, blk_slot_ref, blk_next_ref,
                    xs_ref, wg_hbm, wu_hbm, wd_hbm, ys_ref, x_scr, y_scr, wg_buf, wu_buf, wd_buf, sems):
    b = pl.program_id(0)

    def weight_copies(e, slot):
        return [pltpu.make_async_copy(hbm.at[e], buf.at[slot], sems.at[k, slot])
                for k, (hbm, buf) in enumerate(((wg_hbm, wg_buf), (wu_hbm, wu_buf), (wd_hbm, wd_buf)))]

    @pl.when(b < n_used_ref[0])
    def _():
        slot = blk_slot_ref[b]

        @pl.when(b == 0)
        def _():
            for cp in weight_copies(blk_exp_ref[0], slot):
                cp.start()

        @pl.when(blk_first_ref[b] == 1)
        def _():
            @pl.when(blk_next_ref[b] >= 0)
            def _():
                for cp in weight_copies(blk_next_ref[b], 1 - slot):
                    cp.start()

            for cp in weight_copies(blk_exp_ref[b], slot):
                cp.wait()

        rows, d = x_scr.shape
        for c in range(d // LANE):
            x_scr[:, c * LANE:(c + 1) * LANE] = xs_ref[pl.ds(c, rows, stride=SUBLANE), :]
        ri = lax.broadcasted_iota(jnp.int32, (rows, 1), 0)
        x = jnp.where(ri < blk_rows_ref[b], x_scr[...], 0.0)
        hg = _dot(x, wg_buf[slot])
        hu = _dot(x, wu_buf[slot])
        y_scr[...] = _dot(hg * jax.nn.sigmoid(hg) * hu, wd_buf[slot])
        for c in range(d // LANE):
            ys_ref[pl.ds(c, rows, stride=SUBLANE), :] = y_scr[:, c * LANE:(c + 1) * LANE]


def _experts(blk_exp, blk_rows, n_used, blk_first, blk_slot, blk_next, xs_tiled, w_g, w_u, w_d):
    n_rows = xs_tiled.shape[0] // SUBLANE
    rows = EXPERT_ROWS
    nblk = n_rows // rows
    _, d, ff = w_g.shape
    used = lambda b, be, br, nu, *_: (jnp.minimum(b, nu[0] - 1), 0)
    grid_spec = pltpu.PrefetchScalarGridSpec(
        num_scalar_prefetch=6,
        grid=(nblk,),
        in_specs=[pl.BlockSpec((rows * SUBLANE, LANE), used),
                  pl.BlockSpec(memory_space=pl.ANY), pl.BlockSpec(memory_space=pl.ANY),
                  pl.BlockSpec(memory_space=pl.ANY)],
        out_specs=pl.BlockSpec((rows * SUBLANE, LANE), used),
        scratch_shapes=[pltpu.VMEM((rows, d), F32), pltpu.VMEM((rows, d), F32),
                        pltpu.VMEM((2, d, ff), F32), pltpu.VMEM((2, d, ff), F32), pltpu.VMEM((2, ff, d), F32),
                        pltpu.SemaphoreType.DMA((3, 2))],
    )
    return pl.pallas_call(
        _experts_kernel,
        grid_spec=grid_spec,
        out_shape=jax.ShapeDtypeStruct((n_rows * SUBLANE, LANE), F32),
        compiler_params=_params("arbitrary"),
        name="experts",
    )(blk_exp, blk_rows, n_used, blk_first, blk_slot, blk_next, xs_tiled, w_g, w_u, w_d)


def _combine_kernel(alpha, dcur_ref, dnext_ref, meta_ref, x_ref, g_ref, b_ref, ys_ref, o_ref, buf_ref, sems):
    i = pl.program_id(0)
    tq, d = x_ref.shape
    slot_rows = 2 * tq * SUBLANE

    def issue_all(d_ref, slot):
        def issue(t, c):
            _row_copy(ys_ref, buf_ref, d_ref[0, 0, t], slot * (2 * tq) + t, sems.at[slot]).start(priority=0)
            _row_copy(ys_ref, buf_ref, d_ref[0, 0, tq + t], slot * (2 * tq) + tq + t, sems.at[slot]).start(priority=1)
            return c

        lax.fori_loop(0, tq, issue, 0, unroll=4)

    slot = i % 2

    @pl.when(i == 0)
    def _():
        issue_all(dcur_ref, 0)

    @pl.when(i + 1 < pl.num_programs(0))
    def _():
        issue_all(dnext_ref, 1 - slot)

    off = pl.multiple_of(slot * slot_rows, slot_rows)
    pltpu.make_async_copy(ys_ref.at[pl.ds(0, slot_rows), :], buf_ref.at[pl.ds(off, slot_rows), :],
                          sems.at[slot]).wait()

    meta_rows = jnp.transpose(jnp.concatenate([meta_ref[...], jnp.zeros((LANE - SUBLANE, tq), F32)], axis=0))
    gate1 = meta_rows[:, 4:5]
    gate2 = meta_rows[:, 5:6]
    for c in range(d // LANE):
        sl = slice(c * LANE, (c + 1) * LANE)
        y1 = buf_ref[pl.ds(off + c, tq, stride=SUBLANE), :]
        y2 = buf_ref[pl.ds(off + tq * SUBLANE + c, tq, stride=SUBLANE), :]
        o_ref[:, sl] = alpha * x_ref[:, sl] + (y1 * gate1 + y2 * gate2)
    o_ref[...] = _layer_norm(o_ref[...], g_ref[...], b_ref[...])


def _combine(dest3, meta, x2, g, bb, ys_tiled, alpha):
    t, d = x2.shape
    n = dest3.shape[0]
    tq = t // n
    row = lambda i: (i, 0)
    const = lambda i: (0, 0)
    return pl.pallas_call(
        functools.partial(_combine_kernel, alpha),
        grid=(n,),
        in_specs=[pl.BlockSpec((1, 1, 2 * tq), lambda i: (i, 0, 0), memory_space=pltpu.SMEM),
                  pl.BlockSpec((1, 1, 2 * tq), lambda i: (jnp.minimum(i + 1, n - 1), 0, 0), memory_space=pltpu.SMEM),
                  pl.BlockSpec((SUBLANE, tq), lambda i: (0, i)),
                  pl.BlockSpec((tq, d), row),
                  pl.BlockSpec((1, d), const), pl.BlockSpec((1, d), const),
                  pl.BlockSpec(memory_space=pl.ANY)],
        out_specs=pl.BlockSpec((tq, d), row),
        out_shape=jax.ShapeDtypeStruct((t, d), F32),
        scratch_shapes=[pltpu.VMEM((2 * 2 * tq * SUBLANE, LANE), F32), pltpu.SemaphoreType.DMA((2,))],
        compiler_params=_params("arbitrary"),
        name="combine",
    )(dest3, dest3, meta, x2, g, bb, ys_tiled)


def _moe(x2, x2_tiled, logits, w_g, w_u, w_d, g, bb, alpha):
    t, d = x2.shape
    meta, cnt = _route(logits)
    counts = cnt[N_GROUPS:N_GROUPS + N_EXPERTS, 0].astype(jnp.int32)
    padded = ((counts + EXPERT_ROWS - 1) // EXPERT_ROWS) * EXPERT_ROWS
    pend = jnp.cumsum(padded)
    poff = pend - padded
    first_row = jnp.pad(poff.astype(F32), (N_GROUPS, ROUTE_ROWS - N_GROUPS - N_EXPERTS))
    dest3 = _plan(meta, jnp.broadcast_to(first_row[:, None], (ROUTE_ROWS, LANE)), ROUTE_TILE)
    n_rows = t * 2 + N_EXPERTS * EXPERT_ROWS
    nblk = n_rows // EXPERT_ROWS
    blk_start = jnp.arange(nblk, dtype=jnp.int32) * EXPERT_ROWS
    blk_exp = jnp.sum((pend[None, :] <= blk_start[:, None]).astype(jnp.int32), axis=1)
    blk_exp = jnp.minimum(blk_exp, N_EXPERTS - 1)
    e_idx = jnp.arange(N_EXPERTS, dtype=jnp.int32)
    owner = blk_exp[:, None] == e_idx[None, :]

    def of_block(per_expert):
        return jnp.sum(jnp.where(owner, per_expert[None, :], 0), axis=1).astype(jnp.int32)

    blk_rows = jnp.clip(of_block(poff + counts) - blk_start, 0, EXPERT_ROWS).astype(jnp.int32)
    n_used = (pend[-1:] // EXPERT_ROWS).astype(jnp.int32)
    has_rows = counts > 0
    ordinal = jnp.cumsum(has_rows.astype(jnp.int32)) - 1
    later = has_rows[None, :] & (e_idx[None, :] > e_idx[:, None])
    next_used = jnp.min(jnp.where(later, e_idx[None, :], N_EXPERTS), axis=1)
    next_used = jnp.where(next_used == N_EXPERTS, -1, next_used).astype(jnp.int32)
    blk_first = (blk_start == of_block(poff)).astype(jnp.int32)
    blk_slot = of_block(ordinal % 2)
    blk_next = of_block(next_used)
    xs_tiled = _dispatch(dest3, x2_tiled, n_rows)
    ys_tiled = _experts(blk_exp, blk_rows, n_used, blk_first, blk_slot, blk_next, xs_tiled, w_g, w_u, w_d)
    return _combine(dest3, meta, x2, g, bb, ys_tiled, alpha)


def kernel(x, mem, w_in, w_out, ln_mix_g, ln_mix_b, w_xq, w_xkv, w_xo, ln_x_g, ln_x_b, w_route_group,
           w_route_expert, w_exp_gate, w_exp_up, w_exp_down, ln_moe_g, ln_moe_b):
    b, s, d = x.shape
    depth = w_in.shape[0]
    alpha = (2.0 * depth) ** 0.25
    t = b * s
    ret_cols = (2 * RET_HEADS * RET_QK_DIM + 2 * RET_HEADS * RET_V_DIM)
    assert ret_cols % (DIL_HEADS * DIL_HEAD_DIM) == 0
    dil_col0 = ret_cols // (DIL_HEADS * DIL_HEAD_DIM)
    xc = x.reshape(t, d)
    for l in range(depth):
        h = _proj_in(xc, w_in[l].astype(BF16))
        h3 = h.reshape(b, s, h.shape[1])
        y_ret = _retention(h3).reshape(t, -1)
        y_dil = _dilated(h3, dil_col0).reshape(t, -1)
        x1 = _mix_out(xc, y_ret, y_dil, w_out[l].astype(BF16), ln_mix_g[l][None], ln_mix_b[l][None], alpha)
        kv = _mem_kv(mem.reshape(b * mem.shape[1], d), w_xkv[l].astype(BF16)).reshape(b, mem.shape[1], 2 * d)
        w_r = jnp.concatenate([w_route_group[l], w_route_expert[l]], axis=-1)
        w_r = jnp.pad(w_r.T, ((0, ROUTE_ROWS - w_r.shape[1]), (0, 0))).astype(BF16)
        x2, x2_tiled, logits = _xattn(x1, kv, w_xq[l].astype(BF16), w_xo[l].astype(BF16),
                                      ln_x_g[l][None], ln_x_b[l][None], w_r, alpha, s)
        xc = _moe(x2, x2_tiled, logits, w_exp_gate[l], w_exp_up[l], w_exp_down[l],
                  ln_moe_g[l][None], ln_moe_b[l][None], alpha)
    return xc.reshape(b, s, d)
```

```python
import functools

import jax
import jax.numpy as jnp
from jax import lax
from jax.experimental import pallas as pl
from jax.experimental.pallas import tpu as pltpu

BF16 = jnp.bfloat16
F32 = jnp.float32

LANE = 128
SUBLANE = 8
VMEM_LIMIT = 56 * 1024 * 1024

RET_HEADS = 4
RET_QK_DIM = 64
RET_V_DIM = 128
RET_CHUNK = 128
ROPE_BASE = 10000.0
DIL_HEADS = 8
DIL_HEAD_DIM = 64
DIL_DILATIONS = (1, 4, 16)
DIL_BLOCK = 128
DIL_SUPER = DIL_BLOCK * max(DIL_DILATIONS)
DIL_GROUP = 1
XATTN_HEADS = 4
N_GROUPS = 4
EXPERTS_PER_GROUP = 8
N_EXPERTS = N_GROUPS * EXPERTS_PER_GROUP
ROUTE_ROWS = -(-(N_GROUPS + N_EXPERTS) // SUBLANE) * SUBLANE
EXPERT_ROWS = 256
ROUTE_TILE = 256
LN_EPS = 1e-5
GN_EPS = 1e-6
NEG = -1e30


def _params(*sem):
    return pltpu.CompilerParams(dimension_semantics=sem, vmem_limit_bytes=VMEM_LIMIT)


def _layer_norm(z, g, b):
    mu = jnp.mean(z, axis=-1, keepdims=True)
    zc = z - mu
    var = jnp.mean(zc * zc, axis=-1, keepdims=True)
    return zc * lax.rsqrt(var + LN_EPS) * g + b


def _dot(a, b):
    return jnp.dot(a.astype(BF16), b.astype(BF16), preferred_element_type=F32)


def _dot_nt(a, b):
    return lax.dot_general(a.astype(BF16), b.astype(BF16), (((1,), (1,)), ((), ())),
                           preferred_element_type=F32)


def _dot_tn(a, b):
    return lax.dot_general(a.astype(BF16), b.astype(BF16), (((0,), (0,)), ((), ())),
                           preferred_element_type=F32)


def _proj_in_kernel(x_ref, w_ref, o_ref):
    o_ref[...] = _dot(x_ref[...], w_ref[...])


def _proj_in(x2d, w_bf16, tm=512):
    t, d = x2d.shape
    n = w_bf16.shape[1]
    return pl.pallas_call(
        _proj_in_kernel,
        grid=(t // tm,),
        in_specs=[pl.BlockSpec((tm, d), lambda i: (i, 0)),
                  pl.BlockSpec((d, n), lambda i: (0, 0))],
        out_specs=pl.BlockSpec((tm, n), lambda i: (i, 0)),
        out_shape=jax.ShapeDtypeStruct((t, n), F32),
        compiler_params=_params("parallel"),
        name="proj_in",
    )(x2d, w_bf16)


def _retention_kernel(qk_ref, v_ref, g_ref, cos_ref, sin_ref, decay_ref, zeta_ref, xi_ref, gam_ref,
                      o_ref, state_ref):
    n = pl.program_id(1)

    @pl.when(n == 0)
    def _():
        state_ref[...] = jnp.zeros_like(state_ref)

    c = qk_ref.shape[1]
    lane = lax.broadcasted_iota(jnp.int32, (c, LANE), 1)
    first_half = (lane % RET_QK_DIM) < (RET_QK_DIM // 2)
    n_pairs = RET_HEADS * RET_QK_DIM // LANE

    def rot(col):
        t = qk_ref[0, :, col * LANE:(col + 1) * LANE]
        swapped = jnp.where(first_half,
                            pltpu.roll(t, LANE - RET_QK_DIM // 2, axis=1),
                            pltpu.roll(t, RET_QK_DIM // 2, axis=1))
        return t * cos_ref[...] + swapped * sin_ref[...]

    q_pairs = [rot(p) for p in range(n_pairs)]
    k_pairs = [rot(n_pairs + p) * (RET_QK_DIM ** -0.5) for p in range(n_pairs)]

    for h in range(RET_HEADS):
        p = (h * RET_QK_DIM) // LANE
        lo = (h * RET_QK_DIM) % LANE
        mine = (lane >= lo) & (lane < lo + RET_QK_DIM)
        qm = jnp.where(mine, q_pairs[p], 0.0)
        km = jnp.where(mine, k_pairs[p], 0.0)
        v = v_ref[0, :, h * RET_V_DIM:(h + 1) * RET_V_DIM]
        s = _dot_nt(qm, k_pairs[p]) * decay_ref[h]
        y = _dot(s, v)
        st = state_ref[h]
        y = y + _dot(qm, st) * xi_ref[h]
        kv = _dot_tn(km * zeta_ref[h], v)
        state_ref[h] = gam_ref[h] * st + kv
        mu = jnp.mean(y, axis=-1, keepdims=True)
        yc = y - mu
        var = jnp.mean(yc * yc, axis=-1, keepdims=True)
        yn = yc * lax.rsqrt(var + GN_EPS)
        gate = g_ref[0, :, h * RET_V_DIM:(h + 1) * RET_V_DIM]
        o_ref[0, :, h * RET_V_DIM:(h + 1) * RET_V_DIM] = gate * jax.nn.sigmoid(gate) * yn


def _retention_tables(s):
    half = RET_QK_DIM // 2
    inv = 1.0 / (ROPE_BASE ** (jnp.arange(half, dtype=F32) / half))
    ang = jnp.arange(s, dtype=F32)[:, None] * inv[None, :]
    cos = jnp.cos(ang)
    sin = jnp.sin(ang)
    cos_t = jnp.tile(jnp.concatenate([cos, cos], axis=-1), (1, LANE // RET_QK_DIM))
    sin_t = jnp.tile(jnp.concatenate([-sin, sin], axis=-1), (1, LANE // RET_QK_DIM))
    c = RET_CHUNK
    lg = jnp.log(1.0 - jnp.exp2(-5.0 - jnp.arange(RET_HEADS, dtype=F32)))
    idx = jnp.arange(c, dtype=F32)
    diff = idx[:, None] - idx[None, :]
    decay = jnp.where(diff >= 0, jnp.exp(lg[:, None, None] * jnp.maximum(diff, 0.0)), 0.0)
    zeta = jnp.exp(lg[:, None] * (c - 1.0 - idx))[:, :, None]
    xi = jnp.exp(lg[:, None] * (idx + 1.0))[:, :, None]
    gam = jnp.broadcast_to(jnp.exp(lg * c)[:, None, None], (RET_HEADS, 1, LANE))
    return cos_t, sin_t, decay, zeta, xi, gam


def _retention(h3):
    b, s, _ = h3.shape
    c = RET_CHUNK
    qk_w = 2 * RET_HEADS * RET_QK_DIM
    v_w = RET_HEADS * RET_V_DIM
    assert qk_w == v_w
    cos_t, sin_t, decay, zeta, xi, gam = _retention_tables(s)
    const3 = lambda bi, n: (0, 0, 0)
    return pl.pallas_call(
        _retention_kernel,
        grid=(b, s // c),
        in_specs=[pl.BlockSpec((1, c, qk_w), lambda bi, n: (bi, n, 0)),
                  pl.BlockSpec((1, c, v_w), lambda bi, n: (bi, n, 1)),
                  pl.BlockSpec((1, c, v_w), lambda bi, n: (bi, n, 2)),
                  pl.BlockSpec((c, LANE), lambda bi, n: (n, 0)),
                  pl.BlockSpec((c, LANE), lambda bi, n: (n, 0)),
                  pl.BlockSpec((RET_HEADS, c, c), const3),
                  pl.BlockSpec((RET_HEADS, c, 1), const3),
                  pl.BlockSpec((RET_HEADS, c, 1), const3),
                  pl.BlockSpec((RET_HEADS, 1, LANE), const3)],
        out_specs=pl.BlockSpec((1, c, v_w), lambda bi, n: (bi, n, 0)),
        out_shape=jax.ShapeDtypeStruct((b, s, v_w), F32),
        scratch_shapes=[pltpu.VMEM((RET_HEADS, LANE, RET_V_DIM), F32)],
        compiler_params=_params("parallel", "arbitrary"),
        name="retention",
    )(h3, h3, h3, cos_t, sin_t, decay, zeta, xi, gam)


def _dilated_kernel(q_ref, kp_ref, kc_ref, vp_ref, vc_ref, bias_ref, o_ref, kk_ref, vv_ref, acc_ref, m_ref, l_ref,
                    s0_ref, s1_ref, p0_ref, p1_ref):
    j = pl.program_id(2)
    sup = DIL_SUPER
    q_blk = DIL_BLOCK
    kk_ref[0:sup, :] = kp_ref[0]
    kk_ref[sup:2 * sup, :] = kc_ref[0]
    vv_ref[0:sup, :] = vp_ref[0]
    vv_ref[sup:2 * sup, :] = vc_ref[0]
    scale = DIL_HEAD_DIM ** -0.5
    n_blocks = sup // q_blk
    lane = lax.broadcasted_iota(jnp.int32, (q_blk, LANE), 1)
    head0 = lane < DIL_HEAD_DIM
    first_bias = jnp.where(j == 0, 1, 0)

    groups = []
    for bi, d in enumerate(DIL_DILATIONS):
        n_per_r = sup // (q_blk * d)
        for t0 in range(0, n_blocks, DIL_GROUP):
            blocks = []
            for t in range(t0, t0 + DIL_GROUP):
                r, n = divmod(t, n_per_r)
                blocks.append((n * (q_blk * d) + r, sup + (n - 1) * (q_blk * d) + r, n))
            groups.append((bi, d, blocks))
    s_bufs = (s0_ref, s1_ref)
    p_bufs = (p0_ref, p1_ref)

    def scores(gi):
        bi, d, blocks = groups[gi]
        for g, (q_start, k_start, n) in enumerate(blocks):
            q = q_ref[0, pl.ds(q_start, q_blk, stride=d), :] * scale
            kb = kk_ref[pl.ds(k_start, 2 * q_blk, stride=d), :]
            q2 = jnp.concatenate([jnp.where(head0, q, 0.0), jnp.where(head0, 0.0, q)], axis=0)
            bias = bias_ref[first_bias] if n == 0 else bias_ref[0]
            s_bufs[gi % 2][g] = _dot_nt(q2, kb) + bias

    def softmax(gi):
        bi, d, blocks = groups[gi]
        for g, (q_start, k_start, n) in enumerate(blocks):
            s = s_bufs[gi % 2][g]
            m2 = jnp.max(s, axis=-1, keepdims=True)
            p_bufs[gi % 2][g] = jnp.exp(s - m2).astype(BF16)
            m_ref[bi, pl.ds(q_start, q_blk, stride=d), :] = jnp.where(head0, m2[:q_blk], m2[q_blk:])

    def values(gi):
        bi, d, blocks = groups[gi]
        for g, (q_start, k_start, n) in enumerate(blocks):
            vb = vv_ref[pl.ds(k_start, 2 * q_blk, stride=d), :].astype(BF16)
            o2 = jnp.dot(p_bufs[gi % 2][g], jnp.concatenate([vb, jnp.ones_like(vb)], axis=1),
                         preferred_element_type=F32)
            rows = pl.ds(q_start, q_blk, stride=d)
            acc_ref[bi, rows, :] = jnp.where(head0, o2[:q_blk, :LANE], o2[q_blk:, :LANE])
            l_ref[bi, rows, :] = jnp.where(head0, o2[:q_blk, LANE:], o2[q_blk:, LANE:])

    for step in range(len(groups) + 2):
        if step < len(groups):
            scores(step)
        if 0 <= step - 1 < len(groups):
            softmax(step - 1)
        if step - 2 >= 0:
            values(step - 2)

    def merge(c, carry):
        rows = pl.ds(pl.multiple_of(c * q_blk, q_blk), q_blk)
        ms = [m_ref[bi, rows, :] for bi in range(len(DIL_DILATIONS))]
        m_all = functools.reduce(jnp.maximum, ms)
        ws = [jnp.exp(m - m_all) for m in ms]
        num = functools.reduce(lambda a, b: a + b, [w * acc_ref[bi, rows, :] for bi, w in enumerate(ws)])
        den = functools.reduce(lambda a, b: a + b, [w * l_ref[bi, rows, :] for bi, w in enumerate(ws)])
        o_ref[0, rows, :] = num / den
        return carry

    lax.fori_loop(0, n_blocks, merge, 0)


def _dilated_bias():
    q_blk = DIL_BLOCK
    qi = jnp.arange(2 * q_blk)[:, None] % q_blk
    kj = jnp.arange(2 * q_blk)[None, :]
    band = (kj >= qi) & (kj <= qi + q_blk)
    return jnp.stack([jnp.where(band, 0.0, NEG), jnp.where(band & (kj >= q_blk), 0.0, NEG)]).astype(F32)


def _dilated(h3, col0):
    b, s, _ = h3.shape
    sup = DIL_SUPER
    n_pairs = DIL_HEADS * DIL_HEAD_DIM // LANE
    n_br = len(DIL_DILATIONS)
    cq, ck, cv = col0 * n_pairs, (col0 + 1) * n_pairs, (col0 + 2) * n_pairs
    cur = lambda c: (lambda bi, p, j: (bi, j, c + p))
    prev = lambda c: (lambda bi, p, j: (bi, jnp.maximum(j - 1, 0), c + p))
    blk = (1, sup, LANE)
    return pl.pallas_call(
        _dilated_kernel,
        grid=(b, n_pairs, s // sup),
        in_specs=[pl.BlockSpec(blk, cur(cq)),
                  pl.BlockSpec(blk, prev(ck)), pl.BlockSpec(blk, cur(ck)),
                  pl.BlockSpec(blk, prev(cv)), pl.BlockSpec(blk, cur(cv)),
                  pl.BlockSpec((2, 2 * DIL_BLOCK, 2 * DIL_BLOCK), lambda bi, p, j: (0, 0, 0))],
        out_specs=pl.BlockSpec(blk, lambda bi, p, j: (bi, j, p)),
        out_shape=jax.ShapeDtypeStruct((b, s, n_pairs * LANE), F32),
        scratch_shapes=[pltpu.VMEM((2 * sup, LANE), F32), pltpu.VMEM((2 * sup, LANE), F32),
                        pltpu.VMEM((n_br, sup, LANE), F32), pltpu.VMEM((n_br, sup, LANE), F32),
                        pltpu.VMEM((n_br, sup, LANE), F32),
                        pltpu.VMEM((DIL_GROUP, 2 * DIL_BLOCK, 2 * DIL_BLOCK), F32),
                        pltpu.VMEM((DIL_GROUP, 2 * DIL_BLOCK, 2 * DIL_BLOCK), F32),
                        pltpu.VMEM((DIL_GROUP, 2 * DIL_BLOCK, 2 * DIL_BLOCK), BF16),
                        pltpu.VMEM((DIL_GROUP, 2 * DIL_BLOCK, 2 * DIL_BLOCK), BF16)],
        compiler_params=_params("parallel", "parallel", "arbitrary"),
        name="dilated",
    )(h3, h3, h3, h3, h3, _dilated_bias())


def _mem_kv(mem2d, w_bf16, tn=512):
    m, d = mem2d.shape
    n = w_bf16.shape[1]
    return pl.pallas_call(
        _proj_in_kernel,
        grid=(n // tn,),
        in_specs=[pl.BlockSpec((m, d), lambda i: (0, 0)),
                  pl.BlockSpec((d, tn), lambda i: (0, i))],
        out_specs=pl.BlockSpec((m, tn), lambda i: (0, i)),
        out_shape=jax.ShapeDtypeStruct((m, n), F32),
        compiler_params=_params("parallel"),
        name="mem_kv",
    )(mem2d, w_bf16)


def _xattn_kernel(alpha, x_ref, yr_ref, yd_ref, wout_ref, g1_ref, b1_ref, k_ref, v_ref, wq_ref, wo_ref,
                  g_ref, b_ref, wr_ref, o_ref, ot_ref, lg_ref, att_ref):
    tm, d = x_ref.shape
    dh = d // XATTN_HEADS
    wr = yr_ref.shape[1]
    y = _dot(yr_ref[...], wout_ref[0:wr, :]) + _dot(yd_ref[...], wout_ref[wr:, :])
    x = _layer_norm(alpha * x_ref[...] + y, g1_ref[...], b1_ref[...])
    q = _dot(x, wq_ref[...])
    scale = dh ** -0.5
    for h in range(XATTN_HEADS):
        sl = slice(h * dh, (h + 1) * dh)
        s = _dot_nt(q[:, sl], k_ref[0, :, sl]) * scale
        m = jnp.max(s, axis=-1, keepdims=True)
        e = jnp.exp(s - m)
        p = e / jnp.sum(e, axis=-1, keepdims=True)
        att_ref[:, sl] = _dot(p, v_ref[0, :, sl])
    y = _dot(att_ref[...], wo_ref[...])
    x2 = _layer_norm(alpha * x + y, g_ref[...], b_ref[...])
    o_ref[...] = x2
    for c in range(d // LANE):
        ot_ref[pl.ds(c, tm, stride=SUBLANE), :] = x2[:, c * LANE:(c + 1) * LANE]
    lg_ref[...] = _dot_nt(wr_ref[...], x2)


def _xattn(x2d, y_ret, y_dil, w_out, g1, b1, kv, wq, wo, g, bb, w_r, alpha, seq, tm=256):
    t, d = x2d.shape
    mlen = kv.shape[1]
    tiles_per_seq = seq // tm
    row = lambda i: (i, 0)
    const = lambda i: (0, 0)
    vec = pl.BlockSpec((1, d), const)
    return pl.pallas_call(
        functools.partial(_xattn_kernel, alpha),
        grid=(t // tm,),
        in_specs=[pl.BlockSpec((tm, d), row),
                  pl.BlockSpec((tm, y_ret.shape[1]), row),
                  pl.BlockSpec((tm, y_dil.shape[1]), row),
                  pl.BlockSpec(w_out.shape, const), vec, vec,
                  pl.BlockSpec((1, mlen, d), lambda i: (i // tiles_per_seq, 0, 0)),
                  pl.BlockSpec((1, mlen, d), lambda i: (i // tiles_per_seq, 0, 1)),
                  pl.BlockSpec((d, d), const), pl.BlockSpec((d, d), const), vec, vec,
                  pl.BlockSpec((ROUTE_ROWS, d), const)],
        out_specs=[pl.BlockSpec((tm, d), row),
                   pl.BlockSpec((tm * SUBLANE, LANE), row),
                   pl.BlockSpec((ROUTE_ROWS, tm), lambda i: (0, i))],
        out_shape=[jax.ShapeDtypeStruct((t, d), F32),
                   jax.ShapeDtypeStruct((t * SUBLANE, LANE), F32),
                   jax.ShapeDtypeStruct((ROUTE_ROWS, t), F32)],
        scratch_shapes=[pltpu.VMEM((tm, d), F32)],
        compiler_params=_params("parallel"),
        name="xattn",
    )(x2d, y_ret, y_dil, w_out, g1, b1, kv, kv, wq, wo, g, bb, w_r)


def _route_kernel(lg_ref, before_ref, meta_ref, cnt_ref, carry_ref):
    i = pl.program_id(0)

    @pl.when(i == 0)
    def _():
        carry_ref[...] = jnp.zeros_like(carry_ref)

    rows, tt = lg_ref.shape
    sub = before_ref.shape[0]
    r = lax.broadcasted_iota(jnp.int32, (rows, sub), 0)
    r8 = lax.broadcasted_iota(jnp.int32, (SUBLANE, sub), 0)
    is_group = r < N_GROUPS

    def col_max(a):
        return jnp.max(a, axis=0, keepdims=True)

    def first_row_where(mask):
        return jnp.min(jnp.where(mask, r, rows), axis=0, keepdims=True)

    for c in range(tt // sub):
        lg = lg_ref[:, c * sub:(c + 1) * sub]
        mg = col_max(jnp.where(is_group, lg, NEG))
        eg = jnp.where(is_group, jnp.exp(lg - mg), 0.0)
        pg = eg / jnp.sum(eg, axis=0, keepdims=True)
        g1 = col_max(pg)
        gi = first_row_where(is_group & (pg == g1))
        lo = N_GROUPS + gi * EXPERTS_PER_GROUP
        in_grp = (r >= lo) & (r < lo + EXPERTS_PER_GROUP)
        v1 = col_max(jnp.where(in_grp, lg, NEG))
        i1 = first_row_where(in_grp & (lg == v1))
        rest = in_grp & (r != i1)
        v2 = col_max(jnp.where(rest, lg, NEG))
        i2 = first_row_where(rest & (lg == v2))
        e2 = jnp.exp(v2 - v1)
        den = 1.0 + e2
        gate1 = g1 * (1.0 / den)
        gate2 = g1 * (e2 / den)
        sel1 = r == i1
        sel2 = r == i2
        onehot = jnp.where(sel1 | sel2, 1.0, 0.0)
        rank = _dot(onehot, before_ref[...]) + carry_ref[:, 0:1]
        r1 = jnp.sum(jnp.where(sel1, rank, 0.0), axis=0, keepdims=True)
        r2 = jnp.sum(jnp.where(sel2, rank, 0.0), axis=0, keepdims=True)
        carry_ref[...] = carry_ref[...] + jnp.sum(onehot, axis=1, keepdims=True)
        meta = jnp.where(r8 == 0, (i1 - N_GROUPS).astype(F32), 0.0)
        meta = jnp.where(r8 == 1, (i2 - N_GROUPS).astype(F32), meta)
        meta = jnp.where(r8 == 2, r1, meta)
        meta = jnp.where(r8 == 3, r2, meta)
        meta = jnp.where(r8 == 4, gate1, meta)
        meta = jnp.where(r8 == 5, gate2, meta)
        meta_ref[:, c * sub:(c + 1) * sub] = meta
    cnt_ref[...] = carry_ref[...]


def _route(logits_t, tt=1024, sub=256):
    rows, t = logits_t.shape
    before = (jnp.arange(sub)[:, None] < jnp.arange(sub)[None, :]).astype(BF16)
    return pl.pallas_call(
        _route_kernel,
        grid=(t // tt,),
        in_specs=[pl.BlockSpec((rows, tt), lambda i: (0, i)),
                  pl.BlockSpec((sub, sub), lambda i: (0, 0))],
        out_specs=[pl.BlockSpec((SUBLANE, tt), lambda i: (0, i)),
                   pl.BlockSpec((rows, LANE), lambda i: (0, 0))],
        out_shape=[jax.ShapeDtypeStruct((SUBLANE, t), F32),
                   jax.ShapeDtypeStruct((rows, LANE), F32)],
        scratch_shapes=[pltpu.VMEM((rows, LANE), F32)],
        compiler_params=_params("arbitrary"),
        name="route",
    )(logits_t, before)


def _plan_kernel(meta_ref, first_row_ref, dest_ref):
    n_tiles = dest_ref.shape[0]
    tt = dest_ref.shape[2] // 2
    rows = first_row_ref.shape[0]
    first_row = first_row_ref[:, 0:1]
    r = lax.broadcasted_iota(jnp.int32, (rows, tt), 0)
    for g in range(n_tiles):
        m = meta_ref[:, g * tt:(g + 1) * tt]

        def dest_of(k):
            e_row = m[k:k + 1, :].astype(jnp.int32) + N_GROUPS
            return jnp.sum(jnp.where(r == e_row, first_row, 0.0), axis=0, keepdims=True) + m[2 + k:3 + k, :]

        dest_ref[g] = jnp.concatenate([dest_of(0), dest_of(1)], axis=1).astype(jnp.int32)


def _plan(meta_t, first_row, tt, tiles_per_step=4):
    t = meta_t.shape[1]
    rows = first_row.shape[0]
    return pl.pallas_call(
        _plan_kernel,
        grid=(t // (tt * tiles_per_step),),
        in_specs=[pl.BlockSpec((SUBLANE, tt * tiles_per_step), lambda i: (0, i)),
                  pl.BlockSpec((rows, LANE), lambda i: (0, 0))],
        out_specs=pl.BlockSpec((tiles_per_step, 1, 2 * tt), lambda i: (i, 0, 0)),
        out_shape=jax.ShapeDtypeStruct((t // tt, 1, 2 * tt), jnp.int32),
        compiler_params=_params("parallel"),
        name="plan",
    )(meta_t, first_row)


def _row_copy(src, dst, s_row, d_row, sem):
    return pltpu.make_async_copy(src.at[pl.ds(pl.multiple_of(s_row * SUBLANE, SUBLANE), SUBLANE), :],
                                 dst.at[pl.ds(pl.multiple_of(d_row * SUBLANE, SUBLANE), SUBLANE), :], sem)


def _dispatch_kernel(dest_ref, xt_ref, xs_ref, sem):
    tq = dest_ref.shape[2] // 2

    def issue(t, c):
        _row_copy(xt_ref, xs_ref, t, dest_ref[0, 0, t], sem).start(priority=0)
        _row_copy(xt_ref, xs_ref, t, dest_ref[0, 0, tq + t], sem).start(priority=1)
        return c

    lax.fori_loop(0, tq, issue, 0, unroll=8)
    for _ in range(2):
        pltpu.make_async_copy(xt_ref, xs_ref.at[pl.ds(0, tq * SUBLANE), :], sem).wait()


def _dispatch(dest3, x_tiled, n_rows, tiles_per_step=4):
    n_tiles, _, two_tt = dest3.shape
    tt = two_tt // 2
    tq = tiles_per_step * tt
    n_steps = n_tiles // tiles_per_step
    dest3 = dest3.reshape(n_steps, tiles_per_step, 2, tt).transpose(0, 2, 1, 3).reshape(n_steps, 1, 2 * tq)
    return pl.pallas_call(
        _dispatch_kernel,
        grid=(n_steps,),
        in_specs=[pl.BlockSpec((1, 1, 2 * tq), lambda i: (i, 0, 0), memory_space=pltpu.SMEM),
                  pl.BlockSpec((tq * SUBLANE, LANE), lambda i: (i, 0))],
        out_specs=pl.BlockSpec(memory_space=pl.ANY),
        out_shape=jax.ShapeDtypeStruct((n_rows * SUBLANE, LANE), F32),
        scratch_shapes=[pltpu.SemaphoreType.DMA(())],
        compiler_params=_params("arbitrary"),
        name="dispatch",
    )(dest3, x_tiled)


def _experts_kernel(blk_exp_ref, blk_rows_ref, n_used_ref, blk_first_ref, blk_slot_ref, blk_next_ref,
                    xs_ref, wg_hbm, wu_hbm, wd_hbm, ys_ref, x_scr, y_scr, wg_buf, wu_buf, wd_buf, sems):
    b = pl.program_id(0)

    def weight_copies(e, slot):
        return [pltpu.make_async_copy(hbm.at[e], buf.at[slot], sems.at[k, slot])
                for k, (hbm, buf) in enumerate(((wg_hbm, wg_buf), (wu_hbm, wu_buf), (wd_hbm, wd_buf)))]

    @pl.when(b < n_used_ref[0])
    def _():
        slot = blk_slot_ref[b]

        @pl.when(b == 0)
        def _():
            for cp in weight_copies(blk_exp_ref[0], slot):
                cp.start()

        @pl.when(blk_first_ref[b] == 1)
        def _():
            @pl.when(blk_next_ref[b] >= 0)
            def _():
                for cp in weight_copies(blk_next_ref[b], 1 - slot):
                    cp.start()

            for cp in weight_copies(blk_exp_ref[b], slot):
                cp.wait()

        rows, d = x_scr.shape
        for c in range(d // LANE):
            x_scr[:, c * LANE:(c + 1) * LANE] = xs_ref[pl.ds(c, rows, stride=SUBLANE), :]
        ri = lax.broadcasted_iota(jnp.int32, (rows, 1), 0)
        x = jnp.where(ri < blk_rows_ref[b], x_scr[...], 0.0)
        hg = _dot(x, wg_buf[slot])
        hu = _dot(x, wu_buf[slot])
        y_scr[...] = _dot(hg * jax.nn.sigmoid(hg) * hu, wd_buf[slot])
        for c in range(d // LANE):
            ys_ref[pl.ds(c, rows, stride=SUBLANE), :] = y_scr[:, c * LANE:(c + 1) * LANE]


def _experts(blk_exp, blk_rows, n_used, blk_first, blk_slot, blk_next, xs_tiled, w_g, w_u, w_d):
    n_rows = xs_tiled.shape[0] // SUBLANE
    rows = EXPERT_ROWS
    nblk = n_rows // rows
    _, d, ff = w_g.shape
    used = lambda b, be, br, nu, *_: (jnp.minimum(b, nu[0] - 1), 0)
    grid_spec = pltpu.PrefetchScalarGridSpec(
        num_scalar_prefetch=6,
        grid=(nblk,),
        in_specs=[pl.BlockSpec((rows * SUBLANE, LANE), used),
                  pl.BlockSpec(memory_space=pl.ANY), pl.BlockSpec(memory_space=pl.ANY),
                  pl.BlockSpec(memory_space=pl.ANY)],
        out_specs=pl.BlockSpec((rows * SUBLANE, LANE), used),
        scratch_shapes=[pltpu.VMEM((rows, d), F32), pltpu.VMEM((rows, d), F32),
                        pltpu.VMEM((2, d, ff), F32), pltpu.VMEM((2, d, ff), F32), pltpu.VMEM((2, ff, d), F32),
                        pltpu.SemaphoreType.DMA((3, 2))],
    )
    return pl.pallas_call(
        _experts_kernel,
        grid_spec=grid_spec,
        out_shape=jax.ShapeDtypeStruct((n_rows * SUBLANE, LANE), F32),
        compiler_params=_params("arbitrary"),
        name="experts",
    )(blk_exp, blk_rows, n_used, blk_first, blk_slot, blk_next, xs_tiled, w_g, w_u, w_d)


def _combine_kernel(alpha, dcur_ref, dnext_ref, meta_ref, x_ref, g_ref, b_ref, ys_ref, o_ref, buf_ref, sems):
    i = pl.program_id(0)
    tq, d = x_ref.shape
    slot_rows = 2 * tq * SUBLANE

    def issue_all(d_ref, slot):
        def issue(t, c):
            _row_copy(ys_ref, buf_ref, d_ref[0, 0, t], slot * (2 * tq) + t, sems.at[slot]).start(priority=0)
            _row_copy(ys_ref, buf_ref, d_ref[0, 0, tq + t], slot * (2 * tq) + tq + t, sems.at[slot]).start(priority=1)
            return c

        lax.fori_loop(0, tq, issue, 0, unroll=4)

    slot = i % 2

    @pl.when(i == 0)
    def _():
        issue_all(dcur_ref, 0)

    @pl.when(i + 1 < pl.num_programs(0))
    def _():
        issue_all(dnext_ref, 1 - slot)

    off = pl.multiple_of(slot * slot_rows, slot_rows)
    pltpu.make_async_copy(ys_ref.at[pl.ds(0, slot_rows), :], buf_ref.at[pl.ds(off, slot_rows), :],
                          sems.at[slot]).wait()

    meta_rows = jnp.transpose(jnp.concatenate([meta_ref[...], jnp.zeros((LANE - SUBLANE, tq), F32)], axis=0))
    gate1 = meta_rows[:, 4:5]
    gate2 = meta_rows[:, 5:6]
    for c in range(d // LANE):
        sl = slice(c * LANE, (c + 1) * LANE)
        y1 = buf_ref[pl.ds(off + c, tq, stride=SUBLANE), :]
        y2 = buf_ref[pl.ds(off + tq * SUBLANE + c, tq, stride=SUBLANE), :]
        o_ref[:, sl] = alpha * x_ref[:, sl] + (y1 * gate1 + y2 * gate2)
    o_ref[...] = _layer_norm(o_ref[...], g_ref[...], b_ref[...])


def _combine(dest3, meta, x2, g, bb, ys_tiled, alpha):
    t, d = x2.shape
    n = dest3.shape[0]
    tq = t // n
    row = lambda i: (i, 0)
    const = lambda i: (0, 0)
    return pl.pallas_call(
        functools.partial(_combine_kernel, alpha),
        grid=(n,),
        in_specs=[pl.BlockSpec((1, 1, 2 * tq), lambda i: (i, 0, 0), memory_space=pltpu.SMEM),
                  pl.BlockSpec((1, 1, 2 * tq), lambda i: (jnp.minimum(i + 1, n - 1), 0, 0), memory_space=pltpu.SMEM),
                  pl.BlockSpec((SUBLANE, tq), lambda i: (0, i)),
                  pl.BlockSpec((tq, d), row),
                  pl.BlockSpec((1, d), const), pl.BlockSpec((1, d), const),
                  pl.BlockSpec(memory_space=pl.ANY)],
        out_specs=pl.BlockSpec((tq, d), row),
        out_shape=jax.ShapeDtypeStruct((t, d), F32),
        scratch_shapes=[pltpu.VMEM((2 * 2 * tq * SUBLANE, LANE), F32), pltpu.SemaphoreType.DMA((2,))],
        compiler_params=_params("arbitrary"),
        name="combine",
    )(dest3, dest3, meta, x2, g, bb, ys_tiled)


def _moe(x2, x2_tiled, logits, w_g, w_u, w_d, g, bb, alpha):
    t, d = x2.shape
    meta, cnt = _route(logits)
    counts = cnt[N_GROUPS:N_GROUPS + N_EXPERTS, 0].astype(jnp.int32)
    padded = ((counts + EXPERT_ROWS - 1) // EXPERT_ROWS) * EXPERT_ROWS
    pend = jnp.cumsum(padded)
    poff = pend - padded
    first_row = jnp.pad(poff.astype(F32), (N_GROUPS, ROUTE_ROWS - N_GROUPS - N_EXPERTS))
    dest3 = _plan(meta, jnp.broadcast_to(first_row[:, None], (ROUTE_ROWS, LANE)), ROUTE_TILE)
    n_rows = t * 2 + N_EXPERTS * EXPERT_ROWS
    nblk = n_rows // EXPERT_ROWS
    blk_start = jnp.arange(nblk, dtype=jnp.int32) * EXPERT_ROWS
    blk_exp = jnp.sum((pend[None, :] <= blk_start[:, None]).astype(jnp.int32), axis=1)
    blk_exp = jnp.minimum(blk_exp, N_EXPERTS - 1)
    e_idx = jnp.arange(N_EXPERTS, dtype=jnp.int32)
    owner = blk_exp[:, None] == e_idx[None, :]

    def of_block(per_expert):
        return jnp.sum(jnp.where(owner, per_expert[None, :], 0), axis=1).astype(jnp.int32)

    blk_rows = jnp.clip(of_block(poff + counts) - blk_start, 0, EXPERT_ROWS).astype(jnp.int32)
    n_used = (pend[-1:] // EXPERT_ROWS).astype(jnp.int32)
    has_rows = counts > 0
    ordinal = jnp.cumsum(has_rows.astype(jnp.int32)) - 1
    later = has_rows[None, :] & (e_idx[None, :] > e_idx[:, None])
    next_used = jnp.min(jnp.where(later, e_idx[None, :], N_EXPERTS), axis=1)
    next_used = jnp.where(next_used == N_EXPERTS, -1, next_used).astype(jnp.int32)
    blk_first = (blk_start == of_block(poff)).astype(jnp.int32)
    blk_slot = of_block(ordinal % 2)
    blk_next = of_block(next_used)
    xs_tiled = _dispatch(dest3, x2_tiled, n_rows)
    ys_tiled = _experts(blk_exp, blk_rows, n_used, blk_first, blk_slot, blk_next, xs_tiled, w_g, w_u, w_d)
    return _combine(dest3, meta, x2, g, bb, ys_tiled, alpha)


def kernel(x, mem, w_in, w_out, ln_mix_g, ln_mix_b, w_xq, w_xkv, w_xo, ln_x_g, ln_x_b, w_route_group,
           w_route_expert, w_exp_gate, w_exp_up, w_exp_down, ln_moe_g, ln_moe_b):
    b, s, d = x.shape
    depth = w_in.shape[0]
    alpha = (2.0 * depth) ** 0.25
    t = b * s
    ret_cols = (2 * RET_HEADS * RET_QK_DIM + 2 * RET_HEADS * RET_V_DIM)
    assert ret_cols % (DIL_HEADS * DIL_HEAD_DIM) == 0
    dil_col0 = ret_cols // (DIL_HEADS * DIL_HEAD_DIM)
    xc = x.reshape(t, d)
    for l in range(depth):
        h = _proj_in(xc, w_in[l].astype(BF16))
        h3 = h.reshape(b, s, h.shape[1])
        y_ret = _retention(h3).reshape(t, -1)
        y_dil = _dilated(h3, dil_col0).reshape(t, -1)
        kv =_mem_kv(mem.reshape(b * mem.shape[1], d), w_xkv[l].astype(BF16)).reshape(b, mem.shape[1], 2 * d)
        w_r = jnp.concatenate([w_route_group[l], w_route_expert[l]], axis=-1)
        w_r = jnp.pad(w_r.T, ((0, ROUTE_ROWS - w_r.shape[1]), (0, 0))).astype(BF16)
        x2, x2_tiled, logits = _xattn(xc, y_ret, y_dil, w_out[l].astype(BF16), ln_mix_g[l][None], ln_mix_b[l][None],
                                      kv, w_xq[l].astype(BF16), w_xo[l].astype(BF16),
                                      ln_x_g[l][None], ln_x_b[l][None], w_r, alpha, s)
        xc = _moe(x2, x2_tiled, logits, w_exp_gate[l], w_exp_up[l], w_exp_down[l],
                  ln_moe_g[l][None], ln_moe_b[l][None], alpha)
    return xc.reshape(b, s, d)
```

```python
import functools

import jax
import jax.numpy as jnp
from jax import lax
from jax.experimental import pallas as pl
from jax.experimental.pallas import tpu as pltpu

BF16 = jnp.bfloat16
F32 = jnp.float32

LANE = 128
SUBLANE = 8
VMEM_LIMIT = 56 * 1024 * 1024

RET_HEADS = 4
RET_QK_DIM = 64
RET_V_DIM = 128
RET_CHUNK = 128
RET_STEP_CHUNKS = 2
ROPE_BASE = 10000.0
DIL_HEADS = 8
DIL_HEAD_DIM = 64
DIL_DILATIONS = (1, 4, 16)
DIL_BLOCK = 128
DIL_SUPER = DIL_BLOCK * max(DIL_DILATIONS)
DIL_GROUP = 1
XATTN_HEADS = 4
N_GROUPS = 4
EXPERTS_PER_GROUP = 8
N_EXPERTS = N_GROUPS * EXPERTS_PER_GROUP
ROUTE_ROWS = -(-(N_GROUPS + N_EXPERTS) // SUBLANE) * SUBLANE
EXPERT_ROWS = 256
ROUTE_TILE = 256
LN_EPS = 1e-5
GN_EPS = 1e-6
NEG = -1e30


def _params(*sem):
    return pltpu.CompilerParams(dimension_semantics=sem, vmem_limit_bytes=VMEM_LIMIT)


def _layer_norm(z, g, b):
    mu = jnp.mean(z, axis=-1, keepdims=True)
    zc = z - mu
    var = jnp.mean(zc * zc, axis=-1, keepdims=True)
    return zc * lax.rsqrt(var + LN_EPS) * g + b


def _dot(a, b):
    return jnp.dot(a.astype(BF16), b.astype(BF16), preferred_element_type=F32)


def _dot_nt(a, b):
    return lax.dot_general(a.astype(BF16), b.astype(BF16), (((1,), (1,)), ((), ())),
                           preferred_element_type=F32)


def _dot_tn(a, b):
    return lax.dot_general(a.astype(BF16), b.astype(BF16), (((0,), (0,)), ((), ())),
                           preferred_element_type=F32)


def _proj_in_kernel(x_ref, w_ref, o_ref):
    o_ref[...] = _dot(x_ref[...], w_ref[...])


def _proj_in(x2d, w_bf16, tm=512):
    t, d = x2d.shape
    n = w_bf16.shape[1]
    return pl.pallas_call(
        _proj_in_kernel,
        grid=(t // tm,),
        in_specs=[pl.BlockSpec((tm, d), lambda i: (i, 0)),
                  pl.BlockSpec((d, n), lambda i: (0, 0))],
        out_specs=pl.BlockSpec((tm, n), lambda i: (i, 0)),
        out_shape=jax.ShapeDtypeStruct((t, n), F32),
        compiler_params=_params("parallel"),
        name="proj_in",
    )(x2d, w_bf16)


def _retention_kernel(qk_ref, v_ref, g_ref, cos_ref, sin_ref, decay_ref, zeta_ref, xi_ref, gam_ref,
                      o_ref, state_ref, y_ref):
    n = pl.program_id(1)

    @pl.when(n == 0)
    def _():
        state_ref[...] = jnp.zeros_like(state_ref)

    c = RET_CHUNK
    n_sub = qk_ref.shape[1] // c
    half = RET_QK_DIM // 2
    lane = lax.broadcasted_iota(jnp.int32, (c, LANE), 1)

    rotated = []
    for j in range(n_sub):
        rows = slice(j * c, (j + 1) * c)
        cos = cos_ref[rows, :]
        sin = sin_ref[rows, :]

        def rot(col, rows=rows, cos=cos, sin=sin):
            t1 = qk_ref[0, rows, col * LANE:(col + 1) * LANE]
            t2 = qk_ref[0, rows, (col + 1) * LANE:(col + 2) * LANE]
            return t1 * cos - t2 * sin, t1 * sin + t2 * cos

        q1, q2 = rot(0)
        k1, k2 = (t * (RET_QK_DIM ** -0.5) for t in rot(2))
        rotated.append((q1, q2, k1, k2, jnp.concatenate([k1, k2], axis=1)))

    for h in range(RET_HEADS):
        cols = slice(h * RET_V_DIM, (h + 1) * RET_V_DIM)
        mine = (lane >= h * half) & (lane < (h + 1) * half)
        zeta = zeta_ref[h]
        st = state_ref[h]
        for j in range(n_sub):
            rows = slice(j * c, (j + 1) * c)
            q1, q2, k1, k2, k_all = rotated[j]
            qm = jnp.concatenate([jnp.where(mine, q1, 0.0), jnp.where(mine, q2, 0.0)], axis=1)
            kz = jnp.concatenate([jnp.where(mine, k1, 0.0) * zeta, jnp.where(mine, k2, 0.0) * zeta], axis=1)
            v = v_ref[0, rows, cols]
            s = _dot_nt(qm, k_all) * decay_ref[h]
            y_ref[rows, cols] = _dot(s, v) + _dot(qm, st) * xi_ref[h]
            st = gam_ref[h] * st + _dot_tn(kz, v)
        state_ref[h] = st

    for h in range(RET_HEADS):
        cols = slice(h * RET_V_DIM, (h + 1) * RET_V_DIM)
        y = y_ref[:, cols]
        mu = jnp.mean(y, axis=-1, keepdims=True)
        yc = y - mu
        var = jnp.mean(yc * yc, axis=-1, keepdims=True)
        yn = yc * lax.rsqrt(var + GN_EPS)
        gate = g_ref[0, :, cols]
        o_ref[0, :, cols] = gate * jax.nn.sigmoid(gate) * yn


def _retention_tables(s):
    half = RET_QK_DIM // 2
    inv = 1.0 / (ROPE_BASE ** (jnp.arange(half, dtype=F32) / half))
    ang = jnp.arange(s, dtype=F32)[:, None] * inv[None, :]
    cos_t = jnp.tile(jnp.cos(ang), (1, RET_HEADS))
    sin_t = jnp.tile(jnp.sin(ang), (1, RET_HEADS))
    c = RET_CHUNK
    lg = jnp.log(1.0 - jnp.exp2(-5.0 - jnp.arange(RET_HEADS, dtype=F32)))
    idx = jnp.arange(c, dtype=F32)
    diff = idx[:, None] - idx[None, :]
    decay = jnp.where(diff >= 0, jnp.exp(lg[:, None, None] * jnp.maximum(diff, 0.0)), 0.0)
    lanes = (RET_HEADS, c, LANE)
    zeta = jnp.broadcast_to(jnp.exp(lg[:, None] * (c - 1.0 - idx))[:, :, None], lanes)
    xi = jnp.broadcast_to(jnp.exp(lg[:, None] * (idx + 1.0))[:, :, None], lanes)
    gam = jnp.broadcast_to(jnp.exp(lg * c)[:, None, None], (RET_HEADS, 1, LANE))
    return cos_t, sin_t, decay, zeta, xi, gam


def _retention_weight_layout(w_qk):
    d = w_qk.shape[0]
    half = RET_QK_DIM // 2
    return w_qk.reshape(d, 2, RET_HEADS, 2, half).transpose(0, 1, 3, 2, 4).reshape(d, -1)


def _retention(h3):
    b, s, _ = h3.shape
    c = RET_CHUNK
    qk_w = 2 * RET_HEADS * RET_QK_DIM
    v_w = RET_HEADS * RET_V_DIM
    assert qk_w == v_w
    cos_t, sin_t, decay, zeta, xi, gam = _retention_tables(s)
    const3 = lambda bi, n: (0, 0, 0)
    rows = RET_STEP_CHUNKS * c
    return pl.pallas_call(
        _retention_kernel,
        grid=(b, s // rows),
        in_specs=[pl.BlockSpec((1, rows, qk_w), lambda bi, n: (bi, n, 0)),
                  pl.BlockSpec((1, rows, v_w), lambda bi, n: (bi, n, 1)),
                  pl.BlockSpec((1, rows, v_w), lambda bi, n: (bi, n, 2)),
                  pl.BlockSpec((rows, LANE), lambda bi, n: (n, 0)),
                  pl.BlockSpec((rows, LANE), lambda bi, n: (n, 0)),
                  pl.BlockSpec((RET_HEADS, c, c), const3),
                  pl.BlockSpec((RET_HEADS, c, LANE), const3),
                  pl.BlockSpec((RET_HEADS, c, LANE), const3),
                  pl.BlockSpec((RET_HEADS, 1, LANE), const3)],
        out_specs=pl.BlockSpec((1, rows, v_w), lambda bi, n: (bi, n, 0)),
        out_shape=jax.ShapeDtypeStruct((b, s, v_w), F32),
        scratch_shapes=[pltpu.VMEM((RET_HEADS, RET_HEADS * RET_QK_DIM, RET_V_DIM), F32),
                        pltpu.VMEM((rows, v_w), F32)],
        compiler_params=_params("parallel", "arbitrary"),
        name="retention",
    )(h3, h3, h3, cos_t, sin_t, decay, zeta, xi, gam)


def _dilated_kernel(q_ref, kp_ref, kc_ref, vp_ref, vc_ref, bias_ref, o_ref, kk_ref, vv_ref, acc_ref, m_ref, l_ref,
                    s0_ref, s1_ref, p0_ref, p1_ref):
    j = pl.program_id(2)
    sup = DIL_SUPER
    q_blk = DIL_BLOCK
    kk_ref[0:sup, :] = kp_ref[0]
    kk_ref[sup:2 * sup, :] = kc_ref[0]
    vv_ref[0:sup, :] = vp_ref[0]
    vv_ref[sup:2 * sup, :] = vc_ref[0]
    scale = DIL_HEAD_DIM ** -0.5
    n_blocks = sup // q_blk
    lane = lax.broadcasted_iota(jnp.int32, (q_blk, LANE), 1)
    head0 = lane < DIL_HEAD_DIM
    first_bias = jnp.where(j == 0, 1, 0)

    groups = []
    for bi, d in enumerate(DIL_DILATIONS):
        n_per_r = sup // (q_blk * d)
        for t0 in range(0, n_blocks, DIL_GROUP):
            blocks = []
            for t in range(t0, t0 + DIL_GROUP):
                r, n = divmod(t, n_per_r)
                blocks.append((n * (q_blk * d) + r, sup + (n - 1) * (q_blk * d) + r, n))
            groups.append((bi, d, blocks))
    s_bufs = (s0_ref, s1_ref)
    p_bufs = (p0_ref, p1_ref)

    def scores(gi):
        bi, d, blocks = groups[gi]
        for g, (q_start, k_start, n) in enumerate(blocks):
            q = q_ref[0, pl.ds(q_start, q_blk, stride=d), :] * scale
            kb = kk_ref[pl.ds(k_start, 2 * q_blk, stride=d), :]
            q2 = jnp.concatenate([jnp.where(head0, q, 0.0), jnp.where(head0, 0.0, q)], axis=0)
            bias = bias_ref[first_bias] if n == 0 else bias_ref[0]
            s_bufs[gi % 2][g] = _dot_nt(q2, kb) + bias

    def softmax(gi):
        bi, d, blocks = groups[gi]
        for g, (q_start, k_start, n) in enumerate(blocks):
            s = s_bufs[gi % 2][g]
            m2 = jnp.max(s, axis=-1, keepdims=True)
            p_bufs[gi % 2][g] = jnp.exp(s - m2).astype(BF16)
            m_ref[bi, pl.ds(q_start, q_blk, stride=d), :] = jnp.where(head0, m2[:q_blk], m2[q_blk:])

    def values(gi):
        bi, d, blocks = groups[gi]
        for g, (q_start, k_start, n) in enumerate(blocks):
            vb = vv_ref[pl.ds(k_start, 2 * q_blk, stride=d), :].astype(BF16)
            o2 = jnp.dot(p_bufs[gi % 2][g], jnp.concatenate([vb, jnp.ones_like(vb)], axis=1),
                         preferred_element_type=F32)
            rows = pl.ds(q_start, q_blk, stride=d)
            acc_ref[bi, rows, :] = jnp.where(head0, o2[:q_blk, :LANE], o2[q_blk:, :LANE])
            l_ref[bi, rows, :] = jnp.where(head0, o2[:q_blk, LANE:], o2[q_blk:, LANE:])

    for step in range(len(groups) + 2):
        if step < len(groups):
            scores(step)
        if 0 <= step - 1 < len(groups):
            softmax(step - 1)
        if step - 2 >= 0:
            values(step - 2)

    def merge(c, carry):
        rows = pl.ds(pl.multiple_of(c * q_blk, q_blk), q_blk)
        ms = [m_ref[bi, rows, :] for bi in range(len(DIL_DILATIONS))]
        m_all = functools.reduce(jnp.maximum, ms)
        ws = [jnp.exp(m - m_all) for m in ms]
        num = functools.reduce(lambda a, b: a + b, [w * acc_ref[bi, rows, :] for bi, w in enumerate(ws)])
        den = functools.reduce(lambda a, b: a + b, [w * l_ref[bi, rows, :] for bi, w in enumerate(ws)])
        o_ref[0, rows, :] = num / den
        return carry

    lax.fori_loop(0, n_blocks, merge, 0)


def _dilated_bias():
    q_blk = DIL_BLOCK
    qi = jnp.arange(2 * q_blk)[:, None] % q_blk
    kj = jnp.arange(2 * q_blk)[None, :]
    band = (kj >= qi) & (kj <= qi + q_blk)
    return jnp.stack([jnp.where(band, 0.0, NEG), jnp.where(band & (kj >= q_blk), 0.0, NEG)]).astype(F32)


def _dilated(h3, col0):
    b, s, _ = h3.shape
    sup = DIL_SUPER
    n_pairs = DIL_HEADS * DIL_HEAD_DIM // LANE
    n_br = len(DIL_DILATIONS)
    cq, ck, cv = col0 * n_pairs, (col0 + 1) * n_pairs, (col0 + 2) * n_pairs
    cur = lambda c: (lambda bi, p, j: (bi, j, c + p))
    prev = lambda c: (lambda bi, p, j: (bi, jnp.maximum(j - 1, 0), c + p))
    blk = (1, sup, LANE)
    return pl.pallas_call(
        _dilated_kernel,
        grid=(b, n_pairs, s // sup),
        in_specs=[pl.BlockSpec(blk, cur(cq)),
                  pl.BlockSpec(blk, prev(ck)), pl.BlockSpec(blk, cur(ck)),
                  pl.BlockSpec(blk, prev(cv)), pl.BlockSpec(blk, cur(cv)),
                  pl.BlockSpec((2, 2 * DIL_BLOCK, 2 * DIL_BLOCK), lambda bi, p, j: (0, 0, 0))],
        out_specs=pl.BlockSpec(blk, lambda bi, p, j: (bi, j, p)),
        out_shape=jax.ShapeDtypeStruct((b, s, n_pairs * LANE), F32),
        scratch_shapes=[pltpu.VMEM((2 * sup, LANE), F32), pltpu.VMEM((2 * sup, LANE), F32),
                        pltpu.VMEM((n_br, sup, LANE), F32), pltpu.VMEM((n_br, sup, LANE), F32),
                        pltpu.VMEM((n_br, sup, LANE), F32),
                        pltpu.VMEM((DIL_GROUP, 2 * DIL_BLOCK, 2 * DIL_BLOCK), F32),
                        pltpu.VMEM((DIL_GROUP, 2 * DIL_BLOCK, 2 * DIL_BLOCK), F32),
                        pltpu.VMEM((DIL_GROUP, 2 * DIL_BLOCK, 2 * DIL_BLOCK), BF16),
                        pltpu.VMEM((DIL_GROUP, 2 * DIL_BLOCK, 2 * DIL_BLOCK), BF16)],
        compiler_params=_params("parallel", "parallel", "arbitrary"),
        name="dilated",
    )(h3, h3, h3, h3, h3, _dilated_bias())


def _mem_kv(mem2d, w_bf16, tn=512):
    m, d = mem2d.shape
    n = w_bf16.shape[1]
    return pl.pallas_call(
        _proj_in_kernel,
        grid=(n // tn,),
        in_specs=[pl.BlockSpec((m, d), lambda i: (0, 0)),
                  pl.BlockSpec((d, tn), lambda i: (0, i))],
        out_specs=pl.BlockSpec((m, tn), lambda i: (0, i)),
        out_shape=jax.ShapeDtypeStruct((m, n), F32),
        compiler_params=_params("parallel"),
        name="mem_kv",
    )(mem2d, w_bf16)


def _xattn_kernel(alpha, x_ref, yr_ref, yd_ref, wout_ref, g1_ref, b1_ref, k_ref, v_ref, wq_ref, wo_ref,
                  g_ref, b_ref, wr_ref, o_ref, ot_ref, lg_ref, att_ref):
    tm, d = x_ref.shape
    dh = d // XATTN_HEADS
    wr = yr_ref.shape[1]
    y = _dot(yr_ref[...], wout_ref[0:wr, :]) + _dot(yd_ref[...], wout_ref[wr:, :])
    x = _layer_norm(alpha * x_ref[...] + y, g1_ref[...], b1_ref[...])
    q = _dot(x, wq_ref[...])
    scale = dh ** -0.5
    for h in range(XATTN_HEADS):
        sl = slice(h * dh, (h + 1) * dh)
        s = _dot_nt(q[:, sl], k_ref[0, :, sl]) * scale
        m = jnp.max(s, axis=-1, keepdims=True)
        e = jnp.exp(s - m)
        p = e / jnp.sum(e, axis=-1, keepdims=True)
        att_ref[:, sl] = _dot(p, v_ref[0, :, sl])
    y = _dot(att_ref[...], wo_ref[...])
    x2 = _layer_norm(alpha * x + y, g_ref[...], b_ref[...])
    o_ref[...] = x2
    for c in range(d // LANE):
        ot_ref[pl.ds(c, tm, stride=SUBLANE), :] = x2[:, c * LANE:(c + 1) * LANE]
    lg_ref[...] = _dot_nt(wr_ref[...], x2)


def _xattn(x2d, y_ret, y_dil, w_out, g1, b1, kv, wq, wo, g, bb, w_r, alpha, seq, tm=256):
    t, d = x2d.shape
    mlen = kv.shape[1]
    tiles_per_seq = seq // tm
    row = lambda i: (i, 0)
    const = lambda i: (0, 0)
    vec = pl.BlockSpec((1, d), const)
    return pl.pallas_call(
        functools.partial(_xattn_kernel, alpha),
        grid=(t // tm,),
        in_specs=[pl.BlockSpec((tm, d), row),
                  pl.BlockSpec((tm, y_ret.shape[1]), row),
                  pl.BlockSpec((tm, y_dil.shape[1]), row),
                  pl.BlockSpec(w_out.shape, const), vec, vec,
                  pl.BlockSpec((1, mlen, d), lambda i: (i // tiles_per_seq, 0, 0)),
                  pl.BlockSpec((1, mlen, d), lambda i: (i // tiles_per_seq, 0, 1)),
                  pl.BlockSpec((d, d), const), pl.BlockSpec((d, d), const), vec, vec,
                  pl.BlockSpec((ROUTE_ROWS, d), const)],
        out_specs=[pl.BlockSpec((tm, d), row),
                   pl.BlockSpec((tm * SUBLANE, LANE), row),
                   pl.BlockSpec((ROUTE_ROWS, tm), lambda i: (0, i))],
        out_shape=[jax.ShapeDtypeStruct((t, d), F32),
                   jax.ShapeDtypeStruct((t * SUBLANE, LANE), F32),
                   jax.ShapeDtypeStruct((ROUTE_ROWS, t), F32)],
        scratch_shapes=[pltpu.VMEM((tm, d), F32)],
        compiler_params=_params("parallel"),
        name="xattn",
    )(x2d, y_ret, y_dil, w_out, g1, b1, kv, kv, wq, wo, g, bb, w_r)


def _route_kernel(lg_ref, before_ref, meta_ref, cnt_ref, carry_ref):
    i = pl.program_id(0)

    @pl.when(i == 0)
    def _():
        carry_ref[...] = jnp.zeros_like(carry_ref)

    rows, tt = lg_ref.shape
    sub = before_ref.shape[0]
    r = lax.broadcasted_iota(jnp.int32, (rows, sub), 0)
    r8 = lax.broadcasted_iota(jnp.int32, (SUBLANE, sub), 0)
    is_group = r < N_GROUPS

    def col_max(a):
        return jnp.max(a, axis=0, keepdims=True)

    def first_row_where(mask):
        return jnp.min(jnp.where(mask, r, rows), axis=0, keepdims=True)

    for c in range(tt // sub):
        lg = lg_ref[:, c * sub:(c + 1) * sub]
        mg = col_max(jnp.where(is_group, lg, NEG))
        eg = jnp.where(is_group, jnp.exp(lg - mg), 0.0)
        pg = eg / jnp.sum(eg, axis=0, keepdims=True)
        g1 = col_max(pg)
        gi = first_row_where(is_group & (pg == g1))
        lo = N_GROUPS + gi * EXPERTS_PER_GROUP
        in_grp = (r >= lo) & (r < lo + EXPERTS_PER_GROUP)
        v1 = col_max(jnp.where(in_grp, lg, NEG))
        i1 = first_row_where(in_grp & (lg == v1))
        rest = in_grp & (r != i1)
        v2 = col_max(jnp.where(rest, lg, NEG))
        i2 = first_row_where(rest & (lg == v2))
        e2 = jnp.exp(v2 - v1)
        den = 1.0 + e2
        gate1 = g1 * (1.0 / den)
        gate2 = g1 * (e2 / den)
        sel1 = r == i1
        sel2 = r == i2
        onehot = jnp.where(sel1 | sel2, 1.0, 0.0)
        rank = _dot(onehot, before_ref[...]) + carry_ref[:, 0:1]
        r1 = jnp.sum(jnp.where(sel1, rank, 0.0), axis=0, keepdims=True)
        r2 = jnp.sum(jnp.where(sel2, rank, 0.0), axis=0, keepdims=True)
        carry_ref[...] = carry_ref[...] + jnp.sum(onehot, axis=1, keepdims=True)
        meta = jnp.where(r8 == 0, (i1 - N_GROUPS).astype(F32), 0.0)
        meta = jnp.where(r8 == 1, (i2 - N_GROUPS).astype(F32), meta)
        meta = jnp.where(r8 == 2, r1, meta)
        meta = jnp.where(r8 == 3, r2, meta)
        meta = jnp.where(r8 == 4, gate1, meta)
        meta = jnp.where(r8 == 5, gate2, meta)
        meta_ref[:, c * sub:(c + 1) * sub] = meta
    cnt_ref[...] = carry_ref[...]


def _route(logits_t, tt=1024, sub=256):
    rows, t = logits_t.shape
    before = (jnp.arange(sub)[:, None] < jnp.arange(sub)[None, :]).astype(BF16)
    return pl.pallas_call(
        _route_kernel,
        grid=(t // tt,),
        in_specs=[pl.BlockSpec((rows, tt), lambda i: (0, i)),
                  pl.BlockSpec((sub, sub), lambda i: (0, 0))],
        out_specs=[pl.BlockSpec((SUBLANE, tt), lambda i: (0, i)),
                   pl.BlockSpec((rows, LANE), lambda i: (0, 0))],
        out_shape=[jax.ShapeDtypeStruct((SUBLANE, t), F32),
                   jax.ShapeDtypeStruct((rows, LANE), F32)],
        scratch_shapes=[pltpu.VMEM((rows, LANE), F32)],
        compiler_params=_params("arbitrary"),
        name="route",
    )(logits_t, before)


def _plan_kernel(meta_ref, first_row_ref, dest_ref):
    n_tiles = dest_ref.shape[0]
    tt = dest_ref.shape[2] // 2
    rows = first_row_ref.shape[0]
    first_row = first_row_ref[:, 0:1]
    r = lax.broadcasted_iota(jnp.int32, (rows, tt), 0)
    for g in range(n_tiles):
        m = meta_ref[:, g * tt:(g + 1) * tt]

        def dest_of(k):
            e_row = m[k:k + 1, :].astype(jnp.int32) + N_GROUPS
            return jnp.sum(jnp.where(r == e_row, first_row, 0.0), axis=0, keepdims=True) + m[2 + k:3 + k, :]

        dest_ref[g] = jnp.concatenate([dest_of(0), dest_of(1)], axis=1).astype(jnp.int32)


def _plan(meta_t, first_row, tt, tiles_per_step=4):
    t = meta_t.shape[1]
    rows = first_row.shape[0]
    return pl.pallas_call(
        _plan_kernel,
        grid=(t // (tt * tiles_per_step),),
        in_specs=[pl.BlockSpec((SUBLANE, tt * tiles_per_step), lambda i: (0, i)),
                  pl.BlockSpec((rows, LANE), lambda i: (0, 0))],
        out_specs=pl.BlockSpec((tiles_per_step, 1, 2 * tt), lambda i: (i, 0, 0)),
        out_shape=jax.ShapeDtypeStruct((t // tt, 1, 2 * tt), jnp.int32),
        compiler_params=_params("parallel"),
        name="plan",
    )(meta_t, first_row)


def _row_copy(src, dst, s_row, d_row, sem):
    return pltpu.make_async_copy(src.at[pl.ds(pl.multiple_of(s_row * SUBLANE, SUBLANE), SUBLANE), :],
                                 dst.at[pl.ds(pl.multiple_of(d_row * SUBLANE, SUBLANE), SUBLANE), :], sem)


def _dispatch_kernel(dest_ref, xt_ref, xs_ref, sem):
    tq = dest_ref.shape[2] // 2

    def issue(t, c):
        _row_copy(xt_ref, xs_ref, t, dest_ref[0, 0, t], sem).start(priority=0)
        _row_copy(xt_ref, xs_ref, t, dest_ref[0, 0, tq + t], sem).start(priority=1)
        return c

    lax.fori_loop(0, tq, issue, 0, unroll=8)
    for _ in range(2):
        pltpu.make_async_copy(xt_ref, xs_ref.at[pl.ds(0, tq * SUBLANE), :], sem).wait()


def _dispatch(dest3, x_tiled, n_rows, tiles_per_step=4):
    n_tiles, _, two_tt = dest3.shape
    tt = two_tt // 2
    tq = tiles_per_step * tt
    n_steps = n_tiles // tiles_per_step
    dest3 = dest3.reshape(n_steps, tiles_per_step, 2, tt).transpose(0, 2, 1, 3).reshape(n_steps, 1, 2 * tq)
    return pl.pallas_call(
        _dispatch_kernel,
        grid=(n_steps,),
        in_specs=[pl.BlockSpec((1, 1, 2 * tq), lambda i: (i, 0, 0), memory_space=pltpu.SMEM),
                  pl.BlockSpec((tq * SUBLANE, LANE), lambda i: (i, 0))],
        out_specs=pl.BlockSpec(memory_space=pl.ANY),
        out_shape=jax.ShapeDtypeStruct((n_rows * SUBLANE, LANE), F32),
        scratch_shapes=[pltpu.SemaphoreType.DMA(())],
        compiler_params=_params("arbitrary"),
        name="dispatch",
    )(dest3, x_tiled)


def _experts_kernel(blk_exp_ref, blk_rows_ref, n_used_ref, blk_first_ref, blk_slot_ref, blk_next_ref,
                    xs_ref, wg_hbm, wu_hbm, wd_hbm, ys_ref, x_scr, y_scr, wg_buf, wu_buf, wd_buf, sems):
    b = pl.program_id(0)

    def weight_copies(e, slot):
        return [pltpu.make_async_copy(hbm.at[e], buf.at[slot], sems.at[k, slot])
                for k, (hbm, buf) in enumerate(((wg_hbm, wg_buf), (wu_hbm, wu_buf), (wd_hbm, wd_buf)))]

    @pl.when(b < n_used_ref[0])
    def _():
        slot = blk_slot_ref[b]

        @pl.when(b == 0)
        def _():
            for cp in weight_copies(blk_exp_ref[0], slot):
                cp.start()

        @pl.when(blk_first_ref[b] == 1)
        def _():
            @pl.when(blk_next_ref[b] >= 0)
            def _():
                for cp in weight_copies(blk_next_ref[b], 1 - slot):
                    cp.start()

            for cp in weight_copies(blk_exp_ref[b], slot):
                cp.wait()

        rows, d = x_scr.shape
        for c in range(d // LANE):
            x_scr[:, c * LANE:(c + 1) * LANE] = xs_ref[pl.ds(c, rows, stride=SUBLANE), :]
        ri = lax.broadcasted_iota(jnp.int32, (rows, 1), 0)
        x = jnp.where(ri < blk_rows_ref[b], x_scr[...], 0.0)
        hg = _dot(x, wg_buf[slot])
        hu = _dot(x, wu_buf[slot])
        y_scr[...] = _dot(hg * jax.nn.sigmoid(hg) * hu, wd_buf[slot])
        for c in range(d // LANE):
            ys_ref[pl.ds(c, rows, stride=SUBLANE), :] = y_scr[:, c * LANE:(c + 1) * LANE]


def _experts(blk_exp, blk_rows, n_used, blk_first, blk_slot, blk_next, xs_tiled, w_g, w_u, w_d):
    n_rows = xs_tiled.shape[0] // SUBLANE
    rows = EXPERT_ROWS
    nblk = n_rows // rows
    _, d, ff = w_g.shape
    used = lambda b, be, br, nu, *_: (jnp.minimum(b, nu[0] - 1), 0)
    grid_spec = pltpu.PrefetchScalarGridSpec(
        num_scalar_prefetch=6,
        grid=(nblk,),
        in_specs=[pl.BlockSpec((rows * SUBLANE, LANE), used),
                  pl.BlockSpec(memory_space=pl.ANY), pl.BlockSpec(memory_space=pl.ANY),
                  pl.BlockSpec(memory_space=pl.ANY)],
        out_specs=pl.BlockSpec((rows * SUBLANE, LANE), used),
        scratch_shapes=[pltpu.VMEM((rows, d), F32), pltpu.VMEM((rows, d), F32),
                        pltpu.VMEM((2, d, ff), F32), pltpu.VMEM((2, d, ff), F32), pltpu.VMEM((2, ff, d), F32),
                        pltpu.SemaphoreType.DMA((3, 2))],
    )
    return pl.pallas_call(
        _experts_kernel,
        grid_spec=grid_spec,
        out_shape=jax.ShapeDtypeStruct((n_rows * SUBLANE, LANE), F32),
        compiler_params=_params("arbitrary"),
        name="experts",
    )(blk_exp, blk_rows, n_used, blk_first, blk_slot, blk_next, xs_tiled, w_g, w_u, w_d)


def _combine_kernel(alpha, dcur_ref, dnext_ref, meta_ref, x_ref, g_ref, b_ref, ys_ref, o_ref, buf_ref, sems):
    i = pl.program_id(0)
    tq, d = x_ref.shape
    slot_rows = 2 * tq * SUBLANE

    def issue(d_ref, slot, t):
        _row_copy(ys_ref, buf_ref, d_ref[0, 0, t], slot * (2 * tq) + t, sems.at[slot]).start(priority=0)
        _row_copy(ys_ref, buf_ref, d_ref[0, 0, tq + t], slot * (2 * tq) + tq + t, sems.at[slot]).start(priority=1)

    def wait_slot(slot):
        off = slot * slot_rows
        pltpu.make_async_copy(ys_ref.at[pl.ds(0, slot_rows), :], buf_ref.at[pl.ds(off, slot_rows), :],
                              sems.at[slot]).wait()
        return off

    last = pl.num_programs(0) - 1

    @pl.when(i == 0)
    def _():
        lax.fori_loop(0, tq, lambda t, c: (issue(dcur_ref, 0, t), c)[1], 0, unroll=4)

    def tile(slot):
        off = wait_slot(slot)
        meta_rows = jnp.transpose(jnp.concatenate([meta_ref[...], jnp.zeros((LANE - SUBLANE, tq), F32)], axis=0))
        gate1 = meta_rows[:, 4:5]
        gate2 = meta_rows[:, 5:6]
        n_chunks = d // LANE
        for c in range(n_chunks):
            for t in range(c * tq // n_chunks, (c + 1) * tq // n_chunks):
                issue(dnext_ref, 1 - slot, t)
            sl = slice(c * LANE, (c + 1) * LANE)
            y1 = buf_ref[pl.ds(off + c, tq, stride=SUBLANE), :]
            y2 = buf_ref[pl.ds(off + tq * SUBLANE + c, tq, stride=SUBLANE), :]
            o_ref[:, sl] = alpha * x_ref[:, sl] + (y1 * gate1 + y2 * gate2)
        o_ref[...] = _layer_norm(o_ref[...], g_ref[...], b_ref[...])

        @pl.when(i == last)
        def _():
            wait_slot(1 - slot)

    for slot in range(2):
        pl.when(i % 2 == slot)(functools.partial(tile, slot))


def _combine(dest3, meta, x2, g, bb, ys_tiled, alpha):
    t, d = x2.shape
    n = dest3.shape[0]
    tq = t // n
    row = lambda i: (i, 0)
    const = lambda i: (0, 0)
    return pl.pallas_call(
        functools.partial(_combine_kernel, alpha),
        grid=(n,),
        in_specs=[pl.BlockSpec((1, 1, 2 * tq), lambda i: (i, 0, 0), memory_space=pltpu.SMEM),
                  pl.BlockSpec((1, 1, 2 * tq), lambda i: (jnp.minimum(i + 1, n - 1), 0, 0), memory_space=pltpu.SMEM),
                  pl.BlockSpec((SUBLANE, tq), lambda i: (0, i)),
                  pl.BlockSpec((tq, d), row),
                  pl.BlockSpec((1, d), const), pl.BlockSpec((1, d), const),
                  pl.BlockSpec(memory_space=pl.ANY)],
        out_specs=pl.BlockSpec((tq, d), row),
        out_shape=jax.ShapeDtypeStruct((t, d), F32),
        scratch_shapes=[pltpu.VMEM((2 * 2 * tq * SUBLANE, LANE), F32), pltpu.SemaphoreType.DMA((2,))],
        compiler_params=_params("arbitrary"),
        name="combine",
    )(dest3, dest3, meta, x2, g, bb, ys_tiled)


def _moe(x2, x2_tiled, logits, w_g, w_u, w_d, g, bb, alpha):
    t, d = x2.shape
    meta, cnt = _route(logits)
    counts = cnt[N_GROUPS:N_GROUPS + N_EXPERTS, 0].astype(jnp.int32)
    padded = ((counts + EXPERT_ROWS - 1) // EXPERT_ROWS) * EXPERT_ROWS
    pend = jnp.cumsum(padded)
    poff = pend - padded
    first_row = jnp.pad(poff.astype(F32), (N_GROUPS, ROUTE_ROWS - N_GROUPS - N_EXPERTS))
    dest3 = _plan(meta, jnp.broadcast_to(first_row[:, None], (ROUTE_ROWS, LANE)), ROUTE_TILE)
    n_rows = t * 2 + N_EXPERTS * EXPERT_ROWS
    nblk = n_rows // EXPERT_ROWS
    blk_start = jnp.arange(nblk, dtype=jnp.int32) * EXPERT_ROWS
    blk_exp = jnp.sum((pend[None, :] <= blk_start[:, None]).astype(jnp.int32), axis=1)
    blk_exp = jnp.minimum(blk_exp, N_EXPERTS - 1)
    e_idx = jnp.arange(N_EXPERTS, dtype=jnp.int32)
    owner = blk_exp[:, None] == e_idx[None, :]

    def of_block(per_expert):
        return jnp.sum(jnp.where(owner, per_expert[None, :], 0), axis=1).astype(jnp.int32)

    blk_rows = jnp.clip(of_block(poff + counts) - blk_start, 0, EXPERT_ROWS).astype(jnp.int32)
    n_used = (pend[-1:] // EXPERT_ROWS).astype(jnp.int32)
    has_rows = counts > 0
    ordinal = jnp.cumsum(has_rows.astype(jnp.int32)) - 1
    later = has_rows[None, :] & (e_idx[None, :] > e_idx[:, None])
    next_used = jnp.min(jnp.where(later, e_idx[None, :], N_EXPERTS), axis=1)
    next_used = jnp.where(next_used == N_EXPERTS, -1, next_used).astype(jnp.int32)
    blk_first = (blk_start == of_block(poff)).astype(jnp.int32)
    blk_slot = of_block(ordinal % 2)
    blk_next = of_block(next_used)
    xs_tiled = _dispatch(dest3, x2_tiled, n_rows)
    ys_tiled = _experts(blk_exp, blk_rows, n_used, blk_first, blk_slot, blk_next, xs_tiled, w_g, w_u, w_d)
    return _combine(dest3, meta, x2, g, bb, ys_tiled, alpha)


def kernel(x, mem, w_in, w_out, ln_mix_g, ln_mix_b, w_xq, w_xkv, w_xo, ln_x_g, ln_x_b, w_route_group,
           w_route_expert, w_exp_gate, w_exp_up, w_exp_down, ln_moe_g, ln_moe_b):
    b, s, d = x.shape
    depth = w_in.shape[0]
    alpha = (2.0 * depth) ** 0.25
    t = b * s
    ret_cols = (2 * RET_HEADS * RET_QK_DIM + 2 * RET_HEADS * RET_V_DIM)
    assert ret_cols % (DIL_HEADS * DIL_HEAD_DIM) == 0
    dil_col0 = ret_cols // (DIL_HEADS * DIL_HEAD_DIM)
    xc = x.reshape(t, d)
    for l in range(depth):
        n_qk = 2 * RET_HEADS * RET_QK_DIM
        w_in_l = jnp.concatenate([_retention_weight_layout(w_in[l][:, :n_qk]), w_in[l][:, n_qk:]], axis=1)
        h = _proj_in(xc, w_in_l.astype(BF16))
        h3 = h.reshape(b, s, h.shape[1])
        y_ret = _retention(h3).reshape(t, -1)
        y_dil = _dilated(h3, dil_col0).reshape(t, -1)
        kv =_mem_kv(mem.reshape(b * mem.shape[1], d), w_xkv[l].astype(BF16)).reshape(b, mem.shape[1], 2 * d)
        w_r = jnp.concatenate([w_route_group[l], w_route_expert[l]], axis=-1)
        w_r = jnp.pad(w_r.T, ((0, ROUTE_ROWS - w_r.shape[1]), (0, 0))).astype(BF16)
        x2, x2_tiled, logits = _xattn(xc, y_ret, y_dil, w_out[l].astype(BF16), ln_mix_g[l][None], ln_mix_b[l][None],
                                      kv, w_xq[l].astype(BF16), w_xo[l].astype(BF16),
                                      ln_x_g[l][None], ln_x_b[l][None], w_r, alpha, s)
        xc = _moe(x2, x2_tiled, logits, w_exp_gate[l], w_exp_up[l], w_exp_down[l],
                  ln_moe_g[l][None], ln_moe_b[l][None], alpha)
    return xc.reshape(b, s, d)
```

```python
import functools

import jax
import jax.numpy as jnp
from jax import lax
from jax.experimental import pallas as pl
from jax.experimental.pallas import tpu as pltpu

BF16 = jnp.bfloat16
F32 = jnp.float32

LANE = 128
SUBLANE = 8
VMEM_LIMIT = 56 * 1024 * 1024

RET_HEADS = 4
RET_QK_DIM = 64
RET_V_DIM = 128
RET_CHUNK = 128
RET_STEP_CHUNKS = 2
ROPE_BASE = 10000.0
DIL_HEADS = 8
DIL_HEAD_DIM = 64
DIL_DILATIONS = (1, 4, 16)
DIL_BLOCK = 128
DIL_SUPER = DIL_BLOCK * max(DIL_DILATIONS)
DIL_GROUP = 1
XATTN_HEADS = 4
XATTN_PARTS = 2
N_GROUPS = 4
EXPERTS_PER_GROUP = 8
N_EXPERTS = N_GROUPS * EXPERTS_PER_GROUP
ROUTE_ROWS = -(-(N_GROUPS + N_EXPERTS) // SUBLANE) * SUBLANE
EXPERT_ROWS = 512
EXPERT_QUANTUM = 256
ROUTE_TILE = 256
LN_EPS = 1e-5
GN_EPS = 1e-6
NEG = -1e30


def _params(*sem):
    return pltpu.CompilerParams(dimension_semantics=sem, vmem_limit_bytes=VMEM_LIMIT)


def _layer_norm(z, g, b):
    mu = jnp.mean(z, axis=-1, keepdims=True)
    zc = z - mu
    var = jnp.mean(zc * zc, axis=-1, keepdims=True)
    return zc * lax.rsqrt(var + LN_EPS) * g + b


def _dot(a, b):
    return jnp.dot(a.astype(BF16), b.astype(BF16), preferred_element_type=F32)


def _dot_nt(a, b):
    return lax.dot_general(a.astype(BF16), b.astype(BF16), (((1,), (1,)), ((), ())),
                           preferred_element_type=F32)


def _dot_tn(a, b):
    return lax.dot_general(a.astype(BF16), b.astype(BF16), (((0,), (0,)), ((), ())),
                           preferred_element_type=F32)


def _proj_in_kernel(x_ref, w_ref, o_ref):
    o_ref[...] = _dot(x_ref[...], w_ref[...]).astype(o_ref.dtype)


def _proj_in(x2d, w_bf16, tm=512):
    t, d = x2d.shape
    n = w_bf16.shape[1]
    return pl.pallas_call(
        _proj_in_kernel,
        grid=(t // tm,),
        in_specs=[pl.BlockSpec((tm, d), lambda i: (i, 0)),
                  pl.BlockSpec((d, n), lambda i: (0, 0))],
        out_specs=pl.BlockSpec((tm, n), lambda i: (i, 0)),
        out_shape=jax.ShapeDtypeStruct((t, n), F32),
        compiler_params=_params("parallel"),
        name="proj_in",
    )(x2d, w_bf16)


def _retention_kernel(qk_ref, v_ref, g_ref, cos_ref, sin_ref, decay_ref, zeta_ref, xi_ref, gam_ref,
                      o_ref, state_ref, y_ref):
    n = pl.program_id(1)

    @pl.when(n == 0)
    def _():
        state_ref[...] = jnp.zeros_like(state_ref)

    c = RET_CHUNK
    n_sub = qk_ref.shape[1] // c
    half = RET_QK_DIM // 2
    lane = lax.broadcasted_iota(jnp.int32, (c, LANE), 1)

    rotated = []
    for j in range(n_sub):
        rows = slice(j * c, (j + 1) * c)
        cos = cos_ref[rows, :]
        sin = sin_ref[rows, :]

        def rot(col, rows=rows, cos=cos, sin=sin):
            t1 = qk_ref[0, rows, col * LANE:(col + 1) * LANE]
            t2 = qk_ref[0, rows, (col + 1) * LANE:(col + 2) * LANE]
            return t1 * cos - t2 * sin, t1 * sin + t2 * cos

        q1, q2 = rot(0)
        k1, k2 = (t * (RET_QK_DIM ** -0.5) for t in rot(2))
        rotated.append((q1, q2, k1, k2, jnp.concatenate([k1, k2], axis=1)))

    for h in range(RET_HEADS):
        cols = slice(h * RET_V_DIM, (h + 1) * RET_V_DIM)
        mine = (lane >= h * half) & (lane < (h + 1) * half)
        zeta = zeta_ref[h]
        st = state_ref[h]
        for j in range(n_sub):
            rows = slice(j * c, (j + 1) * c)
            q1, q2, k1, k2, k_all = rotated[j]
            qm = jnp.concatenate([jnp.where(mine, q1, 0.0), jnp.where(mine, q2, 0.0)], axis=1)
            kz = jnp.concatenate([jnp.where(mine, k1, 0.0) * zeta, jnp.where(mine, k2, 0.0) * zeta], axis=1)
            v = v_ref[0, rows, cols]
            s = _dot_nt(qm, k_all) * decay_ref[h]
            y_ref[rows, cols] = _dot(s, v) + _dot(qm, st) * xi_ref[h]
            st = gam_ref[h] * st + _dot_tn(kz, v)
        state_ref[h] = st

    for h in range(RET_HEADS):
        cols = slice(h * RET_V_DIM, (h + 1) * RET_V_DIM)
        y = y_ref[:, cols]
        mu = jnp.mean(y, axis=-1, keepdims=True)
        yc = y - mu
        var = jnp.mean(yc * yc, axis=-1, keepdims=True)
        yn = yc * lax.rsqrt(var + GN_EPS)
        gate = g_ref[0, :, cols]
        o_ref[0, :, cols] = gate * jax.nn.sigmoid(gate) * yn


def _retention_tables(s):
    half = RET_QK_DIM // 2
    inv = 1.0 / (ROPE_BASE ** (jnp.arange(half, dtype=F32) / half))
    ang = jnp.arange(s, dtype=F32)[:, None] * inv[None, :]
    cos_t = jnp.tile(jnp.cos(ang), (1, RET_HEADS))
    sin_t = jnp.tile(jnp.sin(ang), (1, RET_HEADS))
    c = RET_CHUNK
    lg = jnp.log(1.0 - jnp.exp2(-5.0 - jnp.arange(RET_HEADS, dtype=F32)))
    idx = jnp.arange(c, dtype=F32)
    diff = idx[:, None] - idx[None, :]
    decay = jnp.where(diff >= 0, jnp.exp(lg[:, None, None] * jnp.maximum(diff, 0.0)), 0.0)
    lanes = (RET_HEADS, c, LANE)
    zeta = jnp.broadcast_to(jnp.exp(lg[:, None] * (c - 1.0 - idx))[:, :, None], lanes)
    xi = jnp.broadcast_to(jnp.exp(lg[:, None] * (idx + 1.0))[:, :, None], lanes)
    gam = jnp.broadcast_to(jnp.exp(lg * c)[:, None, None], (RET_HEADS, 1, LANE))
    return cos_t, sin_t, decay, zeta, xi, gam


def _retention_weight_layout(w_qk):
    d = w_qk.shape[0]
    half = RET_QK_DIM // 2
    return w_qk.reshape(d, 2, RET_HEADS, 2, half).transpose(0, 1, 3, 2, 4).reshape(d, -1)


def _retention(h3):
    b, s, _ = h3.shape
    c = RET_CHUNK
    qk_w = 2 * RET_HEADS * RET_QK_DIM
    v_w = RET_HEADS * RET_V_DIM
    assert qk_w == v_w
    cos_t, sin_t, decay, zeta, xi, gam = _retention_tables(s)
    const3 = lambda bi, n: (0, 0, 0)
    rows = RET_STEP_CHUNKS * c
    return pl.pallas_call(
        _retention_kernel,
        grid=(b, s // rows),
        in_specs=[pl.BlockSpec((1, rows, qk_w), lambda bi, n: (bi, n, 0)),
                  pl.BlockSpec((1, rows, v_w), lambda bi, n: (bi, n, 1)),
                  pl.BlockSpec((1, rows, v_w), lambda bi, n: (bi, n, 2)),
                  pl.BlockSpec((rows, LANE), lambda bi, n: (n, 0)),
                  pl.BlockSpec((rows, LANE), lambda bi, n: (n, 0)),
                  pl.BlockSpec((RET_HEADS, c, c), const3),
                  pl.BlockSpec((RET_HEADS, c, LANE), const3),
                  pl.BlockSpec((RET_HEADS, c, LANE), const3),
                  pl.BlockSpec((RET_HEADS, 1, LANE), const3)],
        out_specs=pl.BlockSpec((1, rows, v_w), lambda bi, n: (bi, n, 0)),
        out_shape=jax.ShapeDtypeStruct((b, s, v_w), F32),
        scratch_shapes=[pltpu.VMEM((RET_HEADS, RET_HEADS * RET_QK_DIM, RET_V_DIM), F32),
                        pltpu.VMEM((rows, v_w), F32)],
        compiler_params=_params("parallel", "arbitrary"),
        name="retention",
    )(h3, h3, h3, cos_t, sin_t, decay, zeta, xi, gam)


def _dilated_kernel(q_ref, kp_ref, kc_ref, vp_ref, vc_ref, bias_ref, o_ref, kk_ref, vv_ref, acc_ref, m_ref, l_ref,
                    s0_ref, s1_ref, p0_ref, p1_ref):
    j = pl.program_id(2)
    sup = DIL_SUPER
    q_blk = DIL_BLOCK
    kk_ref[0:sup, :] = kp_ref[0]
    kk_ref[sup:2 * sup, :] = kc_ref[0]
    vv_ref[0:sup, :] = vp_ref[0]
    vv_ref[sup:2 * sup, :] = vc_ref[0]
    scale = DIL_HEAD_DIM ** -0.5
    n_blocks = sup // q_blk
    lane = lax.broadcasted_iota(jnp.int32, (q_blk, LANE), 1)
    head0 = lane < DIL_HEAD_DIM
    first_bias = jnp.where(j == 0, 1, 0)

    groups = []
    for bi, d in enumerate(DIL_DILATIONS):
        n_per_r = sup // (q_blk * d)
        for t0 in range(0, n_blocks, DIL_GROUP):
            blocks = []
            for t in range(t0, t0 + DIL_GROUP):
                r, n = divmod(t, n_per_r)
                blocks.append((n * (q_blk * d) + r, sup + (n - 1) * (q_blk * d) + r, n))
            groups.append((bi, d, blocks))
    s_bufs = (s0_ref, s1_ref)
    p_bufs = (p0_ref, p1_ref)

    def scores(gi):
        bi, d, blocks = groups[gi]
        for g, (q_start, k_start, n) in enumerate(blocks):
            q = q_ref[0, pl.ds(q_start, q_blk, stride=d), :] * scale
            kb = kk_ref[pl.ds(k_start, 2 * q_blk, stride=d), :]
            q2 = jnp.concatenate([jnp.where(head0, q, 0.0), jnp.where(head0, 0.0, q)], axis=0)
            bias = bias_ref[first_bias] if n == 0 else bias_ref[0]
            s_bufs[gi % 2][g] = _dot_nt(q2, kb) + bias

    def softmax(gi):
        bi, d, blocks = groups[gi]
        for g, (q_start, k_start, n) in enumerate(blocks):
            s = s_bufs[gi % 2][g]
            m2 = jnp.max(s, axis=-1, keepdims=True)
            p_bufs[gi % 2][g] = jnp.exp(s - m2).astype(BF16)
            m_ref[bi, pl.ds(q_start, q_blk, stride=d), :] = jnp.where(head0, m2[:q_blk], m2[q_blk:])

    def values(gi):
        bi, d, blocks = groups[gi]
        for g, (q_start, k_start, n) in enumerate(blocks):
            vb = vv_ref[pl.ds(k_start, 2 * q_blk, stride=d), :].astype(BF16)
            o2 = jnp.dot(p_bufs[gi % 2][g], jnp.concatenate([vb, jnp.ones_like(vb)], axis=1),
                         preferred_element_type=F32)
            rows = pl.ds(q_start, q_blk, stride=d)
            acc_ref[bi, rows, :] = jnp.where(head0, o2[:q_blk, :LANE], o2[q_blk:, :LANE])
            l_ref[bi, rows, :] = jnp.where(head0, o2[:q_blk, LANE:], o2[q_blk:, LANE:])

    for step in range(len(groups) + 2):
        if step < len(groups):
            scores(step)
        if 0 <= step - 1 < len(groups):
            softmax(step - 1)
        if step - 2 >= 0:
            values(step - 2)

    def merge(c, carry):
        rows = pl.ds(pl.multiple_of(c * q_blk, q_blk), q_blk)
        ms = [m_ref[bi, rows, :] for bi in range(len(DIL_DILATIONS))]
        m_all = functools.reduce(jnp.maximum, ms)
        ws = [jnp.exp(m - m_all) for m in ms]
        num = functools.reduce(lambda a, b: a + b, [w * acc_ref[bi, rows, :] for bi, w in enumerate(ws)])
        den = functools.reduce(lambda a, b: a + b, [w * l_ref[bi, rows, :] for bi, w in enumerate(ws)])
        o_ref[0, rows, :] = num / den
        return carry

    lax.fori_loop(0, n_blocks, merge, 0)


def _dilated_bias():
    q_blk = DIL_BLOCK
    qi = jnp.arange(2 * q_blk)[:, None] % q_blk
    kj = jnp.arange(2 * q_blk)[None, :]
    band = (kj >= qi) & (kj <= qi + q_blk)
    return jnp.stack([jnp.where(band, 0.0, NEG), jnp.where(band & (kj >= q_blk), 0.0, NEG)]).astype(F32)


def _dilated(h3, col0):
    b, s, _ = h3.shape
    sup = DIL_SUPER
    n_pairs = DIL_HEADS * DIL_HEAD_DIM // LANE
    n_br = len(DIL_DILATIONS)
    cq, ck, cv = col0 * n_pairs, (col0 + 1) * n_pairs, (col0 + 2) * n_pairs
    cur = lambda c: (lambda bi, p, j: (bi, j, c + p))
    prev = lambda c: (lambda bi, p, j: (bi, jnp.maximum(j - 1, 0), c + p))
    blk = (1, sup, LANE)
    return pl.pallas_call(
        _dilated_kernel,
        grid=(b, n_pairs, s // sup),
        in_specs=[pl.BlockSpec(blk, cur(cq)),
                  pl.BlockSpec(blk, prev(ck)), pl.BlockSpec(blk, cur(ck)),
                  pl.BlockSpec(blk, prev(cv)), pl.BlockSpec(blk, cur(cv)),
                  pl.BlockSpec((2, 2 * DIL_BLOCK, 2 * DIL_BLOCK), lambda bi, p, j: (0, 0, 0))],
        out_specs=pl.BlockSpec(blk, lambda bi, p, j: (bi, j, p)),
        out_shape=jax.ShapeDtypeStruct((b, s, n_pairs * LANE), F32),
        scratch_shapes=[pltpu.VMEM((2 * sup, LANE), F32), pltpu.VMEM((2 * sup, LANE), F32),
                        pltpu.VMEM((n_br, sup, LANE), F32), pltpu.VMEM((n_br, sup, LANE), F32),
                        pltpu.VMEM((n_br, sup, LANE), F32),
                        pltpu.VMEM((DIL_GROUP, 2 * DIL_BLOCK, 2 * DIL_BLOCK), F32),
                        pltpu.VMEM((DIL_GROUP, 2 * DIL_BLOCK, 2 * DIL_BLOCK), F32),
                        pltpu.VMEM((DIL_GROUP, 2 * DIL_BLOCK, 2 * DIL_BLOCK), BF16),
                        pltpu.VMEM((DIL_GROUP, 2 * DIL_BLOCK, 2 * DIL_BLOCK), BF16)],
        compiler_params=_params("parallel", "parallel", "arbitrary"),
        name="dilated",
    )(h3, h3, h3, h3, h3, _dilated_bias())


def _mem_kv(mem2d, w_bf16, tn=512):
    m, d = mem2d.shape
    n = w_bf16.shape[1]
    return pl.pallas_call(
        _proj_in_kernel,
        grid=(n // tn,),
        in_specs=[pl.BlockSpec((m, d), lambda i: (0, 0)),
                  pl.BlockSpec((d, tn), lambda i: (0, i))],
        out_specs=pl.BlockSpec((m, tn), lambda i: (0, i)),
        out_shape=jax.ShapeDtypeStruct((m, n), BF16),
        compiler_params=_params("parallel"),
        name="mem_kv",
    )(mem2d, w_bf16)


def _xattn_kernel(alpha, x_ref, yr_ref, yd_ref, wout_ref, g1_ref, b1_ref, k_ref, v_ref, wq_ref, wo_ref,
                  g_ref, b_ref, wr_ref, o_ref, ot_ref, lg_ref, x1_ref, q_ref, att_ref):
    tm, d = x_ref.shape
    dh = d // XATTN_HEADS
    wr = yr_ref.shape[1]
    scale = dh ** -0.5
    part = tm // XATTN_PARTS

    def mix(rows):
        y = _dot(yr_ref[rows, :], wout_ref[0:wr, :]) + _dot(yd_ref[rows, :], wout_ref[wr:, :])
        x1_ref[rows, :] = _layer_norm(alpha * x_ref[rows, :] + y, g1_ref[...], b1_ref[...])

    def query(rows):
        q_ref[rows, :] = _dot(x1_ref[rows, :], wq_ref[...]).astype(BF16)

    def attend(rows):
        for h in range(XATTN_HEADS):
            sl = slice(h * dh, (h + 1) * dh)
            s = _dot_nt(q_ref[rows, sl], k_ref[0, :, sl]) * scale
            m = jnp.max(s, axis=-1, keepdims=True)
            e = jnp.exp(s - m)
            p = e / jnp.sum(e, axis=-1, keepdims=True)
            att_ref[rows, sl] = _dot(p, v_ref[0, :, sl]).astype(BF16)

    def finish(rows, r0):
        y = _dot(att_ref[rows, :], wo_ref[...])
        x2 = _layer_norm(alpha * x1_ref[rows, :] + y, g_ref[...], b_ref[...])
        o_ref[rows, :] = x2
        for c in range(d // LANE):
            ot_ref[pl.ds(r0 * SUBLANE + c, part, stride=SUBLANE), :] = x2[:, c * LANE:(c + 1) * LANE]
        lg_ref[:, rows] = _dot_nt(wr_ref[...], x2)

    stages = (mix, query, attend, finish)
    for step in range(XATTN_PARTS + len(stages) - 1):
        for si, stage in enumerate(stages):
            pi = step - si
            if 0 <= pi < XATTN_PARTS:
                rows = slice(pi * part, (pi + 1) * part)
                if stage is finish:
                    stage(rows, pi * part)
                else:
                    stage(rows)


def _xattn(x2d, y_ret, y_dil, w_out, g1, b1, kv, wq, wo, g, bb, w_r, alpha, seq, tm=512):
    t, d = x2d.shape
    mlen = kv.shape[1]
    tiles_per_seq = seq // tm
    row = lambda i: (i, 0)
    const = lambda i: (0, 0)
    vec = pl.BlockSpec((1, d), const)
    return pl.pallas_call(
        functools.partial(_xattn_kernel, alpha),
        grid=(t // tm,),
        in_specs=[pl.BlockSpec((tm, d), row),
                  pl.BlockSpec((tm, y_ret.shape[1]), row),
                  pl.BlockSpec((tm, y_dil.shape[1]), row),
                  pl.BlockSpec(w_out.shape, const), vec, vec,
                  pl.BlockSpec((1, mlen, d), lambda i: (i // tiles_per_seq, 0, 0)),
                  pl.BlockSpec((1, mlen, d), lambda i: (i // tiles_per_seq, 0, 1)),
                  pl.BlockSpec((d, d), const), pl.BlockSpec((d, d), const), vec, vec,
                  pl.BlockSpec((ROUTE_ROWS, d), const)],
        out_specs=[pl.BlockSpec((tm, d), row),
                   pl.BlockSpec((tm * SUBLANE, LANE), row),
                   pl.BlockSpec((ROUTE_ROWS, tm), lambda i: (0, i))],
        out_shape=[jax.ShapeDtypeStruct((t, d), F32),
                   jax.ShapeDtypeStruct((t * SUBLANE, LANE), F32),
                   jax.ShapeDtypeStruct((ROUTE_ROWS, t), F32)],
        scratch_shapes=[pltpu.VMEM((tm, d), F32), pltpu.VMEM((tm, d), BF16), pltpu.VMEM((tm, d), BF16)],
        compiler_params=_params("parallel"),
        name="xattn",
    )(x2d, y_ret, y_dil, w_out, g1, b1, kv, kv, wq, wo, g, bb, w_r)


def _route_kernel(lg_ref, before_ref, meta_ref, cnt_ref, carry_ref):
    i = pl.program_id(0)

    @pl.when(i == 0)
    def _():
        carry_ref[...] = jnp.zeros_like(carry_ref)

    rows, tt = lg_ref.shape
    sub = before_ref.shape[0]
    r = lax.broadcasted_iota(jnp.int32, (rows, sub), 0)
    r8 = lax.broadcasted_iota(jnp.int32, (SUBLANE, sub), 0)
    is_group = r < N_GROUPS

    def col_max(a):
        return jnp.max(a, axis=0, keepdims=True)

    def first_row_where(mask):
        return jnp.min(jnp.where(mask, r, rows), axis=0, keepdims=True)

    for c in range(tt // sub):
        lg = lg_ref[:, c * sub:(c + 1) * sub]
        mg = col_max(jnp.where(is_group, lg, NEG))
        eg = jnp.where(is_group, jnp.exp(lg - mg), 0.0)
        pg = eg / jnp.sum(eg, axis=0, keepdims=True)
        g1 = col_max(pg)
        gi = first_row_where(is_group & (pg == g1))
        lo = N_GROUPS + gi * EXPERTS_PER_GROUP
        in_grp = (r >= lo) & (r < lo + EXPERTS_PER_GROUP)
        v1 = col_max(jnp.where(in_grp, lg, NEG))
        i1 = first_row_where(in_grp & (lg == v1))
        rest = in_grp & (r != i1)
        v2 = col_max(jnp.where(rest, lg, NEG))
        i2 = first_row_where(rest & (lg == v2))
        e2 = jnp.exp(v2 - v1)
        den = 1.0 + e2
        gate1 = g1 * (1.0 / den)
        gate2 = g1 * (e2 / den)
        sel1 = r == i1
        sel2 = r == i2
        onehot = jnp.where(sel1 | sel2, 1.0, 0.0)
        rank = _dot(onehot, before_ref[...]) + carry_ref[:, 0:1]
        r1 = jnp.sum(jnp.where(sel1, rank, 0.0), axis=0, keepdims=True)
        r2 = jnp.sum(jnp.where(sel2, rank, 0.0), axis=0, keepdims=True)
        carry_ref[...] = carry_ref[...] + jnp.sum(onehot, axis=1, keepdims=True)
        meta = jnp.where(r8 == 0, (i1 - N_GROUPS).astype(F32), 0.0)
        meta = jnp.where(r8 == 1, (i2 - N_GROUPS).astype(F32), meta)
        meta = jnp.where(r8 == 2, r1, meta)
        meta = jnp.where(r8 == 3, r2, meta)
        meta = jnp.where(r8 == 4, gate1, meta)
        meta = jnp.where(r8 == 5, gate2, meta)
        meta_ref[:, c * sub:(c + 1) * sub] = meta
    cnt_ref[...] = carry_ref[...]


def _route(logits_t, tt=1024, sub=256):
    rows, t = logits_t.shape
    before = (jnp.arange(sub)[:, None] < jnp.arange(sub)[None, :]).astype(BF16)
    return pl.pallas_call(
        _route_kernel,
        grid=(t // tt,),
        in_specs=[pl.BlockSpec((rows, tt), lambda i: (0, i)),
                  pl.BlockSpec((sub, sub), lambda i: (0, 0))],
        out_specs=[pl.BlockSpec((SUBLANE, tt), lambda i: (0, i)),
                   pl.BlockSpec((rows, LANE), lambda i: (0, 0))],
        out_shape=[jax.ShapeDtypeStruct((SUBLANE, t), F32),
                   jax.ShapeDtypeStruct((rows, LANE), F32)],
        scratch_shapes=[pltpu.VMEM((rows, LANE), F32)],
        compiler_params=_params("arbitrary"),
        name="route",
    )(logits_t, before)


def _plan_kernel(meta_ref, first_row_ref, dest_ref):
    n_tiles = dest_ref.shape[0]
    tt = dest_ref.shape[2] // 2
    rows = first_row_ref.shape[0]
    first_row = first_row_ref[:, 0:1]
    r = lax.broadcasted_iota(jnp.int32, (rows, tt), 0)
    for g in range(n_tiles):
        m = meta_ref[:, g * tt:(g + 1) * tt]

        def dest_of(k):
            e_row = m[k:k + 1, :].astype(jnp.int32) + N_GROUPS
            return jnp.sum(jnp.where(r == e_row, first_row, 0.0), axis=0, keepdims=True) + m[2 + k:3 + k, :]

        dest_ref[g] = jnp.concatenate([dest_of(0), dest_of(1)], axis=1).astype(jnp.int32)


def _plan(meta_t, first_row, tt, tiles_per_step=4):
    t = meta_t.shape[1]
    rows = first_row.shape[0]
    return pl.pallas_call(
        _plan_kernel,
        grid=(t // (tt * tiles_per_step),),
        in_specs=[pl.BlockSpec((SUBLANE, tt * tiles_per_step), lambda i: (0, i)),
                  pl.BlockSpec((rows, LANE), lambda i: (0, 0))],
        out_specs=pl.BlockSpec((tiles_per_step, 1, 2 * tt), lambda i: (i, 0, 0)),
        out_shape=jax.ShapeDtypeStruct((t // tt, 1, 2 * tt), jnp.int32),
        compiler_params=_params("parallel"),
        name="plan",
    )(meta_t, first_row)


def _row_copy(src, dst, s_row, d_row, sem):
    return pltpu.make_async_copy(src.at[pl.ds(pl.multiple_of(s_row * SUBLANE, SUBLANE), SUBLANE), :],
                                 dst.at[pl.ds(pl.multiple_of(d_row * SUBLANE, SUBLANE), SUBLANE), :], sem)


def _dispatch_kernel(dest_ref, xt_ref, xs_ref, sem):
    tq = dest_ref.shape[2] // 2

    def issue(t, c):
        _row_copy(xt_ref, xs_ref, t, dest_ref[0, 0, t], sem).start(priority=0)
        _row_copy(xt_ref, xs_ref, t, dest_ref[0, 0, tq + t], sem).start(priority=1)
        return c

    lax.fori_loop(0, tq, issue, 0, unroll=8)
    for _ in range(2):
        pltpu.make_async_copy(xt_ref, xs_ref.at[pl.ds(0, tq * SUBLANE), :], sem).wait()


def _dispatch(dest3, x_tiled, n_rows, tiles_per_step=4):
    n_tiles, _, two_tt = dest3.shape
    tt = two_tt // 2
    tq = tiles_per_step * tt
    n_steps = n_tiles // tiles_per_step
    dest3 = dest3.reshape(n_steps, tiles_per_step, 2, tt).transpose(0, 2, 1, 3).reshape(n_steps, 1, 2 * tq)
    return pl.pallas_call(
        _dispatch_kernel,
        grid=(n_steps,),
        in_specs=[pl.BlockSpec((1, 1, 2 * tq), lambda i: (i, 0, 0), memory_space=pltpu.SMEM),
                  pl.BlockSpec((tq * SUBLANE, LANE), lambda i: (i, 0))],
        out_specs=pl.BlockSpec(memory_space=pl.ANY),
        out_shape=jax.ShapeDtypeStruct((n_rows * SUBLANE, LANE), F32),
        scratch_shapes=[pltpu.SemaphoreType.DMA(())],
        compiler_params=_params("arbitrary"),
        name="dispatch",
    )(dest3, x_tiled)


def _experts_kernel(blk_exp_ref, blk_rows_ref, n_used_ref, blk_first_ref, blk_slot_ref, blk_next_ref,
                    xs_ref, wg_hbm, wu_hbm, wd_hbm, ys_ref, x_scr, y_scr, wg_buf, wu_buf, wd_buf, sems):
    b = pl.program_id(0)

    def weight_copies(e, slot):
        return [pltpu.make_async_copy(hbm.at[e], buf.at[slot], sems.at[k, slot])
                for k, (hbm, buf) in enumerate(((wg_hbm, wg_buf), (wu_hbm, wu_buf), (wd_hbm, wd_buf)))]

    @pl.when(b < n_used_ref[0])
    def _():
        slot = blk_slot_ref[b]

        @pl.when(b == 0)
        def _():
            for cp in weight_copies(blk_exp_ref[0], slot):
                cp.start()

        @pl.when(blk_first_ref[b] == 1)
        def _():
            @pl.when(blk_next_ref[b] >= 0)
            def _():
                for cp in weight_copies(blk_next_ref[b], 1 - slot):
                    cp.start()

            for cp in weight_copies(blk_exp_ref[b], slot):
                cp.wait()

        rows, d = x_scr.shape
        n_valid = blk_rows_ref[b]

        def run(m):
            ri = lax.broadcasted_iota(jnp.int32, (m, 1), 0)
            for c in range(d // LANE):
                x = xs_ref[pl.ds(c, m, stride=SUBLANE), :]
                x_scr[0:m, c * LANE:(c + 1) * LANE] = jnp.where(ri < n_valid, x, 0.0).astype(BF16)
            x = x_scr[0:m, :]
            hg = _dot(x, wg_buf[slot])
            hu = _dot(x, wu_buf[slot])
            y_scr[0:m, :] = _dot(hg * jax.nn.sigmoid(hg) * hu, wd_buf[slot])
            for c in range(d // LANE):
                ys_ref[pl.ds(c, m, stride=SUBLANE), :] = y_scr[0:m, c * LANE:(c + 1) * LANE]

        n_quanta = rows // EXPERT_QUANTUM
        for k in range(1, n_quanta + 1):
            lo = (k - 1) * EXPERT_QUANTUM
            pl.when((n_valid > lo) & (n_valid <= k * EXPERT_QUANTUM))(functools.partial(run, k * EXPERT_QUANTUM))


def _experts(blk_exp, blk_rows, n_used, blk_first, blk_slot, blk_next, xs_tiled, w_g, w_u, w_d):
    n_rows = xs_tiled.shape[0] // SUBLANE
    rows = EXPERT_ROWS
    nblk = n_rows // rows
    _, d, ff = w_g.shape
    used = lambda b, be, br, nu, *_: (jnp.minimum(b, nu[0] - 1), 0)
    grid_spec = pltpu.PrefetchScalarGridSpec(
        num_scalar_prefetch=6,
        grid=(nblk,),
        in_specs=[pl.BlockSpec((rows * SUBLANE, LANE), used),
                  pl.BlockSpec(memory_space=pl.ANY), pl.BlockSpec(memory_space=pl.ANY),
                  pl.BlockSpec(memory_space=pl.ANY)],
        out_specs=pl.BlockSpec((rows * SUBLANE, LANE), used),
        scratch_shapes=[pltpu.VMEM((rows, d), BF16), pltpu.VMEM((rows, d), F32),
                        pltpu.VMEM((2, d, ff), F32), pltpu.VMEM((2, d, ff), F32), pltpu.VMEM((2, ff, d), F32),
                        pltpu.SemaphoreType.DMA((3, 2))],
    )
    return pl.pallas_call(
        _experts_kernel,
        grid_spec=grid_spec,
        out_shape=jax.ShapeDtypeStruct((n_rows * SUBLANE, LANE), F32),
        compiler_params=_params("arbitrary"),
        name="experts",
    )(blk_exp, blk_rows, n_used, blk_first, blk_slot, blk_next, xs_tiled, w_g, w_u, w_d)


def _combine_kernel(alpha, dcur_ref, dnext_ref, meta_ref, x_ref, g_ref, b_ref, ys_ref, o_ref, buf_ref, sems):
    i = pl.program_id(0)
    tq, d = x_ref.shape
    slot_rows = 2 * tq * SUBLANE

    def issue(d_ref, slot, t):
        _row_copy(ys_ref, buf_ref, d_ref[0, 0, t], slot * (2 * tq) + t, sems.at[slot]).start(priority=0)
        _row_copy(ys_ref, buf_ref, d_ref[0, 0, tq + t], slot * (2 * tq) + tq + t, sems.at[slot]).start(priority=1)

    def wait_slot(slot):
        off = pl.multiple_of(slot * slot_rows, slot_rows)
        pltpu.make_async_copy(ys_ref.at[pl.ds(0, slot_rows), :], buf_ref.at[pl.ds(off, slot_rows), :],
                              sems.at[slot]).wait()
        return off

    def issue_all(d_ref, slot):
        lax.fori_loop(0, tq, lambda t, c: (issue(d_ref, slot, t), c)[1], 0, unroll=4)

    slot = i % 2

    @pl.when(i == 0)
    def _():
        issue_all(dcur_ref, 0)

    @pl.when(i + 1 < pl.num_programs(0))
    def _():
        issue_all(dnext_ref, 1 - slot)

    off = wait_slot(slot)
    meta_rows = jnp.transpose(jnp.concatenate([meta_ref[...], jnp.zeros((LANE - SUBLANE, tq), F32)], axis=0))
    gate1 = meta_rows[:, 4:5]
    gate2 = meta_rows[:, 5:6]
    for c in range(d // LANE):
        sl = slice(c * LANE, (c + 1) * LANE)
        y1 = buf_ref[pl.ds(off + c, tq, stride=SUBLANE), :]
        y2 = buf_ref[pl.ds(off + tq * SUBLANE + c, tq, stride=SUBLANE), :]
        o_ref[:, sl] = alpha * x_ref[:, sl] + (y1 * gate1 + y2 * gate2)
    o_ref[...] = _layer_norm(o_ref[...], g_ref[...], b_ref[...])


def _combine(dest3, meta, x2, g, bb, ys_tiled, alpha):
    t, d = x2.shape
    n = dest3.shape[0]
    tq = t // n
    row = lambda i: (i, 0)
    const = lambda i: (0, 0)
    return pl.pallas_call(
        functools.partial(_combine_kernel, alpha),
        grid=(n,),
        in_specs=[pl.BlockSpec((1, 1, 2 * tq), lambda i: (i, 0, 0), memory_space=pltpu.SMEM),
                  pl.BlockSpec((1, 1, 2 * tq), lambda i: (jnp.minimum(i + 1, n - 1), 0, 0), memory_space=pltpu.SMEM),
                  pl.BlockSpec((SUBLANE, tq), lambda i: (0, i)),
                  pl.BlockSpec((tq, d), row),
                  pl.BlockSpec((1, d), const), pl.BlockSpec((1, d), const),
                  pl.BlockSpec(memory_space=pl.ANY)],
        out_specs=pl.BlockSpec((tq, d), row),
        out_shape=jax.ShapeDtypeStruct((t, d), F32),
        scratch_shapes=[pltpu.VMEM((2 * 2 * tq * SUBLANE, LANE), F32), pltpu.SemaphoreType.DMA((2,))],
        compiler_params=_params("arbitrary"),
        name="combine",
    )(dest3, dest3, meta, x2, g, bb, ys_tiled)


def _moe(x2, x2_tiled, logits, w_g, w_u, w_d, g, bb, alpha):
    t, d = x2.shape
    meta, cnt = _route(logits)
    counts = cnt[N_GROUPS:N_GROUPS + N_EXPERTS, 0].astype(jnp.int32)
    padded = ((counts + EXPERT_ROWS - 1) // EXPERT_ROWS) * EXPERT_ROWS
    pend = jnp.cumsum(padded)
    poff = pend - padded
    first_row = jnp.pad(poff.astype(F32), (N_GROUPS, ROUTE_ROWS - N_GROUPS - N_EXPERTS))
    dest3 = _plan(meta, jnp.broadcast_to(first_row[:, None], (ROUTE_ROWS, LANE)), ROUTE_TILE)
    n_rows = t * 2 + N_EXPERTS * EXPERT_ROWS
    nblk = n_rows // EXPERT_ROWS
    blk_start = jnp.arange(nblk, dtype=jnp.int32) * EXPERT_ROWS
    blk_exp = jnp.sum((pend[None, :] <= blk_start[:, None]).astype(jnp.int32), axis=1)
    blk_exp = jnp.minimum(blk_exp, N_EXPERTS - 1)
    e_idx = jnp.arange(N_EXPERTS, dtype=jnp.int32)
    owner = blk_exp[:, None] == e_idx[None, :]

    def of_block(per_expert):
        return jnp.sum(jnp.where(owner, per_expert[None, :], 0), axis=1).astype(jnp.int32)

    blk_rows = jnp.clip(of_block(poff + counts) - blk_start, 0, EXPERT_ROWS).astype(jnp.int32)
    n_used = (pend[-1:] // EXPERT_ROWS).astype(jnp.int32)
    has_rows = counts > 0
    ordinal = jnp.cumsum(has_rows.astype(jnp.int32)) - 1
    later = has_rows[None, :] & (e_idx[None, :] > e_idx[:, None])
    next_used = jnp.min(jnp.where(later, e_idx[None, :], N_EXPERTS), axis=1)
    next_used = jnp.where(next_used == N_EXPERTS, -1, next_used).astype(jnp.int32)
    blk_first = (blk_start == of_block(poff)).astype(jnp.int32)
    blk_slot = of_block(ordinal % 2)
    blk_next = of_block(next_used)
    xs_tiled = _dispatch(dest3, x2_tiled, n_rows)
    ys_tiled = _experts(blk_exp, blk_rows, n_used, blk_first, blk_slot, blk_next, xs_tiled, w_g, w_u, w_d)
    return _combine(dest3, meta, x2, g, bb, ys_tiled, alpha)


def kernel(x, mem, w_in, w_out, ln_mix_g, ln_mix_b, w_xq, w_xkv, w_xo, ln_x_g, ln_x_b, w_route_group,
           w_route_expert, w_exp_gate, w_exp_up, w_exp_down, ln_moe_g, ln_moe_b):
    b, s, d = x.shape
    depth = w_in.shape[0]
    alpha = (2.0 * depth) ** 0.25
    t = b * s
    ret_cols = (2 * RET_HEADS * RET_QK_DIM + 2 * RET_HEADS * RET_V_DIM)
    assert ret_cols % (DIL_HEADS * DIL_HEAD_DIM) == 0
    dil_col0 = ret_cols // (DIL_HEADS * DIL_HEAD_DIM)
    xc = x.reshape(t, d)
    for l in range(depth):
        n_qk = 2 * RET_HEADS * RET_QK_DIM
        w_in_l = jnp.concatenate([_retention_weight_layout(w_in[l][:, :n_qk]), w_in[l][:, n_qk:]], axis=1)
        h = _proj_in(xc, w_in_l.astype(BF16))
        h3 = h.reshape(b, s, h.shape[1])
        y_ret = _retention(h3).reshape(t, -1)
        y_dil = _dilated(h3, dil_col0).reshape(t, -1)
        kv =_mem_kv(mem.reshape(b * mem.shape[1], d), w_xkv[l].astype(BF16)).reshape(b, mem.shape[1], 2 * d)
        w_r = jnp.concatenate([w_route_group[l], w_route_expert[l]], axis=-1)
        w_r = jnp.pad(w_r.T, ((0, ROUTE_ROWS - w_r.shape[1]), (0, 0))).astype(BF16)
        x2, x2_tiled, logits = _xattn(xc, y_ret, y_dil, w_out[l].astype(BF16), ln_mix_g[l][None], ln_mix_b[l][None],
                                      kv, w_xq[l].astype(BF16), w_xo[l].astype(BF16),
                                      ln_x_g[l][None], ln_x_b[l][None], w_r, alpha, s)
        xc = _moe(x2, x2_tiled, logits, w_exp_gate[l], w_exp_up[l], w_exp_down[l],
                  ln_moe_g[l][None], ln_moe_b[l][None], alpha)
    return xc.reshape(b, s, d)
```

```python
import functools

import jax
import jax.numpy as jnp
from jax import lax
from jax.experimental import pallas as pl
from jax.experimental.pallas import tpu as pltpu

BF16 = jnp.bfloat16
F32 = jnp.float32

LANE = 128
SUBLANE = 8
VMEM_LIMIT = 56 * 1024 * 1024

RET_HEADS = 4
RET_QK_DIM = 64
RET_V_DIM = 128
RET_CHUNK = 128
RET_STEP_CHUNKS = 4
ROPE_BASE = 10000.0
DIL_HEADS = 8
DIL_HEAD_DIM = 64
DIL_DILATIONS = (1, 4, 16)
DIL_BLOCK = 128
DIL_SUPER = DIL_BLOCK * max(DIL_DILATIONS)
DIL_GROUP = 1
XATTN_HEADS = 4
XATTN_PARTS = 2
N_GROUPS = 4
EXPERTS_PER_GROUP = 8
N_EXPERTS = N_GROUPS * EXPERTS_PER_GROUP
ROUTE_ROWS = -(-(N_GROUPS + N_EXPERTS) // SUBLANE) * SUBLANE
EXPERT_ROWS = 1024
EXPERT_QUANTUM = 256
ROUTE_TILE = 256
LN_EPS = 1e-5
GN_EPS = 1e-6
NEG = -1e30


def _params(*sem):
    return pltpu.CompilerParams(dimension_semantics=sem, vmem_limit_bytes=VMEM_LIMIT)


def _layer_norm(z, g, b):
    mu = jnp.mean(z, axis=-1, keepdims=True)
    zc = z - mu
    var = jnp.mean(zc * zc, axis=-1, keepdims=True)
    return zc * lax.rsqrt(var + LN_EPS) * g + b


def _dot(a, b):
    return jnp.dot(a.astype(BF16), b.astype(BF16), preferred_element_type=F32)


def _dot_nt(a, b):
    return lax.dot_general(a.astype(BF16), b.astype(BF16), (((1,), (1,)), ((), ())),
                           preferred_element_type=F32)


def _dot_tn(a, b):
    return lax.dot_general(a.astype(BF16), b.astype(BF16), (((0,), (0,)), ((), ())),
                           preferred_element_type=F32)


def _proj_in_kernel(x_ref, w_ref, o_ref):
    o_ref[...] = _dot(x_ref[...], w_ref[...]).astype(o_ref.dtype)


def _proj_in(x2d, w_bf16, tm=512):
    t, d = x2d.shape
    n = w_bf16.shape[1]
    return pl.pallas_call(
        _proj_in_kernel,
        grid=(t // tm,),
        in_specs=[pl.BlockSpec((tm, d), lambda i: (i, 0)),
                  pl.BlockSpec((d, n), lambda i: (0, 0))],
        out_specs=pl.BlockSpec((tm, n), lambda i: (i, 0)),
        out_shape=jax.ShapeDtypeStruct((t, n), F32),
        compiler_params=_params("parallel"),
        name="proj_in",
    )(x2d, w_bf16)


def _retention_kernel(qk_ref, v_ref, g_ref, cos_ref, sin_ref, decay_ref, zeta_ref, xi_ref, gam_ref,
                      o_ref, state_ref, y_ref):
    n = pl.program_id(1)

    @pl.when(n == 0)
    def _():
        state_ref[...] = jnp.zeros_like(state_ref)

    c = RET_CHUNK
    n_sub = qk_ref.shape[1] // c
    half = RET_QK_DIM // 2
    lane = lax.broadcasted_iota(jnp.int32, (c, LANE), 1)

    rotated = []
    for j in range(n_sub):
        rows = slice(j * c, (j + 1) * c)
        cos = cos_ref[rows, :]
        sin = sin_ref[rows, :]

        def rot(col, rows=rows, cos=cos, sin=sin):
            t1 = qk_ref[0, rows, col * LANE:(col + 1) * LANE]
            t2 = qk_ref[0, rows, (col + 1) * LANE:(col + 2) * LANE]
            return t1 * cos - t2 * sin, t1 * sin + t2 * cos

        q1, q2 = rot(0)
        k1, k2 = (t * (RET_QK_DIM ** -0.5) for t in rot(2))
        rotated.append((q1, q2, k1, k2, jnp.concatenate([k1, k2], axis=1)))

    for h in range(RET_HEADS):
        cols = slice(h * RET_V_DIM, (h + 1) * RET_V_DIM)
        mine = (lane >= h * half) & (lane < (h + 1) * half)
        zeta = zeta_ref[h]
        st = state_ref[h]
        for j in range(n_sub):
            rows = slice(j * c, (j + 1) * c)
            q1, q2, k1, k2, k_all = rotated[j]
            qm = jnp.concatenate([jnp.where(mine, q1, 0.0), jnp.where(mine, q2, 0.0)], axis=1)
            kz = jnp.concatenate([jnp.where(mine, k1, 0.0) * zeta, jnp.where(mine, k2, 0.0) * zeta], axis=1)
            v = v_ref[0, rows, cols]
            s = _dot_nt(qm, k_all) * decay_ref[h]
            y_ref[rows, cols] = _dot(s, v) + _dot(qm, st) * xi_ref[h]
            st = gam_ref[h] * st + _dot_tn(kz, v)
        state_ref[h] = st

    for h in range(RET_HEADS):
        cols = slice(h * RET_V_DIM, (h + 1) * RET_V_DIM)
        y = y_ref[:, cols]
        mu = jnp.mean(y, axis=-1, keepdims=True)
        yc = y - mu
        var = jnp.mean(yc * yc, axis=-1, keepdims=True)
        yn = yc * lax.rsqrt(var + GN_EPS)
        gate = g_ref[0, :, cols]
        o_ref[0, :, cols] = gate * jax.nn.sigmoid(gate) * yn


def _retention_tables(s):
    half = RET_QK_DIM // 2
    inv = 1.0 / (ROPE_BASE ** (jnp.arange(half, dtype=F32) / half))
    ang = jnp.arange(s, dtype=F32)[:, None] * inv[None, :]
    cos_t = jnp.tile(jnp.cos(ang), (1, RET_HEADS))
    sin_t = jnp.tile(jnp.sin(ang), (1, RET_HEADS))
    c = RET_CHUNK
    lg = jnp.log(1.0 - jnp.exp2(-5.0 - jnp.arange(RET_HEADS, dtype=F32)))
    idx = jnp.arange(c, dtype=F32)
    diff = idx[:, None] - idx[None, :]
    decay = jnp.where(diff >= 0, jnp.exp(lg[:, None, None] * jnp.maximum(diff, 0.0)), 0.0)
    lanes = (RET_HEADS, c, LANE)
    zeta = jnp.broadcast_to(jnp.exp(lg[:, None] * (c - 1.0 - idx))[:, :, None], lanes)
    xi = jnp.broadcast_to(jnp.exp(lg[:, None] * (idx + 1.0))[:, :, None], lanes)
    gam = jnp.broadcast_to(jnp.exp(lg * c)[:, None, None], (RET_HEADS, 1, LANE))
    return cos_t, sin_t, decay, zeta, xi, gam


def _retention_weight_layout(w_qk):
    d = w_qk.shape[0]
    half = RET_QK_DIM // 2
    return w_qk.reshape(d, 2, RET_HEADS, 2, half).transpose(0, 1, 3, 2, 4).reshape(d, -1)


def _retention(h3):
    b, s, _ = h3.shape
    c = RET_CHUNK
    qk_w = 2 * RET_HEADS * RET_QK_DIM
    v_w = RET_HEADS * RET_V_DIM
    assert qk_w == v_w
    cos_t, sin_t, decay, zeta, xi, gam = _retention_tables(s)
    const3 = lambda bi, n: (0, 0, 0)
    rows = RET_STEP_CHUNKS * c
    return pl.pallas_call(
        _retention_kernel,
        grid=(b, s // rows),
        in_specs=[pl.BlockSpec((1, rows, qk_w), lambda bi, n: (bi, n, 0)),
                  pl.BlockSpec((1, rows, v_w), lambda bi, n: (bi, n, 1)),
                  pl.BlockSpec((1, rows, v_w), lambda bi, n: (bi, n, 2)),
                  pl.BlockSpec((rows, LANE), lambda bi, n: (n, 0)),
                  pl.BlockSpec((rows, LANE), lambda bi, n: (n, 0)),
                  pl.BlockSpec((RET_HEADS, c, c), const3),
                  pl.BlockSpec((RET_HEADS, c, LANE), const3),
                  pl.BlockSpec((RET_HEADS, c, LANE), const3),
                  pl.BlockSpec((RET_HEADS, 1, LANE), const3)],
        out_specs=pl.BlockSpec((1, rows, v_w), lambda bi, n: (bi, n, 0)),
        out_shape=jax.ShapeDtypeStruct((b, s, v_w), F32),
        scratch_shapes=[pltpu.VMEM((RET_HEADS, RET_HEADS * RET_QK_DIM, RET_V_DIM), F32),
                        pltpu.VMEM((rows, v_w), F32)],
        compiler_params=_params("parallel", "arbitrary"),
        name="retention",
    )(h3, h3, h3, cos_t, sin_t, decay, zeta, xi, gam)


def _dilated_kernel(q_ref, kp_ref, kc_ref, vp_ref, vc_ref, bias_ref, o_ref, kk_ref, vv_ref, acc_ref, m_ref, l_ref,
                    s0_ref, s1_ref, p0_ref, p1_ref):
    j = pl.program_id(2)
    sup = DIL_SUPER
    q_blk = DIL_BLOCK
    kk_ref[0:sup, :] = kp_ref[0]
    kk_ref[sup:2 * sup, :] = kc_ref[0]
    vv_ref[0:sup, :] = vp_ref[0]
    vv_ref[sup:2 * sup, :] = vc_ref[0]
    scale = DIL_HEAD_DIM ** -0.5
    n_blocks = sup // q_blk
    lane = lax.broadcasted_iota(jnp.int32, (q_blk, LANE), 1)
    head0 = lane < DIL_HEAD_DIM
    first_bias = jnp.where(j == 0, 1, 0)

    groups = []
    for bi, d in enumerate(DIL_DILATIONS):
        n_per_r = sup // (q_blk * d)
        for t0 in range(0, n_blocks, DIL_GROUP):
            blocks = []
            for t in range(t0, t0 + DIL_GROUP):
                r, n = divmod(t, n_per_r)
                blocks.append((n * (q_blk * d) + r, sup + (n - 1) * (q_blk * d) + r, n))
            groups.append((bi, d, blocks))
    s_bufs = (s0_ref, s1_ref)
    p_bufs = (p0_ref, p1_ref)

    def scores(gi):
        bi, d, blocks = groups[gi]
        for g, (q_start, k_start, n) in enumerate(blocks):
            q = q_ref[0, pl.ds(q_start, q_blk, stride=d), :] * scale
            kb = kk_ref[pl.ds(k_start, 2 * q_blk, stride=d), :]
            q2 = jnp.concatenate([jnp.where(head0, q, 0.0), jnp.where(head0, 0.0, q)], axis=0)
            bias = bias_ref[first_bias] if n == 0 else bias_ref[0]
            s_bufs[gi % 2][g] = _dot_nt(q2, kb) + bias

    def softmax(gi):
        bi, d, blocks = groups[gi]
        for g, (q_start, k_start, n) in enumerate(blocks):
            s = s_bufs[gi % 2][g]
            m2 = jnp.max(s, axis=-1, keepdims=True)
            p_bufs[gi % 2][g] = jnp.exp(s - m2).astype(BF16)
            m_ref[bi, pl.ds(q_start, q_blk, stride=d), :] = jnp.where(head0, m2[:q_blk], m2[q_blk:])

    def values(gi):
        bi, d, blocks = groups[gi]
        for g, (q_start, k_start, n) in enumerate(blocks):
            vb = vv_ref[pl.ds(k_start, 2 * q_blk, stride=d), :].astype(BF16)
            o2 = jnp.dot(p_bufs[gi % 2][g], jnp.concatenate([vb, jnp.ones_like(vb)], axis=1),
                         preferred_element_type=F32)
            rows = pl.ds(q_start, q_blk, stride=d)
            acc_ref[bi, rows, :] = jnp.where(head0, o2[:q_blk, :LANE], o2[q_blk:, :LANE])
            l_ref[bi, rows, :] = jnp.where(head0, o2[:q_blk, LANE:], o2[q_blk:, LANE:])

    for step in range(len(groups) + 2):
        if step < len(groups):
            scores(step)
        if 0 <= step - 1 < len(groups):
            softmax(step - 1)
        if step - 2 >= 0:
            values(step - 2)

    def merge(c, carry):
        rows = pl.ds(pl.multiple_of(c * q_blk, q_blk), q_blk)
        ms = [m_ref[bi, rows, :] for bi in range(len(DIL_DILATIONS))]
        m_all = functools.reduce(jnp.maximum, ms)
        ws = [jnp.exp(m - m_all) for m in ms]
        num = functools.reduce(lambda a, b: a + b, [w * acc_ref[bi, rows, :] for bi, w in enumerate(ws)])
        den = functools.reduce(lambda a, b: a + b, [w * l_ref[bi, rows, :] for bi, w in enumerate(ws)])
        o_ref[0, rows, :] = num / den
        return carry

    lax.fori_loop(0, n_blocks, merge, 0)


def _dilated_bias():
    q_blk = DIL_BLOCK
    qi = jnp.arange(2 * q_blk)[:, None] % q_blk
    kj = jnp.arange(2 * q_blk)[None, :]
    band = (kj >= qi) & (kj <= qi + q_blk)
    return jnp.stack([jnp.where(band, 0.0, NEG), jnp.where(band & (kj >= q_blk), 0.0, NEG)]).astype(F32)


def _dilated(h3, col0):
    b, s, _ = h3.shape
    sup = DIL_SUPER
    n_pairs = DIL_HEADS * DIL_HEAD_DIM // LANE
    n_br = len(DIL_DILATIONS)
    cq, ck, cv = col0 * n_pairs, (col0 + 1) * n_pairs, (col0 + 2) * n_pairs
    cur = lambda c: (lambda bi, p, j: (bi, j, c + p))
    prev = lambda c: (lambda bi, p, j: (bi, jnp.maximum(j - 1, 0), c + p))
    blk = (1, sup, LANE)
    return pl.pallas_call(
        _dilated_kernel,
        grid=(b, n_pairs, s // sup),
        in_specs=[pl.BlockSpec(blk, cur(cq)),
                  pl.BlockSpec(blk, prev(ck)), pl.BlockSpec(blk, cur(ck)),
                  pl.BlockSpec(blk, prev(cv)), pl.BlockSpec(blk, cur(cv)),
                  pl.BlockSpec((2, 2 * DIL_BLOCK, 2 * DIL_BLOCK), lambda bi, p, j: (0, 0, 0))],
        out_specs=pl.BlockSpec(blk, lambda bi, p, j: (bi, j, p)),
        out_shape=jax.ShapeDtypeStruct((b, s, n_pairs * LANE), F32),
        scratch_shapes=[pltpu.VMEM((2 * sup, LANE), F32), pltpu.VMEM((2 * sup, LANE), F32),
                        pltpu.VMEM((n_br, sup, LANE), F32), pltpu.VMEM((n_br, sup, LANE), F32),
                        pltpu.VMEM((n_br, sup, LANE), F32),
                        pltpu.VMEM((DIL_GROUP, 2 * DIL_BLOCK, 2 * DIL_BLOCK), F32),
                        pltpu.VMEM((DIL_GROUP, 2 * DIL_BLOCK, 2 * DIL_BLOCK), F32),
                        pltpu.VMEM((DIL_GROUP, 2 * DIL_BLOCK, 2 * DIL_BLOCK), BF16),
                        pltpu.VMEM((DIL_GROUP, 2 * DIL_BLOCK, 2 * DIL_BLOCK), BF16)],
        compiler_params=_params("parallel", "parallel", "arbitrary"),
        name="dilated",
    )(h3, h3, h3, h3, h3, _dilated_bias())


def _mem_kv(mem2d, w_bf16, tn=512):
    m, d = mem2d.shape
    n = w_bf16.shape[1]
    return pl.pallas_call(
        _proj_in_kernel,
        grid=(n // tn,),
        in_specs=[pl.BlockSpec((m, d), lambda i: (0, 0)),
                  pl.BlockSpec((d, tn), lambda i: (0, i))],
        out_specs=pl.BlockSpec((m, tn), lambda i: (0, i)),
        out_shape=jax.ShapeDtypeStruct((m, n), BF16),
        compiler_params=_params("parallel"),
        name="mem_kv",
    )(mem2d, w_bf16)


def _xattn_kernel(alpha, x_ref, yr_ref, yd_ref, wout_ref, g1_ref, b1_ref, k_ref, v_ref, wq_ref, wo_ref,
                  g_ref, b_ref, wr_ref, o_ref, ot_ref, lg_ref, x1_ref, q_ref, att_ref):
    tm, d = x_ref.shape
    dh = d // XATTN_HEADS
    wr = yr_ref.shape[1]
    scale = dh ** -0.5
    part = tm // XATTN_PARTS

    def mix(rows):
        y = _dot(yr_ref[rows, :], wout_ref[0:wr, :]) + _dot(yd_ref[rows, :], wout_ref[wr:, :])
        x1_ref[rows, :] = _layer_norm(alpha * x_ref[rows, :] + y, g1_ref[...], b1_ref[...])

    def query(rows):
        q_ref[rows, :] = _dot(x1_ref[rows, :], wq_ref[...]).astype(BF16)

    def attend(rows):
        for h in range(XATTN_HEADS):
            sl = slice(h * dh, (h + 1) * dh)
            s = _dot_nt(q_ref[rows, sl], k_ref[0, :, sl]) * scale
            m = jnp.max(s, axis=-1, keepdims=True)
            e = jnp.exp(s - m)
            p = e / jnp.sum(e, axis=-1, keepdims=True)
            att_ref[rows, sl] = _dot(p, v_ref[0, :, sl]).astype(BF16)

    def finish(rows, r0):
        y = _dot(att_ref[rows, :], wo_ref[...])
        x2 = _layer_norm(alpha * x1_ref[rows, :] + y, g_ref[...], b_ref[...])
        o_ref[rows, :] = x2
        for c in range(d // LANE):
            ot_ref[pl.ds(r0 * SUBLANE + c, part, stride=SUBLANE), :] = x2[:, c * LANE:(c + 1) * LANE]
        lg_ref[:, rows] = _dot_nt(wr_ref[...], x2)

    stages = (mix, query, attend, finish)
    for step in range(XATTN_PARTS + len(stages) - 1):
        for si, stage in enumerate(stages):
            pi = step - si
            if 0 <= pi < XATTN_PARTS:
                rows = slice(pi * part, (pi + 1) * part)
                if stage is finish:
                    stage(rows, pi * part)
                else:
                    stage(rows)


def _xattn(x2d, y_ret, y_dil, w_out, g1, b1, kv, wq, wo, g, bb, w_r, alpha, seq, tm=512):
    t, d = x2d.shape
    mlen = kv.shape[1]
    tiles_per_seq = seq // tm
    row = lambda i: (i, 0)
    const = lambda i: (0, 0)
    vec = pl.BlockSpec((1, d), const)
    return pl.pallas_call(
        functools.partial(_xattn_kernel, alpha),
        grid=(t // tm,),
        in_specs=[pl.BlockSpec((tm, d), row),
                  pl.BlockSpec((tm, y_ret.shape[1]), row),
                  pl.BlockSpec((tm, y_dil.shape[1]), row),
                  pl.BlockSpec(w_out.shape, const), vec, vec,
                  pl.BlockSpec((1, mlen, d), lambda i: (i // tiles_per_seq, 0, 0)),
                  pl.BlockSpec((1, mlen, d), lambda i: (i // tiles_per_seq, 0, 1)),
                  pl.BlockSpec((d, d), const), pl.BlockSpec((d, d), const), vec, vec,
                  pl.BlockSpec((ROUTE_ROWS, d), const)],
        out_specs=[pl.BlockSpec((tm, d), row),
                   pl.BlockSpec((tm * SUBLANE, LANE), row),
                   pl.BlockSpec((ROUTE_ROWS, tm), lambda i: (0, i))],
        out_shape=[jax.ShapeDtypeStruct((t, d), F32),
                   jax.ShapeDtypeStruct((t * SUBLANE, LANE), F32),
                   jax.ShapeDtypeStruct((ROUTE_ROWS, t), F32)],
        scratch_shapes=[pltpu.VMEM((tm, d), F32), pltpu.VMEM((tm, d), BF16), pltpu.VMEM((tm, d), BF16)],
        compiler_params=_params("parallel"),
        name="xattn",
    )(x2d, y_ret, y_dil, w_out, g1, b1, kv, kv, wq, wo, g, bb, w_r)


def _route_kernel(lg_ref, before_ref, meta_ref, cnt_ref, carry_ref):
    i = pl.program_id(0)

    @pl.when(i == 0)
    def _():
        carry_ref[...] = jnp.zeros_like(carry_ref)

    rows, tt = lg_ref.shape
    sub = before_ref.shape[0]
    r = lax.broadcasted_iota(jnp.int32, (rows, sub), 0)
    r8 = lax.broadcasted_iota(jnp.int32, (SUBLANE, sub), 0)
    is_group = r < N_GROUPS

    def col_max(a):
        return jnp.max(a, axis=0, keepdims=True)

    def first_row_where(mask):
        return jnp.min(jnp.where(mask, r, rows), axis=0, keepdims=True)

    for c in range(tt // sub):
        lg = lg_ref[:, c * sub:(c + 1) * sub]
        mg = col_max(jnp.where(is_group, lg, NEG))
        eg = jnp.where(is_group, jnp.exp(lg - mg), 0.0)
        pg = eg / jnp.sum(eg, axis=0, keepdims=True)
        g1 = col_max(pg)
        gi = first_row_where(is_group & (pg == g1))
        lo = N_GROUPS + gi * EXPERTS_PER_GROUP
        in_grp = (r >= lo) & (r < lo + EXPERTS_PER_GROUP)
        v1 = col_max(jnp.where(in_grp, lg, NEG))
        i1 = first_row_where(in_grp & (lg == v1))
        rest = in_grp & (r != i1)
        v2 = col_max(jnp.where(rest, lg, NEG))
        i2 = first_row_where(rest & (lg == v2))
        e2 = jnp.exp(v2 - v1)
        den = 1.0 + e2
        gate1 = g1 * (1.0 / den)
        gate2 = g1 * (e2 / den)
        sel1 = r == i1
        sel2 = r == i2
        onehot = jnp.where(sel1 | sel2, 1.0, 0.0)
        rank = _dot(onehot, before_ref[...]) + carry_ref[:, 0:1]
        r1 = jnp.sum(jnp.where(sel1, rank, 0.0), axis=0, keepdims=True)
        r2 = jnp.sum(jnp.where(sel2, rank, 0.0), axis=0, keepdims=True)
        carry_ref[...] = carry_ref[...] + jnp.sum(onehot, axis=1, keepdims=True)
        meta = jnp.where(r8 == 0, (i1 - N_GROUPS).astype(F32), 0.0)
        meta = jnp.where(r8 == 1, (i2 - N_GROUPS).astype(F32), meta)
        meta = jnp.where(r8 == 2, r1, meta)
        meta = jnp.where(r8 == 3, r2, meta)
        meta = jnp.where(r8 == 4, gate1, meta)
        meta = jnp.where(r8 == 5, gate2, meta)
        meta_ref[:, c * sub:(c + 1) * sub] = meta
    cnt_ref[...] = carry_ref[...]


def _route(logits_t, tt=1024, sub=256):
    rows, t = logits_t.shape
    before = (jnp.arange(sub)[:, None] < jnp.arange(sub)[None, :]).astype(BF16)
    return pl.pallas_call(
        _route_kernel,
        grid=(t // tt,),
        in_specs=[pl.BlockSpec((rows, tt), lambda i: (0, i)),
                  pl.BlockSpec((sub, sub), lambda i: (0, 0))],
        out_specs=[pl.BlockSpec((SUBLANE, tt), lambda i: (0, i)),
                   pl.BlockSpec((rows, LANE), lambda i: (0, 0))],
        out_shape=[jax.ShapeDtypeStruct((SUBLANE, t), F32),
                   jax.ShapeDtypeStruct((rows, LANE), F32)],
        scratch_shapes=[pltpu.VMEM((rows, LANE), F32)],
        compiler_params=_params("arbitrary"),
        name="route",
    )(logits_t, before)


def _plan_kernel(meta_ref, first_row_ref, dest_ref):
    n_tiles = dest_ref.shape[0]
    tt = dest_ref.shape[2] // 2
    rows = first_row_ref.shape[0]
    first_row = first_row_ref[:, 0:1]
    r = lax.broadcasted_iota(jnp.int32, (rows, tt), 0)
    for g in range(n_tiles):
        m = meta_ref[:, g * tt:(g + 1) * tt]

        def dest_of(k):
            e_row = m[k:k + 1, :].astype(jnp.int32) + N_GROUPS
            return jnp.sum(jnp.where(r == e_row, first_row, 0.0), axis=0, keepdims=True) + m[2 + k:3 + k, :]

        dest_ref[g] = jnp.concatenate([dest_of(0), dest_of(1)], axis=1).astype(jnp.int32)


def _plan(meta_t, first_row, tt, tiles_per_step=4):
    t = meta_t.shape[1]
    rows = first_row.shape[0]
    return pl.pallas_call(
        _plan_kernel,
        grid=(t // (tt * tiles_per_step),),
        in_specs=[pl.BlockSpec((SUBLANE, tt * tiles_per_step), lambda i: (0, i)),
                  pl.BlockSpec((rows, LANE), lambda i: (0, 0))],
        out_specs=pl.BlockSpec((tiles_per_step, 1, 2 * tt), lambda i: (i, 0, 0)),
        out_shape=jax.ShapeDtypeStruct((t // tt, 1, 2 * tt), jnp.int32),
        compiler_params=_params("parallel"),
        name="plan",
    )(meta_t, first_row)


def _row_copy(src, dst, s_row, d_row, sem):
    return pltpu.make_async_copy(src.at[pl.ds(pl.multiple_of(s_row * SUBLANE, SUBLANE), SUBLANE), :],
                                 dst.at[pl.ds(pl.multiple_of(d_row * SUBLANE, SUBLANE), SUBLANE), :], sem)


def _dispatch_kernel(dest_ref, xt_ref, xs_ref, sem):
    tq = dest_ref.shape[2] // 2

    def issue(t, c):
        _row_copy(xt_ref, xs_ref, t, dest_ref[0, 0, t], sem).start(priority=0)
        _row_copy(xt_ref, xs_ref, t, dest_ref[0, 0, tq + t], sem).start(priority=1)
        return c

    lax.fori_loop(0, tq, issue, 0, unroll=8)
    for _ in range(2):
        pltpu.make_async_copy(xt_ref, xs_ref.at[pl.ds(0, tq * SUBLANE), :], sem).wait()


def _dispatch(dest3, x_tiled, n_rows, tiles_per_step=4):
    n_tiles, _, two_tt = dest3.shape
    tt = two_tt // 2
    tq = tiles_per_step * tt
    n_steps = n_tiles // tiles_per_step
    dest3 = dest3.reshape(n_steps, tiles_per_step, 2, tt).transpose(0, 2, 1, 3).reshape(n_steps, 1, 2 * tq)
    return pl.pallas_call(
        _dispatch_kernel,
        grid=(n_steps,),
        in_specs=[pl.BlockSpec((1, 1, 2 * tq), lambda i: (i, 0, 0), memory_space=pltpu.SMEM),
                  pl.BlockSpec((tq * SUBLANE, LANE), lambda i: (i, 0))],
        out_specs=pl.BlockSpec(memory_space=pl.ANY),
        out_shape=jax.ShapeDtypeStruct((n_rows * SUBLANE, LANE), F32),
        scratch_shapes=[pltpu.SemaphoreType.DMA(())],
        compiler_params=_params("arbitrary"),
        name="dispatch",
    )(dest3, x_tiled)


def _experts_kernel(blk_exp_ref, blk_rows_ref, n_used_ref, blk_first_ref, blk_slot_ref, blk_next_ref,
                    xs_ref, wg_hbm, wu_hbm, wd_hbm, ys_ref, x_scr, y_scr, wg_buf, wu_buf, wd_buf, sems):
    b = pl.program_id(0)

    def weight_copies(e, slot):
        return [pltpu.make_async_copy(hbm.at[e], buf.at[slot], sems.at[k, slot])
                for k, (hbm, buf) in enumerate(((wg_hbm, wg_buf), (wu_hbm, wu_buf), (wd_hbm, wd_buf)))]

    @pl.when(b < n_used_ref[0])
    def _():
        slot = blk_slot_ref[b]

        @pl.when(b == 0)
        def _():
            for cp in weight_copies(blk_exp_ref[0], slot):
                cp.start()

        @pl.when(blk_first_ref[b] == 1)
        def _():
            @pl.when(blk_next_ref[b] >= 0)
            def _():
                for cp in weight_copies(blk_next_ref[b], 1 - slot):
                    cp.start()

            for cp in weight_copies(blk_exp_ref[b], slot):
                cp.wait()

        rows, d = x_scr.shape
        n_valid = blk_rows_ref[b]

        def run(m):
            ri = lax.broadcasted_iota(jnp.int32, (m, 1), 0)
            for c in range(d // LANE):
                x = xs_ref[pl.ds(c, m, stride=SUBLANE), :]
                x_scr[0:m, c * LANE:(c + 1) * LANE] = jnp.where(ri < n_valid, x, 0.0).astype(BF16)
            x = x_scr[0:m, :]
            hg = _dot(x, wg_buf[slot])
            hu = _dot(x, wu_buf[slot])
            y_scr[0:m, :] = _dot(hg * jax.nn.sigmoid(hg) * hu, wd_buf[slot])
            for c in range(d // LANE):
                ys_ref[pl.ds(c, m, stride=SUBLANE), :] = y_scr[0:m, c * LANE:(c + 1) * LANE]

        n_quanta = rows // EXPERT_QUANTUM
        for k in range(1, n_quanta + 1):
            lo = (k - 1) * EXPERT_QUANTUM
            pl.when((n_valid > lo) & (n_valid <= k * EXPERT_QUANTUM))(functools.partial(run, k * EXPERT_QUANTUM))


def _experts(blk_exp, blk_rows, n_used, blk_first, blk_slot, blk_next, xs_tiled, w_g, w_u, w_d):
    n_rows = xs_tiled.shape[0] // SUBLANE
    rows = EXPERT_ROWS
    nblk = n_rows // rows
    _, d, ff = w_g.shape
    used = lambda b, be, br, nu, *_: (jnp.minimum(b, nu[0] - 1), 0)
    grid_spec = pltpu.PrefetchScalarGridSpec(
        num_scalar_prefetch=6,
        grid=(nblk,),
        in_specs=[pl.BlockSpec((rows * SUBLANE, LANE), used),
                  pl.BlockSpec(memory_space=pl.ANY), pl.BlockSpec(memory_space=pl.ANY),
                  pl.BlockSpec(memory_space=pl.ANY)],
        out_specs=pl.BlockSpec((rows * SUBLANE, LANE), used),
        scratch_shapes=[pltpu.VMEM((rows, d), BF16), pltpu.VMEM((rows, d), F32),
                        pltpu.VMEM((2, d, ff), F32), pltpu.VMEM((2, d, ff), F32), pltpu.VMEM((2, ff, d), F32),
                        pltpu.SemaphoreType.DMA((3, 2))],
    )
    return pl.pallas_call(
        _experts_kernel,
        grid_spec=grid_spec,
        out_shape=jax.ShapeDtypeStruct((n_rows * SUBLANE, LANE), F32),
        compiler_params=_params("arbitrary"),
        name="experts",
    )(blk_exp, blk_rows, n_used, blk_first, blk_slot, blk_next, xs_tiled, w_g, w_u, w_d)


def _combine_kernel(alpha, dcur_ref, dnext_ref, meta_ref, x_ref, g_ref, b_ref, ys_ref, o_ref, buf_ref, sems):
    i = pl.program_id(0)
    tq, d = x_ref.shape
    slot_rows = 2 * tq * SUBLANE

    def issue(d_ref, slot, t):
        _row_copy(ys_ref, buf_ref, d_ref[0, 0, t], slot * (2 * tq) + t, sems.at[slot]).start(priority=0)
        _row_copy(ys_ref, buf_ref, d_ref[0, 0, tq + t], slot * (2 * tq) + tq + t, sems.at[slot]).start(priority=1)

    def wait_slot(slot):
        off = pl.multiple_of(slot * slot_rows, slot_rows)
        pltpu.make_async_copy(ys_ref.at[pl.ds(0, slot_rows), :], buf_ref.at[pl.ds(off, slot_rows), :],
                              sems.at[slot]).wait()
        return off

    def issue_all(d_ref, slot):
        lax.fori_loop(0, tq, lambda t, c: (issue(d_ref, slot, t), c)[1], 0, unroll=4)

    slot = i % 2

    @pl.when(i == 0)
    def _():
        issue_all(dcur_ref, 0)

    @pl.when(i + 1 < pl.num_programs(0))
    def _():
        issue_all(dnext_ref, 1 - slot)

    off = wait_slot(slot)
    meta_rows = jnp.transpose(jnp.concatenate([meta_ref[...], jnp.zeros((LANE - SUBLANE, tq), F32)], axis=0))
    gate1 = meta_rows[:, 4:5]
    gate2 = meta_rows[:, 5:6]
    for c in range(d // LANE):
        sl = slice(c * LANE, (c + 1) * LANE)
        y1 = buf_ref[pl.ds(off + c, tq, stride=SUBLANE), :]
        y2 = buf_ref[pl.ds(off + tq * SUBLANE + c, tq, stride=SUBLANE), :]
        o_ref[:, sl] = alpha * x_ref[:, sl] + (y1 * gate1 + y2 * gate2)
    o_ref[...] = _layer_norm(o_ref[...], g_ref[...], b_ref[...])


def _combine(dest3, meta, x2, g, bb, ys_tiled, alpha):
    t, d = x2.shape
    n = dest3.shape[0]
    tq = t // n
    row = lambda i: (i, 0)
    const = lambda i: (0, 0)
    return pl.pallas_call(
        functools.partial(_combine_kernel, alpha),
        grid=(n,),
        in_specs=[pl.BlockSpec((1, 1, 2 * tq), lambda i: (i, 0, 0), memory_space=pltpu.SMEM),
                  pl.BlockSpec((1, 1, 2 * tq), lambda i: (jnp.minimum(i + 1, n - 1), 0, 0), memory_space=pltpu.SMEM),
                  pl.BlockSpec((SUBLANE, tq), lambda i: (0, i)),
                  pl.BlockSpec((tq, d), row),
                  pl.BlockSpec((1, d), const), pl.BlockSpec((1, d), const),
                  pl.BlockSpec(memory_space=pl.ANY)],
        out_specs=pl.BlockSpec((tq, d), row),
        out_shape=jax.ShapeDtypeStruct((t, d), F32),
        scratch_shapes=[pltpu.VMEM((2 * 2 * tq * SUBLANE, LANE), F32), pltpu.SemaphoreType.DMA((2,))],
        compiler_params=_params("arbitrary"),
        name="combine",
    )(dest3, dest3, meta, x2, g, bb, ys_tiled)


def _moe(x2, x2_tiled, logits, w_g, w_u, w_d, g, bb, alpha):
    t, d = x2.shape
    meta, cnt = _route(logits)
    counts = cnt[N_GROUPS:N_GROUPS + N_EXPERTS, 0].astype(jnp.int32)
    padded = ((counts + EXPERT_ROWS - 1) // EXPERT_ROWS) * EXPERT_ROWS
    pend = jnp.cumsum(padded)
    poff = pend - padded
    first_row = jnp.pad(poff.astype(F32), (N_GROUPS, ROUTE_ROWS - N_GROUPS - N_EXPERTS))
    dest3 = _plan(meta, jnp.broadcast_to(first_row[:, None], (ROUTE_ROWS, LANE)), ROUTE_TILE)
    n_rows = t * 2 + N_EXPERTS * EXPERT_ROWS
    nblk = n_rows // EXPERT_ROWS
    blk_start = jnp.arange(nblk, dtype=jnp.int32) * EXPERT_ROWS
    blk_exp = jnp.sum((pend[None, :] <= blk_start[:, None]).astype(jnp.int32), axis=1)
    blk_exp = jnp.minimum(blk_exp, N_EXPERTS - 1)
    e_idx = jnp.arange(N_EXPERTS, dtype=jnp.int32)
    owner = blk_exp[:, None] == e_idx[None, :]

    def of_block(per_expert):
        return jnp.sum(jnp.where(owner, per_expert[None, :], 0), axis=1).astype(jnp.int32)

    blk_rows = jnp.clip(of_block(poff + counts) - blk_start, 0, EXPERT_ROWS).astype(jnp.int32)
    n_used = (pend[-1:] // EXPERT_ROWS).astype(jnp.int32)
    has_rows = counts > 0
    ordinal = jnp.cumsum(has_rows.astype(jnp.int32)) - 1
    later = has_rows[None, :] & (e_idx[None, :] > e_idx[:, None])
    next_used = jnp.min(jnp.where(later, e_idx[None, :], N_EXPERTS), axis=1)
    next_used = jnp.where(next_used == N_EXPERTS, -1, next_used).astype(jnp.int32)
    blk_first = (blk_start == of_block(poff)).astype(jnp.int32)
    blk_slot = of_block(ordinal % 2)
    blk_next = of_block(next_used)
    xs_tiled = _dispatch(dest3, x2_tiled, n_rows)
    ys_tiled = _experts(blk_exp, blk_rows, n_used, blk_first, blk_slot, blk_next, xs_tiled, w_g, w_u, w_d)
    return _combine(dest3, meta, x2, g, bb, ys_tiled, alpha)


def kernel(x, mem, w_in, w_out, ln_mix_g, ln_mix_b, w_xq, w_xkv, w_xo, ln_x_g, ln_x_b, w_route_group,
           w_route_expert, w_exp_gate, w_exp_up, w_exp_down, ln_moe_g, ln_moe_b):
    b, s, d = x.shape
    depth = w_in.shape[0]
    alpha = (2.0 * depth) ** 0.25
    t = b * s
    ret_cols = (2 * RET_HEADS * RET_QK_DIM + 2 * RET_HEADS * RET_V_DIM)
    assert ret_cols % (DIL_HEADS * DIL_HEAD_DIM) == 0
    dil_col0 = ret_cols // (DIL_HEADS * DIL_HEAD_DIM)
    xc = x.reshape(t, d)
    for l in range(depth):
        n_qk = 2 * RET_HEADS * RET_QK_DIM
        w_in_l = jnp.concatenate([_retention_weight_layout(w_in[l][:, :n_qk]), w_in[l][:, n_qk:]], axis=1)
        h = _proj_in(xc, w_in_l.astype(BF16))
        h3 = h.reshape(b, s, h.shape[1])
        y_ret = _retention(h3).reshape(t, -1)
        y_dil = _dilated(h3, dil_col0).reshape(t, -1)
        kv =_mem_kv(mem.reshape(b * mem.shape[1], d), w_xkv[l].astype(BF16)).reshape(b, mem.shape[1], 2 * d)
        w_r = jnp.concatenate([w_route_group[l], w_route_expert[l]], axis=-1)
        w_r = jnp.pad(w_r.T, ((0, ROUTE_ROWS - w_r.shape[1]), (0, 0))).astype(BF16)
        x2, x2_tiled, logits = _xattn(xc, y_ret, y_dil, w_out[l].astype(BF16), ln_mix_g[l][None], ln_mix_b[l][None],
                                      kv, w_xq[l].astype(BF16), w_xo[l].astype(BF16),
                                      ln_x_g[l][None], ln_x_b[l][None], w_r, alpha, s)
        xc = _moe(x2, x2_tiled, logits, w_exp_gate[l], w_exp_up[l], w_exp_down[l],
                  ln_moe_g[l][None], ln_moe_b[l][None], alpha)
    return xc.reshape(b, s, d)
```

```python
import functools
import math

import jax
import jax.numpy as jnp
from jax import lax
from jax.experimental import pallas as pl
from jax.experimental.pallas import tpu as pltpu

BF16 = jnp.bfloat16
F32 = jnp.float32

LANE = 128
SUBLANE = 8
VMEM_LIMIT = 56 * 1024 * 1024

RET_HEADS = 4
RET_QK_DIM = 64
RET_V_DIM = 128
RET_CHUNK = 128
RET_STEP_CHUNKS = 4
ROPE_BASE = 10000.0
DIL_HEADS = 8
DIL_HEAD_DIM = 64
DIL_DILATIONS = (1, 4, 16)
DIL_BLOCK = 128
DIL_SUPER = DIL_BLOCK * max(DIL_DILATIONS)
DIL_GROUP = 1
XATTN_HEADS = 4
XATTN_PARTS = 2
N_GROUPS = 4
EXPERTS_PER_GROUP = 8
N_EXPERTS = N_GROUPS * EXPERTS_PER_GROUP
ROUTE_ROWS = -(-(N_GROUPS + N_EXPERTS) // SUBLANE) * SUBLANE
EXPERT_ROWS = 512
EXPERT_QUANTUM = 256
ROUTE_TILE = 256
LN_EPS = 1e-5
GN_EPS = 1e-6
NEG = -1e30


def _params(*sem):
    return pltpu.CompilerParams(dimension_semantics=sem, vmem_limit_bytes=VMEM_LIMIT)


def _layer_norm(z, g, b):
    mu = jnp.mean(z, axis=-1, keepdims=True)
    zc = z - mu
    var = jnp.mean(zc * zc, axis=-1, keepdims=True)
    return zc * lax.rsqrt(var + LN_EPS) * g + b


def _dot(a, b):
    return jnp.dot(a.astype(BF16), b.astype(BF16), preferred_element_type=F32)


def _dot_nt(a, b):
    return lax.dot_general(a.astype(BF16), b.astype(BF16), (((1,), (1,)), ((), ())),
                           preferred_element_type=F32)


def _dot_tn(a, b):
    return lax.dot_general(a.astype(BF16), b.astype(BF16), (((0,), (0,)), ((), ())),
                           preferred_element_type=F32)


def _proj_in_kernel(x_ref, w_ref, o_ref):
    o_ref[...] = _dot(x_ref[...], w_ref[...]).astype(o_ref.dtype)


def _proj_in(x2d, w_bf16, tm=512):
    t, d = x2d.shape
    n = w_bf16.shape[1]
    return pl.pallas_call(
        _proj_in_kernel,
        grid=(t // tm,),
        in_specs=[pl.BlockSpec((tm, d), lambda i: (i, 0)),
                  pl.BlockSpec((d, n), lambda i: (0, 0))],
        out_specs=pl.BlockSpec((tm, n), lambda i: (i, 0)),
        out_shape=jax.ShapeDtypeStruct((t, n), F32),
        compiler_params=_params("parallel"),
        name="proj_in",
    )(x2d, w_bf16)


def _retention_kernel(qk_ref, v_ref, g_ref, cos_ref, sin_ref, decay_ref, zeta_ref, xi_ref, gam_ref,
                      o_ref, state_ref, y_ref):
    n = pl.program_id(1)

    @pl.when(n == 0)
    def _():
        state_ref[...] = jnp.zeros_like(state_ref)

    c = RET_CHUNK
    n_sub = qk_ref.shape[1] // c
    half = RET_QK_DIM // 2
    lane = lax.broadcasted_iota(jnp.int32, (c, LANE), 1)

    rotated = []
    for j in range(n_sub):
        rows = slice(j * c, (j + 1) * c)
        cos = cos_ref[rows, :]
        sin = sin_ref[rows, :]

        def rot(col, rows=rows, cos=cos, sin=sin):
            t1 = qk_ref[0, rows, col * LANE:(col + 1) * LANE]
            t2 = qk_ref[0, rows, (col + 1) * LANE:(col + 2) * LANE]
            return t1 * cos - t2 * sin, t1 * sin + t2 * cos

        q1, q2 = rot(0)
        k1, k2 = (t * (RET_QK_DIM ** -0.5) for t in rot(2))
        rotated.append((q1, q2, k1, k2, jnp.concatenate([k1, k2], axis=1)))

    for h in range(RET_HEADS):
        cols = slice(h * RET_V_DIM, (h + 1) * RET_V_DIM)
        mine = (lane >= h * half) & (lane < (h + 1) * half)
        zeta = zeta_ref[h]
        st = state_ref[h]
        for j in range(n_sub):
            rows = slice(j * c, (j + 1) * c)
            q1, q2, k1, k2, k_all = rotated[j]
            qm = jnp.concatenate([jnp.where(mine, q1, 0.0), jnp.where(mine, q2, 0.0)], axis=1)
            kz = jnp.concatenate([jnp.where(mine, k1, 0.0) * zeta, jnp.where(mine, k2, 0.0) * zeta], axis=1)
            v = v_ref[0, rows, cols]
            s = _dot_nt(qm, k_all) * decay_ref[h]
            y_ref[rows, cols] = _dot(s, v) + _dot(qm, st) * xi_ref[h]
            st = gam_ref[h] * st + _dot_tn(kz, v)
        state_ref[h] = st

    for h in range(RET_HEADS):
        cols = slice(h * RET_V_DIM, (h + 1) * RET_V_DIM)
        y = y_ref[:, cols]
        mu = jnp.mean(y, axis=-1, keepdims=True)
        yc = y - mu
        var = jnp.mean(yc * yc, axis=-1, keepdims=True)
        yn = yc * lax.rsqrt(var + GN_EPS)
        gate = g_ref[0, :, cols]
        o_ref[0, :, cols] = (gate * jax.nn.sigmoid(gate) * yn).astype(o_ref.dtype)


def _retention_tables(s):
    half = RET_QK_DIM // 2
    inv = 1.0 / (ROPE_BASE ** (jnp.arange(half, dtype=F32) / half))
    ang = jnp.arange(s, dtype=F32)[:, None] * inv[None, :]
    cos_t = jnp.tile(jnp.cos(ang), (1, RET_HEADS))
    sin_t = jnp.tile(jnp.sin(ang), (1, RET_HEADS))
    c = RET_CHUNK
    lg = jnp.log(1.0 - jnp.exp2(-5.0 - jnp.arange(RET_HEADS, dtype=F32)))
    idx = jnp.arange(c, dtype=F32)
    diff = idx[:, None] - idx[None, :]
    decay = jnp.where(diff >= 0, jnp.exp(lg[:, None, None] * jnp.maximum(diff, 0.0)), 0.0)
    lanes = (RET_HEADS, c, LANE)
    zeta = jnp.broadcast_to(jnp.exp(lg[:, None] * (c - 1.0 - idx))[:, :, None], lanes)
    xi = jnp.broadcast_to(jnp.exp(lg[:, None] * (idx + 1.0))[:, :, None], lanes)
    gam = jnp.broadcast_to(jnp.exp(lg * c)[:, None, None], (RET_HEADS, 1, LANE))
    return cos_t, sin_t, decay, zeta, xi, gam


def _retention_weight_layout(w_qk):
    d = w_qk.shape[0]
    half = RET_QK_DIM // 2
    return w_qk.reshape(d, 2, RET_HEADS, 2, half).transpose(0, 1, 3, 2, 4).reshape(d, -1)


def _retention(h3):
    b, s, _ = h3.shape
    c = RET_CHUNK
    qk_w = 2 * RET_HEADS * RET_QK_DIM
    v_w = RET_HEADS * RET_V_DIM
    assert qk_w == v_w
    cos_t, sin_t, decay, zeta, xi, gam = _retention_tables(s)
    const3 = lambda bi, n: (0, 0, 0)
    rows = RET_STEP_CHUNKS * c
    return pl.pallas_call(
        _retention_kernel,
        grid=(b, s // rows),
        in_specs=[pl.BlockSpec((1, rows, qk_w), lambda bi, n: (bi, n, 0)),
                  pl.BlockSpec((1, rows, v_w), lambda bi, n: (bi, n, 1)),
                  pl.BlockSpec((1, rows, v_w), lambda bi, n: (bi, n, 2)),
                  pl.BlockSpec((rows, LANE), lambda bi, n: (n, 0)),
                  pl.BlockSpec((rows, LANE), lambda bi, n: (n, 0)),
                  pl.BlockSpec((RET_HEADS, c, c), const3),
                  pl.BlockSpec((RET_HEADS, c, LANE), const3),
                  pl.BlockSpec((RET_HEADS, c, LANE), const3),
                  pl.BlockSpec((RET_HEADS, 1, LANE), const3)],
        out_specs=pl.BlockSpec((1, rows, v_w), lambda bi, n: (bi, n, 0)),
        out_shape=jax.ShapeDtypeStruct((b, s, v_w), BF16),
        scratch_shapes=[pltpu.VMEM((RET_HEADS, RET_HEADS * RET_QK_DIM, RET_V_DIM), F32),
                        pltpu.VMEM((rows, v_w), F32)],
        compiler_params=_params("parallel", "arbitrary"),
        name="retention",
    )(h3, h3, h3, cos_t, sin_t, decay, zeta, xi, gam)


def _dilated_kernel(q_ref, kp_ref, kc_ref, vp_ref, vc_ref, bias_ref, o_ref, acc_ref, m_ref, l_ref,
                    s0_ref, s1_ref, p0_ref, p1_ref):
    j = pl.program_id(2)
    sup = DIL_SUPER
    q_blk = DIL_BLOCK
    scale = DIL_HEAD_DIM ** -0.5 * math.log2(math.e)
    n_blocks = sup // q_blk
    lane = lax.broadcasted_iota(jnp.int32, (q_blk, LANE), 1)
    head0 = lane < DIL_HEAD_DIM
    first_bias = jnp.where(j == 0, 1, 0)

    groups = []
    for bi, d in enumerate(DIL_DILATIONS):
        n_per_r = sup // (q_blk * d)
        for t0 in range(0, n_blocks, DIL_GROUP):
            blocks = []
            for t in range(t0, t0 + DIL_GROUP):
                r, n = divmod(t, n_per_r)
                blocks.append((n * (q_blk * d) + r, (n - 1) * (q_blk * d) + r, n))
            groups.append((bi, d, blocks))
    s_bufs = (s0_ref, s1_ref)
    p_bufs = (p0_ref, p1_ref)

    def window(prev_ref, cur_ref, k_start, d):
        if k_start >= 0:
            return cur_ref[0, pl.ds(k_start, 2 * q_blk, stride=d), :]
        return jnp.concatenate([prev_ref[0, pl.ds(sup + k_start, q_blk, stride=d), :],
                                cur_ref[0, pl.ds(k_start + q_blk * d, q_blk, stride=d), :]], axis=0)

    def scores(gi):
        bi, d, blocks = groups[gi]
        for g, (q_start, k_start, n) in enumerate(blocks):
            q = q_ref[0, pl.ds(q_start, q_blk, stride=d), :] * scale
            kb = window(kp_ref, kc_ref, k_start, d)
            q2 = jnp.concatenate([jnp.where(head0, q, 0.0), jnp.where(head0, 0.0, q)], axis=0)
            bias = bias_ref[first_bias] if n == 0 else bias_ref[0]
            s_bufs[gi % 2][g] = _dot_nt(q2, kb) + bias

    def softmax(gi):
        bi, d, blocks = groups[gi]
        for g, (q_start, k_start, n) in enumerate(blocks):
            s = s_bufs[gi % 2][g]
            m2 = jnp.max(s, axis=-1, keepdims=True)
            p_bufs[gi % 2][g] = jnp.exp2(s - m2).astype(BF16)
            m_ref[bi, pl.ds(q_start, q_blk, stride=d), :] = jnp.where(head0, m2[:q_blk], m2[q_blk:])

    def values(gi):
        bi, d, blocks = groups[gi]
        for g, (q_start, k_start, n) in enumerate(blocks):
            vb = window(vp_ref, vc_ref, k_start, d).astype(BF16)
            o2 = jnp.dot(p_bufs[gi % 2][g], jnp.concatenate([vb, jnp.ones_like(vb)], axis=1),
                         preferred_element_type=F32)
            rows = pl.ds(q_start, q_blk, stride=d)
            acc_ref[bi, rows, :] = jnp.where(head0, o2[:q_blk, :LANE], o2[q_blk:, :LANE])
            l_ref[bi, rows, :] = jnp.where(head0, o2[:q_blk, LANE:], o2[q_blk:, LANE:])

    for step in range(len(groups) + 2):
        if step < len(groups):
            scores(step)
        if 0 <= step - 1 < len(groups):
            softmax(step - 1)
        if step - 2 >= 0:
            values(step - 2)

    def merge(c, carry):
        rows = pl.ds(pl.multiple_of(c * q_blk, q_blk), q_blk)
        ms = [m_ref[bi, rows, :] for bi in range(len(DIL_DILATIONS))]
        m_all = functools.reduce(jnp.maximum, ms)
        ws = [jnp.exp2(m - m_all) for m in ms]
        num = functools.reduce(lambda a, b: a + b, [w * acc_ref[bi, rows, :] for bi, w in enumerate(ws)])
        den = functools.reduce(lambda a, b: a + b, [w * l_ref[bi, rows, :] for bi, w in enumerate(ws)])
        o_ref[0, rows, :] = (num / den).astype(o_ref.dtype)
        return carry

    lax.fori_loop(0, n_blocks, merge, 0)


def _dilated_bias():
    q_blk = DIL_BLOCK
    qi = jnp.arange(2 * q_blk)[:, None] % q_blk
    kj = jnp.arange(2 * q_blk)[None, :]
    band = (kj >= qi) & (kj <= qi + q_blk)
    return jnp.stack([jnp.where(band, 0.0, NEG), jnp.where(band & (kj >= q_blk), 0.0, NEG)]).astype(F32)


def _dilated(h3, col0):
    b, s, _ = h3.shape
    sup = DIL_SUPER
    n_pairs = DIL_HEADS * DIL_HEAD_DIM // LANE
    n_br = len(DIL_DILATIONS)
    cq, ck, cv = col0 * n_pairs, (col0 + 1) * n_pairs, (col0 + 2) * n_pairs
    cur = lambda c: (lambda bi, p, j: (bi, j, c + p))
    prev = lambda c: (lambda bi, p, j: (bi, jnp.maximum(j - 1, 0), c + p))
    blk = (1, sup, LANE)
    return pl.pallas_call(
        _dilated_kernel,
        grid=(b, n_pairs, s // sup),
        in_specs=[pl.BlockSpec(blk, cur(cq)),
                  pl.BlockSpec(blk, prev(ck)), pl.BlockSpec(blk, cur(ck)),
                  pl.BlockSpec(blk, prev(cv)), pl.BlockSpec(blk, cur(cv)),
                  pl.BlockSpec((2, 2 * DIL_BLOCK, 2 * DIL_BLOCK), lambda bi, p, j: (0, 0, 0))],
        out_specs=pl.BlockSpec(blk, lambda bi, p, j: (bi, j, p)),
        out_shape=jax.ShapeDtypeStruct((b, s, n_pairs * LANE), BF16),
        scratch_shapes=[pltpu.VMEM((n_br, sup, LANE), F32), pltpu.VMEM((n_br, sup, LANE), F32),
                        pltpu.VMEM((n_br, sup, LANE), F32),
                        pltpu.VMEM((DIL_GROUP, 2 * DIL_BLOCK, 2 * DIL_BLOCK), F32),
                        pltpu.VMEM((DIL_GROUP, 2 * DIL_BLOCK, 2 * DIL_BLOCK), F32),
                        pltpu.VMEM((DIL_GROUP, 2 * DIL_BLOCK, 2 * DIL_BLOCK), BF16),
                        pltpu.VMEM((DIL_GROUP, 2 * DIL_BLOCK, 2 * DIL_BLOCK), BF16)],
        compiler_params=_params("parallel", "parallel", "arbitrary"),
        name="dilated",
    )(h3, h3, h3, h3, h3, _dilated_bias())


def _mem_kv(mem2d, w_bf16, tn=512):
    m, d = mem2d.shape
    n = w_bf16.shape[1]
    return pl.pallas_call(
        _proj_in_kernel,
        grid=(n // tn,),
        in_specs=[pl.BlockSpec((m, d), lambda i: (0, 0)),
                  pl.BlockSpec((d, tn), lambda i: (0, i))],
        out_specs=pl.BlockSpec((m, tn), lambda i: (0, i)),
        out_shape=jax.ShapeDtypeStruct((m, n), BF16),
        compiler_params=_params("parallel"),
        name="mem_kv",
    )(mem2d, w_bf16)


def _xattn_kernel(alpha, x_ref, yr_ref, yd_ref, wout_ref, g1_ref, b1_ref, k_ref, v_ref, wq_ref, wo_ref,
                  g_ref, b_ref, wr_ref, o_ref, ot_ref, lg_ref, x1_ref, q_ref, att_ref):
    tm, d = x_ref.shape
    dh = d // XATTN_HEADS
    wr = yr_ref.shape[1]
    scale = dh ** -0.5 * math.log2(math.e)
    part = tm // XATTN_PARTS

    def mix(rows):
        y = _dot(yr_ref[rows, :], wout_ref[0:wr, :]) + _dot(yd_ref[rows, :], wout_ref[wr:, :])
        x1_ref[rows, :] = _layer_norm(alpha * x_ref[rows, :] + y, g1_ref[...], b1_ref[...])

    def query(rows):
        q_ref[rows, :] = _dot(x1_ref[rows, :], wq_ref[...]).astype(BF16)

    def attend(rows):
        for h in range(XATTN_HEADS):
            sl = slice(h * dh, (h + 1) * dh)
            s = _dot_nt(q_ref[rows, sl], k_ref[0, :, sl]) * scale
            m = jnp.max(s, axis=-1, keepdims=True)
            e = jnp.exp2(s - m)
            p = e / jnp.sum(e, axis=-1, keepdims=True)
            att_ref[rows, sl] = _dot(p, v_ref[0, :, sl]).astype(BF16)

    def finish(rows, r0):
        y = _dot(att_ref[rows, :], wo_ref[...])
        x2 = _layer_norm(alpha * x1_ref[rows, :] + y, g_ref[...], b_ref[...])
        o_ref[rows, :] = x2
        for c in range(d // LANE):
            ot_ref[pl.ds(r0 * SUBLANE + c, part, stride=SUBLANE), :] = x2[:, c * LANE:(c + 1) * LANE]
        lg_ref[:, rows] = _dot_nt(wr_ref[...], x2)

    stages = (mix, query, attend, finish)
    for step in range(XATTN_PARTS + len(stages) - 1):
        for si, stage in enumerate(stages):
            pi = step - si
            if 0 <= pi < XATTN_PARTS:
                rows = slice(pi * part, (pi + 1) * part)
                if stage is finish:
                    stage(rows, pi * part)
                else:
                    stage(rows)


def _xattn(x2d, y_ret, y_dil, w_out, g1, b1, kv, wq, wo, g, bb, w_r, alpha, seq, tm=1024):
    t, d = x2d.shape
    mlen = kv.shape[1]
    tiles_per_seq = seq // tm
    row = lambda i: (i, 0)

    def whole(shape):
        return pl.BlockSpec(shape, lambda i: (0,) * len(shape), pipeline_mode=pl.Buffered(1))

    vec = whole((1, d))
    return pl.pallas_call(
        functools.partial(_xattn_kernel, alpha),
        grid=(t // tm,),
        in_specs=[pl.BlockSpec((tm, d), row),
                  pl.BlockSpec((tm, y_ret.shape[1]), row),
                  pl.BlockSpec((tm, y_dil.shape[1]), row),
                  whole(w_out.shape), vec, vec,
                  pl.BlockSpec((1, mlen, d), lambda i: (i // tiles_per_seq, 0, 0)),
                  pl.BlockSpec((1, mlen, d), lambda i: (i // tiles_per_seq, 0, 1)),
                  whole((d, d)), whole((d, d)), vec, vec,
                  whole((ROUTE_ROWS, d))],
        out_specs=[pl.BlockSpec((tm, d), row),
                   pl.BlockSpec((tm * SUBLANE, LANE), row),
                   pl.BlockSpec((ROUTE_ROWS, tm), lambda i: (0, i))],
        out_shape=[jax.ShapeDtypeStruct((t, d), F32),
                   jax.ShapeDtypeStruct((t * SUBLANE, LANE), F32),
                   jax.ShapeDtypeStruct((ROUTE_ROWS, t), F32)],
        scratch_shapes=[pltpu.VMEM((tm, d), F32), pltpu.VMEM((tm, d), BF16), pltpu.VMEM((tm, d), BF16)],
        compiler_params=_params("parallel"),
        name="xattn",
    )(x2d, y_ret, y_dil, w_out, g1, b1, kv, kv, wq, wo, g, bb, w_r)


def _route_kernel(lg_ref, before_ref, meta_ref, cnt_ref, carry_ref):
    i = pl.program_id(0)

    @pl.when(i == 0)
    def _():
        carry_ref[...] = jnp.zeros_like(carry_ref)

    rows, tt = lg_ref.shape
    sub = before_ref.shape[0]
    r = lax.broadcasted_iota(jnp.int32, (rows, sub), 0)
    r8 = lax.broadcasted_iota(jnp.int32, (SUBLANE, sub), 0)
    is_group = r < N_GROUPS

    def col_max(a):
        return jnp.max(a, axis=0, keepdims=True)

    def first_row_where(mask):
        return jnp.min(jnp.where(mask, r, rows), axis=0, keepdims=True)

    for c in range(tt // sub):
        lg = lg_ref[:, c * sub:(c + 1) * sub]
        mg = col_max(jnp.where(is_group, lg, NEG))
        eg = jnp.where(is_group, jnp.exp(lg - mg), 0.0)
        pg = eg / jnp.sum(eg, axis=0, keepdims=True)
        g1 = col_max(pg)
        gi = first_row_where(is_group & (pg == g1))
        lo = N_GROUPS + gi * EXPERTS_PER_GROUP
        in_grp = (r >= lo) & (r < lo + EXPERTS_PER_GROUP)
        v1 = col_max(jnp.where(in_grp, lg, NEG))
        i1 = first_row_where(in_grp & (lg == v1))
        rest = in_grp & (r != i1)
        v2 = col_max(jnp.where(rest, lg, NEG))
        i2 = first_row_where(rest & (lg == v2))
        e2 = jnp.exp(v2 - v1)
        den = 1.0 + e2
        gate1 = g1 * (1.0 / den)
        gate2 = g1 * (e2 / den)
        sel1 = r == i1
        sel2 = r == i2
        onehot = jnp.where(sel1 | sel2, 1.0, 0.0)
        rank = _dot(onehot, before_ref[...]) + carry_ref[:, 0:1]
        r1 = jnp.sum(jnp.where(sel1, rank, 0.0), axis=0, keepdims=True)
        r2 = jnp.sum(jnp.where(sel2, rank, 0.0), axis=0, keepdims=True)
        carry_ref[...] = carry_ref[...] + jnp.sum(onehot, axis=1, keepdims=True)
        meta = jnp.where(r8 == 0, (i1 - N_GROUPS).astype(F32), 0.0)
        meta = jnp.where(r8 == 1, (i2 - N_GROUPS).astype(F32), meta)
        meta = jnp.where(r8 == 2, r1, meta)
        meta = jnp.where(r8 == 3, r2, meta)
        meta = jnp.where(r8 == 4, gate1, meta)
        meta = jnp.where(r8 == 5, gate2, meta)
        meta_ref[:, c * sub:(c + 1) * sub] = meta
    cnt_ref[...] = carry_ref[...]


def _route(logits_t, tt=1024, sub=256):
    rows, t = logits_t.shape
    before = (jnp.arange(sub)[:, None] < jnp.arange(sub)[None, :]).astype(BF16)
    return pl.pallas_call(
        _route_kernel,
        grid=(t // tt,),
        in_specs=[pl.BlockSpec((rows, tt), lambda i: (0, i)),
                  pl.BlockSpec((sub, sub), lambda i: (0, 0))],
        out_specs=[pl.BlockSpec((SUBLANE, tt), lambda i: (0, i)),
                   pl.BlockSpec((rows, LANE), lambda i: (0, 0))],
        out_shape=[jax.ShapeDtypeStruct((SUBLANE, t), F32),
                   jax.ShapeDtypeStruct((rows, LANE), F32)],
        scratch_shapes=[pltpu.VMEM((rows, LANE), F32)],
        compiler_params=_params("arbitrary"),
        name="route",
    )(logits_t, before)


def _plan_kernel(meta_ref, first_row_ref, dest_ref):
    n_tiles = dest_ref.shape[0]
    tt = dest_ref.shape[2] // 2
    rows = first_row_ref.shape[0]
    first_row = first_row_ref[:, 0:1]
    r = lax.broadcasted_iota(jnp.int32, (rows, tt), 0)
    for g in range(n_tiles):
        m = meta_ref[:, g * tt:(g + 1) * tt]

        def dest_of(k):
            e_row = m[k:k + 1, :].astype(jnp.int32) + N_GROUPS
            return jnp.sum(jnp.where(r == e_row, first_row, 0.0), axis=0, keepdims=True) + m[2 + k:3 + k, :]

        dest_ref[g] = jnp.concatenate([dest_of(0), dest_of(1)], axis=1).astype(jnp.int32)


def _plan(meta_t, first_row, tt, tiles_per_step=4):
    t = meta_t.shape[1]
    rows = first_row.shape[0]
    return pl.pallas_call(
        _plan_kernel,
        grid=(t // (tt * tiles_per_step),),
        in_specs=[pl.BlockSpec((SUBLANE, tt * tiles_per_step), lambda i: (0, i)),
                  pl.BlockSpec((rows, LANE), lambda i: (0, 0))],
        out_specs=pl.BlockSpec((tiles_per_step, 1, 2 * tt), lambda i: (i, 0, 0)),
        out_shape=jax.ShapeDtypeStruct((t // tt, 1, 2 * tt), jnp.int32),
        compiler_params=_params("parallel"),
        name="plan",
    )(meta_t, first_row)


def _row_copy(src, dst, s_row, d_row, sem):
    return pltpu.make_async_copy(src.at[pl.ds(pl.multiple_of(s_row * SUBLANE, SUBLANE), SUBLANE), :],
                                 dst.at[pl.ds(pl.multiple_of(d_row * SUBLANE, SUBLANE), SUBLANE), :], sem)


def _dispatch_kernel(dest_ref, xt_ref, xs_ref, sem):
    tq = dest_ref.shape[2] // 2

    def issue(t, c):
        _row_copy(xt_ref, xs_ref, t, dest_ref[0, 0, t], sem).start(priority=0)
        _row_copy(xt_ref, xs_ref, t, dest_ref[0, 0, tq + t], sem).start(priority=1)
        return c

    lax.fori_loop(0, tq, issue, 0, unroll=8)
    for _ in range(2):
        pltpu.make_async_copy(xt_ref, xs_ref.at[pl.ds(0, tq * SUBLANE), :], sem).wait()


def _dispatch(dest3, x_tiled, n_rows, tiles_per_step=4):
    n_tiles, _, two_tt = dest3.shape
    tt = two_tt // 2
    tq = tiles_per_step * tt
    n_steps = n_tiles // tiles_per_step
    dest3 = dest3.reshape(n_steps, tiles_per_step, 2, tt).transpose(0, 2, 1, 3).reshape(n_steps, 1, 2 * tq)
    return pl.pallas_call(
        _dispatch_kernel,
        grid=(n_steps,),
        in_specs=[pl.BlockSpec((1, 1, 2 * tq), lambda i: (i, 0, 0), memory_space=pltpu.SMEM),
                  pl.BlockSpec((tq * SUBLANE, LANE), lambda i: (i, 0))],
        out_specs=pl.BlockSpec(memory_space=pl.ANY),
        out_shape=jax.ShapeDtypeStruct((n_rows * SUBLANE, LANE), F32),
        scratch_shapes=[pltpu.SemaphoreType.DMA(())],
        compiler_params=_params("arbitrary"),
        name="dispatch",
    )(dest3, x_tiled)


def _experts_kernel(blk_exp_ref, blk_rows_ref, n_used_ref, blk_first_ref, blk_slot_ref, blk_next_ref,
                    xs_ref, wg_hbm, wu_hbm, wd_hbm, ys_ref, x_scr, y_scr, wg_buf, wu_buf, wd_buf, sems):
    b = pl.program_id(0)

    def weight_copies(e, slot):
        return [pltpu.make_async_copy(hbm.at[e], buf.at[slot], sems.at[k, slot])
                for k, (hbm, buf) in enumerate(((wg_hbm, wg_buf), (wu_hbm, wu_buf), (wd_hbm, wd_buf)))]

    @pl.when(b < n_used_ref[0])
    def _():
        slot = blk_slot_ref[b]

        @pl.when(b == 0)
        def _():
            for cp in weight_copies(blk_exp_ref[0], slot):
                cp.start()

        @pl.when(blk_first_ref[b] == 1)
        def _():
            @pl.when(blk_next_ref[b] >= 0)
            def _():
                for cp in weight_copies(blk_next_ref[b], 1 - slot):
                    cp.start()

            for cp in weight_copies(blk_exp_ref[b], slot):
                cp.wait()

        rows, d = x_scr.shape
        n_valid = blk_rows_ref[b]

        def run(m):
            ri = lax.broadcasted_iota(jnp.int32, (m, 1), 0)
            for c in range(d // LANE):
                x = xs_ref[pl.ds(c, m, stride=SUBLANE), :]
                x_scr[0:m, c * LANE:(c + 1) * LANE] = jnp.where(ri < n_valid, x, 0.0).astype(BF16)
            x = x_scr[0:m, :]
            hg = _dot(x, wg_buf[slot])
            hu = _dot(x, wu_buf[slot])
            y_scr[0:m, :] = _dot(hg * jax.nn.sigmoid(hg) * hu, wd_buf[slot])
            for c in range(d // LANE):
                ys_ref[pl.ds(c, m, stride=SUBLANE), :] = y_scr[0:m, c * LANE:(c + 1) * LANE]

        n_quanta = rows // EXPERT_QUANTUM
        for k in range(1, n_quanta + 1):
            lo = (k - 1) * EXPERT_QUANTUM
            pl.when((n_valid > lo) & (n_valid <= k * EXPERT_QUANTUM))(functools.partial(run, k * EXPERT_QUANTUM))


def _experts(blk_exp, blk_rows, n_used, blk_first, blk_slot, blk_next, xs_tiled, w_g, w_u, w_d):
    n_rows = xs_tiled.shape[0] // SUBLANE
    rows = EXPERT_ROWS
    nblk = n_rows // rows
    _, d, ff = w_g.shape
    used = lambda b, be, br, nu, *_: (jnp.minimum(b, nu[0] - 1), 0)
    grid_spec = pltpu.PrefetchScalarGridSpec(
        num_scalar_prefetch=6,
        grid=(nblk,),
        in_specs=[pl.BlockSpec((rows * SUBLANE, LANE), used),
                  pl.BlockSpec(memory_space=pl.ANY), pl.BlockSpec(memory_space=pl.ANY),
                  pl.BlockSpec(memory_space=pl.ANY)],
        out_specs=pl.BlockSpec((rows * SUBLANE, LANE), used),
        scratch_shapes=[pltpu.VMEM((rows, d), BF16), pltpu.VMEM((rows, d), F32),
                        pltpu.VMEM((2, d, ff), F32), pltpu.VMEM((2, d, ff), F32), pltpu.VMEM((2, ff, d), F32),
                        pltpu.SemaphoreType.DMA((3, 2))],
    )
    return pl.pallas_call(
        _experts_kernel,
        grid_spec=grid_spec,
        out_shape=jax.ShapeDtypeStruct((n_rows * SUBLANE, LANE), F32),
        compiler_params=_params("arbitrary"),
        name="experts",
    )(blk_exp, blk_rows, n_used, blk_first, blk_slot, blk_next, xs_tiled, w_g, w_u, w_d)


def _combine_kernel(alpha, dcur_ref, dnext_ref, meta_ref, x_ref, g_ref, b_ref, ys_ref, o_ref, buf_ref, sems):
    i = pl.program_id(0)
    tq, d = x_ref.shape
    slot_rows = 2 * tq * SUBLANE

    def issue(d_ref, slot, t):
        _row_copy(ys_ref, buf_ref, d_ref[0, 0, t], slot * (2 * tq) + t, sems.at[slot]).start(priority=0)
        _row_copy(ys_ref, buf_ref, d_ref[0, 0, tq + t], slot * (2 * tq) + tq + t, sems.at[slot]).start(priority=1)

    def wait_slot(slot):
        off = pl.multiple_of(slot * slot_rows, slot_rows)
        pltpu.make_async_copy(ys_ref.at[pl.ds(0, slot_rows), :], buf_ref.at[pl.ds(off, slot_rows), :],
                              sems.at[slot]).wait()
        return off

    def issue_all(d_ref, slot):
        lax.fori_loop(0, tq, lambda t, c: (issue(d_ref, slot, t), c)[1], 0, unroll=4)

    slot = i % 2

    @pl.when(i == 0)
    def _():
        issue_all(dcur_ref, 0)

    @pl.when(i + 1 < pl.num_programs(0))
    def _():
        issue_all(dnext_ref, 1 - slot)

    off = wait_slot(slot)
    meta_rows = jnp.transpose(jnp.concatenate([meta_ref[...], jnp.zeros((LANE - SUBLANE, tq), F32)], axis=0))
    gate1 = meta_rows[:, 4:5]
    gate2 = meta_rows[:, 5:6]
    for c in range(d // LANE):
        sl = slice(c * LANE, (c + 1) * LANE)
        y1 = buf_ref[pl.ds(off + c, tq, stride=SUBLANE), :]
        y2 = buf_ref[pl.ds(off + tq * SUBLANE + c, tq, stride=SUBLANE), :]
        o_ref[:, sl] = alpha * x_ref[:, sl] + (y1 * gate1 + y2 * gate2)
    o_ref[...] = _layer_norm(o_ref[...], g_ref[...], b_ref[...])


def _combine(dest3, meta, x2, g, bb, ys_tiled, alpha):
    t, d = x2.shape
    n = dest3.shape[0]
    tq = t // n
    row = lambda i: (i, 0)
    const = lambda i: (0, 0)
    return pl.pallas_call(
        functools.partial(_combine_kernel, alpha),
        grid=(n,),
        in_specs=[pl.BlockSpec((1, 1, 2 * tq), lambda i: (i, 0, 0), memory_space=pltpu.SMEM),
                  pl.BlockSpec((1, 1, 2 * tq), lambda i: (jnp.minimum(i + 1, n - 1), 0, 0), memory_space=pltpu.SMEM),
                  pl.BlockSpec((SUBLANE, tq), lambda i: (0, i)),
                  pl.BlockSpec((tq, d), row),
                  pl.BlockSpec((1, d), const), pl.BlockSpec((1, d), const),
                  pl.BlockSpec(memory_space=pl.ANY)],
        out_specs=pl.BlockSpec((tq, d), row),
        out_shape=jax.ShapeDtypeStruct((t, d), F32),
        scratch_shapes=[pltpu.VMEM((2 * 2 * tq * SUBLANE, LANE), F32), pltpu.SemaphoreType.DMA((2,))],
        compiler_params=_params("arbitrary"),
        name="combine",
    )(dest3, dest3, meta, x2, g, bb, ys_tiled)


def _moe(x2, x2_tiled, logits, w_g, w_u, w_d, g, bb, alpha):
    t, d = x2.shape
    meta, cnt = _route(logits)
    counts = cnt[N_GROUPS:N_GROUPS + N_EXPERTS, 0].astype(jnp.int32)
    padded = ((counts + EXPERT_ROWS - 1) // EXPERT_ROWS) * EXPERT_ROWS
    pend = jnp.cumsum(padded)
    poff = pend - padded
    first_row = jnp.pad(poff.astype(F32), (N_GROUPS, ROUTE_ROWS - N_GROUPS - N_EXPERTS))
    dest3 = _plan(meta, jnp.broadcast_to(first_row[:, None], (ROUTE_ROWS, LANE)), ROUTE_TILE)
    n_rows = t * 2 + N_EXPERTS * EXPERT_ROWS
    nblk = n_rows // EXPERT_ROWS
    blk_start = jnp.arange(nblk, dtype=jnp.int32) * EXPERT_ROWS
    blk_exp = jnp.sum((pend[None, :] <= blk_start[:, None]).astype(jnp.int32), axis=1)
    blk_exp = jnp.minimum(blk_exp, N_EXPERTS - 1)
    e_idx = jnp.arange(N_EXPERTS, dtype=jnp.int32)
    owner = blk_exp[:, None] == e_idx[None, :]

    def of_block(per_expert):
        return jnp.sum(jnp.where(owner, per_expert[None, :], 0), axis=1).astype(jnp.int32)

    blk_rows = jnp.clip(of_block(poff + counts) - blk_start, 0, EXPERT_ROWS).astype(jnp.int32)
    n_used = (pend[-1:] // EXPERT_ROWS).astype(jnp.int32)
    has_rows = counts > 0
    ordinal = jnp.cumsum(has_rows.astype(jnp.int32)) - 1
    later = has_rows[None, :] & (e_idx[None, :] > e_idx[:, None])
    next_used = jnp.min(jnp.where(later, e_idx[None, :], N_EXPERTS), axis=1)
    next_used = jnp.where(next_used == N_EXPERTS, -1, next_used).astype(jnp.int32)
    blk_first = (blk_start == of_block(poff)).astype(jnp.int32)
    blk_slot = of_block(ordinal % 2)
    blk_next = of_block(next_used)
    xs_tiled = _dispatch(dest3, x2_tiled, n_rows)
    ys_tiled = _experts(blk_exp, blk_rows, n_used, blk_first, blk_slot, blk_next, xs_tiled, w_g, w_u, w_d)
    return _combine(dest3, meta, x2, g, bb, ys_tiled, alpha)


def kernel(x, mem, w_in, w_out, ln_mix_g, ln_mix_b, w_xq, w_xkv, w_xo, ln_x_g, ln_x_b, w_route_group,
           w_route_expert, w_exp_gate, w_exp_up, w_exp_down, ln_moe_g, ln_moe_b):
    b, s, d = x.shape
    depth = w_in.shape[0]
    alpha = (2.0 * depth) ** 0.25
    t = b * s
    ret_cols = (2 * RET_HEADS * RET_QK_DIM + 2 * RET_HEADS * RET_V_DIM)
    assert ret_cols % (DIL_HEADS * DIL_HEAD_DIM) == 0
    dil_col0 = ret_cols // (DIL_HEADS * DIL_HEAD_DIM)
    xc = x.reshape(t, d)
    for l in range(depth):
        n_qk = 2 * RET_HEADS * RET_QK_DIM
        w_in_l = jnp.concatenate([_retention_weight_layout(w_in[l][:, :n_qk]), w_in[l][:, n_qk:]], axis=1)
        h = _proj_in(xc, w_in_l.astype(BF16))
        h3 = h.reshape(b, s, h.shape[1])
        y_ret = _retention(h3).reshape(t, -1)
        y_dil = _dilated(h3, dil_col0).reshape(t, -1)
        kv =_mem_kv(mem.reshape(b * mem.shape[1], d), w_xkv[l].astype(BF16)).reshape(b, mem.shape[1], 2 * d)
        w_r = jnp.concatenate([w_route_group[l], w_route_expert[l]], axis=-1)
        w_r = jnp.pad(w_r.T, ((0, ROUTE_ROWS - w_r.shape[1]), (0, 0))).astype(BF16)
        x2, x2_tiled, logits = _xattn(xc, y_ret, y_dil, w_out[l].astype(BF16), ln_mix_g[l][None], ln_mix_b[l][None],
                                      kv, w_xq[l].astype(BF16), w_xo[l].astype(BF16),
                                      ln_x_g[l][None], ln_x_b[l][None], w_r, alpha, s)
        xc = _moe(x2, x2_tiled, logits, w_exp_gate[l], w_exp_up[l], w_exp_down[l],
                  ln_moe_g[l][None], ln_moe_b[l][None], alpha)
    return xc.reshape(b, s, d)
```

```python
import functools
import math

import jax
import jax.numpy as jnp
import numpy as np
from jax import lax
from jax.experimental import pallas as pl
from jax.experimental.pallas import tpu as pltpu

BF16 = jnp.bfloat16
F32 = jnp.float32

LANE = 128
SUBLANE = 8
VMEM_LIMIT = 56 * 1024 * 1024

RET_HEADS = 4
RET_QK_DIM = 64
RET_V_DIM = 128
RET_CHUNK = 128
RET_STEP_CHUNKS = 4
ROPE_BASE = 10000.0
DIL_HEADS = 8
DIL_HEAD_DIM = 64
DIL_DILATIONS = (1, 4, 16)
DIL_BLOCK = 128
DIL_SUPER = DIL_BLOCK * max(DIL_DILATIONS)
DIL_GROUP = 1
XATTN_HEADS = 4
XATTN_PARTS = 2
N_GROUPS = 4
EXPERTS_PER_GROUP = 8
N_EXPERTS = N_GROUPS * EXPERTS_PER_GROUP
ROUTE_ROWS = -(-(N_GROUPS + N_EXPERTS) // SUBLANE) * SUBLANE
EXPERT_ROWS = 512
EXPERT_QUANTUM = 256
ROUTE_TILE = 256
LN_EPS = 1e-5
GN_EPS = 1e-6
NEG = -1e30


def _params(*sem):
    return pltpu.CompilerParams(dimension_semantics=sem, vmem_limit_bytes=VMEM_LIMIT)


def _layer_norm(z, g, b):
    mu = jnp.mean(z, axis=-1, keepdims=True)
    zc = z - mu
    var = jnp.mean(zc * zc, axis=-1, keepdims=True)
    return zc * lax.rsqrt(var + LN_EPS) * g + b


def _dot(a, b):
    return jnp.dot(a.astype(BF16), b.astype(BF16), preferred_element_type=F32)


def _dot_nt(a, b):
    return lax.dot_general(a.astype(BF16), b.astype(BF16), (((1,), (1,)), ((), ())),
                           preferred_element_type=F32)


def _dot_tn(a, b):
    return lax.dot_general(a.astype(BF16), b.astype(BF16), (((0,), (0,)), ((), ())),
                           preferred_element_type=F32)


def _proj_in_kernel(x_ref, w_ref, o_ref):
    o_ref[...] = _dot(x_ref[...], w_ref[...]).astype(o_ref.dtype)


def _proj_in2_kernel(x_ref, w_ref, w_head_ref, o_ref, w_bf16):
    @pl.when(pl.program_id(0) == 0)
    def _():
        w_bf16[...] = w_ref[...].astype(BF16)
        w_bf16[:, 0:w_head_ref.shape[1]] = w_head_ref[...]

    o_ref[...] = _dot(x_ref[...], w_bf16[...])


def _proj_in(x2d, w_f32, w_head_bf16, tm=512):
    t, d = x2d.shape
    n = w_f32.shape[1]
    whole = lambda shape: pl.BlockSpec(shape, lambda i: (0, 0), pipeline_mode=pl.Buffered(1))
    return pl.pallas_call(
        _proj_in2_kernel,
        grid=(t // tm,),
        in_specs=[pl.BlockSpec((tm, d), lambda i: (i, 0)), whole((d, n)), whole(w_head_bf16.shape)],
        out_specs=pl.BlockSpec((tm, n), lambda i: (i, 0)),
        out_shape=jax.ShapeDtypeStruct((t, n), F32),
        scratch_shapes=[pltpu.VMEM((d, n), BF16)],
        compiler_params=_params("arbitrary"),
        name="proj_in",
    )(x2d, w_f32, w_head_bf16)


def _retention_kernel(qk_ref, v_ref, g_ref, cos_ref, sin_ref, decay_ref, zeta_ref, xi_ref, gam_ref,
                      o_ref, state_ref, y_ref):
    n = pl.program_id(1)

    @pl.when(n == 0)
    def _():
        state_ref[...] = jnp.zeros_like(state_ref)

    c = RET_CHUNK
    n_sub = qk_ref.shape[1] // c
    half = RET_QK_DIM // 2
    lane = lax.broadcasted_iota(jnp.int32, (c, LANE), 1)

    rotated = []
    for j in range(n_sub):
        rows = slice(j * c, (j + 1) * c)
        cos = cos_ref[rows, :]
        sin = sin_ref[rows, :]

        def rot(col, rows=rows, cos=cos, sin=sin):
            t1 = qk_ref[0, rows, col * LANE:(col + 1) * LANE]
            t2 = qk_ref[0, rows, (col + 1) * LANE:(col + 2) * LANE]
            return t1 * cos - t2 * sin, t1 * sin + t2 * cos

        q1, q2 = rot(0)
        k1, k2 = (t * (RET_QK_DIM ** -0.5) for t in rot(2))
        rotated.append((q1, q2, k1, k2, jnp.concatenate([k1, k2], axis=1)))

    for h in range(RET_HEADS):
        cols = slice(h * RET_V_DIM, (h + 1) * RET_V_DIM)
        mine = (lane >= h * half) & (lane < (h + 1) * half)
        zeta = zeta_ref[h]
        st = state_ref[h]
        for j in range(n_sub):
            rows = slice(j * c, (j + 1) * c)
            q1, q2, k1, k2, k_all = rotated[j]
            qm = jnp.concatenate([jnp.where(mine, q1, 0.0), jnp.where(mine, q2, 0.0)], axis=1)
            kz = jnp.concatenate([jnp.where(mine, k1, 0.0) * zeta, jnp.where(mine, k2, 0.0) * zeta], axis=1)
            v = v_ref[0, rows, cols]
            s = _dot_nt(qm, k_all) * decay_ref[h]
            y_ref[rows, cols] = _dot(s, v) + _dot(qm, st) * xi_ref[h]
            st = gam_ref[h] * st + _dot_tn(kz, v)
        state_ref[h] = st

    for h in range(RET_HEADS):
        cols = slice(h * RET_V_DIM, (h + 1) * RET_V_DIM)
        y = y_ref[:, cols]
        mu = jnp.mean(y, axis=-1, keepdims=True)
        yc = y - mu
        var = jnp.mean(yc * yc, axis=-1, keepdims=True)
        yn = yc * lax.rsqrt(var + GN_EPS)
        gate = g_ref[0, :, cols]
        o_ref[0, :, cols] = (gate * jax.nn.sigmoid(gate) * yn).astype(o_ref.dtype)


def _retention_tables(s):
    half = RET_QK_DIM // 2
    inv = 1.0 / (ROPE_BASE ** (np.arange(half, dtype=np.float64) / half))
    ang = np.arange(s, dtype=np.float64)[:, None] * inv[None, :]
    cos_t = np.tile(np.cos(ang), (1, RET_HEADS))
    sin_t = np.tile(np.sin(ang), (1, RET_HEADS))
    c = RET_CHUNK
    lg = np.log(1.0 - np.exp2(-5.0 - np.arange(RET_HEADS, dtype=np.float64)))
    idx = np.arange(c, dtype=np.float64)
    diff = idx[:, None] - idx[None, :]
    decay = np.where(diff >= 0, np.exp(lg[:, None, None] * np.maximum(diff, 0.0)), 0.0)
    lanes = (RET_HEADS, c, LANE)
    zeta = np.broadcast_to(np.exp(lg[:, None] * (c - 1.0 - idx))[:, :, None], lanes)
    xi = np.broadcast_to(np.exp(lg[:, None] * (idx + 1.0))[:, :, None], lanes)
    gam = np.broadcast_to(np.exp(lg * c)[:, None, None], (RET_HEADS, 1, LANE))
    return tuple(jnp.asarray(np.ascontiguousarray(a), F32) for a in (cos_t, sin_t, decay, zeta, xi, gam))


def _retention_weight_layout(w_qk):
    d = w_qk.shape[0]
    half = RET_QK_DIM // 2
    return w_qk.reshape(d, 2, RET_HEADS, 2, half).transpose(0, 1, 3, 2, 4).reshape(d, -1)


def _retention(h3):
    b, s, _ = h3.shape
    c = RET_CHUNK
    qk_w = 2 * RET_HEADS * RET_QK_DIM
    v_w = RET_HEADS * RET_V_DIM
    assert qk_w == v_w
    cos_t, sin_t, decay, zeta, xi, gam = _retention_tables(s)
    const3 = lambda bi, n: (0, 0, 0)
    rows = RET_STEP_CHUNKS * c
    return pl.pallas_call(
        _retention_kernel,
        grid=(b, s // rows),
        in_specs=[pl.BlockSpec((1, rows, qk_w), lambda bi, n: (bi, n, 0)),
                  pl.BlockSpec((1, rows, v_w), lambda bi, n: (bi, n, 1)),
                  pl.BlockSpec((1, rows, v_w), lambda bi, n: (bi, n, 2)),
                  pl.BlockSpec((rows, LANE), lambda bi, n: (n, 0)),
                  pl.BlockSpec((rows, LANE), lambda bi, n: (n, 0)),
                  pl.BlockSpec((RET_HEADS, c, c), const3),
                  pl.BlockSpec((RET_HEADS, c, LANE), const3),
                  pl.BlockSpec((RET_HEADS, c, LANE), const3),
                  pl.BlockSpec((RET_HEADS, 1, LANE), const3)],
        out_specs=pl.BlockSpec((1, rows, v_w), lambda bi, n: (bi, n, 0)),
        out_shape=jax.ShapeDtypeStruct((b, s, v_w), BF16),
        scratch_shapes=[pltpu.VMEM((RET_HEADS, RET_HEADS * RET_QK_DIM, RET_V_DIM), F32),
                        pltpu.VMEM((rows, v_w), F32)],
        compiler_params=_params("parallel", "arbitrary"),
        name="retention",
    )(h3, h3, h3, cos_t, sin_t, decay, zeta, xi, gam)


def _dilated_kernel(q_ref, kp_ref, kc_ref, vp_ref, vc_ref, bias_ref, o_ref, acc_ref, m_ref, l_ref,
                    s0_ref, s1_ref, p0_ref, p1_ref):
    j = pl.program_id(2)
    sup = DIL_SUPER
    q_blk = DIL_BLOCK
    scale = DIL_HEAD_DIM ** -0.5 * math.log2(math.e)
    n_blocks = sup // q_blk
    lane = lax.broadcasted_iota(jnp.int32, (q_blk, LANE), 1)
    head0 = lane < DIL_HEAD_DIM
    first_bias = jnp.where(j == 0, 1, 0)

    groups = []
    for bi, d in enumerate(DIL_DILATIONS):
        n_per_r = sup // (q_blk * d)
        for t0 in range(0, n_blocks, DIL_GROUP):
            blocks = []
            for t in range(t0, t0 + DIL_GROUP):
                r, n = divmod(t, n_per_r)
                blocks.append((n * (q_blk * d) + r, (n - 1) * (q_blk * d) + r, n))
            groups.append((bi, d, blocks))
    s_bufs = (s0_ref, s1_ref)
    p_bufs = (p0_ref, p1_ref)

    def window(prev_ref, cur_ref, k_start, d):
        if k_start >= 0:
            return cur_ref[0, pl.ds(k_start, 2 * q_blk, stride=d), :]
        return jnp.concatenate([prev_ref[0, pl.ds(sup + k_start, q_blk, stride=d), :],
                                cur_ref[0, pl.ds(k_start + q_blk * d, q_blk, stride=d), :]], axis=0)

    def scores(gi):
        bi, d, blocks = groups[gi]
        for g, (q_start, k_start, n) in enumerate(blocks):
            q = q_ref[0, pl.ds(q_start, q_blk, stride=d), :] * scale
            kb = window(kp_ref, kc_ref, k_start, d)
            q2 = jnp.concatenate([jnp.where(head0, q, 0.0), jnp.where(head0, 0.0, q)], axis=0)
            bias = bias_ref[first_bias] if n == 0 else bias_ref[0]
            s_bufs[gi % 2][g] = _dot_nt(q2, kb) + bias

    def softmax(gi):
        bi, d, blocks = groups[gi]
        for g, (q_start, k_start, n) in enumerate(blocks):
            s = s_bufs[gi % 2][g]
            m2 = jnp.max(s, axis=-1, keepdims=True)
            p_bufs[gi % 2][g] = jnp.exp2(s - m2).astype(BF16)
            m_ref[bi, pl.ds(q_start, q_blk, stride=d), :] = jnp.where(head0, m2[:q_blk], m2[q_blk:])

    def values(gi):
        bi, d, blocks = groups[gi]
        for g, (q_start, k_start, n) in enumerate(blocks):
            vb = window(vp_ref, vc_ref, k_start, d).astype(BF16)
            o2 = jnp.dot(p_bufs[gi % 2][g], jnp.concatenate([vb, jnp.ones_like(vb)], axis=1),
                         preferred_element_type=F32)
            rows = pl.ds(q_start, q_blk, stride=d)
            acc_ref[bi, rows, :] = jnp.where(head0, o2[:q_blk, :LANE], o2[q_blk:, :LANE])
            l_ref[bi, rows, :] = jnp.where(head0, o2[:q_blk, LANE:], o2[q_blk:, LANE:])

    for step in range(len(groups) + 2):
        if step < len(groups):
            scores(step)
        if 0 <= step - 1 < len(groups):
            softmax(step - 1)
        if step - 2 >= 0:
            values(step - 2)

    def merge(c, carry):
        rows = pl.ds(pl.multiple_of(c * q_blk, q_blk), q_blk)
        ms = [m_ref[bi, rows, :] for bi in range(len(DIL_DILATIONS))]
        m_all = functools.reduce(jnp.maximum, ms)
        ws = [jnp.exp2(m - m_all) for m in ms]
        num = functools.reduce(lambda a, b: a + b, [w * acc_ref[bi, rows, :] for bi, w in enumerate(ws)])
        den = functools.reduce(lambda a, b: a + b, [w * l_ref[bi, rows, :] for bi, w in enumerate(ws)])
        o_ref[0, rows, :] = (num / den).astype(o_ref.dtype)
        return carry

    lax.fori_loop(0, n_blocks, merge, 0)


def _dilated_bias():
    q_blk = DIL_BLOCK
    qi = np.arange(2 * q_blk)[:, None] % q_blk
    kj = np.arange(2 * q_blk)[None, :]
    band = (kj >= qi) & (kj <= qi + q_blk)
    return jnp.asarray(np.stack([np.where(band, 0.0, NEG), np.where(band & (kj >= q_blk), 0.0, NEG)]), F32)


def _dilated(h3, col0):
    b, s, _ = h3.shape
    sup = DIL_SUPER
    n_pairs = DIL_HEADS * DIL_HEAD_DIM // LANE
    n_br = len(DIL_DILATIONS)
    cq, ck, cv = col0 * n_pairs, (col0 + 1) * n_pairs, (col0 + 2) * n_pairs
    cur = lambda c: (lambda bi, p, j: (bi, j, c + p))
    prev = lambda c: (lambda bi, p, j: (bi, jnp.maximum(j - 1, 0), c + p))
    blk = (1, sup, LANE)
    return pl.pallas_call(
        _dilated_kernel,
        grid=(b, n_pairs, s // sup),
        in_specs=[pl.BlockSpec(blk, cur(cq)),
                  pl.BlockSpec(blk, prev(ck)), pl.BlockSpec(blk, cur(ck)),
                  pl.BlockSpec(blk, prev(cv)), pl.BlockSpec(blk, cur(cv)),
                  pl.BlockSpec((2, 2 * DIL_BLOCK, 2 * DIL_BLOCK), lambda bi, p, j: (0, 0, 0))],
        out_specs=pl.BlockSpec(blk, lambda bi, p, j: (bi, j, p)),
        out_shape=jax.ShapeDtypeStruct((b, s, n_pairs * LANE), BF16),
        scratch_shapes=[pltpu.VMEM((n_br, sup, LANE), F32), pltpu.VMEM((n_br, sup, LANE), F32),
                        pltpu.VMEM((n_br, sup, LANE), F32),
                        pltpu.VMEM((DIL_GROUP, 2 * DIL_BLOCK, 2 * DIL_BLOCK), F32),
                        pltpu.VMEM((DIL_GROUP, 2 * DIL_BLOCK, 2 * DIL_BLOCK), F32),
                        pltpu.VMEM((DIL_GROUP, 2 * DIL_BLOCK, 2 * DIL_BLOCK), BF16),
                        pltpu.VMEM((DIL_GROUP, 2 * DIL_BLOCK, 2 * DIL_BLOCK), BF16)],
        compiler_params=_params("parallel", "parallel", "arbitrary"),
        name="dilated",
    )(h3, h3, h3, h3, h3, _dilated_bias())


def _mem_kv(mem2d, w_bf16, tn=512):
    m, d = mem2d.shape
    n = w_bf16.shape[1]
    return pl.pallas_call(
        _proj_in_kernel,
        grid=(n // tn,),
        in_specs=[pl.BlockSpec((m, d), lambda i: (0, 0)),
                  pl.BlockSpec((d, tn), lambda i: (0, i))],
        out_specs=pl.BlockSpec((m, tn), lambda i: (0, i)),
        out_shape=jax.ShapeDtypeStruct((m, n), BF16),
        compiler_params=_params("parallel"),
        name="mem_kv",
    )(mem2d, w_bf16)


def _xattn_kernel(alpha, x_ref, yr_ref, yd_ref, wout_ref, g1_ref, b1_ref, k_ref, v_ref, wq_ref, wo_ref,
                  g_ref, b_ref, wr_ref, o_ref, ot_ref, lg_ref, x1_ref, q_ref, att_ref):
    tm, d = x_ref.shape
    dh = d // XATTN_HEADS
    wr = yr_ref.shape[1]
    scale = dh ** -0.5 * math.log2(math.e)
    part = tm // XATTN_PARTS

    def mix(rows):
        y = _dot(yr_ref[rows, :], wout_ref[0:wr, :]) + _dot(yd_ref[rows, :], wout_ref[wr:, :])
        x1_ref[rows, :] = _layer_norm(alpha * x_ref[rows, :] + y, g1_ref[...], b1_ref[...])

    def query(rows):
        q_ref[rows, :] = _dot(x1_ref[rows, :], wq_ref[...]).astype(BF16)

    def attend(rows):
        for h in range(XATTN_HEADS):
            sl = slice(h * dh, (h + 1) * dh)
            s = _dot_nt(q_ref[rows, sl], k_ref[0, :, sl]) * scale
            m = jnp.max(s, axis=-1, keepdims=True)
            e = jnp.exp2(s - m)
            p = e / jnp.sum(e, axis=-1, keepdims=True)
            att_ref[rows, sl] = _dot(p, v_ref[0, :, sl]).astype(BF16)

    def finish(rows, r0):
        y = _dot(att_ref[rows, :], wo_ref[...])
        x2 = _layer_norm(alpha * x1_ref[rows, :] + y, g_ref[...], b_ref[...])
        o_ref[rows, :] = x2
        for c in range(d // LANE):
            ot_ref[pl.ds(r0 * SUBLANE + c, part, stride=SUBLANE), :] = x2[:, c * LANE:(c + 1) * LANE]
        lg_ref[:, rows] = _dot_nt(wr_ref[...], x2)

    stages = (mix, query, attend, finish)
    for step in range(XATTN_PARTS + len(stages) - 1):
        for si, stage in enumerate(stages):
            pi = step - si
            if 0 <= pi < XATTN_PARTS:
                rows = slice(pi * part, (pi + 1) * part)
                if stage is finish:
                    stage(rows, pi * part)
                else:
                    stage(rows)


def _xattn(x2d, y_ret, y_dil, w_out, g1, b1, kv, wq, wo, g, bb, w_r, alpha, seq, tm=1024):
    t, d = x2d.shape
    mlen = kv.shape[1]
    tiles_per_seq = seq // tm
    row = lambda i: (i, 0)

    def whole(shape):
        return pl.BlockSpec(shape, lambda i: (0,) * len(shape), pipeline_mode=pl.Buffered(1))

    vec = whole((1, d))
    return pl.pallas_call(
        functools.partial(_xattn_kernel, alpha),
        grid=(t // tm,),
        in_specs=[pl.BlockSpec((tm, d), row),
                  pl.BlockSpec((tm, y_ret.shape[1]), row),
                  pl.BlockSpec((tm, y_dil.shape[1]), row),
                  whole(w_out.shape), vec, vec,
                  pl.BlockSpec((1, mlen, d), lambda i: (i // tiles_per_seq, 0, 0)),
                  pl.BlockSpec((1, mlen, d), lambda i: (i // tiles_per_seq, 0, 1)),
                  whole((d, d)), whole((d, d)), vec, vec,
                  whole((ROUTE_ROWS, d))],
        out_specs=[pl.BlockSpec((tm, d), row),
                   pl.BlockSpec((tm * SUBLANE, LANE), row),
                   pl.BlockSpec((ROUTE_ROWS, tm), lambda i: (0, i))],
        out_shape=[jax.ShapeDtypeStruct((t, d), F32),
                   jax.ShapeDtypeStruct((t * SUBLANE, LANE), F32),
                   jax.ShapeDtypeStruct((ROUTE_ROWS, t), F32)],
        scratch_shapes=[pltpu.VMEM((tm, d), F32), pltpu.VMEM((tm, d), BF16), pltpu.VMEM((tm, d), BF16)],
        compiler_params=_params("parallel"),
        name="xattn",
    )(x2d, y_ret, y_dil, w_out, g1, b1, kv, kv, wq, wo, g, bb, w_r)


def _route_kernel(lg_ref, before_ref, meta_ref, cnt_ref, carry_ref):
    i = pl.program_id(0)

    @pl.when(i == 0)
    def _():
        carry_ref[...] = jnp.zeros_like(carry_ref)

    rows, tt = lg_ref.shape
    sub = before_ref.shape[0]
    r = lax.broadcasted_iota(jnp.int32, (rows, sub), 0)
    r8 = lax.broadcasted_iota(jnp.int32, (SUBLANE, sub), 0)
    is_group = r < N_GROUPS

    def col_max(a):
        return jnp.max(a, axis=0, keepdims=True)

    def first_row_where(mask):
        return jnp.min(jnp.where(mask, r, rows), axis=0, keepdims=True)

    for c in range(tt // sub):
        lg = lg_ref[:, c * sub:(c + 1) * sub]
        mg = col_max(jnp.where(is_group, lg, NEG))
        eg = jnp.where(is_group, jnp.exp(lg - mg), 0.0)
        pg = eg / jnp.sum(eg, axis=0, keepdims=True)
        g1 = col_max(pg)
        gi = first_row_where(is_group & (pg == g1))
        lo = N_GROUPS + gi * EXPERTS_PER_GROUP
        in_grp = (r >= lo) & (r < lo + EXPERTS_PER_GROUP)
        v1 = col_max(jnp.where(in_grp, lg, NEG))
        i1 = first_row_where(in_grp & (lg == v1))
        rest = in_grp & (r != i1)
        v2 = col_max(jnp.where(rest, lg, NEG))
        i2 = first_row_where(rest & (lg == v2))
        e2 = jnp.exp(v2 - v1)
        den = 1.0 + e2
        gate1 = g1 * (1.0 / den)
        gate2 = g1 * (e2 / den)
        sel1 = r == i1
        sel2 = r == i2
        onehot = jnp.where(sel1 | sel2, 1.0, 0.0)
        rank = _dot(onehot, before_ref[...]) + carry_ref[:, 0:1]
        r1 = jnp.sum(jnp.where(sel1, rank, 0.0), axis=0, keepdims=True)
        r2 = jnp.sum(jnp.where(sel2, rank, 0.0), axis=0, keepdims=True)
        carry_ref[...] = carry_ref[...] + jnp.sum(onehot, axis=1, keepdims=True)
        meta = jnp.where(r8 == 0, (i1 - N_GROUPS).astype(F32), 0.0)
        meta = jnp.where(r8 == 1, (i2 - N_GROUPS).astype(F32), meta)
        meta = jnp.where(r8 == 2, r1, meta)
        meta = jnp.where(r8 == 3, r2, meta)
        meta = jnp.where(r8 == 4, gate1, meta)
        meta = jnp.where(r8 == 5, gate2, meta)
        meta_ref[:, c * sub:(c + 1) * sub] = meta
    cnt_ref[...] = carry_ref[...]


def _route(logits_t, tt=1024, sub=256):
    rows, t = logits_t.shape
    before = jnp.asarray(np.arange(sub)[:, None] < np.arange(sub)[None, :], BF16)
    return pl.pallas_call(
        _route_kernel,
        grid=(t // tt,),
        in_specs=[pl.BlockSpec((rows, tt), lambda i: (0, i)),
                  pl.BlockSpec((sub, sub), lambda i: (0, 0))],
        out_specs=[pl.BlockSpec((SUBLANE, tt), lambda i: (0, i)),
                   pl.BlockSpec((rows, LANE), lambda i: (0, 0))],
        out_shape=[jax.ShapeDtypeStruct((SUBLANE, t), F32),
                   jax.ShapeDtypeStruct((rows, LANE), F32)],
        scratch_shapes=[pltpu.VMEM((rows, LANE), F32)],
        compiler_params=_params("arbitrary"),
        name="route",
    )(logits_t, before)


def _plan_kernel(meta_ref, first_row_ref, dest_ref):
    n_tiles = dest_ref.shape[0]
    tt = dest_ref.shape[2] // 2
    rows = first_row_ref.shape[0]
    first_row = first_row_ref[:, 0:1]
    r = lax.broadcasted_iota(jnp.int32, (rows, tt), 0)
    for g in range(n_tiles):
        m = meta_ref[:, g * tt:(g + 1) * tt]

        def dest_of(k):
            e_row = m[k:k + 1, :].astype(jnp.int32) + N_GROUPS
            return jnp.sum(jnp.where(r == e_row, first_row, 0.0), axis=0, keepdims=True) + m[2 + k:3 + k, :]

        dest_ref[g] = jnp.concatenate([dest_of(0), dest_of(1)], axis=1).astype(jnp.int32)


def _plan(meta_t, first_row, tt, tiles_per_step=4):
    t = meta_t.shape[1]
    rows = first_row.shape[0]
    return pl.pallas_call(
        _plan_kernel,
        grid=(t // (tt * tiles_per_step),),
        in_specs=[pl.BlockSpec((SUBLANE, tt * tiles_per_step), lambda i: (0, i)),
                  pl.BlockSpec((rows, LANE), lambda i: (0, 0))],
        out_specs=pl.BlockSpec((tiles_per_step, 1, 2 * tt), lambda i: (i, 0, 0)),
        out_shape=jax.ShapeDtypeStruct((t // tt, 1, 2 * tt), jnp.int32),
        compiler_params=_params("parallel"),
        name="plan",
    )(meta_t, first_row)


def _row_copy(src, dst, s_row, d_row, sem):
    return pltpu.make_async_copy(src.at[pl.ds(pl.multiple_of(s_row * SUBLANE, SUBLANE), SUBLANE), :],
                                 dst.at[pl.ds(pl.multiple_of(d_row * SUBLANE, SUBLANE), SUBLANE), :], sem)


def _dispatch_kernel(dest_ref, xt_ref, xs_ref, sem):
    tq = dest_ref.shape[2] // 2

    def issue(t, c):
        _row_copy(xt_ref, xs_ref, t, dest_ref[0, 0, t], sem).start(priority=0)
        _row_copy(xt_ref, xs_ref, t, dest_ref[0, 0, tq + t], sem).start(priority=1)
        return c

    lax.fori_loop(0, tq, issue, 0, unroll=8)
    for _ in range(2):
        pltpu.make_async_copy(xt_ref, xs_ref.at[pl.ds(0, tq * SUBLANE), :], sem).wait()


def _dispatch(dest3, x_tiled, n_rows, tiles_per_step=4):
    n_tiles, _, two_tt = dest3.shape
    tt = two_tt // 2
    tq = tiles_per_step * tt
    n_steps = n_tiles // tiles_per_step
    dest3 = dest3.reshape(n_steps, tiles_per_step, 2, tt).transpose(0, 2, 1, 3).reshape(n_steps, 1, 2 * tq)
    return pl.pallas_call(
        _dispatch_kernel,
        grid=(n_steps,),
        in_specs=[pl.BlockSpec((1, 1, 2 * tq), lambda i: (i, 0, 0), memory_space=pltpu.SMEM),
                  pl.BlockSpec((tq * SUBLANE, LANE), lambda i: (i, 0))],
        out_specs=pl.BlockSpec(memory_space=pl.ANY),
        out_shape=jax.ShapeDtypeStruct((n_rows * SUBLANE, LANE), F32),
        scratch_shapes=[pltpu.SemaphoreType.DMA(())],
        compiler_params=_params("arbitrary"),
        name="dispatch",
    )(dest3, x_tiled)


def _experts_kernel(blk_exp_ref, blk_rows_ref, n_used_ref, blk_first_ref, blk_slot_ref, blk_next_ref,
                    xs_ref, wg_hbm, wu_hbm, wd_hbm, ys_ref, x_scr, y_scr, wg_buf, wu_buf, wd_buf, sems):
    b = pl.program_id(0)

    def weight_copies(e, slot):
        return [pltpu.make_async_copy(hbm.at[e], buf.at[slot], sems.at[k, slot])
                for k, (hbm, buf) in enumerate(((wg_hbm, wg_buf), (wu_hbm, wu_buf), (wd_hbm, wd_buf)))]

    @pl.when(b < n_used_ref[0])
    def _():
        slot = blk_slot_ref[b]

        @pl.when(b == 0)
        def _():
            for cp in weight_copies(blk_exp_ref[0], slot):
                cp.start()

        @pl.when(blk_first_ref[b] == 1)
        def _():
            @pl.when(blk_next_ref[b] >= 0)
            def _():
                for cp in weight_copies(blk_next_ref[b], 1 - slot):
                    cp.start()

            for cp in weight_copies(blk_exp_ref[b], slot):
                cp.wait()

        rows, d = x_scr.shape
        n_valid = blk_rows_ref[b]

        def run(m):
            ri = lax.broadcasted_iota(jnp.int32, (m, 1), 0)
            for c in range(d // LANE):
                x = xs_ref[pl.ds(c, m, stride=SUBLANE), :]
                x_scr[0:m, c * LANE:(c + 1) * LANE] = jnp.where(ri < n_valid, x, 0.0).astype(BF16)
            x = x_scr[0:m, :]
            hg = _dot(x, wg_buf[slot])
            hu = _dot(x, wu_buf[slot])
            y_scr[0:m, :] = _dot(hg * jax.nn.sigmoid(hg) * hu, wd_buf[slot])
            for c in range(d // LANE):
                ys_ref[pl.ds(c, m, stride=SUBLANE), :] = y_scr[0:m, c * LANE:(c + 1) * LANE]

        n_quanta = rows // EXPERT_QUANTUM
        for k in range(1, n_quanta + 1):
            lo = (k - 1) * EXPERT_QUANTUM
            pl.when((n_valid > lo) & (n_valid <= k * EXPERT_QUANTUM))(functools.partial(run, k * EXPERT_QUANTUM))


def _experts(blk_exp, blk_rows, n_used, blk_first, blk_slot, blk_next, xs_tiled, w_g, w_u, w_d):
    n_rows = xs_tiled.shape[0] // SUBLANE
    rows = EXPERT_ROWS
    nblk = n_rows // rows
    _, d, ff = w_g.shape
    used = lambda b, be, br, nu, *_: (jnp.minimum(b, nu[0] - 1), 0)
    grid_spec = pltpu.PrefetchScalarGridSpec(
        num_scalar_prefetch=6,
        grid=(nblk,),
        in_specs=[pl.BlockSpec((rows * SUBLANE, LANE), used),
                  pl.BlockSpec(memory_space=pl.ANY), pl.BlockSpec(memory_space=pl.ANY),
                  pl.BlockSpec(memory_space=pl.ANY)],
        out_specs=pl.BlockSpec((rows * SUBLANE, LANE), used),
        scratch_shapes=[pltpu.VMEM((rows, d), BF16), pltpu.VMEM((rows, d), F32),
                        pltpu.VMEM((2, d, ff), F32), pltpu.VMEM((2, d, ff), F32), pltpu.VMEM((2, ff, d), F32),
                        pltpu.SemaphoreType.DMA((3, 2))],
    )
    return pl.pallas_call(
        _experts_kernel,
        grid_spec=grid_spec,
        out_shape=jax.ShapeDtypeStruct((n_rows * SUBLANE, LANE), F32),
        compiler_params=_params("arbitrary"),
        name="experts",
    )(blk_exp, blk_rows, n_used, blk_first, blk_slot, blk_next, xs_tiled, w_g, w_u, w_d)


def _combine_kernel(alpha, dcur_ref, dnext_ref, meta_ref, x_ref, g_ref, b_ref, ys_ref, o_ref, buf_ref, sems):
    i = pl.program_id(0)
    tq, d = x_ref.shape
    slot_rows = 2 * tq * SUBLANE

    def issue(d_ref, slot, t):
        _row_copy(ys_ref, buf_ref, d_ref[0, 0, t], slot * (2 * tq) + t, sems.at[slot]).start(priority=0)
        _row_copy(ys_ref, buf_ref, d_ref[0, 0, tq + t], slot * (2 * tq) + tq + t, sems.at[slot]).start(priority=1)

    def wait_slot(slot):
        off = pl.multiple_of(slot * slot_rows, slot_rows)
        pltpu.make_async_copy(ys_ref.at[pl.ds(0, slot_rows), :], buf_ref.at[pl.ds(off, slot_rows), :],
                              sems.at[slot]).wait()
        return off

    def issue_all(d_ref, slot):
        lax.fori_loop(0, tq, lambda t, c: (issue(d_ref, slot, t), c)[1], 0, unroll=4)

    slot = i % 2

    @pl.when(i == 0)
    def _():
        issue_all(dcur_ref, 0)

    @pl.when(i + 1 < pl.num_programs(0))
    def _():
        issue_all(dnext_ref, 1 - slot)

    off = wait_slot(slot)
    meta_rows = jnp.transpose(jnp.concatenate([meta_ref[...], jnp.zeros((LANE - SUBLANE, tq), F32)], axis=0))
    gate1 = meta_rows[:, 4:5]
    gate2 = meta_rows[:, 5:6]
    for c in range(d // LANE):
        sl = slice(c * LANE, (c + 1) * LANE)
        y1 = buf_ref[pl.ds(off + c, tq, stride=SUBLANE), :]
        y2 = buf_ref[pl.ds(off + tq * SUBLANE + c, tq, stride=SUBLANE), :]
        o_ref[:, sl] = alpha * x_ref[:, sl] + (y1 * gate1 + y2 * gate2)
    o_ref[...] = _layer_norm(o_ref[...], g_ref[...], b_ref[...])


def _combine(dest3, meta, x2, g, bb, ys_tiled, alpha):
    t, d = x2.shape
    n = dest3.shape[0]
    tq = t // n
    row = lambda i: (i, 0)
    const = lambda i: (0, 0)
    return pl.pallas_call(
        functools.partial(_combine_kernel, alpha),
        grid=(n,),
        in_specs=[pl.BlockSpec((1, 1, 2 * tq), lambda i: (i, 0, 0), memory_space=pltpu.SMEM),
                  pl.BlockSpec((1, 1, 2 * tq), lambda i: (jnp.minimum(i + 1, n - 1), 0, 0), memory_space=pltpu.SMEM),
                  pl.BlockSpec((SUBLANE, tq), lambda i: (0, i)),
                  pl.BlockSpec((tq, d), row),
                  pl.BlockSpec((1, d), const), pl.BlockSpec((1, d), const),
                  pl.BlockSpec(memory_space=pl.ANY)],
        out_specs=pl.BlockSpec((tq, d), row),
        out_shape=jax.ShapeDtypeStruct((t, d), F32),
        scratch_shapes=[pltpu.VMEM((2 * 2 * tq * SUBLANE, LANE), F32), pltpu.SemaphoreType.DMA((2,))],
        compiler_params=_params("arbitrary"),
        name="combine",
    )(dest3, dest3, meta, x2, g, bb, ys_tiled)


def _moe(x2, x2_tiled, logits, w_g, w_u, w_d, g, bb, alpha):
    t, d = x2.shape
    meta, cnt = _route(logits)
    counts = cnt[N_GROUPS:N_GROUPS + N_EXPERTS, 0].astype(jnp.int32)
    padded = ((counts + EXPERT_ROWS - 1) // EXPERT_ROWS) * EXPERT_ROWS
    pend = jnp.cumsum(padded)
    poff = pend - padded
    first_row = jnp.pad(poff.astype(F32), (N_GROUPS, ROUTE_ROWS - N_GROUPS - N_EXPERTS))
    dest3 = _plan(meta, jnp.broadcast_to(first_row[:, None], (ROUTE_ROWS, LANE)), ROUTE_TILE)
    n_rows = t * 2 + N_EXPERTS * EXPERT_ROWS
    nblk = n_rows // EXPERT_ROWS
    blk_start = jnp.arange(nblk, dtype=jnp.int32) * EXPERT_ROWS
    blk_exp = jnp.sum((pend[None, :] <= blk_start[:, None]).astype(jnp.int32), axis=1)
    blk_exp = jnp.minimum(blk_exp, N_EXPERTS - 1)
    e_idx = jnp.arange(N_EXPERTS, dtype=jnp.int32)
    owner = blk_exp[:, None] == e_idx[None, :]

    def of_block(per_expert):
        return jnp.sum(jnp.where(owner, per_expert[None, :], 0), axis=1).astype(jnp.int32)

    blk_rows = jnp.clip(of_block(poff + counts) - blk_start, 0, EXPERT_ROWS).astype(jnp.int32)
    n_used = (pend[-1:] // EXPERT_ROWS).astype(jnp.int32)
    has_rows = counts > 0
    ordinal = jnp.cumsum(has_rows.astype(jnp.int32)) - 1
    later = has_rows[None, :] & (e_idx[None, :] > e_idx[:, None])
    next_used = jnp.min(jnp.where(later, e_idx[None, :], N_EXPERTS), axis=1)
    next_used = jnp.where(next_used == N_EXPERTS, -1, next_used).astype(jnp.int32)
    blk_first = (blk_start == of_block(poff)).astype(jnp.int32)
    blk_slot = of_block(ordinal % 2)
    blk_next = of_block(next_used)
    xs_tiled = _dispatch(dest3, x2_tiled, n_rows)
    ys_tiled = _experts(blk_exp, blk_rows, n_used, blk_first, blk_slot, blk_next, xs_tiled, w_g, w_u, w_d)
    return _combine(dest3, meta, x2, g, bb, ys_tiled, alpha)


def kernel(x, mem, w_in, w_out, ln_mix_g, ln_mix_b, w_xq, w_xkv, w_xo, ln_x_g, ln_x_b, w_route_group,
           w_route_expert, w_exp_gate, w_exp_up, w_exp_down, ln_moe_g, ln_moe_b):
    b, s, d = x.shape
    depth = w_in.shape[0]
    alpha = (2.0 * depth) ** 0.25
    t = b * s
    ret_cols = (2 * RET_HEADS * RET_QK_DIM + 2 * RET_HEADS * RET_V_DIM)
    assert ret_cols % (DIL_HEADS * DIL_HEAD_DIM) == 0
    dil_col0 = ret_cols // (DIL_HEADS * DIL_HEAD_DIM)
    xc = x.reshape(t, d)
    for l in range(depth):
        n_qk = 2 * RET_HEADS * RET_QK_DIM
        h = _proj_in(xc, w_in[l], _retention_weight_layout(w_in[l][:, :n_qk]).astype(BF16))
        h3 = h.reshape(b, s, h.shape[1])
        y_ret = _retention(h3).reshape(t, -1)
        y_dil = _dilated(h3, dil_col0).reshape(t, -1)
        kv = _mem_kv(mem.reshape(b * mem.shape[1], d), w_xkv[l]).reshape(b, mem.shape[1], 2 * d)
        w_r = jnp.concatenate([w_route_group[l], w_route_expert[l]], axis=-1)
        w_r = jnp.pad(w_r.T, ((0, ROUTE_ROWS - w_r.shape[1]), (0, 0))).astype(BF16)
        x2, x2_tiled, logits = _xattn(xc, y_ret, y_dil, w_out[l].astype(BF16), ln_mix_g[l][None], ln_mix_b[l][None],
                                      kv, w_xq[l].astype(BF16), w_xo[l].astype(BF16),
                                      ln_x_g[l][None], ln_x_b[l][None], w_r, alpha, s)
        xc = _moe(x2, x2_tiled, logits, w_exp_gate[l], w_exp_up[l], w_exp_down[l],
                  ln_moe_g[l][None], ln_moe_b[l][None], alpha)
    return xc.reshape(b, s, d)
```

```python
import functools
import math

import jax
import jax.numpy as jnp
import numpy as np
from jax import lax
from jax.experimental import pallas as pl
from jax.experimental.pallas import tpu as pltpu

BF16 = jnp.bfloat16
F32 = jnp.float32

LANE = 128
SUBLANE = 8
VMEM_LIMIT = 56 * 1024 * 1024

RET_HEADS = 4
RET_QK_DIM = 64
RET_V_DIM = 128
RET_CHUNK = 128
RET_STEP_CHUNKS = 4
ROPE_BASE = 10000.0
DIL_HEADS = 8
DIL_HEAD_DIM = 64
DIL_DILATIONS = (1, 4, 16)
DIL_BLOCK = 128
DIL_SUPER = DIL_BLOCK * max(DIL_DILATIONS)
DIL_GROUP = 1
XATTN_HEADS = 4
XATTN_PARTS = 2
N_GROUPS = 4
EXPERTS_PER_GROUP = 8
N_EXPERTS = N_GROUPS * EXPERTS_PER_GROUP
ROUTE_ROWS = -(-(N_GROUPS + N_EXPERTS) // SUBLANE) * SUBLANE
EXPERT_ROWS = 512
EXPERT_QUANTUM = 128
ROUTE_TILE = 256
LN_EPS = 1e-5
GN_EPS = 1e-6
NEG = -1e30


def _params(*sem):
    return pltpu.CompilerParams(dimension_semantics=sem, vmem_limit_bytes=VMEM_LIMIT)


def _layer_norm(z, g, b):
    mu = jnp.mean(z, axis=-1, keepdims=True)
    zc = z - mu
    var = jnp.mean(zc * zc, axis=-1, keepdims=True)
    return zc * lax.rsqrt(var + LN_EPS) * g + b


def _dot(a, b):
    return jnp.dot(a.astype(BF16), b.astype(BF16), preferred_element_type=F32)


def _dot_nt(a, b):
    return lax.dot_general(a.astype(BF16), b.astype(BF16), (((1,), (1,)), ((), ())),
                           preferred_element_type=F32)


def _dot_tn(a, b):
    return lax.dot_general(a.astype(BF16), b.astype(BF16), (((0,), (0,)), ((), ())),
                           preferred_element_type=F32)


def _proj_in_kernel(x_ref, w_ref, o_ref):
    o_ref[...] = _dot(x_ref[...], w_ref[...]).astype(o_ref.dtype)


def _proj_in2_kernel(x_ref, w_ref, w_head_ref, o_ref, w_bf16):
    @pl.when(pl.program_id(0) == 0)
    def _():
        w_bf16[...] = w_ref[...].astype(BF16)
        w_bf16[:, 0:w_head_ref.shape[1]] = w_head_ref[...]

    o_ref[...] = _dot(x_ref[...], w_bf16[...])


def _proj_in(x2d, w_f32, w_head_bf16, tm=512):
    t, d = x2d.shape
    n = w_f32.shape[1]
    whole = lambda shape: pl.BlockSpec(shape, lambda i: (0, 0), pipeline_mode=pl.Buffered(1))
    return pl.pallas_call(
        _proj_in2_kernel,
        grid=(t // tm,),
        in_specs=[pl.BlockSpec((tm, d), lambda i: (i, 0)), whole((d, n)), whole(w_head_bf16.shape)],
        out_specs=pl.BlockSpec((tm, n), lambda i: (i, 0)),
        out_shape=jax.ShapeDtypeStruct((t, n), F32),
        scratch_shapes=[pltpu.VMEM((d, n), BF16)],
        compiler_params=_params("arbitrary"),
        name="proj_in",
    )(x2d, w_f32, w_head_bf16)


def _retention_kernel(qk_ref, v_ref, g_ref, cos_ref, sin_ref, decay_ref, zeta_ref, xi_ref, gam_ref,
                      o_ref, state_ref, y_ref):
    n = pl.program_id(1)

    @pl.when(n == 0)
    def _():
        state_ref[...] = jnp.zeros_like(state_ref)

    c = RET_CHUNK
    n_sub = qk_ref.shape[1] // c
    half = RET_QK_DIM // 2
    lane = lax.broadcasted_iota(jnp.int32, (c, LANE), 1)

    rotated = []
    for j in range(n_sub):
        rows = slice(j * c, (j + 1) * c)
        cos = cos_ref[rows, :]
        sin = sin_ref[rows, :]

        def rot(col, rows=rows, cos=cos, sin=sin):
            t1 = qk_ref[0, rows, col * LANE:(col + 1) * LANE]
            t2 = qk_ref[0, rows, (col + 1) * LANE:(col + 2) * LANE]
            return t1 * cos - t2 * sin, t1 * sin + t2 * cos

        q1, q2 = rot(0)
        k1, k2 = (t * (RET_QK_DIM ** -0.5) for t in rot(2))
        rotated.append((q1, q2, k1, k2, jnp.concatenate([k1, k2], axis=1)))

    for h in range(RET_HEADS):
        cols = slice(h * RET_V_DIM, (h + 1) * RET_V_DIM)
        mine = (lane >= h * half) & (lane < (h + 1) * half)
        zeta = zeta_ref[h]
        st = state_ref[h]
        for j in range(n_sub):
            rows = slice(j * c, (j + 1) * c)
            q1, q2, k1, k2, k_all = rotated[j]
            qm = jnp.concatenate([jnp.where(mine, q1, 0.0), jnp.where(mine, q2, 0.0)], axis=1)
            kz = jnp.concatenate([jnp.where(mine, k1, 0.0) * zeta, jnp.where(mine, k2, 0.0) * zeta], axis=1)
            v = v_ref[0, rows, cols]
            s = _dot_nt(qm, k_all) * decay_ref[h]
            y_ref[rows, cols] = _dot(s, v) + _dot(qm, st) * xi_ref[h]
            st = gam_ref[h] * st + _dot_tn(kz, v)
        state_ref[h] = st

    for h in range(RET_HEADS):
        cols = slice(h * RET_V_DIM, (h + 1) * RET_V_DIM)
        y = y_ref[:, cols]
        mu = jnp.mean(y, axis=-1, keepdims=True)
        yc = y - mu
        var = jnp.mean(yc * yc, axis=-1, keepdims=True)
        yn = yc * lax.rsqrt(var + GN_EPS)
        gate = g_ref[0, :, cols]
        o_ref[0, :, cols] = (gate * jax.nn.sigmoid(gate) * yn).astype(o_ref.dtype)


def _retention_tables(s):
    half = RET_QK_DIM // 2
    inv = 1.0 / (ROPE_BASE ** (np.arange(half, dtype=np.float64) / half))
    ang = np.arange(s, dtype=np.float64)[:, None] * inv[None, :]
    cos_t = np.tile(np.cos(ang), (1, RET_HEADS))
    sin_t = np.tile(np.sin(ang), (1, RET_HEADS))
    c = RET_CHUNK
    lg = np.log(1.0 - np.exp2(-5.0 - np.arange(RET_HEADS, dtype=np.float64)))
    idx = np.arange(c, dtype=np.float64)
    diff = idx[:, None] - idx[None, :]
    decay = np.where(diff >= 0, np.exp(lg[:, None, None] * np.maximum(diff, 0.0)), 0.0)
    lanes = (RET_HEADS, c, LANE)
    zeta = np.broadcast_to(np.exp(lg[:, None] * (c - 1.0 - idx))[:, :, None], lanes)
    xi = np.broadcast_to(np.exp(lg[:, None] * (idx + 1.0))[:, :, None], lanes)
    gam = np.broadcast_to(np.exp(lg * c)[:, None, None], (RET_HEADS, 1, LANE))
    return tuple(jnp.asarray(np.ascontiguousarray(a), F32) for a in (cos_t, sin_t, decay, zeta, xi, gam))


def _retention_weight_layout(w_qk):
    d = w_qk.shape[0]
    half = RET_QK_DIM // 2
    return w_qk.reshape(d, 2, RET_HEADS, 2, half).transpose(0, 1, 3, 2, 4).reshape(d, -1)


def _retention(h3):
    b, s, _ = h3.shape
    c = RET_CHUNK
    qk_w = 2 * RET_HEADS * RET_QK_DIM
    v_w = RET_HEADS * RET_V_DIM
    assert qk_w == v_w
    cos_t, sin_t, decay, zeta, xi, gam = _retention_tables(s)
    const3 = lambda bi, n: (0, 0, 0)
    rows = RET_STEP_CHUNKS * c
    return pl.pallas_call(
        _retention_kernel,
        grid=(b, s // rows),
        in_specs=[pl.BlockSpec((1, rows, qk_w), lambda bi, n: (bi, n, 0)),
                  pl.BlockSpec((1, rows, v_w), lambda bi, n: (bi, n, 1)),
                  pl.BlockSpec((1, rows, v_w), lambda bi, n: (bi, n, 2)),
                  pl.BlockSpec((rows, LANE), lambda bi, n: (n, 0)),
                  pl.BlockSpec((rows, LANE), lambda bi, n: (n, 0)),
                  pl.BlockSpec((RET_HEADS, c, c), const3),
                  pl.BlockSpec((RET_HEADS, c, LANE), const3),
                  pl.BlockSpec((RET_HEADS, c, LANE), const3),
                  pl.BlockSpec((RET_HEADS, 1, LANE), const3)],
        out_specs=pl.BlockSpec((1, rows, v_w), lambda bi, n: (bi, n, 0)),
        out_shape=jax.ShapeDtypeStruct((b, s, v_w), BF16),
        scratch_shapes=[pltpu.VMEM((RET_HEADS, RET_HEADS * RET_QK_DIM, RET_V_DIM), F32),
                        pltpu.VMEM((rows, v_w), F32)],
        compiler_params=_params("parallel", "arbitrary"),
        name="retention",
    )(h3, h3, h3, cos_t, sin_t, decay, zeta, xi, gam)


def _dilated_kernel(q_ref, kp_ref, kc_ref, vp_ref, vc_ref, bias_ref, o_ref, acc_ref, m_ref, l_ref,
                    s0_ref, s1_ref, p0_ref, p1_ref):
    j = pl.program_id(2)
    sup = DIL_SUPER
    q_blk = DIL_BLOCK
    scale = DIL_HEAD_DIM ** -0.5 * math.log2(math.e)
    n_blocks = sup // q_blk
    lane = lax.broadcasted_iota(jnp.int32, (q_blk, LANE), 1)
    head0 = lane < DIL_HEAD_DIM
    first_bias = jnp.where(j == 0, 1, 0)

    groups = []
    for bi, d in enumerate(DIL_DILATIONS):
        n_per_r = sup // (q_blk * d)
        for t0 in range(0, n_blocks, DIL_GROUP):
            blocks = []
            for t in range(t0, t0 + DIL_GROUP):
                r, n = divmod(t, n_per_r)
                blocks.append((n * (q_blk * d) + r, (n - 1) * (q_blk * d) + r, n))
            groups.append((bi, d, blocks))
    s_bufs = (s0_ref, s1_ref)
    p_bufs = (p0_ref, p1_ref)

    def window(prev_ref, cur_ref, k_start, d):
        if k_start >= 0:
            return cur_ref[0, pl.ds(k_start, 2 * q_blk, stride=d), :]
        return jnp.concatenate([prev_ref[0, pl.ds(sup + k_start, q_blk, stride=d), :],
                                cur_ref[0, pl.ds(k_start + q_blk * d, q_blk, stride=d), :]], axis=0)

    def scores(gi):
        bi, d, blocks = groups[gi]
        for g, (q_start, k_start, n) in enumerate(blocks):
            q = q_ref[0, pl.ds(q_start, q_blk, stride=d), :] * scale
            kb = window(kp_ref, kc_ref, k_start, d)
            q2 = jnp.concatenate([jnp.where(head0, q, 0.0), jnp.where(head0, 0.0, q)], axis=0)
            bias = bias_ref[first_bias] if n == 0 else bias_ref[0]
            s_bufs[gi % 2][g] = _dot_nt(q2, kb) + bias

    def softmax(gi):
        bi, d, blocks = groups[gi]
        for g, (q_start, k_start, n) in enumerate(blocks):
            s = s_bufs[gi % 2][g]
            m2 = jnp.max(s, axis=-1, keepdims=True)
            p_bufs[gi % 2][g] = jnp.exp2(s - m2).astype(BF16)
            m_ref[bi, pl.ds(q_start, q_blk, stride=d), :] = jnp.where(head0, m2[:q_blk], m2[q_blk:])

    def values(gi):
        bi, d, blocks = groups[gi]
        for g, (q_start, k_start, n) in enumerate(blocks):
            vb = window(vp_ref, vc_ref, k_start, d).astype(BF16)
            o2 = jnp.dot(p_bufs[gi % 2][g], jnp.concatenate([vb, jnp.ones_like(vb)], axis=1),
                         preferred_element_type=F32)
            rows = pl.ds(q_start, q_blk, stride=d)
            acc_ref[bi, rows, :] = jnp.where(head0, o2[:q_blk, :LANE], o2[q_blk:, :LANE])
            l_ref[bi, rows, :] = jnp.where(head0, o2[:q_blk, LANE:], o2[q_blk:, LANE:])

    for step in range(len(groups) + 2):
        if step < len(groups):
            scores(step)
        if 0 <= step - 1 < len(groups):
            softmax(step - 1)
        if step - 2 >= 0:
            values(step - 2)

    def merge(c, carry):
        rows = pl.ds(pl.multiple_of(c * q_blk, q_blk), q_blk)
        ms = [m_ref[bi, rows, :] for bi in range(len(DIL_DILATIONS))]
        m_all = functools.reduce(jnp.maximum, ms)
        ws = [jnp.exp2(m - m_all) for m in ms]
        num = functools.reduce(lambda a, b: a + b, [w * acc_ref[bi, rows, :] for bi, w in enumerate(ws)])
        den = functools.reduce(lambda a, b: a + b, [w * l_ref[bi, rows, :] for bi, w in enumerate(ws)])
        o_ref[0, rows, :] = (num / den).astype(o_ref.dtype)
        return carry

    lax.fori_loop(0, n_blocks, merge, 0)


def _dilated_bias():
    q_blk = DIL_BLOCK
    qi = np.arange(2 * q_blk)[:, None] % q_blk
    kj = np.arange(2 * q_blk)[None, :]
    band = (kj >= qi) & (kj <= qi + q_blk)
    return jnp.asarray(np.stack([np.where(band, 0.0, NEG), np.where(band & (kj >= q_blk), 0.0, NEG)]), F32)


def _dilated(h3, col0):
    b, s, _ = h3.shape
    sup = DIL_SUPER
    n_pairs = DIL_HEADS * DIL_HEAD_DIM // LANE
    n_br = len(DIL_DILATIONS)
    cq, ck, cv = col0 * n_pairs, (col0 + 1) * n_pairs, (col0 + 2) * n_pairs
    cur = lambda c: (lambda bi, p, j: (bi, j, c + p))
    prev = lambda c: (lambda bi, p, j: (bi, jnp.maximum(j - 1, 0), c + p))
    blk = (1, sup, LANE)
    return pl.pallas_call(
        _dilated_kernel,
        grid=(b, n_pairs, s // sup),
        in_specs=[pl.BlockSpec(blk, cur(cq)),
                  pl.BlockSpec(blk, prev(ck)), pl.BlockSpec(blk, cur(ck)),
                  pl.BlockSpec(blk, prev(cv)), pl.BlockSpec(blk, cur(cv)),
                  pl.BlockSpec((2, 2 * DIL_BLOCK, 2 * DIL_BLOCK), lambda bi, p, j: (0, 0, 0))],
        out_specs=pl.BlockSpec(blk, lambda bi, p, j: (bi, j, p)),
        out_shape=jax.ShapeDtypeStruct((b, s, n_pairs * LANE), BF16),
        scratch_shapes=[pltpu.VMEM((n_br, sup, LANE), F32), pltpu.VMEM((n_br, sup, LANE), F32),
                        pltpu.VMEM((n_br, sup, LANE), F32),
                        pltpu.VMEM((DIL_GROUP, 2 * DIL_BLOCK, 2 * DIL_BLOCK), F32),
                        pltpu.VMEM((DIL_GROUP, 2 * DIL_BLOCK, 2 * DIL_BLOCK), F32),
                        pltpu.VMEM((DIL_GROUP, 2 * DIL_BLOCK, 2 * DIL_BLOCK), BF16),
                        pltpu.VMEM((DIL_GROUP, 2 * DIL_BLOCK, 2 * DIL_BLOCK), BF16)],
        compiler_params=_params("parallel", "parallel", "arbitrary"),
        name="dilated",
    )(h3, h3, h3, h3, h3, _dilated_bias())


def _mem_kv(mem2d, w_bf16, tn=512):
    m, d = mem2d.shape
    n = w_bf16.shape[1]
    return pl.pallas_call(
        _proj_in_kernel,
        grid=(n // tn,),
        in_specs=[pl.BlockSpec((m, d), lambda i: (0, 0)),
                  pl.BlockSpec((d, tn), lambda i: (0, i))],
        out_specs=pl.BlockSpec((m, tn), lambda i: (0, i)),
        out_shape=jax.ShapeDtypeStruct((m, n), BF16),
        compiler_params=_params("parallel"),
        name="mem_kv",
    )(mem2d, w_bf16)


def _xattn_kernel(alpha, x_ref, yr_ref, yd_ref, wout_ref, g1_ref, b1_ref, k_ref, v_ref, wq_ref, wo_ref,
                  g_ref, b_ref, wr_ref, o_ref, ot_ref, lg_ref, x1_ref, q_ref, att_ref):
    tm, d = x_ref.shape
    dh = d // XATTN_HEADS
    wr = yr_ref.shape[1]
    scale = dh ** -0.5 * math.log2(math.e)
    part = tm // XATTN_PARTS

    def mix(rows):
        y = _dot(yr_ref[rows, :], wout_ref[0:wr, :]) + _dot(yd_ref[rows, :], wout_ref[wr:, :])
        x1_ref[rows, :] = _layer_norm(alpha * x_ref[rows, :] + y, g1_ref[...], b1_ref[...])

    def query(rows):
        q_ref[rows, :] = _dot(x1_ref[rows, :], wq_ref[...]).astype(BF16)

    def attend(rows):
        for h in range(XATTN_HEADS):
            sl = slice(h * dh, (h + 1) * dh)
            s = _dot_nt(q_ref[rows, sl], k_ref[0, :, sl]) * scale
            m = jnp.max(s, axis=-1, keepdims=True)
            e = jnp.exp2(s - m)
            p = e / jnp.sum(e, axis=-1, keepdims=True)
            att_ref[rows, sl] = _dot(p, v_ref[0, :, sl]).astype(BF16)

    def finish(rows, r0):
        y = _dot(att_ref[rows, :], wo_ref[...])
        x2 = _layer_norm(alpha * x1_ref[rows, :] + y, g_ref[...], b_ref[...])
        o_ref[rows, :] = x2
        for c in range(d // LANE):
            ot_ref[pl.ds(r0 * SUBLANE + c, part, stride=SUBLANE), :] = x2[:, c * LANE:(c + 1) * LANE]
        lg_ref[:, rows] = _dot_nt(wr_ref[...], x2)

    stages = (mix, query, attend, finish)
    for step in range(XATTN_PARTS + len(stages) - 1):
        for si, stage in enumerate(stages):
            pi = step - si
            if 0 <= pi < XATTN_PARTS:
                rows = slice(pi * part, (pi + 1) * part)
                if stage is finish:
                    stage(rows, pi * part)
                else:
                    stage(rows)


def _xattn(x2d, y_ret, y_dil, w_out, g1, b1, kv, wq, wo, g, bb, w_r, alpha, seq, tm=1024):
    t, d = x2d.shape
    mlen = kv.shape[1]
    tiles_per_seq = seq // tm
    row = lambda i: (i, 0)

    def whole(shape):
        return pl.BlockSpec(shape, lambda i: (0,) * len(shape), pipeline_mode=pl.Buffered(1))

    vec = whole((1, d))
    return pl.pallas_call(
        functools.partial(_xattn_kernel, alpha),
        grid=(t // tm,),
        in_specs=[pl.BlockSpec((tm, d), row),
                  pl.BlockSpec((tm, y_ret.shape[1]), row),
                  pl.BlockSpec((tm, y_dil.shape[1]), row),
                  whole(w_out.shape), vec, vec,
                  pl.BlockSpec((1, mlen, d), lambda i: (i // tiles_per_seq, 0, 0)),
                  pl.BlockSpec((1, mlen, d), lambda i: (i // tiles_per_seq, 0, 1)),
                  whole((d, d)), whole((d, d)), vec, vec,
                  whole((ROUTE_ROWS, d))],
        out_specs=[pl.BlockSpec((tm, d), row),
                   pl.BlockSpec((tm * SUBLANE, LANE), row),
                   pl.BlockSpec((ROUTE_ROWS, tm), lambda i: (0, i))],
        out_shape=[jax.ShapeDtypeStruct((t, d), F32),
                   jax.ShapeDtypeStruct((t * SUBLANE, LANE), F32),
                   jax.ShapeDtypeStruct((ROUTE_ROWS, t), F32)],
        scratch_shapes=[pltpu.VMEM((tm, d), F32), pltpu.VMEM((tm, d), BF16), pltpu.VMEM((tm, d), BF16)],
        compiler_params=_params("parallel"),
        name="xattn",
    )(x2d, y_ret, y_dil, w_out, g1, b1, kv, kv, wq, wo, g, bb, w_r)


def _route_kernel(lg_ref, before_ref, meta_ref, cnt_ref, carry_ref):
    i = pl.program_id(0)

    @pl.when(i == 0)
    def _():
        carry_ref[...] = jnp.zeros_like(carry_ref)

    rows, tt = lg_ref.shape
    sub = before_ref.shape[0]
    r = lax.broadcasted_iota(jnp.int32, (rows, sub), 0)
    r8 = lax.broadcasted_iota(jnp.int32, (SUBLANE, sub), 0)
    is_group = r < N_GROUPS

    def col_max(a):
        return jnp.max(a, axis=0, keepdims=True)

    def first_row_where(mask):
        return jnp.min(jnp.where(mask, r, rows), axis=0, keepdims=True)

    for c in range(tt // sub):
        lg = lg_ref[:, c * sub:(c + 1) * sub]
        mg = col_max(jnp.where(is_group, lg, NEG))
        eg = jnp.where(is_group, jnp.exp(lg - mg), 0.0)
        pg = eg / jnp.sum(eg, axis=0, keepdims=True)
        g1 = col_max(pg)
        gi = first_row_where(is_group & (pg == g1))
        lo = N_GROUPS + gi * EXPERTS_PER_GROUP
        in_grp = (r >= lo) & (r < lo + EXPERTS_PER_GROUP)
        v1 = col_max(jnp.where(in_grp, lg, NEG))
        i1 = first_row_where(in_grp & (lg == v1))
        rest = in_grp & (r != i1)
        v2 = col_max(jnp.where(rest, lg, NEG))
        i2 = first_row_where(rest & (lg == v2))
        e2 = jnp.exp(v2 - v1)
        den = 1.0 + e2
        gate1 = g1 * (1.0 / den)
        gate2 = g1 * (e2 / den)
        sel1 = r == i1
        sel2 = r == i2
        onehot = jnp.where(sel1 | sel2, 1.0, 0.0)
        rank = _dot(onehot, before_ref[...]) + carry_ref[:, 0:1]
        r1 = jnp.sum(jnp.where(sel1, rank, 0.0), axis=0, keepdims=True)
        r2 = jnp.sum(jnp.where(sel2, rank, 0.0), axis=0, keepdims=True)
        carry_ref[...] = carry_ref[...] + jnp.sum(onehot, axis=1, keepdims=True)
        meta = jnp.where(r8 == 0, (i1 - N_GROUPS).astype(F32), 0.0)
        meta = jnp.where(r8 == 1, (i2 - N_GROUPS).astype(F32), meta)
        meta = jnp.where(r8 == 2, r1, meta)
        meta = jnp.where(r8 == 3, r2, meta)
        meta = jnp.where(r8 == 4, gate1, meta)
        meta = jnp.where(r8 == 5, gate2, meta)
        meta_ref[:, c * sub:(c + 1) * sub] = meta
    cnt_ref[...] = carry_ref[...]


def _route(logits_t, tt=1024, sub=256):
    rows, t = logits_t.shape
    before = jnp.asarray(np.arange(sub)[:, None] < np.arange(sub)[None, :], BF16)
    return pl.pallas_call(
        _route_kernel,
        grid=(t // tt,),
        in_specs=[pl.BlockSpec((rows, tt), lambda i: (0, i)),
                  pl.BlockSpec((sub, sub), lambda i: (0, 0))],
        out_specs=[pl.BlockSpec((SUBLANE, tt), lambda i: (0, i)),
                   pl.BlockSpec((rows, LANE), lambda i: (0, 0))],
        out_shape=[jax.ShapeDtypeStruct((SUBLANE, t), F32),
                   jax.ShapeDtypeStruct((rows, LANE), F32)],
        scratch_shapes=[pltpu.VMEM((rows, LANE), F32)],
        compiler_params=_params("arbitrary"),
        name="route",
    )(logits_t, before)


def _plan_kernel(meta_ref, first_row_ref, dest_ref):
    n_tiles = dest_ref.shape[0]
    tt = dest_ref.shape[2] // 2
    rows = first_row_ref.shape[0]
    first_row = first_row_ref[:, 0:1]
    r = lax.broadcasted_iota(jnp.int32, (rows, tt), 0)
    for g in range(n_tiles):
        m = meta_ref[:, g * tt:(g + 1) * tt]

        def dest_of(k):
            e_row = m[k:k + 1, :].astype(jnp.int32) + N_GROUPS
            return jnp.sum(jnp.where(r == e_row, first_row, 0.0), axis=0, keepdims=True) + m[2 + k:3 + k, :]

        dest_ref[g] = jnp.concatenate([dest_of(0), dest_of(1)], axis=1).astype(jnp.int32)


def _plan(meta_t, first_row, tt, tiles_per_step=4):
    t = meta_t.shape[1]
    rows = first_row.shape[0]
    return pl.pallas_call(
        _plan_kernel,
        grid=(t // (tt * tiles_per_step),),
        in_specs=[pl.BlockSpec((SUBLANE, tt * tiles_per_step), lambda i: (0, i)),
                  pl.BlockSpec((rows, LANE), lambda i: (0, 0))],
        out_specs=pl.BlockSpec((tiles_per_step, 1, 2 * tt), lambda i: (i, 0, 0)),
        out_shape=jax.ShapeDtypeStruct((t // tt, 1, 2 * tt), jnp.int32),
        compiler_params=_params("parallel"),
        name="plan",
    )(meta_t, first_row)


def _row_copy(src, dst, s_row, d_row, sem):
    return pltpu.make_async_copy(src.at[pl.ds(pl.multiple_of(s_row * SUBLANE, SUBLANE), SUBLANE), :],
                                 dst.at[pl.ds(pl.multiple_of(d_row * SUBLANE, SUBLANE), SUBLANE), :], sem)


def _dispatch_kernel(dest_ref, xt_ref, xs_ref, sem):
    tq = dest_ref.shape[2] // 2

    def issue(t, c):
        _row_copy(xt_ref, xs_ref, t, dest_ref[0, 0, t], sem).start(priority=0)
        _row_copy(xt_ref, xs_ref, t, dest_ref[0, 0, tq + t], sem).start(priority=1)
        return c

    lax.fori_loop(0, tq, issue, 0, unroll=8)
    for _ in range(2):
        pltpu.make_async_copy(xt_ref, xs_ref.at[pl.ds(0, tq * SUBLANE), :], sem).wait()


def _dispatch(dest3, x_tiled, n_rows, tiles_per_step=4):
    n_tiles, _, two_tt = dest3.shape
    tt = two_tt // 2
    tq = tiles_per_step * tt
    n_steps = n_tiles // tiles_per_step
    dest3 = dest3.reshape(n_steps, tiles_per_step, 2, tt).transpose(0, 2, 1, 3).reshape(n_steps, 1, 2 * tq)
    return pl.pallas_call(
        _dispatch_kernel,
        grid=(n_steps,),
        in_specs=[pl.BlockSpec((1, 1, 2 * tq), lambda i: (i, 0, 0), memory_space=pltpu.SMEM),
                  pl.BlockSpec((tq * SUBLANE, LANE), lambda i: (i, 0))],
        out_specs=pl.BlockSpec(memory_space=pl.ANY),
        out_shape=jax.ShapeDtypeStruct((n_rows * SUBLANE, LANE), F32),
        scratch_shapes=[pltpu.SemaphoreType.DMA(())],
        compiler_params=_params("arbitrary"),
        name="dispatch",
    )(dest3, x_tiled)


def _experts_kernel(blk_exp_ref, blk_rows_ref, n_used_ref, blk_first_ref, blk_slot_ref, blk_next_ref,
                    xs_ref, wg_hbm, wu_hbm, wd_hbm, ys_ref, x_scr, y_scr, wg_buf, wu_buf, wd_buf, sems):
    b = pl.program_id(0)

    def weight_copies(e, slot):
        return [pltpu.make_async_copy(hbm.at[e], buf.at[slot], sems.at[k, slot])
                for k, (hbm, buf) in enumerate(((wg_hbm, wg_buf), (wu_hbm, wu_buf), (wd_hbm, wd_buf)))]

    @pl.when(b < n_used_ref[0])
    def _():
        slot = blk_slot_ref[b]

        @pl.when(b == 0)
        def _():
            for cp in weight_copies(blk_exp_ref[0], slot):
                cp.start()

        @pl.when(blk_first_ref[b] == 1)
        def _():
            @pl.when(blk_next_ref[b] >= 0)
            def _():
                for cp in weight_copies(blk_next_ref[b], 1 - slot):
                    cp.start()

            for cp in weight_copies(blk_exp_ref[b], slot):
                cp.wait()

        rows, d = x_scr.shape
        n_valid = blk_rows_ref[b]

        def run(m):
            ri = lax.broadcasted_iota(jnp.int32, (m, 1), 0)
            for c in range(d // LANE):
                x = xs_ref[pl.ds(c, m, stride=SUBLANE), :]
                x_scr[0:m, c * LANE:(c + 1) * LANE] = jnp.where(ri < n_valid, x, 0.0).astype(BF16)
            x = x_scr[0:m, :]
            hg = _dot(x, wg_buf[slot])
            hu = _dot(x, wu_buf[slot])
            y_scr[0:m, :] = _dot(hg * jax.nn.sigmoid(hg) * hu, wd_buf[slot])
            for c in range(d // LANE):
                ys_ref[pl.ds(c, m, stride=SUBLANE), :] = y_scr[0:m, c * LANE:(c + 1) * LANE]

        n_quanta = rows // EXPERT_QUANTUM
        for k in range(1, n_quanta + 1):
            lo = (k - 1) * EXPERT_QUANTUM
            pl.when((n_valid > lo) & (n_valid <= k * EXPERT_QUANTUM))(functools.partial(run, k * EXPERT_QUANTUM))


def _experts(blk_exp, blk_rows, n_used, blk_first, blk_slot, blk_next, xs_tiled, w_g, w_u, w_d):
    n_rows = xs_tiled.shape[0] // SUBLANE
    rows = EXPERT_ROWS
    nblk = n_rows // rows
    _, d, ff = w_g.shape
    used = lambda b, be, br, nu, *_: (jnp.minimum(b, nu[0] - 1), 0)
    grid_spec = pltpu.PrefetchScalarGridSpec(
        num_scalar_prefetch=6,
        grid=(nblk,),
        in_specs=[pl.BlockSpec((rows * SUBLANE, LANE), used),
                  pl.BlockSpec(memory_space=pl.ANY), pl.BlockSpec(memory_space=pl.ANY),
                  pl.BlockSpec(memory_space=pl.ANY)],
        out_specs=pl.BlockSpec((rows * SUBLANE, LANE), used),
        scratch_shapes=[pltpu.VMEM((rows, d), BF16), pltpu.VMEM((rows, d), F32),
                        pltpu.VMEM((2, d, ff), F32), pltpu.VMEM((2, d, ff), F32), pltpu.VMEM((2, ff, d), F32),
                        pltpu.SemaphoreType.DMA((3, 2))],
    )
    return pl.pallas_call(
        _experts_kernel,
        grid_spec=grid_spec,
        out_shape=jax.ShapeDtypeStruct((n_rows * SUBLANE, LANE), F32),
        compiler_params=_params("arbitrary"),
        name="experts",
    )(blk_exp, blk_rows, n_used, blk_first, blk_slot, blk_next, xs_tiled, w_g, w_u, w_d)


def _combine_kernel(alpha, dcur_ref, dnext_ref, meta_ref, x_ref, g_ref, b_ref, ys_ref, o_ref, buf_ref, sems):
    i = pl.program_id(0)
    tq, d = x_ref.shape
    slot_rows = 2 * tq * SUBLANE

    def issue(d_ref, slot, t):
        _row_copy(ys_ref, buf_ref, d_ref[0, 0, t], slot * (2 * tq) + t, sems.at[slot]).start(priority=0)
        _row_copy(ys_ref, buf_ref, d_ref[0, 0, tq + t], slot * (2 * tq) + tq + t, sems.at[slot]).start(priority=1)

    def wait_slot(slot):
        off = pl.multiple_of(slot * slot_rows, slot_rows)
        pltpu.make_async_copy(ys_ref.at[pl.ds(0, slot_rows), :], buf_ref.at[pl.ds(off, slot_rows), :],
                              sems.at[slot]).wait()
        return off

    def issue_all(d_ref, slot):
        lax.fori_loop(0, tq, lambda t, c: (issue(d_ref, slot, t), c)[1], 0, unroll=4)

    slot = i % 2

    @pl.when(i == 0)
    def _():
        issue_all(dcur_ref, 0)

    @pl.when(i + 1 < pl.num_programs(0))
    def _():
        issue_all(dnext_ref, 1 - slot)

    off = wait_slot(slot)
    meta_rows = jnp.transpose(jnp.concatenate([meta_ref[...], jnp.zeros((LANE - SUBLANE, tq), F32)], axis=0))
    gate1 = meta_rows[:, 4:5]
    gate2 = meta_rows[:, 5:6]
    for c in range(d // LANE):
        sl = slice(c * LANE, (c + 1) * LANE)
        y1 = buf_ref[pl.ds(off + c, tq, stride=SUBLANE), :]
        y2 = buf_ref[pl.ds(off + tq * SUBLANE + c, tq, stride=SUBLANE), :]
        o_ref[:, sl] = alpha * x_ref[:, sl] + (y1 * gate1 + y2 * gate2)
    o_ref[...] = _layer_norm(o_ref[...], g_ref[...], b_ref[...])


def _combine(dest3, meta, x2, g, bb, ys_tiled, alpha):
    t, d = x2.shape
    n = dest3.shape[0]
    tq = t // n
    row = lambda i: (i, 0)
    const = lambda i: (0, 0)
    return pl.pallas_call(
        functools.partial(_combine_kernel, alpha),
        grid=(n,),
        in_specs=[pl.BlockSpec((1, 1, 2 * tq), lambda i: (i, 0, 0), memory_space=pltpu.SMEM),
                  pl.BlockSpec((1, 1, 2 * tq), lambda i: (jnp.minimum(i + 1, n - 1), 0, 0), memory_space=pltpu.SMEM),
                  pl.BlockSpec((SUBLANE, tq), lambda i: (0, i)),
                  pl.BlockSpec((tq, d), row),
                  pl.BlockSpec((1, d), const), pl.BlockSpec((1, d), const),
                  pl.BlockSpec(memory_space=pl.ANY)],
        out_specs=pl.BlockSpec((tq, d), row),
        out_shape=jax.ShapeDtypeStruct((t, d), F32),
        scratch_shapes=[pltpu.VMEM((2 * 2 * tq * SUBLANE, LANE), F32), pltpu.SemaphoreType.DMA((2,))],
        compiler_params=_params("arbitrary"),
        name="combine",
    )(dest3, dest3, meta, x2, g, bb, ys_tiled)


def _moe(x2, x2_tiled, logits, w_g, w_u, w_d, g, bb, alpha):
    t, d = x2.shape
    meta, cnt = _route(logits)
    counts = cnt[N_GROUPS:N_GROUPS + N_EXPERTS, 0].astype(jnp.int32)
    padded = ((counts + EXPERT_ROWS - 1) // EXPERT_ROWS) * EXPERT_ROWS
    pend = jnp.cumsum(padded)
    poff = pend - padded
    first_row = jnp.pad(poff.astype(F32), (N_GROUPS, ROUTE_ROWS - N_GROUPS - N_EXPERTS))
    dest3 = _plan(meta, jnp.broadcast_to(first_row[:, None], (ROUTE_ROWS, LANE)), ROUTE_TILE)
    n_rows = t * 2 + N_EXPERTS * EXPERT_ROWS
    nblk = n_rows // EXPERT_ROWS
    blk_start = jnp.arange(nblk, dtype=jnp.int32) * EXPERT_ROWS
    blk_exp = jnp.sum((pend[None, :] <= blk_start[:, None]).astype(jnp.int32), axis=1)
    blk_exp = jnp.minimum(blk_exp, N_EXPERTS - 1)
    e_idx = jnp.arange(N_EXPERTS, dtype=jnp.int32)
    owner = blk_exp[:, None] == e_idx[None, :]

    def of_block(per_expert):
        return jnp.sum(jnp.where(owner, per_expert[None, :], 0), axis=1).astype(jnp.int32)

    blk_rows = jnp.clip(of_block(poff + counts) - blk_start, 0, EXPERT_ROWS).astype(jnp.int32)
    n_used = (pend[-1:] // EXPERT_ROWS).astype(jnp.int32)
    has_rows = counts > 0
    ordinal = jnp.cumsum(has_rows.astype(jnp.int32)) - 1
    later = has_rows[None, :] & (e_idx[None, :] > e_idx[:, None])
    next_used = jnp.min(jnp.where(later, e_idx[None, :], N_EXPERTS), axis=1)
    next_used = jnp.where(next_used == N_EXPERTS, -1, next_used).astype(jnp.int32)
    blk_first = (blk_start == of_block(poff)).astype(jnp.int32)
    blk_slot = of_block(ordinal % 2)
    blk_next = of_block(next_used)
    xs_tiled = _dispatch(dest3, x2_tiled, n_rows)
    ys_tiled = _experts(blk_exp, blk_rows, n_used, blk_first, blk_slot, blk_next, xs_tiled, w_g, w_u, w_d)
    return _combine(dest3, meta, x2, g, bb, ys_tiled, alpha)


def kernel(x, mem, w_in, w_out, ln_mix_g, ln_mix_b, w_xq, w_xkv, w_xo, ln_x_g, ln_x_b, w_route_group,
           w_route_expert, w_exp_gate, w_exp_up, w_exp_down, ln_moe_g, ln_moe_b):
    b, s, d = x.shape
    depth = w_in.shape[0]
    alpha = (2.0 * depth) ** 0.25
    t = b * s
    ret_cols = (2 * RET_HEADS * RET_QK_DIM + 2 * RET_HEADS * RET_V_DIM)
    assert ret_cols % (DIL_HEADS * DIL_HEAD_DIM) == 0
    dil_col0 = ret_cols // (DIL_HEADS * DIL_HEAD_DIM)
    xc = x.reshape(t, d)
    for l in range(depth):
        n_qk = 2 * RET_HEADS * RET_QK_DIM
        h = _proj_in(xc, w_in[l], _retention_weight_layout(w_in[l][:, :n_qk]).astype(BF16))
        h3 = h.reshape(b, s, h.shape[1])
        y_ret = _retention(h3).reshape(t, -1)
        y_dil = _dilated(h3, dil_col0).reshape(t, -1)
        kv = _mem_kv(mem.reshape(b * mem.shape[1], d), w_xkv[l]).reshape(b, mem.shape[1], 2 * d)
        w_r = jnp.concatenate([w_route_group[l], w_route_expert[l]], axis=-1)
        w_r = jnp.pad(w_r.T, ((0, ROUTE_ROWS - w_r.shape[1]), (0, 0))).astype(BF16)
        x2, x2_tiled, logits = _xattn(xc, y_ret, y_dil, w_out[l].astype(BF16), ln_mix_g[l][None], ln_mix_b[l][None],
                                      kv, w_xq[l].astype(BF16), w_xo[l].astype(BF16),
                                      ln_x_g[l][None], ln_x_b[l][None], w_r, alpha, s)
        xc = _moe(x2, x2_tiled, logits, w_exp_gate[l], w_exp_up[l], w_exp_down[l],
                  ln_moe_g[l][None], ln_moe_b[l][None], alpha)
    return xc.reshape(b, s, d)
```

```python
import functools
import math

import jax
import jax.numpy as jnp
import numpy as np
from jax import lax
from jax.experimental import pallas as pl
from jax.experimental.pallas import tpu as pltpu

BF16 = jnp.bfloat16
F32 = jnp.float32

LANE = 128
SUBLANE = 8
VMEM_LIMIT = 56 * 1024 * 1024

RET_HEADS = 4
RET_QK_DIM = 64
RET_V_DIM = 128
RET_CHUNK = 128
RET_STEP_CHUNKS = 4
ROPE_BASE = 10000.0
DIL_HEADS = 8
DIL_HEAD_DIM = 64
DIL_DILATIONS = (1, 4, 16)
DIL_BLOCK = 128
DIL_SUPER = DIL_BLOCK * max(DIL_DILATIONS)
DIL_GROUP = 1
XATTN_HEADS = 4
XATTN_PARTS = 2
N_GROUPS = 4
EXPERTS_PER_GROUP = 8
N_EXPERTS = N_GROUPS * EXPERTS_PER_GROUP
ROUTE_ROWS = -(-(N_GROUPS + N_EXPERTS) // SUBLANE) * SUBLANE
EXPERT_ROWS = 512
EXPERT_QUANTUM = 256
ROUTE_TILE = 256
LN_EPS = 1e-5
GN_EPS = 1e-6
NEG = -1e30


def _params(*sem):
    return pltpu.CompilerParams(dimension_semantics=sem, vmem_limit_bytes=VMEM_LIMIT)


def _layer_norm(z, g, b):
    mu = jnp.mean(z, axis=-1, keepdims=True)
    zc = z - mu
    var = jnp.mean(zc * zc, axis=-1, keepdims=True)
    return zc * lax.rsqrt(var + LN_EPS) * g + b


def _dot(a, b):
    return jnp.dot(a.astype(BF16), b.astype(BF16), preferred_element_type=F32)


def _dot_nt(a, b):
    return lax.dot_general(a.astype(BF16), b.astype(BF16), (((1,), (1,)), ((), ())),
                           preferred_element_type=F32)


def _dot_tn(a, b):
    return lax.dot_general(a.astype(BF16), b.astype(BF16), (((0,), (0,)), ((), ())),
                           preferred_element_type=F32)


def _proj_in_kernel(x_ref, w_ref, o_ref):
    o_ref[...] = _dot(x_ref[...], w_ref[...]).astype(o_ref.dtype)


def _proj_in2_kernel(x_ref, w_ref, w_head_ref, o_ref, slab_ref, w_bf16):
    @pl.when(pl.program_id(0) == 0)
    def _():
        w_bf16[...] = w_ref[...].astype(BF16)
        w_bf16[:, 0:w_head_ref.shape[1]] = w_head_ref[...]

    n_row = o_ref.shape[1]
    x = x_ref[...].astype(BF16)
    o_ref[...] = jnp.dot(x, w_bf16[:, 0:n_row], preferred_element_type=F32)
    rest = jnp.dot(x, w_bf16[:, n_row:], preferred_element_type=F32)
    for c in range(slab_ref.shape[0]):
        slab_ref[c] = rest[:, c * LANE:(c + 1) * LANE]


def _proj_in(x2d, w_f32, w_head_bf16, n_row, tm=512):
    t, d = x2d.shape
    n = w_f32.shape[1]
    n_slabs = (n - n_row) // LANE
    whole = lambda shape: pl.BlockSpec(shape, lambda i: (0, 0), pipeline_mode=pl.Buffered(1))
    return pl.pallas_call(
        _proj_in2_kernel,
        grid=(t // tm,),
        in_specs=[pl.BlockSpec((tm, d), lambda i: (i, 0)), whole((d, n)), whole(w_head_bf16.shape)],
        out_specs=[pl.BlockSpec((tm, n_row), lambda i: (i, 0)),
                   pl.BlockSpec((n_slabs, tm, LANE), lambda i: (0, i, 0))],
        out_shape=[jax.ShapeDtypeStruct((t, n_row), F32),
                   jax.ShapeDtypeStruct((n_slabs, t, LANE), F32)],
        scratch_shapes=[pltpu.VMEM((d, n), BF16)],
        compiler_params=_params("arbitrary"),
        name="proj_in",
    )(x2d, w_f32, w_head_bf16)


def _retention_kernel(qk_ref, v_ref, g_ref, cos_ref, sin_ref, decay_ref, zeta_ref, xi_ref, gam_ref,
                      o_ref, state_ref, y_ref):
    n = pl.program_id(1)

    @pl.when(n == 0)
    def _():
        state_ref[...] = jnp.zeros_like(state_ref)

    c = RET_CHUNK
    n_sub = qk_ref.shape[1] // c
    half = RET_QK_DIM // 2
    lane = lax.broadcasted_iota(jnp.int32, (c, LANE), 1)

    rotated = []
    for j in range(n_sub):
        rows = slice(j * c, (j + 1) * c)
        cos = cos_ref[rows, :]
        sin = sin_ref[rows, :]

        def rot(col, rows=rows, cos=cos, sin=sin):
            t1 = qk_ref[0, rows, col * LANE:(col + 1) * LANE]
            t2 = qk_ref[0, rows, (col + 1) * LANE:(col + 2) * LANE]
            return t1 * cos - t2 * sin, t1 * sin + t2 * cos

        q1, q2 = rot(0)
        k1, k2 = (t * (RET_QK_DIM ** -0.5) for t in rot(2))
        rotated.append((q1, q2, k1, k2, jnp.concatenate([k1, k2], axis=1)))

    for h in range(RET_HEADS):
        cols = slice(h * RET_V_DIM, (h + 1) * RET_V_DIM)
        mine = (lane >= h * half) & (lane < (h + 1) * half)
        zeta = zeta_ref[h]
        st = state_ref[h]
        for j in range(n_sub):
            rows = slice(j * c, (j + 1) * c)
            q1, q2, k1, k2, k_all = rotated[j]
            qm = jnp.concatenate([jnp.where(mine, q1, 0.0), jnp.where(mine, q2, 0.0)], axis=1)
            kz = jnp.concatenate([jnp.where(mine, k1, 0.0) * zeta, jnp.where(mine, k2, 0.0) * zeta], axis=1)
            v = v_ref[0, rows, cols]
            s = _dot_nt(qm, k_all) * decay_ref[h]
            y_ref[rows, cols] = _dot(s, v) + _dot(qm, st) * xi_ref[h]
            st = gam_ref[h] * st + _dot_tn(kz, v)
        state_ref[h] = st

    for h in range(RET_HEADS):
        cols = slice(h * RET_V_DIM, (h + 1) * RET_V_DIM)
        y = y_ref[:, cols]
        mu = jnp.mean(y, axis=-1, keepdims=True)
        yc = y - mu
        var = jnp.mean(yc * yc, axis=-1, keepdims=True)
        yn = yc * lax.rsqrt(var + GN_EPS)
        gate = g_ref[0, :, cols]
        o_ref[0, :, cols] = (gate * jax.nn.sigmoid(gate) * yn).astype(o_ref.dtype)


def _retention_tables(s):
    half = RET_QK_DIM // 2
    inv = 1.0 / (ROPE_BASE ** (np.arange(half, dtype=np.float64) / half))
    ang = np.arange(s, dtype=np.float64)[:, None] * inv[None, :]
    cos_t = np.tile(np.cos(ang), (1, RET_HEADS))
    sin_t = np.tile(np.sin(ang), (1, RET_HEADS))
    c = RET_CHUNK
    lg = np.log(1.0 - np.exp2(-5.0 - np.arange(RET_HEADS, dtype=np.float64)))
    idx = np.arange(c, dtype=np.float64)
    diff = idx[:, None] - idx[None, :]
    decay = np.where(diff >= 0, np.exp(lg[:, None, None] * np.maximum(diff, 0.0)), 0.0)
    lanes = (RET_HEADS, c, LANE)
    zeta = np.broadcast_to(np.exp(lg[:, None] * (c - 1.0 - idx))[:, :, None], lanes)
    xi = np.broadcast_to(np.exp(lg[:, None] * (idx + 1.0))[:, :, None], lanes)
    gam = np.broadcast_to(np.exp(lg * c)[:, None, None], (RET_HEADS, 1, LANE))
    return tuple(jnp.asarray(np.ascontiguousarray(a), F32) for a in (cos_t, sin_t, decay, zeta, xi, gam))


def _retention_weight_layout(w_qk):
    d = w_qk.shape[0]
    half = RET_QK_DIM // 2
    return w_qk.reshape(d, 2, RET_HEADS, 2, half).transpose(0, 1, 3, 2, 4).reshape(d, -1)


def _retention(h3):
    b, s, _ = h3.shape
    c = RET_CHUNK
    qk_w = 2 * RET_HEADS * RET_QK_DIM
    v_w = RET_HEADS * RET_V_DIM
    assert qk_w == v_w
    cos_t, sin_t, decay, zeta, xi, gam = _retention_tables(s)
    const3 = lambda bi, n: (0, 0, 0)
    rows = RET_STEP_CHUNKS * c
    return pl.pallas_call(
        _retention_kernel,
        grid=(b, s // rows),
        in_specs=[pl.BlockSpec((1, rows, qk_w), lambda bi, n: (bi, n, 0)),
                  pl.BlockSpec((1, rows, v_w), lambda bi, n: (bi, n, 1)),
                  pl.BlockSpec((1, rows, v_w), lambda bi, n: (bi, n, 2)),
                  pl.BlockSpec((rows, LANE), lambda bi, n: (n, 0)),
                  pl.BlockSpec((rows, LANE), lambda bi, n: (n, 0)),
                  pl.BlockSpec((RET_HEADS, c, c), const3),
                  pl.BlockSpec((RET_HEADS, c, LANE), const3),
                  pl.BlockSpec((RET_HEADS, c, LANE), const3),
                  pl.BlockSpec((RET_HEADS, 1, LANE), const3)],
        out_specs=pl.BlockSpec((1, rows, v_w), lambda bi, n: (bi, n, 0)),
        out_shape=jax.ShapeDtypeStruct((b, s, v_w), BF16),
        scratch_shapes=[pltpu.VMEM((RET_HEADS, RET_HEADS * RET_QK_DIM, RET_V_DIM), F32),
                        pltpu.VMEM((rows, v_w), F32)],
        compiler_params=_params("parallel", "arbitrary"),
        name="retention",
    )(h3, h3, h3, cos_t, sin_t, decay, zeta, xi, gam)


def _dilated_kernel(q_ref, kp_ref, kc_ref, vp_ref, vc_ref, bias_ref, o_ref, acc_ref, m_ref, l_ref,
                    s0_ref, s1_ref, p0_ref, p1_ref):
    j = pl.program_id(2)
    sup = DIL_SUPER
    q_blk = DIL_BLOCK
    scale = DIL_HEAD_DIM ** -0.5 * math.log2(math.e)
    n_blocks = sup // q_blk
    lane = lax.broadcasted_iota(jnp.int32, (q_blk, LANE), 1)
    head0 = lane < DIL_HEAD_DIM
    first_bias = jnp.where(j == 0, 1, 0)

    groups = []
    for bi, d in enumerate(DIL_DILATIONS):
        n_per_r = sup // (q_blk * d)
        for t0 in range(0, n_blocks, DIL_GROUP):
            blocks = []
            for t in range(t0, t0 + DIL_GROUP):
                r, n = divmod(t, n_per_r)
                blocks.append((n * (q_blk * d) + r, (n - 1) * (q_blk * d) + r, n))
            groups.append((bi, d, blocks))
    s_bufs = (s0_ref, s1_ref)
    p_bufs = (p0_ref, p1_ref)

    def window(prev_ref, cur_ref, k_start, d):
        if k_start >= 0:
            return cur_ref[0, pl.ds(k_start, 2 * q_blk, stride=d), :]
        return jnp.concatenate([prev_ref[0, pl.ds(sup + k_start, q_blk, stride=d), :],
                                cur_ref[0, pl.ds(k_start + q_blk * d, q_blk, stride=d), :]], axis=0)

    def scores(gi):
        bi, d, blocks = groups[gi]
        for g, (q_start, k_start, n) in enumerate(blocks):
            q = q_ref[0, pl.ds(q_start, q_blk, stride=d), :] * scale
            kb = window(kp_ref, kc_ref, k_start, d)
            q2 = jnp.concatenate([jnp.where(head0, q, 0.0), jnp.where(head0, 0.0, q)], axis=0)
            bias = bias_ref[first_bias] if n == 0 else bias_ref[0]
            s_bufs[gi % 2][g] = _dot_nt(q2, kb) + bias

    def softmax(gi):
        bi, d, blocks = groups[gi]
        for g, (q_start, k_start, n) in enumerate(blocks):
            s = s_bufs[gi % 2][g]
            m2 = jnp.max(s, axis=-1, keepdims=True)
            p_bufs[gi % 2][g] = jnp.exp2(s - m2).astype(BF16)
            m_ref[bi, pl.ds(q_start, q_blk, stride=d), :] = jnp.where(head0, m2[:q_blk], m2[q_blk:])

    def values(gi):
        bi, d, blocks = groups[gi]
        for g, (q_start, k_start, n) in enumerate(blocks):
            vb = window(vp_ref, vc_ref, k_start, d).astype(BF16)
            o2 = jnp.dot(p_bufs[gi % 2][g], jnp.concatenate([vb, jnp.ones_like(vb)], axis=1),
                         preferred_element_type=F32)
            rows = pl.ds(q_start, q_blk, stride=d)
            acc_ref[bi, rows, :] = jnp.where(head0, o2[:q_blk, :LANE], o2[q_blk:, :LANE])
            l_ref[bi, rows, :] = jnp.where(head0, o2[:q_blk, LANE:], o2[q_blk:, LANE:])

    for step in range(len(groups) + 2):
        if step < len(groups):
            scores(step)
        if 0 <= step - 1 < len(groups):
            softmax(step - 1)
        if step - 2 >= 0:
            values(step - 2)

    def merge(c, carry):
        rows = pl.ds(pl.multiple_of(c * q_blk, q_blk), q_blk)
        ms = [m_ref[bi, rows, :] for bi in range(len(DIL_DILATIONS))]
        m_all = functools.reduce(jnp.maximum, ms)
        ws = [jnp.exp2(m - m_all) for m in ms]
        num = functools.reduce(lambda a, b: a + b, [w * acc_ref[bi, rows, :] for bi, w in enumerate(ws)])
        den = functools.reduce(lambda a, b: a + b, [w * l_ref[bi, rows, :] for bi, w in enumerate(ws)])
        o_ref[0, rows, :] = (num / den).astype(o_ref.dtype)
        return carry

    lax.fori_loop(0, n_blocks, merge, 0)


def _dilated_bias():
    q_blk = DIL_BLOCK
    qi = np.arange(2 * q_blk)[:, None] % q_blk
    kj = np.arange(2 * q_blk)[None, :]
    band = (kj >= qi) & (kj <= qi + q_blk)
    return jnp.asarray(np.stack([np.where(band, 0.0, NEG), np.where(band & (kj >= q_blk), 0.0, NEG)]), F32)


def _dilated(slabs, b):
    n_slabs, t, _ = slabs.shape
    s = t // b
    h3 = slabs.reshape(n_slabs * b, s, LANE)
    sup = DIL_SUPER
    n_pairs = DIL_HEADS * DIL_HEAD_DIM // LANE
    n_br = len(DIL_DILATIONS)
    cq, ck, cv = 0, n_pairs, 2 * n_pairs
    cur = lambda c: (lambda bi, p, j: ((c + p) * b + bi, j, 0))
    prev = lambda c: (lambda bi, p, j: ((c + p) * b + bi, jnp.maximum(j - 1, 0), 0))
    blk = (1, sup, LANE)
    return pl.pallas_call(
        _dilated_kernel,
        grid=(b, n_pairs, s // sup),
        in_specs=[pl.BlockSpec(blk, cur(cq)),
                  pl.BlockSpec(blk, prev(ck)), pl.BlockSpec(blk, cur(ck)),
                  pl.BlockSpec(blk, prev(cv)), pl.BlockSpec(blk, cur(cv)),
                  pl.BlockSpec((2, 2 * DIL_BLOCK, 2 * DIL_BLOCK), lambda bi, p, j: (0, 0, 0))],
        out_specs=pl.BlockSpec(blk, lambda bi, p, j: (bi, j, p)),
        out_shape=jax.ShapeDtypeStruct((b, s, n_pairs * LANE), BF16),
        scratch_shapes=[pltpu.VMEM((n_br, sup, LANE), F32), pltpu.VMEM((n_br, sup, LANE), F32),
                        pltpu.VMEM((n_br, sup, LANE), F32),
                        pltpu.VMEM((DIL_GROUP, 2 * DIL_BLOCK, 2 * DIL_BLOCK), F32),
                        pltpu.VMEM((DIL_GROUP, 2 * DIL_BLOCK, 2 * DIL_BLOCK), F32),
                        pltpu.VMEM((DIL_GROUP, 2 * DIL_BLOCK, 2 * DIL_BLOCK), BF16),
                        pltpu.VMEM((DIL_GROUP, 2 * DIL_BLOCK, 2 * DIL_BLOCK), BF16)],
        compiler_params=_params("parallel", "parallel", "arbitrary"),
        name="dilated",
    )(h3, h3, h3, h3, h3, _dilated_bias())


def _mem_kv(mem2d, w_bf16, tn=512):
    m, d = mem2d.shape
    n = w_bf16.shape[1]
    return pl.pallas_call(
        _proj_in_kernel,
        grid=(n // tn,),
        in_specs=[pl.BlockSpec((m, d), lambda i: (0, 0)),
                  pl.BlockSpec((d, tn), lambda i: (0, i))],
        out_specs=pl.BlockSpec((m, tn), lambda i: (0, i)),
        out_shape=jax.ShapeDtypeStruct((m, n), BF16),
        compiler_params=_params("parallel"),
        name="mem_kv",
    )(mem2d, w_bf16)


def _xattn_kernel(alpha, x_ref, yr_ref, yd_ref, wout_ref, g1_ref, b1_ref, k_ref, v_ref, wq_ref, wo_ref,
                  g_ref, b_ref, wr_ref, o_ref, ot_ref, lg_ref, x1_ref, q_ref, att_ref):
    tm, d = x_ref.shape
    dh = d // XATTN_HEADS
    wr = yr_ref.shape[1]
    scale = dh ** -0.5 * math.log2(math.e)
    part = tm // XATTN_PARTS

    def mix(rows):
        y = _dot(yr_ref[rows, :], wout_ref[0:wr, :]) + _dot(yd_ref[rows, :], wout_ref[wr:, :])
        x1_ref[rows, :] = _layer_norm(alpha * x_ref[rows, :] + y, g1_ref[...], b1_ref[...])

    def query(rows):
        q_ref[rows, :] = _dot(x1_ref[rows, :], wq_ref[...]).astype(BF16)

    def attend(rows):
        for h in range(XATTN_HEADS):
            sl = slice(h * dh, (h + 1) * dh)
            s = _dot_nt(q_ref[rows, sl], k_ref[0, :, sl]) * scale
            m = jnp.max(s, axis=-1, keepdims=True)
            e = jnp.exp2(s - m)
            p = e / jnp.sum(e, axis=-1, keepdims=True)
            att_ref[rows, sl] = _dot(p, v_ref[0, :, sl]).astype(BF16)

    def finish(rows, r0):
        y = _dot(att_ref[rows, :], wo_ref[...])
        x2 = _layer_norm(alpha * x1_ref[rows, :] + y, g_ref[...], b_ref[...])
        o_ref[rows, :] = x2
        for c in range(d // LANE):
            ot_ref[pl.ds(r0 * SUBLANE + c, part, stride=SUBLANE), :] = x2[:, c * LANE:(c + 1) * LANE]
        lg_ref[:, rows] = _dot_nt(wr_ref[...], x2)

    stages = (mix, query, attend, finish)
    for step in range(XATTN_PARTS + len(stages) - 1):
        for si, stage in enumerate(stages):
            pi = step - si
            if 0 <= pi < XATTN_PARTS:
                rows = slice(pi * part, (pi + 1) * part)
                if stage is finish:
                    stage(rows, pi * part)
                else:
                    stage(rows)


def _xattn(x2d, y_ret, y_dil, w_out, g1, b1, kv, wq, wo, g, bb, w_r, alpha, seq, tm=1024):
    t, d = x2d.shape
    mlen = kv.shape[1]
    tiles_per_seq = seq // tm
    row = lambda i: (i, 0)

    def whole(shape):
        return pl.BlockSpec(shape, lambda i: (0,) * len(shape), pipeline_mode=pl.Buffered(1))

    vec = whole((1, d))
    return pl.pallas_call(
        functools.partial(_xattn_kernel, alpha),
        grid=(t // tm,),
        in_specs=[pl.BlockSpec((tm, d), row),
                  pl.BlockSpec((tm, y_ret.shape[1]), row),
                  pl.BlockSpec((tm, y_dil.shape[1]), row),
                  whole(w_out.shape), vec, vec,
                  pl.BlockSpec((1, mlen, d), lambda i: (i // tiles_per_seq, 0, 0)),
                  pl.BlockSpec((1, mlen, d), lambda i: (i // tiles_per_seq, 0, 1)),
                  whole((d, d)), whole((d, d)), vec, vec,
                  whole((ROUTE_ROWS, d))],
        out_specs=[pl.BlockSpec((tm, d), row),
                   pl.BlockSpec((tm * SUBLANE, LANE), row),
                   pl.BlockSpec((ROUTE_ROWS, tm), lambda i: (0, i))],
        out_shape=[jax.ShapeDtypeStruct((t, d), F32),
                   jax.ShapeDtypeStruct((t * SUBLANE, LANE), F32),
                   jax.ShapeDtypeStruct((ROUTE_ROWS, t), F32)],
        scratch_shapes=[pltpu.VMEM((tm, d), F32), pltpu.VMEM((tm, d), BF16), pltpu.VMEM((tm, d), BF16)],
        compiler_params=_params("parallel"),
        name="xattn",
    )(x2d, y_ret, y_dil, w_out, g1, b1, kv, kv, wq, wo, g, bb, w_r)


def _route_kernel(lg_ref, before_ref, meta_ref, cnt_ref, carry_ref):
    i = pl.program_id(0)

    @pl.when(i == 0)
    def _():
        carry_ref[...] = jnp.zeros_like(carry_ref)

    rows, tt = lg_ref.shape
    sub = before_ref.shape[0]
    r = lax.broadcasted_iota(jnp.int32, (rows, sub), 0)
    r8 = lax.broadcasted_iota(jnp.int32, (SUBLANE, sub), 0)
    is_group = r < N_GROUPS

    def col_max(a):
        return jnp.max(a, axis=0, keepdims=True)

    def first_row_where(mask):
        return jnp.min(jnp.where(mask, r, rows), axis=0, keepdims=True)

    for c in range(tt // sub):
        lg = lg_ref[:, c * sub:(c + 1) * sub]
        mg = col_max(jnp.where(is_group, lg, NEG))
        eg = jnp.where(is_group, jnp.exp(lg - mg), 0.0)
        pg = eg / jnp.sum(eg, axis=0, keepdims=True)
        g1 = col_max(pg)
        gi = first_row_where(is_group & (pg == g1))
        lo = N_GROUPS + gi * EXPERTS_PER_GROUP
        in_grp = (r >= lo) & (r < lo + EXPERTS_PER_GROUP)
        v1 = col_max(jnp.where(in_grp, lg, NEG))
        i1 = first_row_where(in_grp & (lg == v1))
        rest = in_grp & (r != i1)
        v2 = col_max(jnp.where(rest, lg, NEG))
        i2 = first_row_where(rest & (lg == v2))
        e2 = jnp.exp(v2 - v1)
        den = 1.0 + e2
        gate1 = g1 * (1.0 / den)
        gate2 = g1 * (e2 / den)
        sel1 = r == i1
        sel2 = r == i2
        onehot = jnp.where(sel1 | sel2, 1.0, 0.0)
        rank = _dot(onehot, before_ref[...]) + carry_ref[:, 0:1]
        r1 = jnp.sum(jnp.where(sel1, rank, 0.0), axis=0, keepdims=True)
        r2 = jnp.sum(jnp.where(sel2, rank, 0.0), axis=0, keepdims=True)
        carry_ref[...] = carry_ref[...] + jnp.sum(onehot, axis=1, keepdims=True)
        meta = jnp.where(r8 == 0, (i1 - N_GROUPS).astype(F32), 0.0)
        meta = jnp.where(r8 == 1, (i2 - N_GROUPS).astype(F32), meta)
        meta = jnp.where(r8 == 2, r1, meta)
        meta = jnp.where(r8 == 3, r2, meta)
        meta = jnp.where(r8 == 4, gate1, meta)
        meta = jnp.where(r8 == 5, gate2, meta)
        meta_ref[:, c * sub:(c + 1) * sub] = meta
    cnt_ref[...] = carry_ref[...]


def _route(logits_t, tt=1024, sub=256):
    rows, t = logits_t.shape
    before = jnp.asarray(np.arange(sub)[:, None] < np.arange(sub)[None, :], BF16)
    return pl.pallas_call(
        _route_kernel,
        grid=(t // tt,),
        in_specs=[pl.BlockSpec((rows, tt), lambda i: (0, i)),
                  pl.BlockSpec((sub, sub), lambda i: (0, 0))],
        out_specs=[pl.BlockSpec((SUBLANE, tt), lambda i: (0, i)),
                   pl.BlockSpec((rows, LANE), lambda i: (0, 0))],
        out_shape=[jax.ShapeDtypeStruct((SUBLANE, t), F32),
                   jax.ShapeDtypeStruct((rows, LANE), F32)],
        scratch_shapes=[pltpu.VMEM((rows, LANE), F32)],
        compiler_params=_params("arbitrary"),
        name="route",
    )(logits_t, before)


def _plan_kernel(meta_ref, first_row_ref, dest_ref):
    n_tiles = dest_ref.shape[0]
    tt = dest_ref.shape[2] // 2
    rows = first_row_ref.shape[0]
    first_row = first_row_ref[:, 0:1]
    r = lax.broadcasted_iota(jnp.int32, (rows, tt), 0)
    for g in range(n_tiles):
        m = meta_ref[:, g * tt:(g + 1) * tt]

        def dest_of(k):
            e_row = m[k:k + 1, :].astype(jnp.int32) + N_GROUPS
            return jnp.sum(jnp.where(r == e_row, first_row, 0.0), axis=0, keepdims=True) + m[2 + k:3 + k, :]

        dest_ref[g] = jnp.concatenate([dest_of(0), dest_of(1)], axis=1).astype(jnp.int32)


def _plan(meta_t, first_row, tt, tiles_per_step=4):
    t = meta_t.shape[1]
    rows = first_row.shape[0]
    return pl.pallas_call(
        _plan_kernel,
        grid=(t // (tt * tiles_per_step),),
        in_specs=[pl.BlockSpec((SUBLANE, tt * tiles_per_step), lambda i: (0, i)),
                  pl.BlockSpec((rows, LANE), lambda i: (0, 0))],
        out_specs=pl.BlockSpec((tiles_per_step, 1, 2 * tt), lambda i: (i, 0, 0)),
        out_shape=jax.ShapeDtypeStruct((t // tt, 1, 2 * tt), jnp.int32),
        compiler_params=_params("parallel"),
        name="plan",
    )(meta_t, first_row)


def _row_copy(src, dst, s_row, d_row, sem):
    return pltpu.make_async_copy(src.at[pl.ds(pl.multiple_of(s_row * SUBLANE, SUBLANE), SUBLANE), :],
                                 dst.at[pl.ds(pl.multiple_of(d_row * SUBLANE, SUBLANE), SUBLANE), :], sem)


def _dispatch_kernel(dest_ref, xt_ref, xs_ref, sem):
    tq = dest_ref.shape[2] // 2

    def issue(t, c):
        _row_copy(xt_ref, xs_ref, t, dest_ref[0, 0, t], sem).start(priority=0)
        _row_copy(xt_ref, xs_ref, t, dest_ref[0, 0, tq + t], sem).start(priority=1)
        return c

    lax.fori_loop(0, tq, issue, 0, unroll=8)
    for _ in range(2):
        pltpu.make_async_copy(xt_ref, xs_ref.at[pl.ds(0, tq * SUBLANE), :], sem).wait()


def _dispatch(dest3, x_tiled, n_rows, tiles_per_step=4):
    n_tiles, _, two_tt = dest3.shape
    tt = two_tt // 2
    tq = tiles_per_step * tt
    n_steps = n_tiles // tiles_per_step
    dest3 = dest3.reshape(n_steps, tiles_per_step, 2, tt).transpose(0, 2, 1, 3).reshape(n_steps, 1, 2 * tq)
    return pl.pallas_call(
        _dispatch_kernel,
        grid=(n_steps,),
        in_specs=[pl.BlockSpec((1, 1, 2 * tq), lambda i: (i, 0, 0), memory_space=pltpu.SMEM),
                  pl.BlockSpec((tq * SUBLANE, LANE), lambda i: (i, 0))],
        out_specs=pl.BlockSpec(memory_space=pl.ANY),
        out_shape=jax.ShapeDtypeStruct((n_rows * SUBLANE, LANE), F32),
        scratch_shapes=[pltpu.SemaphoreType.DMA(())],
        compiler_params=_params("arbitrary"),
        name="dispatch",
    )(dest3, x_tiled)


def _experts_kernel(blk_exp_ref, blk_rows_ref, n_used_ref, blk_first_ref, blk_slot_ref, blk_next_ref,
                    xs_ref, wg_hbm, wu_hbm, wd_hbm, ys_ref, x_scr, y_scr, wg_buf, wu_buf, wd_buf, sems):
    b = pl.program_id(0)

    def weight_copies(e, slot):
        return [pltpu.make_async_copy(hbm.at[e], buf.at[slot], sems.at[k, slot])
                for k, (hbm, buf) in enumerate(((wg_hbm, wg_buf), (wu_hbm, wu_buf), (wd_hbm, wd_buf)))]

    @pl.when(b < n_used_ref[0])
    def _():
        slot = blk_slot_ref[b]

        @pl.when(b == 0)
        def _():
            for cp in weight_copies(blk_exp_ref[0], slot):
                cp.start()

        @pl.when(blk_first_ref[b] == 1)
        def _():
            @pl.when(blk_next_ref[b] >= 0)
            def _():
                for cp in weight_copies(blk_next_ref[b], 1 - slot):
                    cp.start()

            for cp in weight_copies(blk_exp_ref[b], slot):
                cp.wait()

        rows, d = x_scr.shape
        n_valid = blk_rows_ref[b]

        def run(m):
            ri = lax.broadcasted_iota(jnp.int32, (m, 1), 0)
            for c in range(d // LANE):
                x = xs_ref[pl.ds(c, m, stride=SUBLANE), :]
                x_scr[0:m, c * LANE:(c + 1) * LANE] = jnp.where(ri < n_valid, x, 0.0).astype(BF16)
            x = x_scr[0:m, :]
            hg = _dot(x, wg_buf[slot])
            hu = _dot(x, wu_buf[slot])
            y_scr[0:m, :] = _dot(hg * jax.nn.sigmoid(hg) * hu, wd_buf[slot])
            for c in range(d // LANE):
                ys_ref[pl.ds(c, m, stride=SUBLANE), :] = y_scr[0:m, c * LANE:(c + 1) * LANE]

        n_quanta = rows // EXPERT_QUANTUM
        for k in range(1, n_quanta + 1):
            lo = (k - 1) * EXPERT_QUANTUM
            pl.when((n_valid > lo) & (n_valid <= k * EXPERT_QUANTUM))(functools.partial(run, k * EXPERT_QUANTUM))


def _experts(blk_exp, blk_rows, n_used, blk_first, blk_slot, blk_next, xs_tiled, w_g, w_u, w_d):
    n_rows = xs_tiled.shape[0] // SUBLANE
    rows = EXPERT_ROWS
    nblk = n_rows // rows
    _, d, ff = w_g.shape
    used = lambda b, be, br, nu, *_: (jnp.minimum(b, nu[0] - 1), 0)
    grid_spec = pltpu.PrefetchScalarGridSpec(
        num_scalar_prefetch=6,
        grid=(nblk,),
        in_specs=[pl.BlockSpec((rows * SUBLANE, LANE), used),
                  pl.BlockSpec(memory_space=pl.ANY), pl.BlockSpec(memory_space=pl.ANY),
                  pl.BlockSpec(memory_space=pl.ANY)],
        out_specs=pl.BlockSpec((rows * SUBLANE, LANE), used),
        scratch_shapes=[pltpu.VMEM((rows, d), BF16), pltpu.VMEM((rows, d), F32),
                        pltpu.VMEM((2, d, ff), F32), pltpu.VMEM((2, d, ff), F32), pltpu.VMEM((2, ff, d), F32),
                        pltpu.SemaphoreType.DMA((3, 2))],
    )
    return pl.pallas_call(
        _experts_kernel,
        grid_spec=grid_spec,
        out_shape=jax.ShapeDtypeStruct((n_rows * SUBLANE, LANE), F32),
        compiler_params=_params("arbitrary"),
        name="experts",
    )(blk_exp, blk_rows, n_used, blk_first, blk_slot, blk_next, xs_tiled, w_g, w_u, w_d)


def _combine_kernel(alpha, dcur_ref, dnext_ref, meta_ref, x_ref, g_ref, b_ref, ys_ref, o_ref, buf_ref, sems):
    i = pl.program_id(0)
    tq, d = x_ref.shape
    slot_rows = 2 * tq * SUBLANE

    def issue(d_ref, slot, t):
        _row_copy(ys_ref, buf_ref, d_ref[0, 0, t], slot * (2 * tq) + t, sems.at[slot]).start(priority=0)
        _row_copy(ys_ref, buf_ref, d_ref[0, 0, tq + t], slot * (2 * tq) + tq + t, sems.at[slot]).start(priority=1)

    def wait_slot(slot):
        off = pl.multiple_of(slot * slot_rows, slot_rows)
        pltpu.make_async_copy(ys_ref.at[pl.ds(0, slot_rows), :], buf_ref.at[pl.ds(off, slot_rows), :],
                              sems.at[slot]).wait()
        return off

    def issue_all(d_ref, slot):
        lax.fori_loop(0, tq, lambda t, c: (issue(d_ref, slot, t), c)[1], 0, unroll=4)

    slot = i % 2

    @pl.when(i == 0)
    def _():
        issue_all(dcur_ref, 0)

    @pl.when(i + 1 < pl.num_programs(0))
    def _():
        issue_all(dnext_ref, 1 - slot)

    off = wait_slot(slot)
    meta_rows = jnp.transpose(jnp.concatenate([meta_ref[...], jnp.zeros((LANE - SUBLANE, tq), F32)], axis=0))
    gate1 = meta_rows[:, 4:5]
    gate2 = meta_rows[:, 5:6]
    for c in range(d // LANE):
        sl = slice(c * LANE, (c + 1) * LANE)
        y1 = buf_ref[pl.ds(off + c, tq, stride=SUBLANE), :]
        y2 = buf_ref[pl.ds(off + tq * SUBLANE + c, tq, stride=SUBLANE), :]
        o_ref[:, sl] = alpha * x_ref[:, sl] + (y1 * gate1 + y2 * gate2)
    o_ref[...] = _layer_norm(o_ref[...], g_ref[...], b_ref[...])


def _combine(dest3, meta, x2, g, bb, ys_tiled, alpha):
    t, d = x2.shape
    n = dest3.shape[0]
    tq = t // n
    row = lambda i: (i, 0)
    const = lambda i: (0, 0)
    return pl.pallas_call(
        functools.partial(_combine_kernel, alpha),
        grid=(n,),
        in_specs=[pl.BlockSpec((1, 1, 2 * tq), lambda i: (i, 0, 0), memory_space=pltpu.SMEM),
                  pl.BlockSpec((1, 1, 2 * tq), lambda i: (jnp.minimum(i + 1, n - 1), 0, 0), memory_space=pltpu.SMEM),
                  pl.BlockSpec((SUBLANE, tq), lambda i: (0, i)),
                  pl.BlockSpec((tq, d), row),
                  pl.BlockSpec((1, d), const), pl.BlockSpec((1, d), const),
                  pl.BlockSpec(memory_space=pl.ANY)],
        out_specs=pl.BlockSpec((tq, d), row),
        out_shape=jax.ShapeDtypeStruct((t, d), F32),
        scratch_shapes=[pltpu.VMEM((2 * 2 * tq * SUBLANE, LANE), F32), pltpu.SemaphoreType.DMA((2,))],
        compiler_params=_params("arbitrary"),
        name="combine",
    )(dest3, dest3, meta, x2, g, bb, ys_tiled)


def _moe(x2, x2_tiled, logits, w_g, w_u, w_d, g, bb, alpha):
    t, d = x2.shape
    meta, cnt = _route(logits)
    counts = cnt[N_GROUPS:N_GROUPS + N_EXPERTS, 0].astype(jnp.int32)
    padded = ((counts + EXPERT_ROWS - 1) // EXPERT_ROWS) * EXPERT_ROWS
    pend = jnp.cumsum(padded)
    poff = pend - padded
    first_row = jnp.pad(poff.astype(F32), (N_GROUPS, ROUTE_ROWS - N_GROUPS - N_EXPERTS))
    dest3 = _plan(meta, jnp.broadcast_to(first_row[:, None], (ROUTE_ROWS, LANE)), ROUTE_TILE)
    n_rows = t * 2 + N_EXPERTS * EXPERT_ROWS
    nblk = n_rows // EXPERT_ROWS
    blk_start = jnp.arange(nblk, dtype=jnp.int32) * EXPERT_ROWS
    blk_exp = jnp.sum((pend[None, :] <= blk_start[:, None]).astype(jnp.int32), axis=1)
    blk_exp = jnp.minimum(blk_exp, N_EXPERTS - 1)
    e_idx = jnp.arange(N_EXPERTS, dtype=jnp.int32)
    owner = blk_exp[:, None] == e_idx[None, :]

    def of_block(per_expert):
        return jnp.sum(jnp.where(owner, per_expert[None, :], 0), axis=1).astype(jnp.int32)

    blk_rows = jnp.clip(of_block(poff + counts) - blk_start, 0, EXPERT_ROWS).astype(jnp.int32)
    n_used = (pend[-1:] // EXPERT_ROWS).astype(jnp.int32)
    has_rows = counts > 0
    ordinal = jnp.cumsum(has_rows.astype(jnp.int32)) - 1
    later = has_rows[None, :] & (e_idx[None, :] > e_idx[:, None])
    next_used = jnp.min(jnp.where(later, e_idx[None, :], N_EXPERTS), axis=1)
    next_used = jnp.where(next_used == N_EXPERTS, -1, next_used).astype(jnp.int32)
    blk_first = (blk_start == of_block(poff)).astype(jnp.int32)
    blk_slot = of_block(ordinal % 2)
    blk_next = of_block(next_used)
    xs_tiled = _dispatch(dest3, x2_tiled, n_rows)
    ys_tiled = _experts(blk_exp, blk_rows, n_used, blk_first, blk_slot, blk_next, xs_tiled, w_g, w_u, w_d)
    return _combine(dest3, meta, x2, g, bb, ys_tiled, alpha)


def kernel(x, mem, w_in, w_out, ln_mix_g, ln_mix_b, w_xq, w_xkv, w_xo, ln_x_g, ln_x_b, w_route_group,
           w_route_expert, w_exp_gate, w_exp_up, w_exp_down, ln_moe_g, ln_moe_b):
    b, s, d = x.shape
    depth = w_in.shape[0]
    alpha = (2.0 * depth) ** 0.25
    t = b * s
    ret_cols = (2 * RET_HEADS * RET_QK_DIM + 2 * RET_HEADS * RET_V_DIM)
    xc = x.reshape(t, d)
    for l in range(depth):
        n_qk = 2 * RET_HEADS * RET_QK_DIM
        h_ret, h_dil = _proj_in(xc, w_in[l], _retention_weight_layout(w_in[l][:, :n_qk]).astype(BF16), ret_cols)
        y_ret = _retention(h_ret.reshape(b, s, ret_cols)).reshape(t, -1)
        y_dil = _dilated(h_dil, b).reshape(t, -1)
        kv = _mem_kv(mem.reshape(b * mem.shape[1], d), w_xkv[l]).reshape(b, mem.shape[1], 2 * d)
        w_r = jnp.concatenate([w_route_group[l], w_route_expert[l]], axis=-1)
        w_r = jnp.pad(w_r.T, ((0, ROUTE_ROWS - w_r.shape[1]), (0, 0))).astype(BF16)
        x2, x2_tiled, logits = _xattn(xc, y_ret, y_dil, w_out[l].astype(BF16), ln_mix_g[l][None], ln_mix_b[l][None],
                                      kv, w_xq[l].astype(BF16), w_xo[l].astype(BF16),
                                      ln_x_g[l][None], ln_x_b[l][None], w_r, alpha, s)
        xc = _moe(x2, x2_tiled, logits, w_exp_gate[l], w_exp_up[l], w_exp_down[l],
                  ln_moe_g[l][None], ln_moe_b[l][None], alpha)
    return xc.reshape(b, s, d)
```

```python
import functools
import math

import jax
import jax.numpy as jnp
import numpy as np
from jax import lax
from jax.experimental import pallas as pl
from jax.experimental.pallas import tpu as pltpu

BF16 = jnp.bfloat16
F32 = jnp.float32

LANE = 128
SUBLANE = 8
VMEM_LIMIT = 56 * 1024 * 1024

RET_HEADS = 4
RET_QK_DIM = 64
RET_V_DIM = 128
RET_CHUNK = 128
RET_STEP_CHUNKS = 4
ROPE_BASE = 10000.0
DIL_HEADS = 8
DIL_HEAD_DIM = 64
DIL_DILATIONS = (1, 4, 16)
DIL_BLOCK = 128
DIL_SUPER = DIL_BLOCK * max(DIL_DILATIONS)
DIL_GROUP = 1
XATTN_HEADS = 4
XATTN_PARTS = 2
N_GROUPS = 4
EXPERTS_PER_GROUP = 8
N_EXPERTS = N_GROUPS * EXPERTS_PER_GROUP
ROUTE_ROWS = -(-(N_GROUPS + N_EXPERTS) // SUBLANE) * SUBLANE
EXPERT_ROWS = 512
EXPERT_QUANTUM = 256
ROUTE_TILE = 256
LN_EPS = 1e-5
GN_EPS = 1e-6
NEG = -1e30


def _params(*sem):
    return pltpu.CompilerParams(dimension_semantics=sem, vmem_limit_bytes=VMEM_LIMIT)


def _layer_norm(z, g, b):
    mu = jnp.mean(z, axis=-1, keepdims=True)
    zc = z - mu
    var = jnp.mean(zc * zc, axis=-1, keepdims=True)
    return zc * lax.rsqrt(var + LN_EPS) * g + b


def _dot(a, b):
    return jnp.dot(a.astype(BF16), b.astype(BF16), preferred_element_type=F32)


def _dot_nt(a, b):
    return lax.dot_general(a.astype(BF16), b.astype(BF16), (((1,), (1,)), ((), ())),
                           preferred_element_type=F32)


def _dot_tn(a, b):
    return lax.dot_general(a.astype(BF16), b.astype(BF16), (((0,), (0,)), ((), ())),
                           preferred_element_type=F32)


def _proj_in_kernel(x_ref, w_ref, o_ref):
    o_ref[...] = _dot(x_ref[...], w_ref[...]).astype(o_ref.dtype)


def _proj_in2_kernel(x_ref, w_ref, w_head_ref, o_ref, slab_ref, w_bf16):
    @pl.when(pl.program_id(0) == 0)
    def _():
        w_bf16[...] = w_ref[...].astype(BF16)
        w_bf16[:, 0:w_head_ref.shape[1]] = w_head_ref[...]

    n_row = o_ref.shape[1]
    x = x_ref[...].astype(BF16)
    o_ref[...] = jnp.dot(x, w_bf16[:, 0:n_row], preferred_element_type=F32)
    rest = jnp.dot(x, w_bf16[:, n_row:], preferred_element_type=F32)
    for c in range(slab_ref.shape[0]):
        slab_ref[c] = rest[:, c * LANE:(c + 1) * LANE]


def _proj_in(x2d, w_f32, w_head_bf16, n_row, tm=512):
    t, d = x2d.shape
    n = w_f32.shape[1]
    n_slabs = (n - n_row) // LANE
    whole = lambda shape: pl.BlockSpec(shape, lambda i: (0, 0), pipeline_mode=pl.Buffered(1))
    return pl.pallas_call(
        _proj_in2_kernel,
        grid=(t // tm,),
        in_specs=[pl.BlockSpec((tm, d), lambda i: (i, 0)), whole((d, n)), whole(w_head_bf16.shape)],
        out_specs=[pl.BlockSpec((tm, n_row), lambda i: (i, 0)),
                   pl.BlockSpec((n_slabs, tm, LANE), lambda i: (0, i, 0))],
        out_shape=[jax.ShapeDtypeStruct((t, n_row), F32),
                   jax.ShapeDtypeStruct((n_slabs, t, LANE), F32)],
        scratch_shapes=[pltpu.VMEM((d, n), BF16)],
        compiler_params=_params("arbitrary"),
        name="proj_in",
    )(x2d, w_f32, w_head_bf16)


def _retention_kernel(qk_ref, v_ref, g_ref, cos_ref, sin_ref, decay_ref, zeta_ref, xi_ref, gam_ref,
                      o_ref, state_ref, y_ref):
    n = pl.program_id(1)

    @pl.when(n == 0)
    def _():
        state_ref[...] = jnp.zeros_like(state_ref)

    c = RET_CHUNK
    n_sub = qk_ref.shape[1] // c
    half = RET_QK_DIM // 2
    lane = lax.broadcasted_iota(jnp.int32, (c, LANE), 1)

    rotated = []
    for j in range(n_sub):
        rows = slice(j * c, (j + 1) * c)
        cos = cos_ref[rows, :]
        sin = sin_ref[rows, :]

        def rot(col, rows=rows, cos=cos, sin=sin):
            t1 = qk_ref[0, rows, col * LANE:(col + 1) * LANE]
            t2 = qk_ref[0, rows, (col + 1) * LANE:(col + 2) * LANE]
            return t1 * cos - t2 * sin, t1 * sin + t2 * cos

        q1, q2 = rot(0)
        k1, k2 = (t * (RET_QK_DIM ** -0.5) for t in rot(2))
        rotated.append((q1, q2, k1, k2, jnp.concatenate([k1, k2], axis=1)))

    for h in range(RET_HEADS):
        cols = slice(h * RET_V_DIM, (h + 1) * RET_V_DIM)
        mine = (lane >= h * half) & (lane < (h + 1) * half)
        zeta = zeta_ref[h]
        st = state_ref[h]
        for j in range(n_sub):
            rows = slice(j * c, (j + 1) * c)
            q1, q2, k1, k2, k_all = rotated[j]
            qm = jnp.concatenate([jnp.where(mine, q1, 0.0), jnp.where(mine, q2, 0.0)], axis=1)
            kz = jnp.concatenate([jnp.where(mine, k1, 0.0) * zeta, jnp.where(mine, k2, 0.0) * zeta], axis=1)
            v = v_ref[0, rows, cols]
            s = _dot_nt(qm, k_all) * decay_ref[h]
            y_ref[rows, cols] = _dot(s, v) + _dot(qm, st) * xi_ref[h]
            st = gam_ref[h] * st + _dot_tn(kz, v)
        state_ref[h] = st

    for h in range(RET_HEADS):
        cols = slice(h * RET_V_DIM, (h + 1) * RET_V_DIM)
        y = y_ref[:, cols]
        mu = jnp.mean(y, axis=-1, keepdims=True)
        yc = y - mu
        var = jnp.mean(yc * yc, axis=-1, keepdims=True)
        yn = yc * lax.rsqrt(var + GN_EPS)
        gate = g_ref[0, :, cols]
        o_ref[0, :, cols] = (gate * jax.nn.sigmoid(gate) * yn).astype(o_ref.dtype)


def _retention_tables(s):
    half = RET_QK_DIM // 2
    inv = 1.0 / (ROPE_BASE ** (np.arange(half, dtype=np.float64) / half))
    ang = np.arange(s, dtype=np.float64)[:, None] * inv[None, :]
    cos_t = np.tile(np.cos(ang), (1, RET_HEADS))
    sin_t = np.tile(np.sin(ang), (1, RET_HEADS))
    c = RET_CHUNK
    lg = np.log(1.0 - np.exp2(-5.0 - np.arange(RET_HEADS, dtype=np.float64)))
    idx = np.arange(c, dtype=np.float64)
    diff = idx[:, None] - idx[None, :]
    decay = np.where(diff >= 0, np.exp(lg[:, None, None] * np.maximum(diff, 0.0)), 0.0)
    lanes = (RET_HEADS, c, LANE)
    zeta = np.broadcast_to(np.exp(lg[:, None] * (c - 1.0 - idx))[:, :, None], lanes)
    xi = np.broadcast_to(np.exp(lg[:, None] * (idx + 1.0))[:, :, None], lanes)
    gam = np.broadcast_to(np.exp(lg * c)[:, None, None], (RET_HEADS, 1, LANE))
    return tuple(jnp.asarray(np.ascontiguousarray(a), F32) for a in (cos_t, sin_t, decay, zeta, xi, gam))


def _retention_weight_layout(w_qk):
    d = w_qk.shape[0]
    half = RET_QK_DIM // 2
    return w_qk.reshape(d, 2, RET_HEADS, 2, half).transpose(0, 1, 3, 2, 4).reshape(d, -1)


def _retention(h3):
    b, s, _ = h3.shape
    c = RET_CHUNK
    qk_w = 2 * RET_HEADS * RET_QK_DIM
    v_w = RET_HEADS * RET_V_DIM
    assert qk_w == v_w
    cos_t, sin_t, decay, zeta, xi, gam = _retention_tables(s)
    const3 = lambda bi, n: (0, 0, 0)
    rows = RET_STEP_CHUNKS * c
    return pl.pallas_call(
        _retention_kernel,
        grid=(b, s // rows),
        in_specs=[pl.BlockSpec((1, rows, qk_w), lambda bi, n: (bi, n, 0)),
                  pl.BlockSpec((1, rows, v_w), lambda bi, n: (bi, n, 1)),
                  pl.BlockSpec((1, rows, v_w), lambda bi, n: (bi, n, 2)),
                  pl.BlockSpec((rows, LANE), lambda bi, n: (n, 0)),
                  pl.BlockSpec((rows, LANE), lambda bi, n: (n, 0)),
                  pl.BlockSpec((RET_HEADS, c, c), const3),
                  pl.BlockSpec((RET_HEADS, c, LANE), const3),
                  pl.BlockSpec((RET_HEADS, c, LANE), const3),
                  pl.BlockSpec((RET_HEADS, 1, LANE), const3)],
        out_specs=pl.BlockSpec((1, rows, v_w), lambda bi, n: (bi, n, 0)),
        out_shape=jax.ShapeDtypeStruct((b, s, v_w), BF16),
        scratch_shapes=[pltpu.VMEM((RET_HEADS, RET_HEADS * RET_QK_DIM, RET_V_DIM), F32),
                        pltpu.VMEM((rows, v_w), F32)],
        compiler_params=_params("parallel", "arbitrary"),
        name="retention",
    )(h3, h3, h3, cos_t, sin_t, decay, zeta, xi, gam)


def _dilated_kernel(q_ref, kp_ref, kc_ref, vp_ref, vc_ref, bias_ref, o_ref, acc_ref, m_ref, l_ref,
                    s0_ref, s1_ref, p0_ref, p1_ref):
    j = pl.program_id(2)
    sup = DIL_SUPER
    q_blk = DIL_BLOCK
    scale = DIL_HEAD_DIM ** -0.5 * math.log2(math.e)
    n_blocks = sup // q_blk
    lane = lax.broadcasted_iota(jnp.int32, (q_blk, LANE), 1)
    head0 = lane < DIL_HEAD_DIM
    first_bias = jnp.where(j == 0, 1, 0)

    groups = []
    for bi, d in enumerate(DIL_DILATIONS):
        n_per_r = sup // (q_blk * d)
        for t0 in range(0, n_blocks, DIL_GROUP):
            blocks = []
            for t in range(t0, t0 + DIL_GROUP):
                r, n = divmod(t, n_per_r)
                blocks.append((n * (q_blk * d) + r, (n - 1) * (q_blk * d) + r, n))
            groups.append((bi, d, blocks))
    s_bufs = (s0_ref, s1_ref)
    p_bufs = (p0_ref, p1_ref)

    def window(prev_ref, cur_ref, k_start, d):
        if k_start >= 0:
            return cur_ref[0, pl.ds(k_start, 2 * q_blk, stride=d), :]
        return jnp.concatenate([prev_ref[0, pl.ds(sup + k_start, q_blk, stride=d), :],
                                cur_ref[0, pl.ds(k_start + q_blk * d, q_blk, stride=d), :]], axis=0)

    def scores(gi):
        bi, d, blocks = groups[gi]
        for g, (q_start, k_start, n) in enumerate(blocks):
            q = q_ref[0, pl.ds(q_start, q_blk, stride=d), :] * scale
            kb = window(kp_ref, kc_ref, k_start, d)
            q2 = jnp.concatenate([jnp.where(head0, q, 0.0), jnp.where(head0, 0.0, q)], axis=0)
            bias = bias_ref[first_bias] if n == 0 else bias_ref[0]
            s_bufs[gi % 2][g] = _dot_nt(q2, kb) + bias

    def softmax(gi):
        bi, d, blocks = groups[gi]
        for g, (q_start, k_start, n) in enumerate(blocks):
            s = s_bufs[gi % 2][g]
            m2 = jnp.max(s, axis=-1, keepdims=True)
            p_bufs[gi % 2][g] = jnp.exp2(s - m2).astype(BF16)
            m_ref[bi, pl.ds(q_start, q_blk, stride=d), :] = jnp.where(head0, m2[:q_blk], m2[q_blk:])

    def values(gi):
        bi, d, blocks = groups[gi]
        for g, (q_start, k_start, n) in enumerate(blocks):
            vb = window(vp_ref, vc_ref, k_start, d).astype(BF16)
            o2 = jnp.dot(p_bufs[gi % 2][g], jnp.concatenate([vb, jnp.ones_like(vb)], axis=1),
                         preferred_element_type=F32)
            rows = pl.ds(q_start, q_blk, stride=d)
            acc_ref[bi, rows, :] = jnp.where(head0, o2[:q_blk, :LANE], o2[q_blk:, :LANE])
            l_ref[bi, rows, :] = jnp.where(head0, o2[:q_blk, LANE:], o2[q_blk:, LANE:])

    for step in range(len(groups) + 2):
        if step < len(groups):
            scores(step)
        if 0 <= step - 1 < len(groups):
            softmax(step - 1)
        if step - 2 >= 0:
            values(step - 2)

    def merge(c, carry):
        rows = pl.ds(pl.multiple_of(c * q_blk, q_blk), q_blk)
        ms = [m_ref[bi, rows, :] for bi in range(len(DIL_DILATIONS))]
        m_all = functools.reduce(jnp.maximum, ms)
        ws = [jnp.exp2(m - m_all) for m in ms]
        num = functools.reduce(lambda a, b: a + b, [w * acc_ref[bi, rows, :] for bi, w in enumerate(ws)])
        den = functools.reduce(lambda a, b: a + b, [w * l_ref[bi, rows, :] for bi, w in enumerate(ws)])
        o_ref[0, rows, :] = (num / den).astype(o_ref.dtype)
        return carry

    lax.fori_loop(0, n_blocks, merge, 0)


def _dilated_bias():
    q_blk = DIL_BLOCK
    qi = np.arange(2 * q_blk)[:, None] % q_blk
    kj = np.arange(2 * q_blk)[None, :]
    band = (kj >= qi) & (kj <= qi + q_blk)
    return jnp.asarray(np.stack([np.where(band, 0.0, NEG), np.where(band & (kj >= q_blk), 0.0, NEG)]), F32)


def _dilated(slabs, b):
    n_slabs, t, _ = slabs.shape
    s = t // b
    h3 = slabs.reshape(n_slabs * b, s, LANE)
    sup = DIL_SUPER
    n_pairs = DIL_HEADS * DIL_HEAD_DIM // LANE
    n_br = len(DIL_DILATIONS)
    cq, ck, cv = 0, n_pairs, 2 * n_pairs
    cur = lambda c: (lambda bi, p, j: ((c + p) * b + bi, j, 0))
    prev = lambda c: (lambda bi, p, j: ((c + p) * b + bi, jnp.maximum(j - 1, 0), 0))
    blk = (1, sup, LANE)
    return pl.pallas_call(
        _dilated_kernel,
        grid=(b, n_pairs, s // sup),
        in_specs=[pl.BlockSpec(blk, cur(cq)),
                  pl.BlockSpec(blk, prev(ck)), pl.BlockSpec(blk, cur(ck)),
                  pl.BlockSpec(blk, prev(cv)), pl.BlockSpec(blk, cur(cv)),
                  pl.BlockSpec((2, 2 * DIL_BLOCK, 2 * DIL_BLOCK), lambda bi, p, j: (0, 0, 0))],
        out_specs=pl.BlockSpec(blk, lambda bi, p, j: (bi, j, p)),
        out_shape=jax.ShapeDtypeStruct((b, s, n_pairs * LANE), BF16),
        scratch_shapes=[pltpu.VMEM((n_br, sup, LANE), F32), pltpu.VMEM((n_br, sup, LANE), F32),
                        pltpu.VMEM((n_br, sup, LANE), F32),
                        pltpu.VMEM((DIL_GROUP, 2 * DIL_BLOCK, 2 * DIL_BLOCK), F32),
                        pltpu.VMEM((DIL_GROUP, 2 * DIL_BLOCK, 2 * DIL_BLOCK), F32),
                        pltpu.VMEM((DIL_GROUP, 2 * DIL_BLOCK, 2 * DIL_BLOCK), BF16),
                        pltpu.VMEM((DIL_GROUP, 2 * DIL_BLOCK, 2 * DIL_BLOCK), BF16)],
        compiler_params=_params("parallel", "parallel", "arbitrary"),
        name="dilated",
    )(h3, h3, h3, h3, h3, _dilated_bias())


def _mem_kv(mem2d, w_bf16, tn=512):
    m, d = mem2d.shape
    n = w_bf16.shape[1]
    return pl.pallas_call(
        _proj_in_kernel,
        grid=(n // tn,),
        in_specs=[pl.BlockSpec((m, d), lambda i: (0, 0)),
                  pl.BlockSpec((d, tn), lambda i: (0, i))],
        out_specs=pl.BlockSpec((m, tn), lambda i: (0, i)),
        out_shape=jax.ShapeDtypeStruct((m, n), BF16),
        compiler_params=_params("parallel"),
        name="mem_kv",
    )(mem2d, w_bf16)


def _xattn_kernel(alpha, x_ref, yr_ref, yd_ref, wout_ref, g1_ref, b1_ref, k_ref, v_ref, wq_ref, wo_ref,
                  g_ref, b_ref, wr_ref, o_ref, ot_ref, lg_ref, x1_ref, q_ref, att_ref):
    tm, d = x_ref.shape
    dh = d // XATTN_HEADS
    wr = yr_ref.shape[1]
    scale = dh ** -0.5 * math.log2(math.e)
    part = tm // XATTN_PARTS

    def mix(rows):
        y = _dot(yr_ref[rows, :], wout_ref[0:wr, :]) + _dot(yd_ref[rows, :], wout_ref[wr:, :])
        x1_ref[rows, :] = _layer_norm(alpha * x_ref[rows, :] + y, g1_ref[...], b1_ref[...])

    def query(rows):
        q_ref[rows, :] = _dot(x1_ref[rows, :], wq_ref[...]).astype(BF16)

    def attend(rows):
        for h in range(XATTN_HEADS):
            sl = slice(h * dh, (h + 1) * dh)
            s = _dot_nt(q_ref[rows, sl], k_ref[0, :, sl]) * scale
            m = jnp.max(s, axis=-1, keepdims=True)
            e = jnp.exp2(s - m)
            p = e / jnp.sum(e, axis=-1, keepdims=True)
            att_ref[rows, sl] = _dot(p, v_ref[0, :, sl]).astype(BF16)

    def finish(rows, r0):
        y = _dot(att_ref[rows, :], wo_ref[...])
        x2 = _layer_norm(alpha * x1_ref[rows, :] + y, g_ref[...], b_ref[...])
        o_ref[rows, :] = x2
        for c in range(d // LANE):
            ot_ref[pl.ds(r0 * SUBLANE + c, part, stride=SUBLANE), :] = x2[:, c * LANE:(c + 1) * LANE]
        lg_ref[:, rows] = _dot_nt(wr_ref[...], x2)

    stages = (mix, query, attend, finish)
    for step in range(XATTN_PARTS + len(stages) - 1):
        for si, stage in enumerate(stages):
            pi = step - si
            if 0 <= pi < XATTN_PARTS:
                rows = slice(pi * part, (pi + 1) * part)
                if stage is finish:
                    stage(rows, pi * part)
                else:
                    stage(rows)


def _xattn(x2d, y_ret, y_dil, w_out, g1, b1, kv, wq, wo, g, bb, w_r, alpha, seq, tm=1024):
    t, d = x2d.shape
    mlen = kv.shape[1]
    tiles_per_seq = seq // tm
    row = lambda i: (i, 0)

    def whole(shape):
        return pl.BlockSpec(shape, lambda i: (0,) * len(shape), pipeline_mode=pl.Buffered(1))

    vec = whole((1, d))
    return pl.pallas_call(
        functools.partial(_xattn_kernel, alpha),
        grid=(t // tm,),
        in_specs=[pl.BlockSpec((tm, d), row),
                  pl.BlockSpec((tm, y_ret.shape[1]), row),
                  pl.BlockSpec((tm, y_dil.shape[1]), row),
                  whole(w_out.shape), vec, vec,
                  pl.BlockSpec((1, mlen, d), lambda i: (i // tiles_per_seq, 0, 0)),
                  pl.BlockSpec((1, mlen, d), lambda i: (i // tiles_per_seq, 0, 1)),
                  whole((d, d)), whole((d, d)), vec, vec,
                  whole((ROUTE_ROWS, d))],
        out_specs=[pl.BlockSpec((tm, d), row),
                   pl.BlockSpec((tm * SUBLANE, LANE), row),
                   pl.BlockSpec((ROUTE_ROWS, tm), lambda i: (0, i))],
        out_shape=[jax.ShapeDtypeStruct((t, d), F32),
                   jax.ShapeDtypeStruct((t * SUBLANE, LANE), F32),
                   jax.ShapeDtypeStruct((ROUTE_ROWS, t), F32)],
        scratch_shapes=[pltpu.VMEM((tm, d), F32), pltpu.VMEM((tm, d), BF16), pltpu.VMEM((tm, d), BF16)],
        compiler_params=_params("parallel"),
        name="xattn",
    )(x2d, y_ret, y_dil, w_out, g1, b1, kv, kv, wq, wo, g, bb, w_r)


def _route_kernel(lg_ref, before_ref, meta_ref, cnt_ref, carry_ref):
    i = pl.program_id(0)

    @pl.when(i == 0)
    def _():
        carry_ref[...] = jnp.zeros_like(carry_ref)

    rows, tt = lg_ref.shape
    sub = before_ref.shape[0]
    r = lax.broadcasted_iota(jnp.int32, (rows, sub), 0)
    r8 = lax.broadcasted_iota(jnp.int32, (SUBLANE, sub), 0)
    is_group = r < N_GROUPS

    def col_max(a):
        return jnp.max(a, axis=0, keepdims=True)

    def first_row_where(mask):
        return jnp.min(jnp.where(mask, r, rows), axis=0, keepdims=True)

    for c in range(tt // sub):
        lg = lg_ref[:, c * sub:(c + 1) * sub]
        mg = col_max(jnp.where(is_group, lg, NEG))
        eg = jnp.where(is_group, jnp.exp(lg - mg), 0.0)
        pg = eg / jnp.sum(eg, axis=0, keepdims=True)
        g1 = col_max(pg)
        gi = first_row_where(is_group & (pg == g1))
        lo = N_GROUPS + gi * EXPERTS_PER_GROUP
        in_grp = (r >= lo) & (r < lo + EXPERTS_PER_GROUP)
        v1 = col_max(jnp.where(in_grp, lg, NEG))
        i1 = first_row_where(in_grp & (lg == v1))
        rest = in_grp & (r != i1)
        v2 = col_max(jnp.where(rest, lg, NEG))
        i2 = first_row_where(rest & (lg == v2))
        e2 = jnp.exp(v2 - v1)
        den = 1.0 + e2
        gate1 = g1 * (1.0 / den)
        gate2 = g1 * (e2 / den)
        sel1 = r == i1
        sel2 = r == i2
        onehot = jnp.where(sel1 | sel2, 1.0, 0.0)
        rank = _dot(onehot, before_ref[...]) + carry_ref[:, 0:1]
        r1 = jnp.sum(jnp.where(sel1, rank, 0.0), axis=0, keepdims=True)
        r2 = jnp.sum(jnp.where(sel2, rank, 0.0), axis=0, keepdims=True)
        carry_ref[...] = carry_ref[...] + jnp.sum(onehot, axis=1, keepdims=True)
        meta = jnp.where(r8 == 0, (i1 - N_GROUPS).astype(F32), 0.0)
        meta = jnp.where(r8 == 1, (i2 - N_GROUPS).astype(F32), meta)
        meta = jnp.where(r8 == 2, r1, meta)
        meta = jnp.where(r8 == 3, r2, meta)
        meta = jnp.where(r8 == 4, gate1, meta)
        meta = jnp.where(r8 == 5, gate2, meta)
        meta_ref[:, c * sub:(c + 1) * sub] = meta
    cnt_ref[...] = carry_ref[...]


def _route(logits_t, tt=1024, sub=256):
    rows, t = logits_t.shape
    before = jnp.asarray(np.arange(sub)[:, None] < np.arange(sub)[None, :], BF16)
    return pl.pallas_call(
        _route_kernel,
        grid=(t // tt,),
        in_specs=[pl.BlockSpec((rows, tt), lambda i: (0, i)),
                  pl.BlockSpec((sub, sub), lambda i: (0, 0))],
        out_specs=[pl.BlockSpec((SUBLANE, tt), lambda i: (0, i)),
                   pl.BlockSpec((rows, LANE), lambda i: (0, 0))],
        out_shape=[jax.ShapeDtypeStruct((SUBLANE, t), F32),
                   jax.ShapeDtypeStruct((rows, LANE), F32)],
        scratch_shapes=[pltpu.VMEM((rows, LANE), F32)],
        compiler_params=_params("arbitrary"),
        name="route",
    )(logits_t, before)


def _plan_kernel(meta_ref, first_row_ref, dest_ref):
    n_tiles = dest_ref.shape[0]
    tt = dest_ref.shape[2] // 2
    rows = first_row_ref.shape[0]
    first_row = first_row_ref[:, 0:1]
    r = lax.broadcasted_iota(jnp.int32, (rows, tt), 0)
    for g in range(n_tiles):
        m = meta_ref[:, g * tt:(g + 1) * tt]

        def dest_of(k):
            e_row = m[k:k + 1, :].astype(jnp.int32) + N_GROUPS
            return jnp.sum(jnp.where(r == e_row, first_row, 0.0), axis=0, keepdims=True) + m[2 + k:3 + k, :]

        dest_ref[g] = jnp.concatenate([dest_of(0), dest_of(1)], axis=1).astype(jnp.int32)


def _plan(meta_t, first_row, tt, tiles_per_step=4):
    t = meta_t.shape[1]
    rows = first_row.shape[0]
    return pl.pallas_call(
        _plan_kernel,
        grid=(t // (tt * tiles_per_step),),
        in_specs=[pl.BlockSpec((SUBLANE, tt * tiles_per_step), lambda i: (0, i)),
                  pl.BlockSpec((rows, LANE), lambda i: (0, 0))],
        out_specs=pl.BlockSpec((tiles_per_step, 1, 2 * tt), lambda i: (i, 0, 0)),
        out_shape=jax.ShapeDtypeStruct((t // tt, 1, 2 * tt), jnp.int32),
        compiler_params=_params("parallel"),
        name="plan",
    )(meta_t, first_row)


def _row_copy(src, dst, s_row, d_row, sem):
    return pltpu.make_async_copy(src.at[pl.ds(pl.multiple_of(s_row * SUBLANE, SUBLANE), SUBLANE), :],
                                 dst.at[pl.ds(pl.multiple_of(d_row * SUBLANE, SUBLANE), SUBLANE), :], sem)


def _dispatch_kernel(dest_ref, xt_ref, xs_ref, sem):
    tq = dest_ref.shape[2] // 2

    def issue(t, c):
        _row_copy(xt_ref, xs_ref, t, dest_ref[0, 0, t], sem).start(priority=0)
        _row_copy(xt_ref, xs_ref, t, dest_ref[0, 0, tq + t], sem).start(priority=1)
        return c

    lax.fori_loop(0, tq, issue, 0, unroll=8)
    for _ in range(2):
        pltpu.make_async_copy(xt_ref, xs_ref.at[pl.ds(0, tq * SUBLANE), :], sem).wait()


def _dispatch(dest3, x_tiled, n_rows, tiles_per_step=4):
    n_tiles, _, two_tt = dest3.shape
    tt = two_tt // 2
    tq = tiles_per_step * tt
    n_steps = n_tiles // tiles_per_step
    dest3 = dest3.reshape(n_steps, tiles_per_step, 2, tt).transpose(0, 2, 1, 3).reshape(n_steps, 1, 2 * tq)
    return pl.pallas_call(
        _dispatch_kernel,
        grid=(n_steps,),
        in_specs=[pl.BlockSpec((1, 1, 2 * tq), lambda i: (i, 0, 0), memory_space=pltpu.SMEM),
                  pl.BlockSpec((tq * SUBLANE, LANE), lambda i: (i, 0))],
        out_specs=pl.BlockSpec(memory_space=pl.ANY),
        out_shape=jax.ShapeDtypeStruct((n_rows * SUBLANE, LANE), F32),
        scratch_shapes=[pltpu.SemaphoreType.DMA(())],
        compiler_params=_params("arbitrary"),
        name="dispatch",
    )(dest3, x_tiled)


def _experts_kernel(blk_exp_ref, blk_rows_ref, n_used_ref, blk_first_ref, blk_slot_ref, blk_next_ref,
                    xs_hbm, wg_hbm, wu_hbm, wd_hbm, ys_hbm, xbuf, ybuf, x_scr, y_scr, wg_buf, wu_buf, wd_buf,
                    wsems, xsems, ysems):
    b = pl.program_id(0)
    n_used = n_used_ref[0]
    rows, d = x_scr.shape
    block_rows = rows * SUBLANE
    n_quanta = rows // EXPERT_QUANTUM

    def weight_copies(e, slot):
        return [pltpu.make_async_copy(hbm.at[e], buf.at[slot], wsems.at[k, slot])
                for k, (hbm, buf) in enumerate(((wg_hbm, wg_buf), (wu_hbm, wu_buf), (wd_hbm, wd_buf)))]

    def x_copy(blk, slot, m):
        return pltpu.make_async_copy(
            xs_hbm.at[pl.ds(pl.multiple_of(blk * block_rows, block_rows), m * SUBLANE), :],
            xbuf.at[pl.ds(pl.multiple_of(slot * block_rows, block_rows), m * SUBLANE), :], xsems.at[slot])

    def y_copy(blk, slot, m):
        return pltpu.make_async_copy(
            ybuf.at[pl.ds(pl.multiple_of(slot * block_rows, block_rows), m * SUBLANE), :],
            ys_hbm.at[pl.ds(pl.multiple_of(blk * block_rows, block_rows), m * SUBLANE), :], ysems.at[slot])

    def for_quanta(blk, fn):
        n_valid = blk_rows_ref[blk]
        for k in range(1, n_quanta + 1):
            lo = (k - 1) * EXPERT_QUANTUM
            pl.when((n_valid > lo) & (n_valid <= k * EXPERT_QUANTUM))(functools.partial(fn, k * EXPERT_QUANTUM))

    @pl.when(b < n_used)
    def _():
        wslot = blk_slot_ref[b]
        slot = b % 2

        @pl.when(b == 0)
        def _():
            for cp in weight_copies(blk_exp_ref[0], wslot):
                cp.start()
            for_quanta(0, lambda m: x_copy(0, 0, m).start())

        @pl.when(b + 1 < n_used)
        def _():
            for_quanta(b + 1, lambda m: x_copy(b + 1, 1 - slot, m).start())

        @pl.when(blk_first_ref[b] == 1)
        def _():
            @pl.when(blk_next_ref[b] >= 0)
            def _():
                for cp in weight_copies(blk_next_ref[b], 1 - wslot):
                    cp.start()

            for cp in weight_copies(blk_exp_ref[b], wslot):
                cp.wait()

        @pl.when(b >= 2)
        def _():
            for_quanta(b - 2, lambda m: y_copy(b - 2, slot, m).wait())

        n_valid = blk_rows_ref[b]
        off = pl.multiple_of(slot * block_rows, block_rows)

        def run(m):
            x_copy(b, slot, m).wait()
            ri = lax.broadcasted_iota(jnp.int32, (m, 1), 0)
            for c in range(d // LANE):
                x = xbuf[pl.ds(off + c, m, stride=SUBLANE), :]
                x_scr[0:m, c * LANE:(c + 1) * LANE] = jnp.where(ri < n_valid, x, 0.0).astype(BF16)
            x = x_scr[0:m, :]
            hg = _dot(x, wg_buf[wslot])
            hu = _dot(x, wu_buf[wslot])
            y_scr[0:m, :] = _dot(hg * jax.nn.sigmoid(hg) * hu, wd_buf[wslot])
            for c in range(d // LANE):
                ybuf[pl.ds(off + c, m, stride=SUBLANE), :] = y_scr[0:m, c * LANE:(c + 1) * LANE]
            y_copy(b, slot, m).start()

        for_quanta(b, run)

        @pl.when(b == n_used - 1)
        def _():
            for_quanta(b, lambda m: y_copy(b, slot, m).wait())

            @pl.when(b >= 1)
            def _():
                for_quanta(b - 1, lambda m: y_copy(b - 1, 1 - slot, m).wait())


def _experts(blk_exp, blk_rows, n_used, blk_first, blk_slot, blk_next, xs_tiled, w_g, w_u, w_d):
    n_rows = xs_tiled.shape[0] // SUBLANE
    rows = EXPERT_ROWS
    nblk = n_rows // rows
    _, d, ff = w_g.shape
    any_space = pl.BlockSpec(memory_space=pl.ANY)
    grid_spec = pltpu.PrefetchScalarGridSpec(
        num_scalar_prefetch=6,
        grid=(nblk,),
        in_specs=[any_space, any_space, any_space, any_space],
        out_specs=any_space,
        scratch_shapes=[pltpu.VMEM((2 * rows * SUBLANE, LANE), F32), pltpu.VMEM((2 * rows * SUBLANE, LANE), F32),
                        pltpu.VMEM((rows, d), BF16), pltpu.VMEM((rows, d), F32),
                        pltpu.VMEM((2, d, ff), F32), pltpu.VMEM((2, d, ff), F32), pltpu.VMEM((2, ff, d), F32),
                        pltpu.SemaphoreType.DMA((3, 2)), pltpu.SemaphoreType.DMA((2,)),
                        pltpu.SemaphoreType.DMA((2,))],
    )
    return pl.pallas_call(
        _experts_kernel,
        grid_spec=grid_spec,
        out_shape=jax.ShapeDtypeStruct((n_rows * SUBLANE, LANE), F32),
        compiler_params=_params("arbitrary"),
        name="experts",
    )(blk_exp, blk_rows, n_used, blk_first, blk_slot, blk_next, xs_tiled, w_g, w_u, w_d)


def _combine_kernel(alpha, dcur_ref, dnext_ref, meta_ref, x_ref, g_ref, b_ref, ys_ref, o_ref, buf_ref, sems):
    i = pl.program_id(0)
    tq, d = x_ref.shape
    slot_rows = 2 * tq * SUBLANE

    def issue(d_ref, slot, t):
        _row_copy(ys_ref, buf_ref, d_ref[0, 0, t], slot * (2 * tq) + t, sems.at[slot]).start(priority=0)
        _row_copy(ys_ref, buf_ref, d_ref[0, 0, tq + t], slot * (2 * tq) + tq + t, sems.at[slot]).start(priority=1)

    def wait_slot(slot):
        off = pl.multiple_of(slot * slot_rows, slot_rows)
        pltpu.make_async_copy(ys_ref.at[pl.ds(0, slot_rows), :], buf_ref.at[pl.ds(off, slot_rows), :],
                              sems.at[slot]).wait()
        return off

    def issue_all(d_ref, slot):
        lax.fori_loop(0, tq, lambda t, c: (issue(d_ref, slot, t), c)[1], 0, unroll=4)

    slot = i % 2

    @pl.when(i == 0)
    def _():
        issue_all(dcur_ref, 0)

    @pl.when(i + 1 < pl.num_programs(0))
    def _():
        issue_all(dnext_ref, 1 - slot)

    off = wait_slot(slot)
    meta_rows = jnp.transpose(jnp.concatenate([meta_ref[...], jnp.zeros((LANE - SUBLANE, tq), F32)], axis=0))
    gate1 = meta_rows[:, 4:5]
    gate2 = meta_rows[:, 5:6]
    for c in range(d // LANE):
        sl = slice(c * LANE, (c + 1) * LANE)
        y1 = buf_ref[pl.ds(off + c, tq, stride=SUBLANE), :]
        y2 = buf_ref[pl.ds(off + tq * SUBLANE + c, tq, stride=SUBLANE), :]
        o_ref[:, sl] = alpha * x_ref[:, sl] + (y1 * gate1 + y2 * gate2)
    o_ref[...] = _layer_norm(o_ref[...], g_ref[...], b_ref[...])


def _combine(dest3, meta, x2, g, bb, ys_tiled, alpha):
    t, d = x2.shape
    n = dest3.shape[0]
    tq = t // n
    row = lambda i: (i, 0)
    const = lambda i: (0, 0)
    return pl.pallas_call(
        functools.partial(_combine_kernel, alpha),
        grid=(n,),
        in_specs=[pl.BlockSpec((1, 1, 2 * tq), lambda i: (i, 0, 0), memory_space=pltpu.SMEM),
                  pl.BlockSpec((1, 1, 2 * tq), lambda i: (jnp.minimum(i + 1, n - 1), 0, 0), memory_space=pltpu.SMEM),
                  pl.BlockSpec((SUBLANE, tq), lambda i: (0, i)),
                  pl.BlockSpec((tq, d), row),
                  pl.BlockSpec((1, d), const), pl.BlockSpec((1, d), const),
                  pl.BlockSpec(memory_space=pl.ANY)],
        out_specs=pl.BlockSpec((tq, d), row),
        out_shape=jax.ShapeDtypeStruct((t, d), F32),
        scratch_shapes=[pltpu.VMEM((2 * 2 * tq * SUBLANE, LANE), F32), pltpu.SemaphoreType.DMA((2,))],
        compiler_params=_params("arbitrary"),
        name="combine",
    )(dest3, dest3, meta, x2, g, bb, ys_tiled)


def _moe(x2, x2_tiled, logits, w_g, w_u, w_d, g, bb, alpha):
    t, d = x2.shape
    meta, cnt = _route(logits)
    counts = cnt[N_GROUPS:N_GROUPS + N_EXPERTS, 0].astype(jnp.int32)
    padded = ((counts + EXPERT_ROWS - 1) // EXPERT_ROWS) * EXPERT_ROWS
    pend = jnp.cumsum(padded)
    poff = pend - padded
    first_row = jnp.pad(poff.astype(F32), (N_GROUPS, ROUTE_ROWS - N_GROUPS - N_EXPERTS))
    dest3 = _plan(meta, jnp.broadcast_to(first_row[:, None], (ROUTE_ROWS, LANE)), ROUTE_TILE)
    n_rows = t * 2 + N_EXPERTS * EXPERT_ROWS
    nblk = n_rows // EXPERT_ROWS
    blk_start = jnp.arange(nblk, dtype=jnp.int32) * EXPERT_ROWS
    blk_exp = jnp.sum((pend[None, :] <= blk_start[:, None]).astype(jnp.int32), axis=1)
    blk_exp = jnp.minimum(blk_exp, N_EXPERTS - 1)
    e_idx = jnp.arange(N_EXPERTS, dtype=jnp.int32)
    owner = blk_exp[:, None] == e_idx[None, :]

    def of_block(per_expert):
        return jnp.sum(jnp.where(owner, per_expert[None, :], 0), axis=1).astype(jnp.int32)

    blk_rows = jnp.clip(of_block(poff + counts) - blk_start, 0, EXPERT_ROWS).astype(jnp.int32)
    n_used = (pend[-1:] // EXPERT_ROWS).astype(jnp.int32)
    has_rows = counts > 0
    ordinal = jnp.cumsum(has_rows.astype(jnp.int32)) - 1
    later = has_rows[None, :] & (e_idx[None, :] > e_idx[:, None])
    next_used = jnp.min(jnp.where(later, e_idx[None, :], N_EXPERTS), axis=1)
    next_used = jnp.where(next_used == N_EXPERTS, -1, next_used).astype(jnp.int32)
    blk_first = (blk_start == of_block(poff)).astype(jnp.int32)
    blk_slot = of_block(ordinal % 2)
    blk_next = of_block(next_used)
    xs_tiled = _dispatch(dest3, x2_tiled, n_rows)
    ys_tiled = _experts(blk_exp, blk_rows, n_used, blk_first, blk_slot, blk_next, xs_tiled, w_g, w_u, w_d)
    return _combine(dest3, meta, x2, g, bb, ys_tiled, alpha)


def kernel(x, mem, w_in, w_out, ln_mix_g, ln_mix_b, w_xq, w_xkv, w_xo, ln_x_g, ln_x_b, w_route_group,
           w_route_expert, w_exp_gate, w_exp_up, w_exp_down, ln_moe_g, ln_moe_b):
    b, s, d = x.shape
    depth = w_in.shape[0]
    alpha = (2.0 * depth) ** 0.25
    t = b * s
    ret_cols = (2 * RET_HEADS * RET_QK_DIM + 2 * RET_HEADS * RET_V_DIM)
    xc = x.reshape(t, d)
    for l in range(depth):
        n_qk = 2 * RET_HEADS * RET_QK_DIM
        h_ret, h_dil = _proj_in(xc, w_in[l], _retention_weight_layout(w_in[l][:, :n_qk]).astype(BF16), ret_cols)
        y_ret = _retention(h_ret.reshape(b, s, ret_cols)).reshape(t, -1)
        y_dil = _dilated(h_dil, b).reshape(t, -1)
        kv = _mem_kv(mem.reshape(b * mem.shape[1], d), w_xkv[l]).reshape(b, mem.shape[1], 2 * d)
        w_r = jnp.concatenate([w_route_group[l], w_route_expert[l]], axis=-1)
        w_r = jnp.pad(w_r.T, ((0, ROUTE_ROWS - w_r.shape[1]), (0, 0))).astype(BF16)
        x2, x2_tiled, logits = _xattn(xc, y_ret, y_dil, w_out[l].astype(BF16), ln_mix_g[l][None], ln_mix_b[l][None],
                                      kv, w_xq[l].astype(BF16), w_xo[l].astype(BF16),
                                      ln_x_g[l][None], ln_x_b[l][None], w_r, alpha, s)
        xc = _moe(x2, x2_tiled, logits, w_exp_gate[l], w_exp_up[l], w_exp_down[l],
                  ln_moe_g[l][None], ln_moe_b[l][None], alpha)
    return xc.reshape(b, s, d)
```

```python
import functools
import math

import jax
import jax.numpy as jnp
import numpy as np
from jax import lax
from jax.experimental import pallas as pl
from jax.experimental.pallas import tpu as pltpu

BF16 = jnp.bfloat16
F32 = jnp.float32

LANE = 128
SUBLANE = 8
VMEM_LIMIT = 56 * 1024 * 1024

RET_HEADS = 4
RET_QK_DIM = 64
RET_V_DIM = 128
RET_CHUNK = 128
RET_STEP_CHUNKS = 4
ROPE_BASE = 10000.0
DIL_HEADS = 8
DIL_HEAD_DIM = 64
DIL_DILATIONS = (1, 4, 16)
DIL_BLOCK = 128
DIL_SUPER = DIL_BLOCK * max(DIL_DILATIONS)
DIL_GROUP = 1
XATTN_HEADS = 4
XATTN_PARTS = 2
N_GROUPS = 4
EXPERTS_PER_GROUP = 8
N_EXPERTS = N_GROUPS * EXPERTS_PER_GROUP
ROUTE_ROWS = -(-(N_GROUPS + N_EXPERTS) // SUBLANE) * SUBLANE
EXPERT_ROWS = 512
EXPERT_QUANTUM = 256
ROUTE_TILE = 256
LN_EPS = 1e-5
GN_EPS = 1e-6
NEG = -1e30


def _params(*sem):
    return pltpu.CompilerParams(dimension_semantics=sem, vmem_limit_bytes=VMEM_LIMIT)


def _layer_norm(z, g, b):
    mu = jnp.mean(z, axis=-1, keepdims=True)
    zc = z - mu
    var = jnp.mean(zc * zc, axis=-1, keepdims=True)
    return zc * lax.rsqrt(var + LN_EPS) * g + b


def _dot(a, b):
    return jnp.dot(a.astype(BF16), b.astype(BF16), preferred_element_type=F32)


def _dot_nt(a, b):
    return lax.dot_general(a.astype(BF16), b.astype(BF16), (((1,), (1,)), ((), ())),
                           preferred_element_type=F32)


def _dot_tn(a, b):
    return lax.dot_general(a.astype(BF16), b.astype(BF16), (((0,), (0,)), ((), ())),
                           preferred_element_type=F32)


def _proj_in_kernel(x_ref, w_ref, o_ref):
    o_ref[...] = _dot(x_ref[...], w_ref[...]).astype(o_ref.dtype)


def _proj_in2_kernel(x_ref, w_ref, w_head_ref, o_ref, slab_ref, w_bf16):
    @pl.when(pl.program_id(0) == 0)
    def _():
        w_bf16[...] = w_ref[...].astype(BF16)
        w_bf16[:, 0:w_head_ref.shape[1]] = w_head_ref[...]

    n_row = o_ref.shape[1]
    x = x_ref[...].astype(BF16)
    o_ref[...] = jnp.dot(x, w_bf16[:, 0:n_row], preferred_element_type=F32)
    rest = jnp.dot(x, w_bf16[:, n_row:], preferred_element_type=F32)
    for c in range(slab_ref.shape[0]):
        slab_ref[c] = rest[:, c * LANE:(c + 1) * LANE]


def _proj_in(x2d, w_f32, w_head_bf16, n_row, tm=512):
    t, d = x2d.shape
    n = w_f32.shape[1]
    n_slabs = (n - n_row) // LANE
    whole = lambda shape: pl.BlockSpec(shape, lambda i: (0, 0), pipeline_mode=pl.Buffered(1))
    return pl.pallas_call(
        _proj_in2_kernel,
        grid=(t // tm,),
        in_specs=[pl.BlockSpec((tm, d), lambda i: (i, 0)), whole((d, n)), whole(w_head_bf16.shape)],
        out_specs=[pl.BlockSpec((tm, n_row), lambda i: (i, 0)),
                   pl.BlockSpec((n_slabs, tm, LANE), lambda i: (0, i, 0))],
        out_shape=[jax.ShapeDtypeStruct((t, n_row), F32),
                   jax.ShapeDtypeStruct((n_slabs, t, LANE), F32)],
        scratch_shapes=[pltpu.VMEM((d, n), BF16)],
        compiler_params=_params("arbitrary"),
        name="proj_in",
    )(x2d, w_f32, w_head_bf16)


def _retention_kernel(qk_ref, v_ref, g_ref, cos_ref, sin_ref, decay_ref, zeta_ref, xi_ref, gam_ref,
                      o_ref, state_ref, y_ref):
    n = pl.program_id(1)

    @pl.when(n == 0)
    def _():
        state_ref[...] = jnp.zeros_like(state_ref)

    c = RET_CHUNK
    n_sub = qk_ref.shape[1] // c
    half = RET_QK_DIM // 2
    lane = lax.broadcasted_iota(jnp.int32, (c, LANE), 1)

    rotated = []
    for j in range(n_sub):
        rows = slice(j * c, (j + 1) * c)
        cos = cos_ref[rows, :]
        sin = sin_ref[rows, :]

        def rot(col, rows=rows, cos=cos, sin=sin):
            t1 = qk_ref[0, rows, col * LANE:(col + 1) * LANE]
            t2 = qk_ref[0, rows, (col + 1) * LANE:(col + 2) * LANE]
            return t1 * cos - t2 * sin, t1 * sin + t2 * cos

        q1, q2 = rot(0)
        k1, k2 = (t * (RET_QK_DIM ** -0.5) for t in rot(2))
        rotated.append((q1, q2, k1, k2, jnp.concatenate([k1, k2], axis=1)))

    for h in range(RET_HEADS):
        cols = slice(h * RET_V_DIM, (h + 1) * RET_V_DIM)
        mine = (lane >= h * half) & (lane < (h + 1) * half)
        zeta = zeta_ref[h]
        st = state_ref[h]
        pad_lo = [jnp.zeros((h * half, RET_V_DIM), BF16)] if h > 0 else []
        pad_hi = [jnp.zeros(((RET_HEADS - 1 - h) * half, RET_V_DIM), BF16)] if h < RET_HEADS - 1 else []
        for j in range(n_sub):
            rows = slice(j * c, (j + 1) * c)
            q1, q2, k1, k2, k_all = rotated[j]
            qm = jnp.concatenate([jnp.where(mine, q1, 0.0), jnp.where(mine, q2, 0.0)], axis=1)
            v = v_ref[0, rows, cols]
            s = _dot_nt(qm, k_all) * decay_ref[h]
            sb = st.astype(BF16)
            st_rows = jnp.concatenate(pad_lo + [sb[:half]] + pad_hi + pad_lo + [sb[half:]] + pad_hi, axis=0)
            y_ref[rows, cols] = _dot(s, v) + _dot(qm, st_rows) * xi_ref[h]
            kv = _dot_tn(jnp.concatenate([k1 * zeta, k2 * zeta], axis=1), v)
            lo = h * half
            hi = RET_HEADS * half + h * half
            st = gam_ref[h, :, :] * st + jnp.concatenate([kv[lo:lo + half], kv[hi:hi + half]], axis=0)
        state_ref[h] = st

    for h in range(RET_HEADS):
        cols = slice(h * RET_V_DIM, (h + 1) * RET_V_DIM)
        y = y_ref[:, cols]
        mu = jnp.mean(y, axis=-1, keepdims=True)
        yc = y - mu
        var = jnp.mean(yc * yc, axis=-1, keepdims=True)
        yn = yc * lax.rsqrt(var + GN_EPS)
        gate = g_ref[0, :, cols]
        o_ref[0, :, cols] = (gate * jax.nn.sigmoid(gate) * yn).astype(o_ref.dtype)


def _retention_tables(s):
    half = RET_QK_DIM // 2
    inv = 1.0 / (ROPE_BASE ** (np.arange(half, dtype=np.float64) / half))
    ang = np.arange(s, dtype=np.float64)[:, None] * inv[None, :]
    cos_t = np.tile(np.cos(ang), (1, RET_HEADS))
    sin_t = np.tile(np.sin(ang), (1, RET_HEADS))
    c = RET_CHUNK
    lg = np.log(1.0 - np.exp2(-5.0 - np.arange(RET_HEADS, dtype=np.float64)))
    idx = np.arange(c, dtype=np.float64)
    diff = idx[:, None] - idx[None, :]
    decay = np.where(diff >= 0, np.exp(lg[:, None, None] * np.maximum(diff, 0.0)), 0.0)
    lanes = (RET_HEADS, c, LANE)
    zeta = np.broadcast_to(np.exp(lg[:, None] * (c - 1.0 - idx))[:, :, None], lanes)
    xi = np.broadcast_to(np.exp(lg[:, None] * (idx + 1.0))[:, :, None], lanes)
    gam = np.broadcast_to(np.exp(lg * c)[:, None, None], (RET_HEADS, 1, LANE))
    return tuple(jnp.asarray(np.ascontiguousarray(a), F32) for a in (cos_t, sin_t, decay, zeta, xi, gam))


def _retention_weight_layout(w_qk):
    d = w_qk.shape[0]
    half = RET_QK_DIM // 2
    return w_qk.reshape(d, 2, RET_HEADS, 2, half).transpose(0, 1, 3, 2, 4).reshape(d, -1)


def _retention(h3):
    b, s, _ = h3.shape
    c = RET_CHUNK
    qk_w = 2 * RET_HEADS * RET_QK_DIM
    v_w = RET_HEADS * RET_V_DIM
    assert qk_w == v_w
    cos_t, sin_t, decay, zeta, xi, gam = _retention_tables(s)
    const3 = lambda bi, n: (0, 0, 0)
    rows = RET_STEP_CHUNKS * c
    return pl.pallas_call(
        _retention_kernel,
        grid=(b, s // rows),
        in_specs=[pl.BlockSpec((1, rows, qk_w), lambda bi, n: (bi, n, 0)),
                  pl.BlockSpec((1, rows, v_w), lambda bi, n: (bi, n, 1)),
                  pl.BlockSpec((1, rows, v_w), lambda bi, n: (bi, n, 2)),
                  pl.BlockSpec((rows, LANE), lambda bi, n: (n, 0)),
                  pl.BlockSpec((rows, LANE), lambda bi, n: (n, 0)),
                  pl.BlockSpec((RET_HEADS, c, c), const3),
                  pl.BlockSpec((RET_HEADS, c, LANE), const3),
                  pl.BlockSpec((RET_HEADS, c, LANE), const3),
                  pl.BlockSpec((RET_HEADS, 1, LANE), const3)],
        out_specs=pl.BlockSpec((1, rows, v_w), lambda bi, n: (bi, n, 0)),
        out_shape=jax.ShapeDtypeStruct((b, s, v_w), BF16),
        scratch_shapes=[pltpu.VMEM((RET_HEADS, RET_QK_DIM, RET_V_DIM), F32),
                        pltpu.VMEM((rows, v_w), F32)],
        compiler_params=_params("parallel", "arbitrary"),
        name="retention",
    )(h3, h3, h3, cos_t, sin_t, decay, zeta, xi, gam)


def _dilated_kernel(q_ref, kp_ref, kc_ref, vp_ref, vc_ref, bias_ref, o_ref, acc_ref, m_ref, l_ref,
                    s0_ref, s1_ref, p0_ref, p1_ref):
    j = pl.program_id(2)
    sup = DIL_SUPER
    q_blk = DIL_BLOCK
    scale = DIL_HEAD_DIM ** -0.5 * math.log2(math.e)
    n_blocks = sup // q_blk
    lane = lax.broadcasted_iota(jnp.int32, (q_blk, LANE), 1)
    head0 = lane < DIL_HEAD_DIM
    first_bias = jnp.where(j == 0, 1, 0)

    groups = []
    for bi, d in enumerate(DIL_DILATIONS):
        n_per_r = sup // (q_blk * d)
        for t0 in range(0, n_blocks, DIL_GROUP):
            blocks = []
            for t in range(t0, t0 + DIL_GROUP):
                r, n = divmod(t, n_per_r)
                blocks.append((n * (q_blk * d) + r, (n - 1) * (q_blk * d) + r, n))
            groups.append((bi, d, blocks))
    s_bufs = (s0_ref, s1_ref)
    p_bufs = (p0_ref, p1_ref)

    def window(prev_ref, cur_ref, k_start, d):
        if k_start >= 0:
            return cur_ref[0, pl.ds(k_start, 2 * q_blk, stride=d), :]
        return jnp.concatenate([prev_ref[0, pl.ds(sup + k_start, q_blk, stride=d), :],
                                cur_ref[0, pl.ds(k_start + q_blk * d, q_blk, stride=d), :]], axis=0)

    def scores(gi):
        bi, d, blocks = groups[gi]
        for g, (q_start, k_start, n) in enumerate(blocks):
            q = q_ref[0, pl.ds(q_start, q_blk, stride=d), :] * scale
            kb = window(kp_ref, kc_ref, k_start, d)
            q2 = jnp.concatenate([jnp.where(head0, q, 0.0), jnp.where(head0, 0.0, q)], axis=0)
            bias = bias_ref[first_bias] if n == 0 else bias_ref[0]
            s_bufs[gi % 2][g] = _dot_nt(q2, kb) + bias

    def softmax(gi):
        bi, d, blocks = groups[gi]
        for g, (q_start, k_start, n) in enumerate(blocks):
            s = s_bufs[gi % 2][g]
            m2 = jnp.max(s, axis=-1, keepdims=True)
            p_bufs[gi % 2][g] = jnp.exp2(s - m2).astype(BF16)
            m_ref[bi, pl.ds(q_start, q_blk, stride=d), :] = jnp.where(head0, m2[:q_blk], m2[q_blk:])

    def values(gi):
        bi, d, blocks = groups[gi]
        for g, (q_start, k_start, n) in enumerate(blocks):
            vb = window(vp_ref, vc_ref, k_start, d).astype(BF16)
            o2 = jnp.dot(p_bufs[gi % 2][g], jnp.concatenate([vb, jnp.ones_like(vb)], axis=1),
                         preferred_element_type=F32)
            rows = pl.ds(q_start, q_blk, stride=d)
            acc_ref[bi, rows, :] = jnp.where(head0, o2[:q_blk, :LANE], o2[q_blk:, :LANE])
            l_ref[bi, rows, :] = jnp.where(head0, o2[:q_blk, LANE:], o2[q_blk:, LANE:])

    for step in range(len(groups) + 2):
        if step < len(groups):
            scores(step)
        if 0 <= step - 1 < len(groups):
            softmax(step - 1)
        if step - 2 >= 0:
            values(step - 2)

    def merge(c, carry):
        rows = pl.ds(pl.multiple_of(c * q_blk, q_blk), q_blk)
        ms = [m_ref[bi, rows, :] for bi in range(len(DIL_DILATIONS))]
        m_all = functools.reduce(jnp.maximum, ms)
        ws = [jnp.exp2(m - m_all) for m in ms]
        num = functools.reduce(lambda a, b: a + b, [w * acc_ref[bi, rows, :] for bi, w in enumerate(ws)])
        den = functools.reduce(lambda a, b: a + b, [w * l_ref[bi, rows, :] for bi, w in enumerate(ws)])
        o_ref[0, rows, :] = (num / den).astype(o_ref.dtype)
        return carry

    lax.fori_loop(0, n_blocks, merge, 0)


def _dilated_bias():
    q_blk = DIL_BLOCK
    qi = np.arange(2 * q_blk)[:, None] % q_blk
    kj = np.arange(2 * q_blk)[None, :]
    band = (kj >= qi) & (kj <= qi + q_blk)
    return jnp.asarray(np.stack([np.where(band, 0.0, NEG), np.where(band & (kj >= q_blk), 0.0, NEG)]), F32)


def _dilated(slabs, b):
    n_slabs, t, _ = slabs.shape
    s = t // b
    h3 = slabs.reshape(n_slabs * b, s, LANE)
    sup = DIL_SUPER
    n_pairs = DIL_HEADS * DIL_HEAD_DIM // LANE
    n_br = len(DIL_DILATIONS)
    cq, ck, cv = 0, n_pairs, 2 * n_pairs
    cur = lambda c: (lambda bi, p, j: ((c + p) * b + bi, j, 0))
    prev = lambda c: (lambda bi, p, j: ((c + p) * b + bi, jnp.maximum(j - 1, 0), 0))
    blk = (1, sup, LANE)
    return pl.pallas_call(
        _dilated_kernel,
        grid=(b, n_pairs, s // sup),
        in_specs=[pl.BlockSpec(blk, cur(cq)),
                  pl.BlockSpec(blk, prev(ck)), pl.BlockSpec(blk, cur(ck)),
                  pl.BlockSpec(blk, prev(cv)), pl.BlockSpec(blk, cur(cv)),
                  pl.BlockSpec((2, 2 * DIL_BLOCK, 2 * DIL_BLOCK), lambda bi, p, j: (0, 0, 0))],
        out_specs=pl.BlockSpec(blk, lambda bi, p, j: (bi, j, p)),
        out_shape=jax.ShapeDtypeStruct((b, s, n_pairs * LANE), BF16),
        scratch_shapes=[pltpu.VMEM((n_br, sup, LANE), F32), pltpu.VMEM((n_br, sup, LANE), F32),
                        pltpu.VMEM((n_br, sup, LANE), F32),
                        pltpu.VMEM((DIL_GROUP, 2 * DIL_BLOCK, 2 * DIL_BLOCK), F32),
                        pltpu.VMEM((DIL_GROUP, 2 * DIL_BLOCK, 2 * DIL_BLOCK), F32),
                        pltpu.VMEM((DIL_GROUP, 2 * DIL_BLOCK, 2 * DIL_BLOCK), BF16),
                        pltpu.VMEM((DIL_GROUP, 2 * DIL_BLOCK, 2 * DIL_BLOCK), BF16)],
        compiler_params=_params("parallel", "parallel", "arbitrary"),
        name="dilated",
    )(h3, h3, h3, h3, h3, _dilated_bias())


def _mem_kv(mem2d, w_bf16, tn=512):
    m, d = mem2d.shape
    n = w_bf16.shape[1]
    return pl.pallas_call(
        _proj_in_kernel,
        grid=(n // tn,),
        in_specs=[pl.BlockSpec((m, d), lambda i: (0, 0)),
                  pl.BlockSpec((d, tn), lambda i: (0, i))],
        out_specs=pl.BlockSpec((m, tn), lambda i: (0, i)),
        out_shape=jax.ShapeDtypeStruct((m, n), BF16),
        compiler_params=_params("parallel"),
        name="mem_kv",
    )(mem2d, w_bf16)


def _xattn_kernel(alpha, x_ref, yr_ref, yd_ref, wout_ref, g1_ref, b1_ref, k_ref, v_ref, wq_ref, wo_ref,
                  g_ref, b_ref, wr_ref, o_ref, ot_ref, lg_ref, x1_ref, q_ref, att_ref):
    tm, d = x_ref.shape
    dh = d // XATTN_HEADS
    wr = yr_ref.shape[1]
    scale = dh ** -0.5 * math.log2(math.e)
    part = tm // XATTN_PARTS

    def mix(rows):
        y = _dot(yr_ref[rows, :], wout_ref[0:wr, :]) + _dot(yd_ref[rows, :], wout_ref[wr:, :])
        x1_ref[rows, :] = _layer_norm(alpha * x_ref[rows, :] + y, g1_ref[...], b1_ref[...])

    def query(rows):
        q_ref[rows, :] = _dot(x1_ref[rows, :], wq_ref[...]).astype(BF16)

    def attend(rows):
        for h in range(XATTN_HEADS):
            sl = slice(h * dh, (h + 1) * dh)
            s = _dot_nt(q_ref[rows, sl], k_ref[0, :, sl]) * scale
            m = jnp.max(s, axis=-1, keepdims=True)
            e = jnp.exp2(s - m)
            p = e / jnp.sum(e, axis=-1, keepdims=True)
            att_ref[rows, sl] = _dot(p, v_ref[0, :, sl]).astype(BF16)

    def finish(rows, r0):
        y = _dot(att_ref[rows, :], wo_ref[...])
        x2 = _layer_norm(alpha * x1_ref[rows, :] + y, g_ref[...], b_ref[...])
        o_ref[rows, :] = x2
        for c in range(d // LANE):
            ot_ref[pl.ds(r0 * SUBLANE + c, part, stride=SUBLANE), :] = x2[:, c * LANE:(c + 1) * LANE]
        lg_ref[:, rows] = _dot_nt(wr_ref[...], x2)

    stages = (mix, query, attend, finish)
    for step in range(XATTN_PARTS + len(stages) - 1):
        for si, stage in enumerate(stages):
            pi = step - si
            if 0 <= pi < XATTN_PARTS:
                rows = slice(pi * part, (pi + 1) * part)
                if stage is finish:
                    stage(rows, pi * part)
                else:
                    stage(rows)


def _xattn(x2d, y_ret, y_dil, w_out, g1, b1, kv, wq, wo, g, bb, w_r, alpha, seq, tm=1024):
    t, d = x2d.shape
    mlen = kv.shape[1]
    tiles_per_seq = seq // tm
    row = lambda i: (i, 0)

    def whole(shape):
        return pl.BlockSpec(shape, lambda i: (0,) * len(shape), pipeline_mode=pl.Buffered(1))

    vec = whole((1, d))
    return pl.pallas_call(
        functools.partial(_xattn_kernel, alpha),
        grid=(t // tm,),
        in_specs=[pl.BlockSpec((tm, d), row),
                  pl.BlockSpec((tm, y_ret.shape[1]), row),
                  pl.BlockSpec((tm, y_dil.shape[1]), row),
                  whole(w_out.shape), vec, vec,
                  pl.BlockSpec((1, mlen, d), lambda i: (i // tiles_per_seq, 0, 0)),
                  pl.BlockSpec((1, mlen, d), lambda i: (i // tiles_per_seq, 0, 1)),
                  whole((d, d)), whole((d, d)), vec, vec,
                  whole((ROUTE_ROWS, d))],
        out_specs=[pl.BlockSpec((tm, d), row),
                   pl.BlockSpec((tm * SUBLANE, LANE), row),
                   pl.BlockSpec((ROUTE_ROWS, tm), lambda i: (0, i))],
        out_shape=[jax.ShapeDtypeStruct((t, d), F32),
                   jax.ShapeDtypeStruct((t * SUBLANE, LANE), F32),
                   jax.ShapeDtypeStruct((ROUTE_ROWS, t), F32)],
        scratch_shapes=[pltpu.VMEM((tm, d), F32), pltpu.VMEM((tm, d), BF16), pltpu.VMEM((tm, d), BF16)],
        compiler_params=_params("parallel"),
        name="xattn",
    )(x2d, y_ret, y_dil, w_out, g1, b1, kv, kv, wq, wo, g, bb, w_r)


def _route_kernel(lg_ref, before_ref, meta_ref, cnt_ref, carry_ref):
    i = pl.program_id(0)

    @pl.when(i == 0)
    def _():
        carry_ref[...] = jnp.zeros_like(carry_ref)

    rows, tt = lg_ref.shape
    sub = before_ref.shape[0]
    r = lax.broadcasted_iota(jnp.int32, (rows, sub), 0)
    r8 = lax.broadcasted_iota(jnp.int32, (SUBLANE, sub), 0)
    is_group = r < N_GROUPS

    def col_max(a):
        return jnp.max(a, axis=0, keepdims=True)

    def first_row_where(mask):
        return jnp.min(jnp.where(mask, r, rows), axis=0, keepdims=True)

    for c in range(tt // sub):
        lg = lg_ref[:, c * sub:(c + 1) * sub]
        mg = col_max(jnp.where(is_group, lg, NEG))
        eg = jnp.where(is_group, jnp.exp(lg - mg), 0.0)
        pg = eg / jnp.sum(eg, axis=0, keepdims=True)
        g1 = col_max(pg)
        gi = first_row_where(is_group & (pg == g1))
        lo = N_GROUPS + gi * EXPERTS_PER_GROUP
        in_grp = (r >= lo) & (r < lo + EXPERTS_PER_GROUP)
        v1 = col_max(jnp.where(in_grp, lg, NEG))
        i1 = first_row_where(in_grp & (lg == v1))
        rest = in_grp & (r != i1)
        v2 = col_max(jnp.where(rest, lg, NEG))
        i2 = first_row_where(rest & (lg == v2))
        e2 = jnp.exp(v2 - v1)
        den = 1.0 + e2
        gate1 = g1 * (1.0 / den)
        gate2 = g1 * (e2 / den)
        sel1 = r == i1
        sel2 = r == i2
        onehot = jnp.where(sel1 | sel2, 1.0, 0.0)
        rank = _dot(onehot, before_ref[...]) + carry_ref[:, 0:1]
        r1 = jnp.sum(jnp.where(sel1, rank, 0.0), axis=0, keepdims=True)
        r2 = jnp.sum(jnp.where(sel2, rank, 0.0), axis=0, keepdims=True)
        carry_ref[...] = carry_ref[...] + jnp.sum(onehot, axis=1, keepdims=True)
        meta = jnp.where(r8 == 0, (i1 - N_GROUPS).astype(F32), 0.0)
        meta = jnp.where(r8 == 1, (i2 - N_GROUPS).astype(F32), meta)
        meta = jnp.where(r8 == 2, r1, meta)
        meta = jnp.where(r8 == 3, r2, meta)
        meta = jnp.where(r8 == 4, gate1, meta)
        meta = jnp.where(r8 == 5, gate2, meta)
        meta_ref[:, c * sub:(c + 1) * sub] = meta
    cnt_ref[...] = carry_ref[...]


def _route(logits_t, tt=1024, sub=256):
    rows, t = logits_t.shape
    before = jnp.asarray(np.arange(sub)[:, None] < np.arange(sub)[None, :], BF16)
    return pl.pallas_call(
        _route_kernel,
        grid=(t // tt,),
        in_specs=[pl.BlockSpec((rows, tt), lambda i: (0, i)),
                  pl.BlockSpec((sub, sub), lambda i: (0, 0))],
        out_specs=[pl.BlockSpec((SUBLANE, tt), lambda i: (0, i)),
                   pl.BlockSpec((rows, LANE), lambda i: (0, 0))],
        out_shape=[jax.ShapeDtypeStruct((SUBLANE, t), F32),
                   jax.ShapeDtypeStruct((rows, LANE), F32)],
        scratch_shapes=[pltpu.VMEM((rows, LANE), F32)],
        compiler_params=_params("arbitrary"),
        name="route",
    )(logits_t, before)


def _plan_kernel(meta_ref, first_row_ref, dest_ref):
    n_tiles = dest_ref.shape[0]
    tt = dest_ref.shape[2] // 2
    rows = first_row_ref.shape[0]
    first_row = first_row_ref[:, 0:1]
    r = lax.broadcasted_iota(jnp.int32, (rows, tt), 0)
    for g in range(n_tiles):
        m = meta_ref[:, g * tt:(g + 1) * tt]

        def dest_of(k):
            e_row = m[k:k + 1, :].astype(jnp.int32) + N_GROUPS
            return jnp.sum(jnp.where(r == e_row, first_row, 0.0), axis=0, keepdims=True) + m[2 + k:3 + k, :]

        dest_ref[g] = jnp.concatenate([dest_of(0), dest_of(1)], axis=1).astype(jnp.int32)


def _plan(meta_t, first_row, tt, tiles_per_step=4):
    t = meta_t.shape[1]
    rows = first_row.shape[0]
    return pl.pallas_call(
        _plan_kernel,
        grid=(t // (tt * tiles_per_step),),
        in_specs=[pl.BlockSpec((SUBLANE, tt * tiles_per_step), lambda i: (0, i)),
                  pl.BlockSpec((rows, LANE), lambda i: (0, 0))],
        out_specs=pl.BlockSpec((tiles_per_step, 1, 2 * tt), lambda i: (i, 0, 0)),
        out_shape=jax.ShapeDtypeStruct((t // tt, 1, 2 * tt), jnp.int32),
        compiler_params=_params("parallel"),
        name="plan",
    )(meta_t, first_row)


def _row_copy(src, dst, s_row, d_row, sem):
    return pltpu.make_async_copy(src.at[pl.ds(pl.multiple_of(s_row * SUBLANE, SUBLANE), SUBLANE), :],
                                 dst.at[pl.ds(pl.multiple_of(d_row * SUBLANE, SUBLANE), SUBLANE), :], sem)


def _dispatch_kernel(dest_ref, xt_ref, xs_ref, sem):
    tq = dest_ref.shape[2] // 2

    def issue(t, c):
        _row_copy(xt_ref, xs_ref, t, dest_ref[0, 0, t], sem).start(priority=0)
        _row_copy(xt_ref, xs_ref, t, dest_ref[0, 0, tq + t], sem).start(priority=1)
        return c

    lax.fori_loop(0, tq, issue, 0, unroll=8)
    for _ in range(2):
        pltpu.make_async_copy(xt_ref, xs_ref.at[pl.ds(0, tq * SUBLANE), :], sem).wait()


def _dispatch(dest3, x_tiled, n_rows, tiles_per_step=16):
    n_tiles, _, two_tt = dest3.shape
    tt = two_tt // 2
    tq = tiles_per_step * tt
    n_steps = n_tiles // tiles_per_step
    dest3 = dest3.reshape(n_steps, tiles_per_step, 2, tt).transpose(0, 2, 1, 3).reshape(n_steps, 1, 2 * tq)
    return pl.pallas_call(
        _dispatch_kernel,
        grid=(n_steps,),
        in_specs=[pl.BlockSpec((1, 1, 2 * tq), lambda i: (i, 0, 0), memory_space=pltpu.SMEM),
                  pl.BlockSpec((tq * SUBLANE, LANE), lambda i: (i, 0))],
        out_specs=pl.BlockSpec(memory_space=pl.ANY),
        out_shape=jax.ShapeDtypeStruct((n_rows * SUBLANE, LANE), F32),
        scratch_shapes=[pltpu.SemaphoreType.DMA(())],
        compiler_params=_params("arbitrary"),
        name="dispatch",
    )(dest3, x_tiled)


def _experts_kernel(blk_exp_ref, blk_rows_ref, n_used_ref, blk_first_ref, blk_slot_ref, blk_next_ref,
                    xs_ref, wg_hbm, wu_hbm, wd_hbm, ys_ref, x_scr, y_scr, wg_buf, wu_buf, wd_buf, sems):
    b = pl.program_id(0)

    def weight_copies(e, slot):
        return [pltpu.make_async_copy(hbm.at[e], buf.at[slot], sems.at[k, slot])
                for k, (hbm, buf) in enumerate(((wg_hbm, wg_buf), (wu_hbm, wu_buf), (wd_hbm, wd_buf)))]

    @pl.when(b < n_used_ref[0])
    def _():
        slot = blk_slot_ref[b]

        @pl.when(b == 0)
        def _():
            for cp in weight_copies(blk_exp_ref[0], slot):
                cp.start()

        @pl.when(blk_first_ref[b] == 1)
        def _():
            @pl.when(blk_next_ref[b] >= 0)
            def _():
                for cp in weight_copies(blk_next_ref[b], 1 - slot):
                    cp.start()

            for cp in weight_copies(blk_exp_ref[b], slot):
                cp.wait()

        rows, d = x_scr.shape
        n_valid = blk_rows_ref[b]

        def run(m):
            ri = lax.broadcasted_iota(jnp.int32, (m, 1), 0)
            for c in range(d // LANE):
                x = xs_ref[pl.ds(c, m, stride=SUBLANE), :]
                x_scr[0:m, c * LANE:(c + 1) * LANE] = jnp.where(ri < n_valid, x, 0.0).astype(BF16)
            x = x_scr[0:m, :]
            hg = _dot(x, wg_buf[slot])
            hu = _dot(x, wu_buf[slot])
            y_scr[0:m, :] = _dot(hg * jax.nn.sigmoid(hg) * hu, wd_buf[slot])
            for c in range(d // LANE):
                ys_ref[pl.ds(c, m, stride=SUBLANE), :] = y_scr[0:m, c * LANE:(c + 1) * LANE]

        n_quanta = rows // EXPERT_QUANTUM
        for k in range(1, n_quanta + 1):
            lo = (k - 1) * EXPERT_QUANTUM
            pl.when((n_valid > lo) & (n_valid <= k * EXPERT_QUANTUM))(functools.partial(run, k * EXPERT_QUANTUM))


def _experts(blk_exp, blk_rows, n_used, blk_first, blk_slot, blk_next, xs_tiled, w_g, w_u, w_d):
    n_rows = xs_tiled.shape[0] // SUBLANE
    rows = EXPERT_ROWS
    nblk = n_rows // rows
    _, d, ff = w_g.shape
    used = lambda b, be, br, nu, *_: (jnp.minimum(b, nu[0] - 1), 0)
    grid_spec = pltpu.PrefetchScalarGridSpec(
        num_scalar_prefetch=6,
        grid=(nblk,),
        in_specs=[pl.BlockSpec((rows * SUBLANE, LANE), used),
                  pl.BlockSpec(memory_space=pl.ANY), pl.BlockSpec(memory_space=pl.ANY),
                  pl.BlockSpec(memory_space=pl.ANY)],
        out_specs=pl.BlockSpec((rows * SUBLANE, LANE), used),
        scratch_shapes=[pltpu.VMEM((rows, d), BF16), pltpu.VMEM((rows, d), F32),
                        pltpu.VMEM((2, d, ff), F32), pltpu.VMEM((2, d, ff), F32), pltpu.VMEM((2, ff, d), F32),
                        pltpu.SemaphoreType.DMA((3, 2))],
    )
    return pl.pallas_call(
        _experts_kernel,
        grid_spec=grid_spec,
        out_shape=jax.ShapeDtypeStruct((n_rows * SUBLANE, LANE), F32),
        compiler_params=_params("arbitrary"),
        name="experts",
    )(blk_exp, blk_rows, n_used, blk_first, blk_slot, blk_next, xs_tiled, w_g, w_u, w_d)


def _combine_kernel(alpha, dcur_ref, dnext_ref, meta_ref, x_ref, g_ref, b_ref, ys_ref, o_ref, buf_ref, sems):
    i = pl.program_id(0)
    tq, d = x_ref.shape
    slot_rows = 2 * tq * SUBLANE

    def issue(d_ref, slot, t):
        _row_copy(ys_ref, buf_ref, d_ref[0, 0, t], slot * (2 * tq) + t, sems.at[slot]).start(priority=0)
        _row_copy(ys_ref, buf_ref, d_ref[0, 0, tq + t], slot * (2 * tq) + tq + t, sems.at[slot]).start(priority=1)

    def wait_slot(slot):
        off = pl.multiple_of(slot * slot_rows, slot_rows)
        pltpu.make_async_copy(ys_ref.at[pl.ds(0, slot_rows), :], buf_ref.at[pl.ds(off, slot_rows), :],
                              sems.at[slot]).wait()
        return off

    def issue_all(d_ref, slot):
        lax.fori_loop(0, tq, lambda t, c: (issue(d_ref, slot, t), c)[1], 0, unroll=4)

    slot = i % 2

    @pl.when(i == 0)
    def _():
        issue_all(dcur_ref, 0)

    @pl.when(i + 1 < pl.num_programs(0))
    def _():
        issue_all(dnext_ref, 1 - slot)

    off = wait_slot(slot)
    meta_rows = jnp.transpose(jnp.concatenate([meta_ref[...], jnp.zeros((LANE - SUBLANE, tq), F32)], axis=0))
    gate1 = meta_rows[:, 4:5]
    gate2 = meta_rows[:, 5:6]
    for c in range(d // LANE):
        sl = slice(c * LANE, (c + 1) * LANE)
        y1 = buf_ref[pl.ds(off + c, tq, stride=SUBLANE), :]
        y2 = buf_ref[pl.ds(off + tq * SUBLANE + c, tq, stride=SUBLANE), :]
        o_ref[:, sl] = alpha * x_ref[:, sl] + (y1 * gate1 + y2 * gate2)
    o_ref[...] = _layer_norm(o_ref[...], g_ref[...], b_ref[...])


def _combine(dest3, meta, x2, g, bb, ys_tiled, alpha):
    t, d = x2.shape
    n = dest3.shape[0]
    tq = t // n
    row = lambda i: (i, 0)
    const = lambda i: (0, 0)
    return pl.pallas_call(
        functools.partial(_combine_kernel, alpha),
        grid=(n,),
        in_specs=[pl.BlockSpec((1, 1, 2 * tq), lambda i: (i, 0, 0), memory_space=pltpu.SMEM),
                  pl.BlockSpec((1, 1, 2 * tq), lambda i: (jnp.minimum(i + 1, n - 1), 0, 0), memory_space=pltpu.SMEM),
                  pl.BlockSpec((SUBLANE, tq), lambda i: (0, i)),
                  pl.BlockSpec((tq, d), row),
                  pl.BlockSpec((1, d), const), pl.BlockSpec((1, d), const),
                  pl.BlockSpec(memory_space=pl.ANY)],
        out_specs=pl.BlockSpec((tq, d), row),
        out_shape=jax.ShapeDtypeStruct((t, d), F32),
        scratch_shapes=[pltpu.VMEM((2 * 2 * tq * SUBLANE, LANE), F32), pltpu.SemaphoreType.DMA((2,))],
        compiler_params=_params("arbitrary"),
        name="combine",
    )(dest3, dest3, meta, x2, g, bb, ys_tiled)


def _moe(x2, x2_tiled, logits, w_g, w_u, w_d, g, bb, alpha):
    t, d = x2.shape
    meta, cnt = _route(logits)
    counts = cnt[N_GROUPS:N_GROUPS + N_EXPERTS, 0].astype(jnp.int32)
    padded = ((counts + EXPERT_ROWS - 1) // EXPERT_ROWS) * EXPERT_ROWS
    pend = jnp.cumsum(padded)
    poff = pend - padded
    first_row = jnp.pad(poff.astype(F32), (N_GROUPS, ROUTE_ROWS - N_GROUPS - N_EXPERTS))
    dest3 = _plan(meta, jnp.broadcast_to(first_row[:, None], (ROUTE_ROWS, LANE)), ROUTE_TILE)
    n_rows = t * 2 + N_EXPERTS * EXPERT_ROWS
    nblk = n_rows // EXPERT_ROWS
    blk_start = jnp.arange(nblk, dtype=jnp.int32) * EXPERT_ROWS
    blk_exp = jnp.sum((pend[None, :] <= blk_start[:, None]).astype(jnp.int32), axis=1)
    blk_exp = jnp.minimum(blk_exp, N_EXPERTS - 1)
    e_idx = jnp.arange(N_EXPERTS, dtype=jnp.int32)
    owner = blk_exp[:, None] == e_idx[None, :]

    def of_block(per_expert):
        return jnp.sum(jnp.where(owner, per_expert[None, :], 0), axis=1).astype(jnp.int32)

    blk_rows = jnp.clip(of_block(poff + counts) - blk_start, 0, EXPERT_ROWS).astype(jnp.int32)
    n_used = (pend[-1:] // EXPERT_ROWS).astype(jnp.int32)
    has_rows = counts > 0
    ordinal = jnp.cumsum(has_rows.astype(jnp.int32)) - 1
    later = has_rows[None, :] & (e_idx[None, :] > e_idx[:, None])
    next_used = jnp.min(jnp.where(later, e_idx[None, :], N_EXPERTS), axis=1)
    next_used = jnp.where(next_used == N_EXPERTS, -1, next_used).astype(jnp.int32)
    blk_first = (blk_start == of_block(poff)).astype(jnp.int32)
    blk_slot = of_block(ordinal % 2)
    blk_next = of_block(next_used)
    xs_tiled = _dispatch(dest3, x2_tiled, n_rows)
    ys_tiled = _experts(blk_exp, blk_rows, n_used, blk_first, blk_slot, blk_next, xs_tiled, w_g, w_u, w_d)
    return _combine(dest3, meta, x2, g, bb, ys_tiled, alpha)


def kernel(x, mem, w_in, w_out, ln_mix_g, ln_mix_b, w_xq, w_xkv, w_xo, ln_x_g, ln_x_b, w_route_group,
           w_route_expert, w_exp_gate, w_exp_up, w_exp_down, ln_moe_g, ln_moe_b):
    b, s, d = x.shape
    depth = w_in.shape[0]
    alpha = (2.0 * depth) ** 0.25
    t = b * s
    ret_cols = (2 * RET_HEADS * RET_QK_DIM + 2 * RET_HEADS * RET_V_DIM)
    xc = x.reshape(t, d)
    for l in range(depth):
        n_qk = 2 * RET_HEADS * RET_QK_DIM
        h_ret, h_dil = _proj_in(xc, w_in[l], _retention_weight_layout(w_in[l][:, :n_qk]).astype(BF16), ret_cols)
        y_ret = _retention(h_ret.reshape(b, s, ret_cols)).reshape(t, -1)
        y_dil = _dilated(h_dil, b).reshape(t, -1)
        kv = _mem_kv(mem.reshape(b * mem.shape[1], d), w_xkv[l]).reshape(b, mem.shape[1], 2 * d)
        w_r = jnp.concatenate([w_route_group[l], w_route_expert[l]], axis=-1)
        w_r = jnp.pad(w_r.T, ((0, ROUTE_ROWS - w_r.shape[1]), (0, 0))).astype(BF16)
        x2, x2_tiled, logits = _xattn(xc, y_ret, y_dil, w_out[l].astype(BF16), ln_mix_g[l][None], ln_mix_b[l][None],
                                      kv, w_xq[l].astype(BF16), w_xo[l].astype(BF16),
                                      ln_x_g[l][None], ln_x_b[l][None], w_r, alpha, s)
        xc = _moe(x2, x2_tiled, logits, w_exp_gate[l], w_exp_up[l], w_exp_down[l],
                  ln_moe_g[l][None], ln_moe_b[l][None], alpha)
    return xc.reshape(b, s, d)
```

```python
import functools
import math

import jax
import jax.numpy as jnp
import numpy as np
from jax import lax
from jax.experimental import pallas as pl
from jax.experimental.pallas import tpu as pltpu

BF16 = jnp.bfloat16
F32 = jnp.float32

LANE = 128
SUBLANE = 8
VMEM_LIMIT = 56 * 1024 * 1024

RET_HEADS = 4
RET_QK_DIM = 64
RET_V_DIM = 128
RET_CHUNK = 128
RET_STEP_CHUNKS = 4
ROPE_BASE = 10000.0
DIL_HEADS = 8
DIL_HEAD_DIM = 64
DIL_DILATIONS = (1, 4, 16)
DIL_BLOCK = 128
DIL_SUPER = DIL_BLOCK * max(DIL_DILATIONS)
DIL_GROUP = 1
XATTN_HEADS = 4
XATTN_PARTS = 2
N_GROUPS = 4
EXPERTS_PER_GROUP = 8
N_EXPERTS = N_GROUPS * EXPERTS_PER_GROUP
ROUTE_ROWS = -(-(N_GROUPS + N_EXPERTS) // SUBLANE) * SUBLANE
EXPERT_ROWS = 512
EXPERT_QUANTUM = 256
ROUTE_TILE = 512
LN_EPS = 1e-5
GN_EPS = 1e-6
NEG = -1e30


def _params(*sem):
    return pltpu.CompilerParams(dimension_semantics=sem, vmem_limit_bytes=VMEM_LIMIT)


def _layer_norm(z, g, b):
    mu = jnp.mean(z, axis=-1, keepdims=True)
    zc = z - mu
    var = jnp.mean(zc * zc, axis=-1, keepdims=True)
    return zc * lax.rsqrt(var + LN_EPS) * g + b


def _dot(a, b):
    return jnp.dot(a.astype(BF16), b.astype(BF16), preferred_element_type=F32)


def _dot_nt(a, b):
    return lax.dot_general(a.astype(BF16), b.astype(BF16), (((1,), (1,)), ((), ())),
                           preferred_element_type=F32)


def _dot_tn(a, b):
    return lax.dot_general(a.astype(BF16), b.astype(BF16), (((0,), (0,)), ((), ())),
                           preferred_element_type=F32)


def _proj_in_kernel(x_ref, w_ref, o_ref):
    o_ref[...] = _dot(x_ref[...], w_ref[...]).astype(o_ref.dtype)


def _proj_in2_kernel(x_ref, w_ref, w_head_ref, o_ref, slab_ref, w_bf16):
    @pl.when(pl.program_id(0) == 0)
    def _():
        w_bf16[...] = w_ref[...].astype(BF16)
        w_bf16[:, 0:w_head_ref.shape[1]] = w_head_ref[...]

    n_row = o_ref.shape[1]
    x = x_ref[...].astype(BF16)
    o_ref[...] = jnp.dot(x, w_bf16[:, 0:n_row], preferred_element_type=F32)
    rest = jnp.dot(x, w_bf16[:, n_row:], preferred_element_type=F32)
    for c in range(slab_ref.shape[0]):
        slab_ref[c] = rest[:, c * LANE:(c + 1) * LANE]


def _proj_in(x2d, w_f32, w_head_bf16, n_row, tm=512):
    t, d = x2d.shape
    n = w_f32.shape[1]
    n_slabs = (n - n_row) // LANE
    whole = lambda shape: pl.BlockSpec(shape, lambda i: (0, 0), pipeline_mode=pl.Buffered(1))
    return pl.pallas_call(
        _proj_in2_kernel,
        grid=(t // tm,),
        in_specs=[pl.BlockSpec((tm, d), lambda i: (i, 0)), whole((d, n)), whole(w_head_bf16.shape)],
        out_specs=[pl.BlockSpec((tm, n_row), lambda i: (i, 0)),
                   pl.BlockSpec((n_slabs, tm, LANE), lambda i: (0, i, 0))],
        out_shape=[jax.ShapeDtypeStruct((t, n_row), F32),
                   jax.ShapeDtypeStruct((n_slabs, t, LANE), F32)],
        scratch_shapes=[pltpu.VMEM((d, n), BF16)],
        compiler_params=_params("arbitrary"),
        name="proj_in",
    )(x2d, w_f32, w_head_bf16)


def _retention_kernel(qk_ref, v_ref, g_ref, cos_ref, sin_ref, decay_ref, zeta_ref, xi_ref, gam_ref,
                      o_ref, state_ref, y_ref):
    n = pl.program_id(1)

    @pl.when(n == 0)
    def _():
        state_ref[...] = jnp.zeros_like(state_ref)

    c = RET_CHUNK
    n_sub = qk_ref.shape[1] // c
    half = RET_QK_DIM // 2
    lane = lax.broadcasted_iota(jnp.int32, (c, LANE), 1)

    rotated = []
    for j in range(n_sub):
        rows = slice(j * c, (j + 1) * c)
        cos = cos_ref[rows, :]
        sin = sin_ref[rows, :]

        def rot(col, rows=rows, cos=cos, sin=sin):
            t1 = qk_ref[0, rows, col * LANE:(col + 1) * LANE]
            t2 = qk_ref[0, rows, (col + 1) * LANE:(col + 2) * LANE]
            return t1 * cos - t2 * sin, t1 * sin + t2 * cos

        q1, q2 = rot(0)
        k1, k2 = (t * (RET_QK_DIM ** -0.5) for t in rot(2))
        rotated.append((q1, q2, k1, k2, jnp.concatenate([k1, k2], axis=1)))

    for h in range(RET_HEADS):
        cols = slice(h * RET_V_DIM, (h + 1) * RET_V_DIM)
        mine = (lane >= h * half) & (lane < (h + 1) * half)
        zeta = zeta_ref[h]
        st = state_ref[h]
        pad_lo = [jnp.zeros((h * half, RET_V_DIM), BF16)] if h > 0 else []
        pad_hi = [jnp.zeros(((RET_HEADS - 1 - h) * half, RET_V_DIM), BF16)] if h < RET_HEADS - 1 else []
        for j in range(n_sub):
            rows = slice(j * c, (j + 1) * c)
            q1, q2, k1, k2, k_all = rotated[j]
            qm = jnp.concatenate([jnp.where(mine, q1, 0.0), jnp.where(mine, q2, 0.0)], axis=1)
            v = v_ref[0, rows, cols]
            s = _dot_nt(qm, k_all) * decay_ref[h]
            sb = st.astype(BF16)
            st_rows = jnp.concatenate(pad_lo + [sb[:half]] + pad_hi + pad_lo + [sb[half:]] + pad_hi, axis=0)
            y_ref[rows, cols] = _dot(s, v) + _dot(qm, st_rows) * xi_ref[h]
            kv = _dot_tn(jnp.concatenate([k1 * zeta, k2 * zeta], axis=1), v)
            lo = h * half
            hi = RET_HEADS * half + h * half
            st = gam_ref[h, :, :] * st + jnp.concatenate([kv[lo:lo + half], kv[hi:hi + half]], axis=0)
        state_ref[h] = st

    for h in range(RET_HEADS):
        cols = slice(h * RET_V_DIM, (h + 1) * RET_V_DIM)
        y = y_ref[:, cols]
        mu = jnp.mean(y, axis=-1, keepdims=True)
        yc = y - mu
        var = jnp.mean(yc * yc, axis=-1, keepdims=True)
        yn = yc * lax.rsqrt(var + GN_EPS)
        gate = g_ref[0, :, cols]
        o_ref[0, :, cols] = (gate * jax.nn.sigmoid(gate) * yn).astype(o_ref.dtype)


def _retention_tables(s):
    half = RET_QK_DIM // 2
    inv = 1.0 / (ROPE_BASE ** (np.arange(half, dtype=np.float64) / half))
    ang = np.arange(s, dtype=np.float64)[:, None] * inv[None, :]
    cos_t = np.tile(np.cos(ang), (1, RET_HEADS))
    sin_t = np.tile(np.sin(ang), (1, RET_HEADS))
    c = RET_CHUNK
    lg = np.log(1.0 - np.exp2(-5.0 - np.arange(RET_HEADS, dtype=np.float64)))
    idx = np.arange(c, dtype=np.float64)
    diff = idx[:, None] - idx[None, :]
    decay = np.where(diff >= 0, np.exp(lg[:, None, None] * np.maximum(diff, 0.0)), 0.0)
    lanes = (RET_HEADS, c, LANE)
    zeta = np.broadcast_to(np.exp(lg[:, None] * (c - 1.0 - idx))[:, :, None], lanes)
    xi = np.broadcast_to(np.exp(lg[:, None] * (idx + 1.0))[:, :, None], lanes)
    gam = np.broadcast_to(np.exp(lg * c)[:, None, None], (RET_HEADS, 1, LANE))
    return tuple(jnp.asarray(np.ascontiguousarray(a), F32) for a in (cos_t, sin_t, decay, zeta, xi, gam))


def _retention_weight_layout(w_qk):
    d = w_qk.shape[0]
    half = RET_QK_DIM // 2
    return w_qk.reshape(d, 2, RET_HEADS, 2, half).transpose(0, 1, 3, 2, 4).reshape(d, -1)


def _retention(h3):
    b, s, _ = h3.shape
    c = RET_CHUNK
    qk_w = 2 * RET_HEADS * RET_QK_DIM
    v_w = RET_HEADS * RET_V_DIM
    assert qk_w == v_w
    cos_t, sin_t, decay, zeta, xi, gam = _retention_tables(s)
    const3 = lambda bi, n: (0, 0, 0)
    rows = RET_STEP_CHUNKS * c
    return pl.pallas_call(
        _retention_kernel,
        grid=(b, s // rows),
        in_specs=[pl.BlockSpec((1, rows, qk_w), lambda bi, n: (bi, n, 0)),
                  pl.BlockSpec((1, rows, v_w), lambda bi, n: (bi, n, 1)),
                  pl.BlockSpec((1, rows, v_w), lambda bi, n: (bi, n, 2)),
                  pl.BlockSpec((rows, LANE), lambda bi, n: (n, 0)),
                  pl.BlockSpec((rows, LANE), lambda bi, n: (n, 0)),
                  pl.BlockSpec((RET_HEADS, c, c), const3),
                  pl.BlockSpec((RET_HEADS, c, LANE), const3),
                  pl.BlockSpec((RET_HEADS, c, LANE), const3),
                  pl.BlockSpec((RET_HEADS, 1, LANE), const3)],
        out_specs=pl.BlockSpec((1, rows, v_w), lambda bi, n: (bi, n, 0)),
        out_shape=jax.ShapeDtypeStruct((b, s, v_w), BF16),
        scratch_shapes=[pltpu.VMEM((RET_HEADS, RET_QK_DIM, RET_V_DIM), F32),
                        pltpu.VMEM((rows, v_w), F32)],
        compiler_params=_params("parallel", "arbitrary"),
        name="retention",
    )(h3, h3, h3, cos_t, sin_t, decay, zeta, xi, gam)


def _dilated_kernel(q_ref, kp_ref, kc_ref, vp_ref, vc_ref, bias_ref, o_ref, acc_ref, m_ref, l_ref,
                    s0_ref, s1_ref, p0_ref, p1_ref):
    j = pl.program_id(2)
    sup = DIL_SUPER
    q_blk = DIL_BLOCK
    scale = DIL_HEAD_DIM ** -0.5 * math.log2(math.e)
    n_blocks = sup // q_blk
    lane = lax.broadcasted_iota(jnp.int32, (q_blk, LANE), 1)
    head0 = lane < DIL_HEAD_DIM
    first_bias = jnp.where(j == 0, 1, 0)

    groups = []
    for bi, d in enumerate(DIL_DILATIONS):
        n_per_r = sup // (q_blk * d)
        for t0 in range(0, n_blocks, DIL_GROUP):
            blocks = []
            for t in range(t0, t0 + DIL_GROUP):
                r, n = divmod(t, n_per_r)
                blocks.append((n * (q_blk * d) + r, (n - 1) * (q_blk * d) + r, n))
            groups.append((bi, d, blocks))
    s_bufs = (s0_ref, s1_ref)
    p_bufs = (p0_ref, p1_ref)

    def window(prev_ref, cur_ref, k_start, d):
        if k_start >= 0:
            return cur_ref[0, pl.ds(k_start, 2 * q_blk, stride=d), :]
        return jnp.concatenate([prev_ref[0, pl.ds(sup + k_start, q_blk, stride=d), :],
                                cur_ref[0, pl.ds(k_start + q_blk * d, q_blk, stride=d), :]], axis=0)

    def scores(gi):
        bi, d, blocks = groups[gi]
        for g, (q_start, k_start, n) in enumerate(blocks):
            q = q_ref[0, pl.ds(q_start, q_blk, stride=d), :] * scale
            kb = window(kp_ref, kc_ref, k_start, d)
            q2 = jnp.concatenate([jnp.where(head0, q, 0.0), jnp.where(head0, 0.0, q)], axis=0)
            bias = bias_ref[first_bias] if n == 0 else bias_ref[0]
            s_bufs[gi % 2][g] = _dot_nt(q2, kb) + bias

    def softmax(gi):
        bi, d, blocks = groups[gi]
        for g, (q_start, k_start, n) in enumerate(blocks):
            s = s_bufs[gi % 2][g]
            m2 = jnp.max(s, axis=-1, keepdims=True)
            p_bufs[gi % 2][g] = jnp.exp2(s - m2).astype(BF16)
            m_ref[bi, pl.ds(q_start, q_blk, stride=d), :] = jnp.where(head0, m2[:q_blk], m2[q_blk:])

    def values(gi):
        bi, d, blocks = groups[gi]
        for g, (q_start, k_start, n) in enumerate(blocks):
            vb = window(vp_ref, vc_ref, k_start, d).astype(BF16)
            o2 = jnp.dot(p_bufs[gi % 2][g], jnp.concatenate([vb, jnp.ones_like(vb)], axis=1),
                         preferred_element_type=F32)
            rows = pl.ds(q_start, q_blk, stride=d)
            acc_ref[bi, rows, :] = jnp.where(head0, o2[:q_blk, :LANE], o2[q_blk:, :LANE])
            l_ref[bi, rows, :] = jnp.where(head0, o2[:q_blk, LANE:], o2[q_blk:, LANE:])

    for step in range(len(groups) + 2):
        if step < len(groups):
            scores(step)
        if 0 <= step - 1 < len(groups):
            softmax(step - 1)
        if step - 2 >= 0:
            values(step - 2)

    def merge(c, carry):
        rows = pl.ds(pl.multiple_of(c * q_blk, q_blk), q_blk)
        ms = [m_ref[bi, rows, :] for bi in range(len(DIL_DILATIONS))]
        m_all = functools.reduce(jnp.maximum, ms)
        ws = [jnp.exp2(m - m_all) for m in ms]
        num = functools.reduce(lambda a, b: a + b, [w * acc_ref[bi, rows, :] for bi, w in enumerate(ws)])
        den = functools.reduce(lambda a, b: a + b, [w * l_ref[bi, rows, :] for bi, w in enumerate(ws)])
        o_ref[0, rows, :] = (num / den).astype(o_ref.dtype)
        return carry

    lax.fori_loop(0, n_blocks, merge, 0)


def _dilated_bias():
    q_blk = DIL_BLOCK
    qi = np.arange(2 * q_blk)[:, None] % q_blk
    kj = np.arange(2 * q_blk)[None, :]
    band = (kj >= qi) & (kj <= qi + q_blk)
    return jnp.asarray(np.stack([np.where(band, 0.0, NEG), np.where(band & (kj >= q_blk), 0.0, NEG)]), F32)


def _dilated(slabs, b):
    n_slabs, t, _ = slabs.shape
    s = t // b
    h3 = slabs.reshape(n_slabs * b, s, LANE)
    sup = DIL_SUPER
    n_pairs = DIL_HEADS * DIL_HEAD_DIM // LANE
    n_br = len(DIL_DILATIONS)
    cq, ck, cv = 0, n_pairs, 2 * n_pairs
    cur = lambda c: (lambda bi, p, j: ((c + p) * b + bi, j, 0))
    prev = lambda c: (lambda bi, p, j: ((c + p) * b + bi, jnp.maximum(j - 1, 0), 0))
    blk = (1, sup, LANE)
    return pl.pallas_call(
        _dilated_kernel,
        grid=(b, n_pairs, s // sup),
        in_specs=[pl.BlockSpec(blk, cur(cq)),
                  pl.BlockSpec(blk, prev(ck)), pl.BlockSpec(blk, cur(ck)),
                  pl.BlockSpec(blk, prev(cv)), pl.BlockSpec(blk, cur(cv)),
                  pl.BlockSpec((2, 2 * DIL_BLOCK, 2 * DIL_BLOCK), lambda bi, p, j: (0, 0, 0))],
        out_specs=pl.BlockSpec(blk, lambda bi, p, j: (bi, j, p)),
        out_shape=jax.ShapeDtypeStruct((b, s, n_pairs * LANE), BF16),
        scratch_shapes=[pltpu.VMEM((n_br, sup, LANE), F32), pltpu.VMEM((n_br, sup, LANE), F32),
                        pltpu.VMEM((n_br, sup, LANE), F32),
                        pltpu.VMEM((DIL_GROUP, 2 * DIL_BLOCK, 2 * DIL_BLOCK), F32),
                        pltpu.VMEM((DIL_GROUP, 2 * DIL_BLOCK, 2 * DIL_BLOCK), F32),
                        pltpu.VMEM((DIL_GROUP, 2 * DIL_BLOCK, 2 * DIL_BLOCK), BF16),
                        pltpu.VMEM((DIL_GROUP, 2 * DIL_BLOCK, 2 * DIL_BLOCK), BF16)],
        compiler_params=_params("parallel", "parallel", "arbitrary"),
        name="dilated",
    )(h3, h3, h3, h3, h3, _dilated_bias())


def _mem_kv(mem2d, w_bf16, tn=512):
    m, d = mem2d.shape
    n = w_bf16.shape[1]
    return pl.pallas_call(
        _proj_in_kernel,
        grid=(n // tn,),
        in_specs=[pl.BlockSpec((m, d), lambda i: (0, 0)),
                  pl.BlockSpec((d, tn), lambda i: (0, i))],
        out_specs=pl.BlockSpec((m, tn), lambda i: (0, i)),
        out_shape=jax.ShapeDtypeStruct((m, n), BF16),
        compiler_params=_params("parallel"),
        name="mem_kv",
    )(mem2d, w_bf16)


def _xattn_kernel(alpha, x_ref, yr_ref, yd_ref, wout_ref, g1_ref, b1_ref, k_ref, v_ref, wq_ref, wo_ref,
                  g_ref, b_ref, wr_ref, o_ref, ot_ref, lg_ref, x1_ref, q_ref, att_ref):
    tm, d = x_ref.shape
    dh = d // XATTN_HEADS
    wr = yr_ref.shape[1]
    scale = dh ** -0.5 * math.log2(math.e)
    part = tm // XATTN_PARTS

    def mix(rows):
        y = _dot(yr_ref[rows, :], wout_ref[0:wr, :]) + _dot(yd_ref[rows, :], wout_ref[wr:, :])
        x1_ref[rows, :] = _layer_norm(alpha * x_ref[rows, :] + y, g1_ref[...], b1_ref[...])

    def query(rows):
        q_ref[rows, :] = _dot(x1_ref[rows, :], wq_ref[...]).astype(BF16)

    def attend(rows):
        for h in range(XATTN_HEADS):
            sl = slice(h * dh, (h + 1) * dh)
            s = _dot_nt(q_ref[rows, sl], k_ref[0, :, sl]) * scale
            m = jnp.max(s, axis=-1, keepdims=True)
            e = jnp.exp2(s - m)
            p = e / jnp.sum(e, axis=-1, keepdims=True)
            att_ref[rows, sl] = _dot(p, v_ref[0, :, sl]).astype(BF16)

    def finish(rows, r0):
        y = _dot(att_ref[rows, :], wo_ref[...])
        x2 = _layer_norm(alpha * x1_ref[rows, :] + y, g_ref[...], b_ref[...])
        o_ref[rows, :] = x2
        for c in range(d // LANE):
            ot_ref[pl.ds(r0 * SUBLANE + c, part, stride=SUBLANE), :] = x2[:, c * LANE:(c + 1) * LANE]
        lg_ref[:, rows] = _dot_nt(wr_ref[...], x2)

    stages = (mix, query, attend, finish)
    for step in range(XATTN_PARTS + len(stages) - 1):
        for si, stage in enumerate(stages):
            pi = step - si
            if 0 <= pi < XATTN_PARTS:
                rows = slice(pi * part, (pi + 1) * part)
                if stage is finish:
                    stage(rows, pi * part)
                else:
                    stage(rows)


def _xattn(x2d, y_ret, y_dil, w_out, g1, b1, kv, wq, wo, g, bb, w_r, alpha, seq, tm=1024):
    t, d = x2d.shape
    mlen = kv.shape[1]
    tiles_per_seq = seq // tm
    row = lambda i: (i, 0)

    def whole(shape):
        return pl.BlockSpec(shape, lambda i: (0,) * len(shape), pipeline_mode=pl.Buffered(1))

    vec = whole((1, d))
    return pl.pallas_call(
        functools.partial(_xattn_kernel, alpha),
        grid=(t // tm,),
        in_specs=[pl.BlockSpec((tm, d), row),
                  pl.BlockSpec((tm, y_ret.shape[1]), row),
                  pl.BlockSpec((tm, y_dil.shape[1]), row),
                  whole(w_out.shape), vec, vec,
                  pl.BlockSpec((1, mlen, d), lambda i: (i // tiles_per_seq, 0, 0)),
                  pl.BlockSpec((1, mlen, d), lambda i: (i // tiles_per_seq, 0, 1)),
                  whole((d, d)), whole((d, d)), vec, vec,
                  whole((ROUTE_ROWS, d))],
        out_specs=[pl.BlockSpec((tm, d), row),
                   pl.BlockSpec((tm * SUBLANE, LANE), row),
                   pl.BlockSpec((ROUTE_ROWS, tm), lambda i: (0, i))],
        out_shape=[jax.ShapeDtypeStruct((t, d), F32),
                   jax.ShapeDtypeStruct((t * SUBLANE, LANE), F32),
                   jax.ShapeDtypeStruct((ROUTE_ROWS, t), F32)],
        scratch_shapes=[pltpu.VMEM((tm, d), F32), pltpu.VMEM((tm, d), BF16), pltpu.VMEM((tm, d), BF16)],
        compiler_params=_params("parallel"),
        name="xattn",
    )(x2d, y_ret, y_dil, w_out, g1, b1, kv, kv, wq, wo, g, bb, w_r)


def _route_kernel(lg_ref, before_ref, meta_ref, cnt_ref, carry_ref):
    i = pl.program_id(0)

    @pl.when(i == 0)
    def _():
        carry_ref[...] = jnp.zeros_like(carry_ref)

    rows, tt = lg_ref.shape
    sub = before_ref.shape[0]
    r = lax.broadcasted_iota(jnp.int32, (rows, sub), 0)
    r8 = lax.broadcasted_iota(jnp.int32, (SUBLANE, sub), 0)
    is_group = r < N_GROUPS

    def col_max(a):
        return jnp.max(a, axis=0, keepdims=True)

    def first_row_where(mask):
        return jnp.min(jnp.where(mask, r, rows), axis=0, keepdims=True)

    for c in range(tt // sub):
        lg = lg_ref[:, c * sub:(c + 1) * sub]
        mg = col_max(jnp.where(is_group, lg, NEG))
        eg = jnp.where(is_group, jnp.exp(lg - mg), 0.0)
        pg = eg / jnp.sum(eg, axis=0, keepdims=True)
        g1 = col_max(pg)
        gi = first_row_where(is_group & (pg == g1))
        lo = N_GROUPS + gi * EXPERTS_PER_GROUP
        in_grp = (r >= lo) & (r < lo + EXPERTS_PER_GROUP)
        v1 = col_max(jnp.where(in_grp, lg, NEG))
        i1 = first_row_where(in_grp & (lg == v1))
        rest = in_grp & (r != i1)
        v2 = col_max(jnp.where(rest, lg, NEG))
        i2 = first_row_where(rest & (lg == v2))
        e2 = jnp.exp(v2 - v1)
        den = 1.0 + e2
        gate1 = g1 * (1.0 / den)
        gate2 = g1 * (e2 / den)
        sel1 = r == i1
        sel2 = r == i2
        onehot = jnp.where(sel1 | sel2, 1.0, 0.0)
        rank = _dot(onehot, before_ref[...]) + carry_ref[:, 0:1]
        r1 = jnp.sum(jnp.where(sel1, rank, 0.0), axis=0, keepdims=True)
        r2 = jnp.sum(jnp.where(sel2, rank, 0.0), axis=0, keepdims=True)
        carry_ref[...] = carry_ref[...] + jnp.sum(onehot, axis=1, keepdims=True)
        meta = jnp.where(r8 == 0, (i1 - N_GROUPS).astype(F32), 0.0)
        meta = jnp.where(r8 == 1, (i2 - N_GROUPS).astype(F32), meta)
        meta = jnp.where(r8 == 2, r1, meta)
        meta = jnp.where(r8 == 3, r2, meta)
        meta = jnp.where(r8 == 4, gate1, meta)
        meta = jnp.where(r8 == 5, gate2, meta)
        meta_ref[:, c * sub:(c + 1) * sub] = meta
    cnt_ref[...] = carry_ref[...]


def _route(logits_t, tt=1024, sub=256):
    rows, t = logits_t.shape
    before = jnp.asarray(np.arange(sub)[:, None] < np.arange(sub)[None, :], BF16)
    return pl.pallas_call(
        _route_kernel,
        grid=(t // tt,),
        in_specs=[pl.BlockSpec((rows, tt), lambda i: (0, i)),
                  pl.BlockSpec((sub, sub), lambda i: (0, 0))],
        out_specs=[pl.BlockSpec((SUBLANE, tt), lambda i: (0, i)),
                   pl.BlockSpec((rows, LANE), lambda i: (0, 0))],
        out_shape=[jax.ShapeDtypeStruct((SUBLANE, t), F32),
                   jax.ShapeDtypeStruct((rows, LANE), F32)],
        scratch_shapes=[pltpu.VMEM((rows, LANE), F32)],
        compiler_params=_params("arbitrary"),
        name="route",
    )(logits_t, before)


def _plan_kernel(meta_ref, first_row_ref, dest_ref):
    n_tiles = dest_ref.shape[0]
    tt = dest_ref.shape[2] // 2
    rows = first_row_ref.shape[0]
    first_row = first_row_ref[:, 0:1]
    r = lax.broadcasted_iota(jnp.int32, (rows, tt), 0)
    for g in range(n_tiles):
        m = meta_ref[:, g * tt:(g + 1) * tt]

        def dest_of(k):
            e_row = m[k:k + 1, :].astype(jnp.int32) + N_GROUPS
            return jnp.sum(jnp.where(r == e_row, first_row, 0.0), axis=0, keepdims=True) + m[2 + k:3 + k, :]

        dest_ref[g] = jnp.concatenate([dest_of(0), dest_of(1)], axis=1).astype(jnp.int32)


def _plan(meta_t, first_row, tt, tiles_per_step=4):
    t = meta_t.shape[1]
    rows = first_row.shape[0]
    return pl.pallas_call(
        _plan_kernel,
        grid=(t // (tt * tiles_per_step),),
        in_specs=[pl.BlockSpec((SUBLANE, tt * tiles_per_step), lambda i: (0, i)),
                  pl.BlockSpec((rows, LANE), lambda i: (0, 0))],
        out_specs=pl.BlockSpec((tiles_per_step, 1, 2 * tt), lambda i: (i, 0, 0)),
        out_shape=jax.ShapeDtypeStruct((t // tt, 1, 2 * tt), jnp.int32),
        compiler_params=_params("parallel"),
        name="plan",
    )(meta_t, first_row)


def _row_copy(src, dst, s_row, d_row, sem):
    return pltpu.make_async_copy(src.at[pl.ds(pl.multiple_of(s_row * SUBLANE, SUBLANE), SUBLANE), :],
                                 dst.at[pl.ds(pl.multiple_of(d_row * SUBLANE, SUBLANE), SUBLANE), :], sem)


def _dispatch_kernel(dest_ref, xt_ref, xs_ref, sem):
    tq = dest_ref.shape[2] // 2

    def issue(t, c):
        _row_copy(xt_ref, xs_ref, t, dest_ref[0, 0, t], sem).start(priority=0)
        _row_copy(xt_ref, xs_ref, t, dest_ref[0, 0, tq + t], sem).start(priority=1)
        return c

    lax.fori_loop(0, tq, issue, 0, unroll=8)
    for _ in range(2):
        pltpu.make_async_copy(xt_ref, xs_ref.at[pl.ds(0, tq * SUBLANE), :], sem).wait()


def _dispatch(dest3, x_tiled, n_rows, tq=4096):
    n_tiles, _, two_tt = dest3.shape
    tt = two_tt // 2
    tiles_per_step = tq // tt
    n_steps = n_tiles // tiles_per_step
    dest3 = dest3.reshape(n_steps, tiles_per_step, 2, tt).transpose(0, 2, 1, 3).reshape(n_steps, 1, 2 * tq)
    return pl.pallas_call(
        _dispatch_kernel,
        grid=(n_steps,),
        in_specs=[pl.BlockSpec((1, 1, 2 * tq), lambda i: (i, 0, 0), memory_space=pltpu.SMEM),
                  pl.BlockSpec((tq * SUBLANE, LANE), lambda i: (i, 0))],
        out_specs=pl.BlockSpec(memory_space=pl.ANY),
        out_shape=jax.ShapeDtypeStruct((n_rows * SUBLANE, LANE), F32),
        scratch_shapes=[pltpu.SemaphoreType.DMA(())],
        compiler_params=_params("arbitrary"),
        name="dispatch",
    )(dest3, x_tiled)


def _experts_kernel(blk_exp_ref, blk_rows_ref, n_used_ref, blk_first_ref, blk_slot_ref, blk_next_ref,
                    xs_ref, wg_hbm, wu_hbm, wd_hbm, ys_ref, x_scr, y_scr, wg_buf, wu_buf, wd_buf, sems):
    b = pl.program_id(0)

    def weight_copies(e, slot):
        return [pltpu.make_async_copy(hbm.at[e], buf.at[slot], sems.at[k, slot])
                for k, (hbm, buf) in enumerate(((wg_hbm, wg_buf), (wu_hbm, wu_buf), (wd_hbm, wd_buf)))]

    @pl.when(b < n_used_ref[0])
    def _():
        slot = blk_slot_ref[b]

        @pl.when(b == 0)
        def _():
            for cp in weight_copies(blk_exp_ref[0], slot):
                cp.start()

        @pl.when(blk_first_ref[b] == 1)
        def _():
            @pl.when(blk_next_ref[b] >= 0)
            def _():
                for cp in weight_copies(blk_next_ref[b], 1 - slot):
                    cp.start()

            for cp in weight_copies(blk_exp_ref[b], slot):
                cp.wait()

        rows, d = x_scr.shape
        n_valid = blk_rows_ref[b]

        def run(m):
            ri = lax.broadcasted_iota(jnp.int32, (m, 1), 0)
            for c in range(d // LANE):
                x = xs_ref[pl.ds(c, m, stride=SUBLANE), :]
                x_scr[0:m, c * LANE:(c + 1) * LANE] = jnp.where(ri < n_valid, x, 0.0).astype(BF16)
            x = x_scr[0:m, :]
            hg = _dot(x, wg_buf[slot])
            hu = _dot(x, wu_buf[slot])
            y_scr[0:m, :] = _dot(hg * jax.nn.sigmoid(hg) * hu, wd_buf[slot])
            for c in range(d // LANE):
                ys_ref[pl.ds(c, m, stride=SUBLANE), :] = y_scr[0:m, c * LANE:(c + 1) * LANE]

        n_quanta = rows // EXPERT_QUANTUM
        for k in range(1, n_quanta + 1):
            lo = (k - 1) * EXPERT_QUANTUM
            pl.when((n_valid > lo) & (n_valid <= k * EXPERT_QUANTUM))(functools.partial(run, k * EXPERT_QUANTUM))


def _experts(blk_exp, blk_rows, n_used, blk_first, blk_slot, blk_next, xs_tiled, w_g, w_u, w_d):
    n_rows = xs_tiled.shape[0] // SUBLANE
    rows = EXPERT_ROWS
    nblk = n_rows // rows
    _, d, ff = w_g.shape
    used = lambda b, be, br, nu, *_: (jnp.minimum(b, nu[0] - 1), 0)
    grid_spec = pltpu.PrefetchScalarGridSpec(
        num_scalar_prefetch=6,
        grid=(nblk,),
        in_specs=[pl.BlockSpec((rows * SUBLANE, LANE), used),
                  pl.BlockSpec(memory_space=pl.ANY), pl.BlockSpec(memory_space=pl.ANY),
                  pl.BlockSpec(memory_space=pl.ANY)],
        out_specs=pl.BlockSpec((rows * SUBLANE, LANE), used),
        scratch_shapes=[pltpu.VMEM((rows, d), BF16), pltpu.VMEM((rows, d), F32),
                        pltpu.VMEM((2, d, ff), F32), pltpu.VMEM((2, d, ff), F32), pltpu.VMEM((2, ff, d), F32),
                        pltpu.SemaphoreType.DMA((3, 2))],
    )
    return pl.pallas_call(
        _experts_kernel,
        grid_spec=grid_spec,
        out_shape=jax.ShapeDtypeStruct((n_rows * SUBLANE, LANE), F32),
        compiler_params=_params("arbitrary"),
        name="experts",
    )(blk_exp, blk_rows, n_used, blk_first, blk_slot, blk_next, xs_tiled, w_g, w_u, w_d)


def _combine_kernel(alpha, dcur_ref, dnext_ref, meta_ref, x_ref, g_ref, b_ref, ys_ref, o_ref, buf_ref, sems):
    i = pl.program_id(0)
    tq, d = x_ref.shape
    slot_rows = 2 * tq * SUBLANE

    def issue(d_ref, slot, t):
        _row_copy(ys_ref, buf_ref, d_ref[0, 0, t], slot * (2 * tq) + t, sems.at[slot]).start(priority=0)
        _row_copy(ys_ref, buf_ref, d_ref[0, 0, tq + t], slot * (2 * tq) + tq + t, sems.at[slot]).start(priority=1)

    def wait_slot(slot):
        off = pl.multiple_of(slot * slot_rows, slot_rows)
        pltpu.make_async_copy(ys_ref.at[pl.ds(0, slot_rows), :], buf_ref.at[pl.ds(off, slot_rows), :],
                              sems.at[slot]).wait()
        return off

    def issue_all(d_ref, slot):
        lax.fori_loop(0, tq, lambda t, c: (issue(d_ref, slot, t), c)[1], 0, unroll=4)

    slot = i % 2

    @pl.when(i == 0)
    def _():
        issue_all(dcur_ref, 0)

    @pl.when(i + 1 < pl.num_programs(0))
    def _():
        issue_all(dnext_ref, 1 - slot)

    off = wait_slot(slot)
    meta_rows = jnp.transpose(jnp.concatenate([meta_ref[...], jnp.zeros((LANE - SUBLANE, tq), F32)], axis=0))
    gate1 = meta_rows[:, 4:5]
    gate2 = meta_rows[:, 5:6]
    for c in range(d // LANE):
        sl = slice(c * LANE, (c + 1) * LANE)
        y1 = buf_ref[pl.ds(off + c, tq, stride=SUBLANE), :]
        y2 = buf_ref[pl.ds(off + tq * SUBLANE + c, tq, stride=SUBLANE), :]
        o_ref[:, sl] = alpha * x_ref[:, sl] + (y1 * gate1 + y2 * gate2)
    o_ref[...] = _layer_norm(o_ref[...], g_ref[...], b_ref[...])


def _combine(dest3, meta, x2, g, bb, ys_tiled, alpha):
    t, d = x2.shape
    n = dest3.shape[0]
    tq = t // n
    row = lambda i: (i, 0)
    const = lambda i: (0, 0)
    return pl.pallas_call(
        functools.partial(_combine_kernel, alpha),
        grid=(n,),
        in_specs=[pl.BlockSpec((1, 1, 2 * tq), lambda i: (i, 0, 0), memory_space=pltpu.SMEM),
                  pl.BlockSpec((1, 1, 2 * tq), lambda i: (jnp.minimum(i + 1, n - 1), 0, 0), memory_space=pltpu.SMEM),
                  pl.BlockSpec((SUBLANE, tq), lambda i: (0, i)),
                  pl.BlockSpec((tq, d), row),
                  pl.BlockSpec((1, d), const), pl.BlockSpec((1, d), const),
                  pl.BlockSpec(memory_space=pl.ANY)],
        out_specs=pl.BlockSpec((tq, d), row),
        out_shape=jax.ShapeDtypeStruct((t, d), F32),
        scratch_shapes=[pltpu.VMEM((2 * 2 * tq * SUBLANE, LANE), F32), pltpu.SemaphoreType.DMA((2,))],
        compiler_params=_params("arbitrary"),
        name="combine",
    )(dest3, dest3, meta, x2, g, bb, ys_tiled)


def _moe(x2, x2_tiled, logits, w_g, w_u, w_d, g, bb, alpha):
    t, d = x2.shape
    meta, cnt = _route(logits)
    counts = cnt[N_GROUPS:N_GROUPS + N_EXPERTS, 0].astype(jnp.int32)
    padded = ((counts + EXPERT_ROWS - 1) // EXPERT_ROWS) * EXPERT_ROWS
    pend = jnp.cumsum(padded)
    poff = pend - padded
    first_row = jnp.pad(poff.astype(F32), (N_GROUPS, ROUTE_ROWS - N_GROUPS - N_EXPERTS))
    dest3 = _plan(meta, jnp.broadcast_to(first_row[:, None], (ROUTE_ROWS, LANE)), ROUTE_TILE)
    n_rows = t * 2 + N_EXPERTS * EXPERT_ROWS
    nblk = n_rows // EXPERT_ROWS
    blk_start = jnp.arange(nblk, dtype=jnp.int32) * EXPERT_ROWS
    blk_exp = jnp.sum((pend[None, :] <= blk_start[:, None]).astype(jnp.int32), axis=1)
    blk_exp = jnp.minimum(blk_exp, N_EXPERTS - 1)
    e_idx = jnp.arange(N_EXPERTS, dtype=jnp.int32)
    owner = blk_exp[:, None] == e_idx[None, :]

    def of_block(per_expert):
        return jnp.sum(jnp.where(owner, per_expert[None, :], 0), axis=1).astype(jnp.int32)

    blk_rows = jnp.clip(of_block(poff + counts) - blk_start, 0, EXPERT_ROWS).astype(jnp.int32)
    n_used = (pend[-1:] // EXPERT_ROWS).astype(jnp.int32)
    has_rows = counts > 0
    ordinal = jnp.cumsum(has_rows.astype(jnp.int32)) - 1
    later = has_rows[None, :] & (e_idx[None, :] > e_idx[:, None])
    next_used = jnp.min(jnp.where(later, e_idx[None, :], N_EXPERTS), axis=1)
    next_used = jnp.where(next_used == N_EXPERTS, -1, next_used).astype(jnp.int32)
    blk_first = (blk_start == of_block(poff)).astype(jnp.int32)
    blk_slot = of_block(ordinal % 2)
    blk_next = of_block(next_used)
    xs_tiled = _dispatch(dest3, x2_tiled, n_rows)
    ys_tiled = _experts(blk_exp, blk_rows, n_used, blk_first, blk_slot, blk_next, xs_tiled, w_g, w_u, w_d)
    return _combine(dest3, meta, x2, g, bb, ys_tiled, alpha)


def kernel(x, mem, w_in, w_out, ln_mix_g, ln_mix_b, w_xq, w_xkv, w_xo, ln_x_g, ln_x_b, w_route_group,
           w_route_expert, w_exp_gate, w_exp_up, w_exp_down, ln_moe_g, ln_moe_b):
    b, s, d = x.shape
    depth = w_in.shape[0]
    alpha = (2.0 * depth) ** 0.25
    t = b * s
    ret_cols = (2 * RET_HEADS * RET_QK_DIM + 2 * RET_HEADS * RET_V_DIM)
    xc = x.reshape(t, d)
    for l in range(depth):
        n_qk = 2 * RET_HEADS * RET_QK_DIM
        h_ret, h_dil = _proj_in(xc, w_in[l], _retention_weight_layout(w_in[l][:, :n_qk]).astype(BF16), ret_cols)
        y_ret = _retention(h_ret.reshape(b, s, ret_cols)).reshape(t, -1)
        y_dil = _dilated(h_dil, b).reshape(t, -1)
        kv = _mem_kv(mem.reshape(b * mem.shape[1], d), w_xkv[l]).reshape(b, mem.shape[1], 2 * d)
        w_r = jnp.concatenate([w_route_group[l], w_route_expert[l]], axis=-1)
        w_r = jnp.pad(w_r.T, ((0, ROUTE_ROWS - w_r.shape[1]), (0, 0))).astype(BF16)
        x2, x2_tiled, logits = _xattn(xc, y_ret, y_dil, w_out[l].astype(BF16), ln_mix_g[l][None], ln_mix_b[l][None],
                                      kv, w_xq[l].astype(BF16), w_xo[l].astype(BF16),
                                      ln_x_g[l][None], ln_x_b[l][None], w_r, alpha, s)
        xc = _moe(x2, x2_tiled, logits, w_exp_gate[l], w_exp_up[l], w_exp_down[l],
                  ln_moe_g[l][None], ln_moe_b[l][None], alpha)
    return xc.reshape(b, s, d)
```

```python
import functools
import math

import jax
import jax.numpy as jnp
import numpy as np
from jax import lax
from jax.experimental import pallas as pl
from jax.experimental.pallas import tpu as pltpu

BF16 = jnp.bfloat16
F32 = jnp.float32

LANE = 128
SUBLANE = 8
VMEM_LIMIT = 56 * 1024 * 1024

RET_HEADS = 4
RET_QK_DIM = 64
RET_V_DIM = 128
RET_CHUNK = 128
RET_STEP_CHUNKS = 4
ROPE_BASE = 10000.0
DIL_HEADS = 8
DIL_HEAD_DIM = 64
DIL_DILATIONS = (1, 4, 16)
DIL_BLOCK = 128
DIL_SUPER = DIL_BLOCK * max(DIL_DILATIONS)
DIL_GROUP = 1
XATTN_HEADS = 4
XATTN_PARTS = 2
N_GROUPS = 4
EXPERTS_PER_GROUP = 8
N_EXPERTS = N_GROUPS * EXPERTS_PER_GROUP
ROUTE_ROWS = -(-(N_GROUPS + N_EXPERTS) // SUBLANE) * SUBLANE
EXPERT_ROWS = 512
EXPERT_QUANTUM = 256
ROUTE_TILE = 512
LN_EPS = 1e-5
GN_EPS = 1e-6
NEG = -1e30


def _params(*sem):
    return pltpu.CompilerParams(dimension_semantics=sem, vmem_limit_bytes=VMEM_LIMIT)


def _layer_norm(z, g, b):
    mu = jnp.mean(z, axis=-1, keepdims=True)
    zc = z - mu
    var = jnp.mean(zc * zc, axis=-1, keepdims=True)
    return zc * lax.rsqrt(var + LN_EPS) * g + b


def _dot(a, b):
    return jnp.dot(a.astype(BF16), b.astype(BF16), preferred_element_type=F32)


def _dot_nt(a, b):
    return lax.dot_general(a.astype(BF16), b.astype(BF16), (((1,), (1,)), ((), ())),
                           preferred_element_type=F32)


def _dot_tn(a, b):
    return lax.dot_general(a.astype(BF16), b.astype(BF16), (((0,), (0,)), ((), ())),
                           preferred_element_type=F32)


def _proj_in_kernel(x_ref, w_ref, o_ref):
    o_ref[...] = _dot(x_ref[...], w_ref[...]).astype(o_ref.dtype)


def _proj_in2_kernel(x_ref, w_ref, w_head_ref, o_ref, slab_ref, w_bf16):
    @pl.when(pl.program_id(0) == 0)
    def _():
        w_bf16[...] = w_ref[...].astype(BF16)
        w_bf16[:, 0:w_head_ref.shape[1]] = w_head_ref[...]

    n_row = o_ref.shape[1]
    x = x_ref[...].astype(BF16)
    o_ref[...] = jnp.dot(x, w_bf16[:, 0:n_row], preferred_element_type=F32)
    rest = jnp.dot(x, w_bf16[:, n_row:], preferred_element_type=F32)
    for c in range(slab_ref.shape[0]):
        slab_ref[c] = rest[:, c * LANE:(c + 1) * LANE]


def _proj_in(x2d, w_f32, w_head_bf16, n_row, tm=1024):
    t, d = x2d.shape
    n = w_f32.shape[1]
    n_slabs = (n - n_row) // LANE
    whole = lambda shape: pl.BlockSpec(shape, lambda i: (0, 0), pipeline_mode=pl.Buffered(1))
    return pl.pallas_call(
        _proj_in2_kernel,
        grid=(t // tm,),
        in_specs=[pl.BlockSpec((tm, d), lambda i: (i, 0)), whole((d, n)), whole(w_head_bf16.shape)],
        out_specs=[pl.BlockSpec((tm, n_row), lambda i: (i, 0)),
                   pl.BlockSpec((n_slabs, tm, LANE), lambda i: (0, i, 0))],
        out_shape=[jax.ShapeDtypeStruct((t, n_row), F32),
                   jax.ShapeDtypeStruct((n_slabs, t, LANE), F32)],
        scratch_shapes=[pltpu.VMEM((d, n), BF16)],
        compiler_params=_params("arbitrary"),
        name="proj_in",
    )(x2d, w_f32, w_head_bf16)


def _retention_kernel(qk_ref, v_ref, g_ref, cos_ref, sin_ref, decay_ref, zeta_ref, xi_ref, gam_ref,
                      o_ref, state_ref, y_ref):
    n = pl.program_id(1)

    @pl.when(n == 0)
    def _():
        state_ref[...] = jnp.zeros_like(state_ref)

    c = RET_CHUNK
    n_sub = qk_ref.shape[1] // c
    half = RET_QK_DIM // 2
    lane = lax.broadcasted_iota(jnp.int32, (c, LANE), 1)

    rotated = []
    for j in range(n_sub):
        rows = slice(j * c, (j + 1) * c)
        cos = cos_ref[rows, :]
        sin = sin_ref[rows, :]

        def rot(col, rows=rows, cos=cos, sin=sin):
            t1 = qk_ref[0, rows, col * LANE:(col + 1) * LANE]
            t2 = qk_ref[0, rows, (col + 1) * LANE:(col + 2) * LANE]
            return t1 * cos - t2 * sin, t1 * sin + t2 * cos

        q1, q2 = rot(0)
        k1, k2 = (t * (RET_QK_DIM ** -0.5) for t in rot(2))
        rotated.append((q1, q2, k1, k2, jnp.concatenate([k1, k2], axis=1)))

    for h in range(RET_HEADS):
        cols = slice(h * RET_V_DIM, (h + 1) * RET_V_DIM)
        mine = (lane >= h * half) & (lane < (h + 1) * half)
        zeta = zeta_ref[h]
        st = state_ref[h]
        pad_lo = [jnp.zeros((h * half, RET_V_DIM), BF16)] if h > 0 else []
        pad_hi = [jnp.zeros(((RET_HEADS - 1 - h) * half, RET_V_DIM), BF16)] if h < RET_HEADS - 1 else []
        for j in range(n_sub):
            rows = slice(j * c, (j + 1) * c)
            q1, q2, k1, k2, k_all = rotated[j]
            qm = jnp.concatenate([jnp.where(mine, q1, 0.0), jnp.where(mine, q2, 0.0)], axis=1)
            v = v_ref[0, rows, cols]
            s = _dot_nt(qm, k_all) * decay_ref[h]
            sb = st.astype(BF16)
            st_rows = jnp.concatenate(pad_lo + [sb[:half]] + pad_hi + pad_lo + [sb[half:]] + pad_hi, axis=0)
            y_ref[rows, cols] = _dot(s, v) + _dot(qm, st_rows) * xi_ref[h]
            kv = _dot_tn(jnp.concatenate([k1 * zeta, k2 * zeta], axis=1), v)
            lo = h * half
            hi = RET_HEADS * half + h * half
            st = gam_ref[h, :, :] * st + jnp.concatenate([kv[lo:lo + half], kv[hi:hi + half]], axis=0)
        state_ref[h] = st

    for h in range(RET_HEADS):
        cols = slice(h * RET_V_DIM, (h + 1) * RET_V_DIM)
        y = y_ref[:, cols]
        mu = jnp.mean(y, axis=-1, keepdims=True)
        yc = y - mu
        var = jnp.mean(yc * yc, axis=-1, keepdims=True)
        yn = yc * lax.rsqrt(var + GN_EPS)
        gate = g_ref[0, :, cols]
        o_ref[0, :, cols] = (gate * jax.nn.sigmoid(gate) * yn).astype(o_ref.dtype)


def _retention_tables(s):
    half = RET_QK_DIM // 2
    inv = 1.0 / (ROPE_BASE ** (np.arange(half, dtype=np.float64) / half))
    ang = np.arange(s, dtype=np.float64)[:, None] * inv[None, :]
    cos_t = np.tile(np.cos(ang), (1, RET_HEADS))
    sin_t = np.tile(np.sin(ang), (1, RET_HEADS))
    c = RET_CHUNK
    lg = np.log(1.0 - np.exp2(-5.0 - np.arange(RET_HEADS, dtype=np.float64)))
    idx = np.arange(c, dtype=np.float64)
    diff = idx[:, None] - idx[None, :]
    decay = np.where(diff >= 0, np.exp(lg[:, None, None] * np.maximum(diff, 0.0)), 0.0)
    lanes = (RET_HEADS, c, LANE)
    zeta = np.broadcast_to(np.exp(lg[:, None] * (c - 1.0 - idx))[:, :, None], lanes)
    xi = np.broadcast_to(np.exp(lg[:, None] * (idx + 1.0))[:, :, None], lanes)
    gam = np.broadcast_to(np.exp(lg * c)[:, None, None], (RET_HEADS, 1, LANE))
    return tuple(jnp.asarray(np.ascontiguousarray(a), F32) for a in (cos_t, sin_t, decay, zeta, xi, gam))


def _retention_weight_layout(w_qk):
    d = w_qk.shape[0]
    half = RET_QK_DIM // 2
    return w_qk.reshape(d, 2, RET_HEADS, 2, half).transpose(0, 1, 3, 2, 4).reshape(d, -1)


def _retention(h3):
    b, s, _ = h3.shape
    c = RET_CHUNK
    qk_w = 2 * RET_HEADS * RET_QK_DIM
    v_w = RET_HEADS * RET_V_DIM
    assert qk_w == v_w
    cos_t, sin_t, decay, zeta, xi, gam = _retention_tables(s)
    const3 = lambda bi, n: (0, 0, 0)
    rows = RET_STEP_CHUNKS * c
    return pl.pallas_call(
        _retention_kernel,
        grid=(b, s // rows),
        in_specs=[pl.BlockSpec((1, rows, qk_w), lambda bi, n: (bi, n, 0)),
                  pl.BlockSpec((1, rows, v_w), lambda bi, n: (bi, n, 1)),
                  pl.BlockSpec((1, rows, v_w), lambda bi, n: (bi, n, 2)),
                  pl.BlockSpec((rows, LANE), lambda bi, n: (n, 0)),
                  pl.BlockSpec((rows, LANE), lambda bi, n: (n, 0)),
                  pl.BlockSpec((RET_HEADS, c, c), const3),
                  pl.BlockSpec((RET_HEADS, c, LANE), const3),
                  pl.BlockSpec((RET_HEADS, c, LANE), const3),
                  pl.BlockSpec((RET_HEADS, 1, LANE), const3)],
        out_specs=pl.BlockSpec((1, rows, v_w), lambda bi, n: (bi, n, 0)),
        out_shape=jax.ShapeDtypeStruct((b, s, v_w), BF16),
        scratch_shapes=[pltpu.VMEM((RET_HEADS, RET_QK_DIM, RET_V_DIM), F32),
                        pltpu.VMEM((rows, v_w), F32)],
        compiler_params=_params("parallel", "arbitrary"),
        name="retention",
    )(h3, h3, h3, cos_t, sin_t, decay, zeta, xi, gam)


def _dilated_kernel(q_ref, kp_ref, kc_ref, vp_ref, vc_ref, qw_ref, kpw_ref, kcw_ref, vpw_ref, vcw_ref, bias_ref,
                    o_ref, acc_ref, m_ref, l_ref, s0_ref, s1_ref, p0_ref, p1_ref):
    j = pl.program_id(2)
    sup = DIL_SUPER
    q_blk = DIL_BLOCK
    scale = DIL_HEAD_DIM ** -0.5 * math.log2(math.e)
    n_blocks = sup // q_blk
    lane = lax.broadcasted_iota(jnp.int32, (q_blk, LANE), 1)
    head0 = lane < DIL_HEAD_DIM
    first_bias = jnp.where(j == 0, 1, 0)

    groups = []
    for bi, d in enumerate(DIL_DILATIONS):
        n_per_r = sup // (q_blk * d)
        for t0 in range(0, n_blocks, DIL_GROUP):
            blocks = []
            for t in range(t0, t0 + DIL_GROUP):
                r, n = divmod(t, n_per_r)
                blocks.append((n * (q_blk * d) + r, (n - 1) * (q_blk * d) + r, n))
            groups.append((bi, d, blocks))
    s_bufs = (s0_ref, s1_ref)
    p_bufs = (p0_ref, p1_ref)

    def window(prev_ref, cur_ref, k_start, d):
        if k_start >= 0:
            return cur_ref[0, pl.ds(k_start, 2 * q_blk, stride=d), :]
        return jnp.concatenate([prev_ref[0, pl.ds(sup + k_start, q_blk, stride=d), :],
                                cur_ref[0, pl.ds(k_start + q_blk * d, q_blk, stride=d), :]], axis=0)

    def residue(ref, q_start):
        return ref[0, :, q_start * LANE:(q_start + 1) * LANE]

    def scores(gi):
        bi, d, blocks = groups[gi]
        for g, (q_start, k_start, n) in enumerate(blocks):
            if d == n_blocks:
                q = residue(qw_ref, q_start) * scale
                kb = jnp.concatenate([residue(kpw_ref, q_start), residue(kcw_ref, q_start)], axis=0)
            else:
                q = q_ref[0, pl.ds(q_start, q_blk, stride=d), :] * scale
                kb = window(kp_ref, kc_ref, k_start, d)
            q2 = jnp.concatenate([jnp.where(head0, q, 0.0), jnp.where(head0, 0.0, q)], axis=0)
            bias = bias_ref[first_bias] if n == 0 else bias_ref[0]
            s_bufs[gi % 2][g] = _dot_nt(q2, kb) + bias

    def softmax(gi):
        bi, d, blocks = groups[gi]
        for g, (q_start, k_start, n) in enumerate(blocks):
            s = s_bufs[gi % 2][g]
            m2 = jnp.max(s, axis=-1, keepdims=True)
            p_bufs[gi % 2][g] = jnp.exp2(s - m2).astype(BF16)
            m_ref[bi, pl.ds(q_start, q_blk, stride=d), :] = jnp.where(head0, m2[:q_blk], m2[q_blk:])

    def values(gi):
        bi, d, blocks = groups[gi]
        for g, (q_start, k_start, n) in enumerate(blocks):
            if d == n_blocks:
                vb = jnp.concatenate([residue(vpw_ref, q_start), residue(vcw_ref, q_start)], axis=0).astype(BF16)
            else:
                vb = window(vp_ref, vc_ref, k_start, d).astype(BF16)
            o2 = jnp.dot(p_bufs[gi % 2][g], jnp.concatenate([vb, jnp.ones_like(vb)], axis=1),
                         preferred_element_type=F32)
            rows = pl.ds(q_start, q_blk, stride=d)
            acc_ref[bi, rows, :] = jnp.where(head0, o2[:q_blk, :LANE], o2[q_blk:, :LANE])
            l_ref[bi, rows, :] = jnp.where(head0, o2[:q_blk, LANE:], o2[q_blk:, LANE:])

    for step in range(len(groups) + 2):
        if step < len(groups):
            scores(step)
        if 0 <= step - 1 < len(groups):
            softmax(step - 1)
        if step - 2 >= 0:
            values(step - 2)

    def merge(c, carry):
        rows = pl.ds(pl.multiple_of(c * q_blk, q_blk), q_blk)
        ms = [m_ref[bi, rows, :] for bi in range(len(DIL_DILATIONS))]
        m_all = functools.reduce(jnp.maximum, ms)
        ws = [jnp.exp2(m - m_all) for m in ms]
        num = functools.reduce(lambda a, b: a + b, [w * acc_ref[bi, rows, :] for bi, w in enumerate(ws)])
        den = functools.reduce(lambda a, b: a + b, [w * l_ref[bi, rows, :] for bi, w in enumerate(ws)])
        o_ref[0, rows, :] = (num / den).astype(o_ref.dtype)
        return carry

    lax.fori_loop(0, n_blocks, merge, 0)


def _dilated_bias():
    q_blk = DIL_BLOCK
    qi = np.arange(2 * q_blk)[:, None] % q_blk
    kj = np.arange(2 * q_blk)[None, :]
    band = (kj >= qi) & (kj <= qi + q_blk)
    return jnp.asarray(np.stack([np.where(band, 0.0, NEG), np.where(band & (kj >= q_blk), 0.0, NEG)]), F32)


def _dilated(slabs, b):
    n_slabs, t, _ = slabs.shape
    s = t // b
    h3 = slabs.reshape(n_slabs * b, s, LANE)
    sup = DIL_SUPER
    n_pairs = DIL_HEADS * DIL_HEAD_DIM // LANE
    n_br = len(DIL_DILATIONS)
    cq, ck, cv = 0, n_pairs, 2 * n_pairs
    cur = lambda c: (lambda bi, p, j: ((c + p) * b + bi, j, 0))
    prev = lambda c: (lambda bi, p, j: ((c + p) * b + bi, jnp.maximum(j - 1, 0), 0))
    blk = (1, sup, LANE)
    d_wide = sup // DIL_BLOCK
    assert d_wide == max(DIL_DILATIONS)
    hw = slabs.reshape(n_slabs * b, s // d_wide, d_wide * LANE)
    blk_w = (1, DIL_BLOCK, d_wide * LANE)
    return pl.pallas_call(
        _dilated_kernel,
        grid=(b, n_pairs, s // sup),
        in_specs=[pl.BlockSpec(blk, cur(cq)),
                  pl.BlockSpec(blk, prev(ck)), pl.BlockSpec(blk, cur(ck)),
                  pl.BlockSpec(blk, prev(cv)), pl.BlockSpec(blk, cur(cv)),
                  pl.BlockSpec(blk_w, cur(cq)),
                  pl.BlockSpec(blk_w, prev(ck)), pl.BlockSpec(blk_w, cur(ck)),
                  pl.BlockSpec(blk_w, prev(cv)), pl.BlockSpec(blk_w, cur(cv)),
                  pl.BlockSpec((2, 2 * DIL_BLOCK, 2 * DIL_BLOCK), lambda bi, p, j: (0, 0, 0))],
        out_specs=pl.BlockSpec(blk, lambda bi, p, j: (bi, j, p)),
        out_shape=jax.ShapeDtypeStruct((b, s, n_pairs * LANE), BF16),
        scratch_shapes=[pltpu.VMEM((n_br, sup, LANE), F32), pltpu.VMEM((n_br, sup, LANE), F32),
                        pltpu.VMEM((n_br, sup, LANE), F32),
                        pltpu.VMEM((DIL_GROUP, 2 * DIL_BLOCK, 2 * DIL_BLOCK), F32),
                        pltpu.VMEM((DIL_GROUP, 2 * DIL_BLOCK, 2 * DIL_BLOCK), F32),
                        pltpu.VMEM((DIL_GROUP, 2 * DIL_BLOCK, 2 * DIL_BLOCK), BF16),
                        pltpu.VMEM((DIL_GROUP, 2 * DIL_BLOCK, 2 * DIL_BLOCK), BF16)],
        compiler_params=_params("parallel", "parallel", "arbitrary"),
        name="dilated",
    )(h3, h3, h3, h3, h3, hw, hw, hw, hw, hw, _dilated_bias())


def _mem_kv(mem2d, w_bf16, tn=512):
    m, d = mem2d.shape
    n = w_bf16.shape[1]
    return pl.pallas_call(
        _proj_in_kernel,
        grid=(n // tn,),
        in_specs=[pl.BlockSpec((m, d), lambda i: (0, 0)),
                  pl.BlockSpec((d, tn), lambda i: (0, i))],
        out_specs=pl.BlockSpec((m, tn), lambda i: (0, i)),
        out_shape=jax.ShapeDtypeStruct((m, n), BF16),
        compiler_params=_params("parallel"),
        name="mem_kv",
    )(mem2d, w_bf16)


def _xattn_kernel(alpha, x_ref, yr_ref, yd_ref, wout_ref, g1_ref, b1_ref, k_ref, v_ref, wq_ref, wo_ref,
                  g_ref, b_ref, wr_ref, o_ref, ot_ref, lg_ref, x1_ref, q_ref, att_ref):
    tm, d = x_ref.shape
    dh = d // XATTN_HEADS
    wr = yr_ref.shape[1]
    scale = dh ** -0.5 * math.log2(math.e)
    part = tm // XATTN_PARTS

    def mix(rows):
        y = _dot(yr_ref[rows, :], wout_ref[0:wr, :]) + _dot(yd_ref[rows, :], wout_ref[wr:, :])
        x1_ref[rows, :] = _layer_norm(alpha * x_ref[rows, :] + y, g1_ref[...], b1_ref[...])

    def query(rows):
        q_ref[rows, :] = _dot(x1_ref[rows, :], wq_ref[...]).astype(BF16)

    def attend(rows):
        for h in range(XATTN_HEADS):
            sl = slice(h * dh, (h + 1) * dh)
            s = _dot_nt(q_ref[rows, sl], k_ref[0, :, sl]) * scale
            m = jnp.max(s, axis=-1, keepdims=True)
            e = jnp.exp2(s - m)
            p = e / jnp.sum(e, axis=-1, keepdims=True)
            att_ref[rows, sl] = _dot(p, v_ref[0, :, sl]).astype(BF16)

    def finish(rows, r0):
        y = _dot(att_ref[rows, :], wo_ref[...])
        x2 = _layer_norm(alpha * x1_ref[rows, :] + y, g_ref[...], b_ref[...])
        o_ref[rows, :] = x2
        for c in range(d // LANE):
            ot_ref[pl.ds(r0 * SUBLANE + c, part, stride=SUBLANE), :] = x2[:, c * LANE:(c + 1) * LANE]
        lg_ref[:, rows] = _dot_nt(wr_ref[...], x2)

    stages = (mix, query, attend, finish)
    for step in range(XATTN_PARTS + len(stages) - 1):
        for si, stage in enumerate(stages):
            pi = step - si
            if 0 <= pi < XATTN_PARTS:
                rows = slice(pi * part, (pi + 1) * part)
                if stage is finish:
                    stage(rows, pi * part)
                else:
                    stage(rows)


def _xattn(x2d, y_ret, y_dil, w_out, g1, b1, kv, wq, wo, g, bb, w_r, alpha, seq, tm=1024):
    t, d = x2d.shape
    mlen = kv.shape[1]
    tiles_per_seq = seq // tm
    row = lambda i: (i, 0)

    def whole(shape):
        return pl.BlockSpec(shape, lambda i: (0,) * len(shape), pipeline_mode=pl.Buffered(1))

    vec = whole((1, d))
    return pl.pallas_call(
        functools.partial(_xattn_kernel, alpha),
        grid=(t // tm,),
        in_specs=[pl.BlockSpec((tm, d), row),
                  pl.BlockSpec((tm, y_ret.shape[1]), row),
                  pl.BlockSpec((tm, y_dil.shape[1]), row),
                  whole(w_out.shape), vec, vec,
                  pl.BlockSpec((1, mlen, d), lambda i: (i // tiles_per_seq, 0, 0)),
                  pl.BlockSpec((1, mlen, d), lambda i: (i // tiles_per_seq, 0, 1)),
                  whole((d, d)), whole((d, d)), vec, vec,
                  whole((ROUTE_ROWS, d))],
        out_specs=[pl.BlockSpec((tm, d), row),
                   pl.BlockSpec((tm * SUBLANE, LANE), row),
                   pl.BlockSpec((ROUTE_ROWS, tm), lambda i: (0, i))],
        out_shape=[jax.ShapeDtypeStruct((t, d), F32),
                   jax.ShapeDtypeStruct((t * SUBLANE, LANE), F32),
                   jax.ShapeDtypeStruct((ROUTE_ROWS, t), F32)],
        scratch_shapes=[pltpu.VMEM((tm, d), F32), pltpu.VMEM((tm, d), BF16), pltpu.VMEM((tm, d), BF16)],
        compiler_params=_params("parallel"),
        name="xattn",
    )(x2d, y_ret, y_dil, w_out, g1, b1, kv, kv, wq, wo, g, bb, w_r)


def _route_kernel(lg_ref, before_ref, meta_ref, cnt_ref, carry_ref):
    i = pl.program_id(0)

    @pl.when(i == 0)
    def _():
        carry_ref[...] = jnp.zeros_like(carry_ref)

    rows, tt = lg_ref.shape
    sub = before_ref.shape[0]
    r = lax.broadcasted_iota(jnp.int32, (rows, sub), 0)
    r8 = lax.broadcasted_iota(jnp.int32, (SUBLANE, sub), 0)
    is_group = r < N_GROUPS

    def col_max(a):
        return jnp.max(a, axis=0, keepdims=True)

    def first_row_where(mask):
        return jnp.min(jnp.where(mask, r, rows), axis=0, keepdims=True)

    for c in range(tt // sub):
        lg = lg_ref[:, c * sub:(c + 1) * sub]
        mg = col_max(jnp.where(is_group, lg, NEG))
        eg = jnp.where(is_group, jnp.exp(lg - mg), 0.0)
        pg = eg / jnp.sum(eg, axis=0, keepdims=True)
        g1 = col_max(pg)
        gi = first_row_where(is_group & (pg == g1))
        lo = N_GROUPS + gi * EXPERTS_PER_GROUP
        in_grp = (r >= lo) & (r < lo + EXPERTS_PER_GROUP)
        v1 = col_max(jnp.where(in_grp, lg, NEG))
        i1 = first_row_where(in_grp & (lg == v1))
        rest = in_grp & (r != i1)
        v2 = col_max(jnp.where(rest, lg, NEG))
        i2 = first_row_where(rest & (lg == v2))
        e2 = jnp.exp(v2 - v1)
        den = 1.0 + e2
        gate1 = g1 * (1.0 / den)
        gate2 = g1 * (e2 / den)
        sel1 = r == i1
        sel2 = r == i2
        onehot = jnp.where(sel1 | sel2, 1.0, 0.0)
        rank = _dot(onehot, before_ref[...]) + carry_ref[:, 0:1]
        r1 = jnp.sum(jnp.where(sel1, rank, 0.0), axis=0, keepdims=True)
        r2 = jnp.sum(jnp.where(sel2, rank, 0.0), axis=0, keepdims=True)
        carry_ref[...] = carry_ref[...] + jnp.sum(onehot, axis=1, keepdims=True)
        meta = jnp.where(r8 == 0, (i1 - N_GROUPS).astype(F32), 0.0)
        meta = jnp.where(r8 == 1, (i2 - N_GROUPS).astype(F32), meta)
        meta = jnp.where(r8 == 2, r1, meta)
        meta = jnp.where(r8 == 3, r2, meta)
        meta = jnp.where(r8 == 4, gate1, meta)
        meta = jnp.where(r8 == 5, gate2, meta)
        meta_ref[:, c * sub:(c + 1) * sub] = meta
    cnt_ref[...] = carry_ref[...]


def _route(logits_t, tt=1024, sub=256):
    rows, t = logits_t.shape
    before = jnp.asarray(np.arange(sub)[:, None] < np.arange(sub)[None, :], BF16)
    return pl.pallas_call(
        _route_kernel,
        grid=(t // tt,),
        in_specs=[pl.BlockSpec((rows, tt), lambda i: (0, i)),
                  pl.BlockSpec((sub, sub), lambda i: (0, 0))],
        out_specs=[pl.BlockSpec((SUBLANE, tt), lambda i: (0, i)),
                   pl.BlockSpec((rows, LANE), lambda i: (0, 0))],
        out_shape=[jax.ShapeDtypeStruct((SUBLANE, t), F32),
                   jax.ShapeDtypeStruct((rows, LANE), F32)],
        scratch_shapes=[pltpu.VMEM((rows, LANE), F32)],
        compiler_params=_params("arbitrary"),
        name="route",
    )(logits_t, before)


def _plan_kernel(meta_ref, first_row_ref, dest_ref):
    n_tiles = dest_ref.shape[0]
    tt = dest_ref.shape[2] // 2
    rows = first_row_ref.shape[0]
    first_row = first_row_ref[:, 0:1]
    r = lax.broadcasted_iota(jnp.int32, (rows, tt), 0)
    for g in range(n_tiles):
        m = meta_ref[:, g * tt:(g + 1) * tt]

        def dest_of(k):
            e_row = m[k:k + 1, :].astype(jnp.int32) + N_GROUPS
            return jnp.sum(jnp.where(r == e_row, first_row, 0.0), axis=0, keepdims=True) + m[2 + k:3 + k, :]

        dest_ref[g] = jnp.concatenate([dest_of(0), dest_of(1)], axis=1).astype(jnp.int32)


def _plan(meta_t, first_row, tt, tiles_per_step=4):
    t = meta_t.shape[1]
    rows = first_row.shape[0]
    return pl.pallas_call(
        _plan_kernel,
        grid=(t // (tt * tiles_per_step),),
        in_specs=[pl.BlockSpec((SUBLANE, tt * tiles_per_step), lambda i: (0, i)),
                  pl.BlockSpec((rows, LANE), lambda i: (0, 0))],
        out_specs=pl.BlockSpec((tiles_per_step, 1, 2 * tt), lambda i: (i, 0, 0)),
        out_shape=jax.ShapeDtypeStruct((t // tt, 1, 2 * tt), jnp.int32),
        compiler_params=_params("parallel"),
        name="plan",
    )(meta_t, first_row)


def _row_copy(src, dst, s_row, d_row, sem):
    return pltpu.make_async_copy(src.at[pl.ds(pl.multiple_of(s_row * SUBLANE, SUBLANE), SUBLANE), :],
                                 dst.at[pl.ds(pl.multiple_of(d_row * SUBLANE, SUBLANE), SUBLANE), :], sem)


def _dispatch_kernel(dest_ref, xt_ref, xs_ref, sem):
    tq = dest_ref.shape[2] // 2

    def issue(t, c):
        _row_copy(xt_ref, xs_ref, t, dest_ref[0, 0, t], sem).start(priority=0)
        _row_copy(xt_ref, xs_ref, t, dest_ref[0, 0, tq + t], sem).start(priority=1)
        return c

    lax.fori_loop(0, tq, issue, 0, unroll=8)
    for _ in range(2):
        pltpu.make_async_copy(xt_ref, xs_ref.at[pl.ds(0, tq * SUBLANE), :], sem).wait()


def _dispatch(dest3, x_tiled, n_rows, tq=4096):
    n_tiles, _, two_tt = dest3.shape
    tt = two_tt // 2
    tiles_per_step = tq // tt
    n_steps = n_tiles // tiles_per_step
    dest3 = dest3.reshape(n_steps, tiles_per_step, 2, tt).transpose(0, 2, 1, 3).reshape(n_steps, 1, 2 * tq)
    return pl.pallas_call(
        _dispatch_kernel,
        grid=(n_steps,),
        in_specs=[pl.BlockSpec((1, 1, 2 * tq), lambda i: (i, 0, 0), memory_space=pltpu.SMEM),
                  pl.BlockSpec((tq * SUBLANE, LANE), lambda i: (i, 0))],
        out_specs=pl.BlockSpec(memory_space=pl.ANY),
        out_shape=jax.ShapeDtypeStruct((n_rows * SUBLANE, LANE), F32),
        scratch_shapes=[pltpu.SemaphoreType.DMA(())],
        compiler_params=_params("arbitrary"),
        name="dispatch",
    )(dest3, x_tiled)


def _experts_kernel(blk_exp_ref, blk_rows_ref, n_used_ref, blk_first_ref, blk_slot_ref, blk_next_ref,
                    xs_ref, wg_hbm, wu_hbm, wd_hbm, ys_ref, x_scr, y_scr, wg_buf, wu_buf, wd_buf, sems):
    b = pl.program_id(0)

    def weight_copies(e, slot):
        return [pltpu.make_async_copy(hbm.at[e], buf.at[slot], sems.at[k, slot])
                for k, (hbm, buf) in enumerate(((wg_hbm, wg_buf), (wu_hbm, wu_buf), (wd_hbm, wd_buf)))]

    @pl.when(b < n_used_ref[0])
    def _():
        slot = blk_slot_ref[b]

        @pl.when(b == 0)
        def _():
            for cp in weight_copies(blk_exp_ref[0], slot):
                cp.start()

        @pl.when(blk_first_ref[b] == 1)
        def _():
            @pl.when(blk_next_ref[b] >= 0)
            def _():
                for cp in weight_copies(blk_next_ref[b], 1 - slot):
                    cp.start()

            for cp in weight_copies(blk_exp_ref[b], slot):
                cp.wait()

        rows, d = x_scr.shape
        n_valid = blk_rows_ref[b]

        def run(m):
            ri = lax.broadcasted_iota(jnp.int32, (m, 1), 0)
            for c in range(d // LANE):
                x = xs_ref[pl.ds(c, m, stride=SUBLANE), :]
                x_scr[0:m, c * LANE:(c + 1) * LANE] = jnp.where(ri < n_valid, x, 0.0).astype(BF16)
            x = x_scr[0:m, :]
            hg = _dot(x, wg_buf[slot])
            hu = _dot(x, wu_buf[slot])
            y_scr[0:m, :] = _dot(hg * jax.nn.sigmoid(hg) * hu, wd_buf[slot])
            for c in range(d // LANE):
                ys_ref[pl.ds(c, m, stride=SUBLANE), :] = y_scr[0:m, c * LANE:(c + 1) * LANE]

        n_quanta = rows // EXPERT_QUANTUM
        for k in range(1, n_quanta + 1):
            lo = (k - 1) * EXPERT_QUANTUM
            pl.when((n_valid > lo) & (n_valid <= k * EXPERT_QUANTUM))(functools.partial(run, k * EXPERT_QUANTUM))


def _experts(blk_exp, blk_rows, n_used, blk_first, blk_slot, blk_next, xs_tiled, w_g, w_u, w_d):
    n_rows = xs_tiled.shape[0] // SUBLANE
    rows = EXPERT_ROWS
    nblk = n_rows // rows
    _, d, ff = w_g.shape
    used = lambda b, be, br, nu, *_: (jnp.minimum(b, nu[0] - 1), 0)
    grid_spec = pltpu.PrefetchScalarGridSpec(
        num_scalar_prefetch=6,
        grid=(nblk,),
        in_specs=[pl.BlockSpec((rows * SUBLANE, LANE), used),
                  pl.BlockSpec(memory_space=pl.ANY), pl.BlockSpec(memory_space=pl.ANY),
                  pl.BlockSpec(memory_space=pl.ANY)],
        out_specs=pl.BlockSpec((rows * SUBLANE, LANE), used),
        scratch_shapes=[pltpu.VMEM((rows, d), BF16), pltpu.VMEM((rows, d), F32),
                        pltpu.VMEM((2, d, ff), F32), pltpu.VMEM((2, d, ff), F32), pltpu.VMEM((2, ff, d), F32),
                        pltpu.SemaphoreType.DMA((3, 2))],
    )
    return pl.pallas_call(
        _experts_kernel,
        grid_spec=grid_spec,
        out_shape=jax.ShapeDtypeStruct((n_rows * SUBLANE, LANE), F32),
        compiler_params=_params("arbitrary"),
        name="experts",
    )(blk_exp, blk_rows, n_used, blk_first, blk_slot, blk_next, xs_tiled, w_g, w_u, w_d)


def _combine_kernel(alpha, dcur_ref, dnext_ref, meta_ref, x_ref, g_ref, b_ref, ys_ref, o_ref, buf_ref, sems):
    i = pl.program_id(0)
    tq, d = x_ref.shape
    slot_rows = 2 * tq * SUBLANE

    def issue(d_ref, slot, t):
        _row_copy(ys_ref, buf_ref, d_ref[0, 0, t], slot * (2 * tq) + t, sems.at[slot]).start(priority=0)
        _row_copy(ys_ref, buf_ref, d_ref[0, 0, tq + t], slot * (2 * tq) + tq + t, sems.at[slot]).start(priority=1)

    def wait_slot(slot):
        off = pl.multiple_of(slot * slot_rows, slot_rows)
        pltpu.make_async_copy(ys_ref.at[pl.ds(0, slot_rows), :], buf_ref.at[pl.ds(off, slot_rows), :],
                              sems.at[slot]).wait()
        return off

    def issue_all(d_ref, slot):
        lax.fori_loop(0, tq, lambda t, c: (issue(d_ref, slot, t), c)[1], 0, unroll=4)

    slot = i % 2

    @pl.when(i == 0)
    def _():
        issue_all(dcur_ref, 0)

    @pl.when(i + 1 < pl.num_programs(0))
    def _():
        issue_all(dnext_ref, 1 - slot)

    off = wait_slot(slot)
    meta_rows = jnp.transpose(jnp.concatenate([meta_ref[...], jnp.zeros((LANE - SUBLANE, tq), F32)], axis=0))
    gate1 = meta_rows[:, 4:5]
    gate2 = meta_rows[:, 5:6]
    for c in range(d // LANE):
        sl = slice(c * LANE, (c + 1) * LANE)
        y1 = buf_ref[pl.ds(off + c, tq, stride=SUBLANE), :]
        y2 = buf_ref[pl.ds(off + tq * SUBLANE + c, tq, stride=SUBLANE), :]
        o_ref[:, sl] = alpha * x_ref[:, sl] + (y1 * gate1 + y2 * gate2)
    o_ref[...] = _layer_norm(o_ref[...], g_ref[...], b_ref[...])


def _combine(dest3, meta, x2, g, bb, ys_tiled, alpha):
    t, d = x2.shape
    n = dest3.shape[0]
    tq = t // n
    row = lambda i: (i, 0)
    const = lambda i: (0, 0)
    return pl.pallas_call(
        functools.partial(_combine_kernel, alpha),
        grid=(n,),
        in_specs=[pl.BlockSpec((1, 1, 2 * tq), lambda i: (i, 0, 0), memory_space=pltpu.SMEM),
                  pl.BlockSpec((1, 1, 2 * tq), lambda i: (jnp.minimum(i + 1, n - 1), 0, 0), memory_space=pltpu.SMEM),
                  pl.BlockSpec((SUBLANE, tq), lambda i: (0, i)),
                  pl.BlockSpec((tq, d), row),
                  pl.BlockSpec((1, d), const), pl.BlockSpec((1, d), const),
                  pl.BlockSpec(memory_space=pl.ANY)],
        out_specs=pl.BlockSpec((tq, d), row),
        out_shape=jax.ShapeDtypeStruct((t, d), F32),
        scratch_shapes=[pltpu.VMEM((2 * 2 * tq * SUBLANE, LANE), F32), pltpu.SemaphoreType.DMA((2,))],
        compiler_params=_params("arbitrary"),
        name="combine",
    )(dest3, dest3, meta, x2, g, bb, ys_tiled)


def _moe(x2, x2_tiled, logits, w_g, w_u, w_d, g, bb, alpha):
    t, d = x2.shape
    meta, cnt = _route(logits)
    counts = cnt[N_GROUPS:N_GROUPS + N_EXPERTS, 0].astype(jnp.int32)
    padded = ((counts + EXPERT_ROWS - 1) // EXPERT_ROWS) * EXPERT_ROWS
    pend = jnp.cumsum(padded)
    poff = pend - padded
    first_row = jnp.pad(poff.astype(F32), (N_GROUPS, ROUTE_ROWS - N_GROUPS - N_EXPERTS))
    dest3 = _plan(meta, jnp.broadcast_to(first_row[:, None], (ROUTE_ROWS, LANE)), ROUTE_TILE)
    n_rows = t * 2 + N_EXPERTS * EXPERT_ROWS
    nblk = n_rows // EXPERT_ROWS
    blk_start = jnp.arange(nblk, dtype=jnp.int32) * EXPERT_ROWS
    blk_exp = jnp.sum((pend[None, :] <= blk_start[:, None]).astype(jnp.int32), axis=1)
    blk_exp = jnp.minimum(blk_exp, N_EXPERTS - 1)
    e_idx = jnp.arange(N_EXPERTS, dtype=jnp.int32)
    owner = blk_exp[:, None] == e_idx[None, :]

    def of_block(per_expert):
        return jnp.sum(jnp.where(owner, per_expert[None, :], 0), axis=1).astype(jnp.int32)

    blk_rows = jnp.clip(of_block(poff + counts) - blk_start, 0, EXPERT_ROWS).astype(jnp.int32)
    n_used = (pend[-1:] // EXPERT_ROWS).astype(jnp.int32)
    has_rows = counts > 0
    ordinal = jnp.cumsum(has_rows.astype(jnp.int32)) - 1
    later = has_rows[None, :] & (e_idx[None, :] > e_idx[:, None])
    next_used = jnp.min(jnp.where(later, e_idx[None, :], N_EXPERTS), axis=1)
    next_used = jnp.where(next_used == N_EXPERTS, -1, next_used).astype(jnp.int32)
    blk_first = (blk_start == of_block(poff)).astype(jnp.int32)
    blk_slot = of_block(ordinal % 2)
    blk_next = of_block(next_used)
    xs_tiled = _dispatch(dest3, x2_tiled, n_rows)
    ys_tiled = _experts(blk_exp, blk_rows, n_used, blk_first, blk_slot, blk_next, xs_tiled, w_g, w_u, w_d)
    return _combine(dest3, meta, x2, g, bb, ys_tiled, alpha)


def kernel(x, mem, w_in, w_out, ln_mix_g, ln_mix_b, w_xq, w_xkv, w_xo, ln_x_g, ln_x_b, w_route_group,
           w_route_expert, w_exp_gate, w_exp_up, w_exp_down, ln_moe_g, ln_moe_b):
    b, s, d = x.shape
    depth = w_in.shape[0]
    alpha = (2.0 * depth) ** 0.25
    t = b * s
    ret_cols = (2 * RET_HEADS * RET_QK_DIM + 2 * RET_HEADS * RET_V_DIM)
    assert s % DIL_SUPER == 0 and s % (RET_STEP_CHUNKS * RET_CHUNK) == 0, "sequence length not tileable"
    assert w_in.shape[2] == ret_cols + 3 * DIL_HEADS * DIL_HEAD_DIM and d % (XATTN_HEADS * LANE) == 0
    assert w_route_group.shape[2] == N_GROUPS and w_route_expert.shape[2] == N_EXPERTS
    xc = x.reshape(t, d)
    for l in range(depth):
        n_qk = 2 * RET_HEADS * RET_QK_DIM
        h_ret, h_dil = _proj_in(xc, w_in[l], _retention_weight_layout(w_in[l][:, :n_qk]).astype(BF16), ret_cols)
        y_ret = _retention(h_ret.reshape(b, s, ret_cols)).reshape(t, -1)
        y_dil = _dilated(h_dil, b).reshape(t, -1)
        kv = _mem_kv(mem.reshape(b * mem.shape[1], d), w_xkv[l]).reshape(b, mem.shape[1], 2 * d)
        w_r = jnp.concatenate([w_route_group[l], w_route_expert[l]], axis=-1)
        w_r = jnp.pad(w_r.T, ((0, ROUTE_ROWS - w_r.shape[1]), (0, 0))).astype(BF16)
        x2, x2_tiled, logits = _xattn(xc, y_ret, y_dil, w_out[l].astype(BF16), ln_mix_g[l][None], ln_mix_b[l][None],
                                      kv, w_xq[l].astype(BF16), w_xo[l].astype(BF16),
                                      ln_x_g[l][None], ln_x_b[l][None], w_r, alpha, s)
        xc = _moe(x2, x2_tiled, logits, w_exp_gate[l], w_exp_up[l], w_exp_down[l],
                  ln_moe_g[l][None], ln_moe_b[l][None], alpha)
    return xc.reshape(b, s, d)
```

```python
import functools
import math

import jax
import jax.numpy as jnp
import numpy as np
from jax import lax
from jax.experimental import pallas as pl
from jax.experimental.pallas import tpu as pltpu

BF16 = jnp.bfloat16
F32 = jnp.float32

LANE = 128
SUBLANE = 8
VMEM_LIMIT = 56 * 1024 * 1024

RET_HEADS = 4
RET_QK_DIM = 64
RET_V_DIM = 128
RET_CHUNK = 128
RET_STEP_CHUNKS = 4
ROPE_BASE = 10000.0
DIL_HEADS = 8
DIL_HEAD_DIM = 64
DIL_DILATIONS = (1, 4, 16)
DIL_BLOCK = 128
DIL_SUPER = DIL_BLOCK * max(DIL_DILATIONS)
DIL_GROUP = 1
XATTN_HEADS = 4
XATTN_PARTS = 2
N_GROUPS = 4
EXPERTS_PER_GROUP = 8
N_EXPERTS = N_GROUPS * EXPERTS_PER_GROUP
ROUTE_ROWS = -(-(N_GROUPS + N_EXPERTS) // SUBLANE) * SUBLANE
EXPERT_ROWS = 512
EXPERT_QUANTUM = 256
ROUTE_TILE = 512
LN_EPS = 1e-5
GN_EPS = 1e-6
NEG = -1e30


def _params(*sem):
    return pltpu.CompilerParams(dimension_semantics=sem, vmem_limit_bytes=VMEM_LIMIT)


def _layer_norm(z, g, b):
    mu = jnp.mean(z, axis=-1, keepdims=True)
    zc = z - mu
    var = jnp.mean(zc * zc, axis=-1, keepdims=True)
    return zc * lax.rsqrt(var + LN_EPS) * g + b


def _dot(a, b):
    return jnp.dot(a.astype(BF16), b.astype(BF16), preferred_element_type=F32)


def _dot_nt(a, b):
    return lax.dot_general(a.astype(BF16), b.astype(BF16), (((1,), (1,)), ((), ())),
                           preferred_element_type=F32)


def _dot_tn(a, b):
    return lax.dot_general(a.astype(BF16), b.astype(BF16), (((0,), (0,)), ((), ())),
                           preferred_element_type=F32)


def _proj_in_kernel(x_ref, w_ref, o_ref):
    o_ref[...] = _dot(x_ref[...], w_ref[...]).astype(o_ref.dtype)


def _proj_in2_kernel(x_ref, w_ref, w_head_ref, o_ref, slab_ref, w_bf16):
    @pl.when(pl.program_id(0) == 0)
    def _():
        w_bf16[...] = w_ref[...].astype(BF16)
        w_bf16[:, 0:w_head_ref.shape[1]] = w_head_ref[...]

    n_row = o_ref.shape[1]
    x = x_ref[...].astype(BF16)
    o_ref[...] = jnp.dot(x, w_bf16[:, 0:n_row], preferred_element_type=F32)
    rest = jnp.dot(x, w_bf16[:, n_row:], preferred_element_type=F32)
    for c in range(slab_ref.shape[0]):
        slab_ref[c] = rest[:, c * LANE:(c + 1) * LANE]


def _proj_in(x2d, w_f32, w_head_bf16, n_row, tm=1024):
    t, d = x2d.shape
    n = w_f32.shape[1]
    n_slabs = (n - n_row) // LANE
    whole = lambda shape: pl.BlockSpec(shape, lambda i: (0, 0), pipeline_mode=pl.Buffered(1))
    return pl.pallas_call(
        _proj_in2_kernel,
        grid=(t // tm,),
        in_specs=[pl.BlockSpec((tm, d), lambda i: (i, 0)), whole((d, n)), whole(w_head_bf16.shape)],
        out_specs=[pl.BlockSpec((tm, n_row), lambda i: (i, 0)),
                   pl.BlockSpec((n_slabs, tm, LANE), lambda i: (0, i, 0))],
        out_shape=[jax.ShapeDtypeStruct((t, n_row), F32),
                   jax.ShapeDtypeStruct((n_slabs, t, LANE), F32)],
        scratch_shapes=[pltpu.VMEM((d, n), BF16)],
        compiler_params=_params("arbitrary"),
        name="proj_in",
    )(x2d, w_f32, w_head_bf16)


def _retention_kernel(qk_ref, v_ref, g_ref, cos_ref, sin_ref, decay_ref, zeta_ref, xi_ref, gam_ref,
                      o_ref, state_ref, y_ref):
    n = pl.program_id(1)

    @pl.when(n == 0)
    def _():
        state_ref[...] = jnp.zeros_like(state_ref)

    c = RET_CHUNK
    n_sub = qk_ref.shape[1] // c
    half = RET_QK_DIM // 2
    lane = lax.broadcasted_iota(jnp.int32, (c, LANE), 1)

    rotated = []
    for j in range(n_sub):
        rows = slice(j * c, (j + 1) * c)
        cos = cos_ref[rows, :]
        sin = sin_ref[rows, :]

        def rot(col, rows=rows, cos=cos, sin=sin):
            t1 = qk_ref[0, rows, col * LANE:(col + 1) * LANE]
            t2 = qk_ref[0, rows, (col + 1) * LANE:(col + 2) * LANE]
            return t1 * cos - t2 * sin, t1 * sin + t2 * cos

        q1, q2 = rot(0)
        k1, k2 = (t * (RET_QK_DIM ** -0.5) for t in rot(2))
        rotated.append((q1, q2, k1, k2, jnp.concatenate([k1, k2], axis=1)))

    for h in range(RET_HEADS):
        cols = slice(h * RET_V_DIM, (h + 1) * RET_V_DIM)
        mine = (lane >= h * half) & (lane < (h + 1) * half)
        zeta = zeta_ref[h]
        st = state_ref[h]
        pad_lo = [jnp.zeros((h * half, RET_V_DIM), BF16)] if h > 0 else []
        pad_hi = [jnp.zeros(((RET_HEADS - 1 - h) * half, RET_V_DIM), BF16)] if h < RET_HEADS - 1 else []
        for j in range(n_sub):
            rows = slice(j * c, (j + 1) * c)
            q1, q2, k1, k2, k_all = rotated[j]
            qm = jnp.concatenate([jnp.where(mine, q1, 0.0), jnp.where(mine, q2, 0.0)], axis=1)
            v = v_ref[0, rows, cols]
            s = _dot_nt(qm, k_all) * decay_ref[h]
            sb = st.astype(BF16)
            st_rows = jnp.concatenate(pad_lo + [sb[:half]] + pad_hi + pad_lo + [sb[half:]] + pad_hi, axis=0)
            y_ref[rows, cols] = _dot(s, v) + _dot(qm, st_rows) * xi_ref[h]
            kv = _dot_tn(jnp.concatenate([k1 * zeta, k2 * zeta], axis=1), v)
            lo = h * half
            hi = RET_HEADS * half + h * half
            st = gam_ref[h, :, :] * st + jnp.concatenate([kv[lo:lo + half], kv[hi:hi + half]], axis=0)
        state_ref[h] = st

    for h in range(RET_HEADS):
        cols = slice(h * RET_V_DIM, (h + 1) * RET_V_DIM)
        y = y_ref[:, cols]
        mu = jnp.mean(y, axis=-1, keepdims=True)
        yc = y - mu
        var = jnp.mean(yc * yc, axis=-1, keepdims=True)
        yn = yc * lax.rsqrt(var + GN_EPS)
        gate = g_ref[0, :, cols]
        o_ref[0, :, cols] = (gate * jax.nn.sigmoid(gate) * yn).astype(o_ref.dtype)


def _retention_tables(s):
    half = RET_QK_DIM // 2
    inv = 1.0 / (ROPE_BASE ** (np.arange(half, dtype=np.float64) / half))
    ang = np.arange(s, dtype=np.float64)[:, None] * inv[None, :]
    cos_t = np.tile(np.cos(ang), (1, RET_HEADS))
    sin_t = np.tile(np.sin(ang), (1, RET_HEADS))
    c = RET_CHUNK
    lg = np.log(1.0 - np.exp2(-5.0 - np.arange(RET_HEADS, dtype=np.float64)))
    idx = np.arange(c, dtype=np.float64)
    diff = idx[:, None] - idx[None, :]
    decay = np.where(diff >= 0, np.exp(lg[:, None, None] * np.maximum(diff, 0.0)), 0.0)
    lanes = (RET_HEADS, c, LANE)
    zeta = np.broadcast_to(np.exp(lg[:, None] * (c - 1.0 - idx))[:, :, None], lanes)
    xi = np.broadcast_to(np.exp(lg[:, None] * (idx + 1.0))[:, :, None], lanes)
    gam = np.broadcast_to(np.exp(lg * c)[:, None, None], (RET_HEADS, 1, LANE))
    return tuple(jnp.asarray(np.ascontiguousarray(a), F32) for a in (cos_t, sin_t, decay, zeta, xi, gam))


def _retention_weight_layout(w_qk):
    d = w_qk.shape[0]
    half = RET_QK_DIM // 2
    return w_qk.reshape(d, 2, RET_HEADS, 2, half).transpose(0, 1, 3, 2, 4).reshape(d, -1)


def _retention(h3):
    b, s, _ = h3.shape
    c = RET_CHUNK
    qk_w = 2 * RET_HEADS * RET_QK_DIM
    v_w = RET_HEADS * RET_V_DIM
    assert qk_w == v_w
    cos_t, sin_t, decay, zeta, xi, gam = _retention_tables(s)
    const3 = lambda bi, n: (0, 0, 0)
    rows = RET_STEP_CHUNKS * c
    return pl.pallas_call(
        _retention_kernel,
        grid=(b, s // rows),
        in_specs=[pl.BlockSpec((1, rows, qk_w), lambda bi, n: (bi, n, 0)),
                  pl.BlockSpec((1, rows, v_w), lambda bi, n: (bi, n, 1)),
                  pl.BlockSpec((1, rows, v_w), lambda bi, n: (bi, n, 2)),
                  pl.BlockSpec((rows, LANE), lambda bi, n: (n, 0)),
                  pl.BlockSpec((rows, LANE), lambda bi, n: (n, 0)),
                  pl.BlockSpec((RET_HEADS, c, c), const3),
                  pl.BlockSpec((RET_HEADS, c, LANE), const3),
                  pl.BlockSpec((RET_HEADS, c, LANE), const3),
                  pl.BlockSpec((RET_HEADS, 1, LANE), const3)],
        out_specs=pl.BlockSpec((1, rows, v_w), lambda bi, n: (bi, n, 0)),
        out_shape=jax.ShapeDtypeStruct((b, s, v_w), BF16),
        scratch_shapes=[pltpu.VMEM((RET_HEADS, RET_QK_DIM, RET_V_DIM), F32),
                        pltpu.VMEM((rows, v_w), F32)],
        compiler_params=_params("parallel", "arbitrary"),
        name="retention",
    )(h3, h3, h3, cos_t, sin_t, decay, zeta, xi, gam)


def _dilated_kernel(q_ref, kp_ref, kc_ref, vp_ref, vc_ref, bias_ref,
                    o_ref, acc_ref, m_ref, l_ref, s0_ref, s1_ref, p0_ref, p1_ref, qw_ref, kw_ref, vw_ref):
    j = pl.program_id(2)
    sup = DIL_SUPER
    q_blk = DIL_BLOCK
    scale = DIL_HEAD_DIM ** -0.5 * math.log2(math.e)
    n_blocks = sup // q_blk
    lane = lax.broadcasted_iota(jnp.int32, (q_blk, LANE), 1)
    head0 = lane < DIL_HEAD_DIM
    first_bias = jnp.where(j == 0, 1, 0)

    slot = j % 2

    def relay(r):
        cols = slice(r * LANE, (r + 1) * LANE)
        qw_ref[:, cols] = q_ref[0, pl.ds(r, q_blk, stride=n_blocks), :]
        kw_ref[slot, :, cols] = kc_ref[0, pl.ds(r, q_blk, stride=n_blocks), :]
        vw_ref[slot, :, cols] = vc_ref[0, pl.ds(r, q_blk, stride=n_blocks), :]

    @pl.when(j == 0)
    def _():
        kw_ref[1] = jnp.zeros(kw_ref.shape[1:], F32)
        vw_ref[1] = jnp.zeros(vw_ref.shape[1:], F32)

    groups = []
    for bi, d in enumerate(DIL_DILATIONS):
        n_per_r = sup // (q_blk * d)
        for t0 in range(0, n_blocks, DIL_GROUP):
            blocks = []
            for t in range(t0, t0 + DIL_GROUP):
                r, n = divmod(t, n_per_r)
                blocks.append((n * (q_blk * d) + r, (n - 1) * (q_blk * d) + r, n))
            groups.append((bi, d, blocks))
    s_bufs = (s0_ref, s1_ref)
    p_bufs = (p0_ref, p1_ref)

    def window(prev_ref, cur_ref, k_start, d):
        if k_start >= 0:
            return cur_ref[0, pl.ds(k_start, 2 * q_blk, stride=d), :]
        return jnp.concatenate([prev_ref[0, pl.ds(sup + k_start, q_blk, stride=d), :],
                                cur_ref[0, pl.ds(k_start + q_blk * d, q_blk, stride=d), :]], axis=0)

    def residue_window(ref, q_start):
        cols = slice(q_start * LANE, (q_start + 1) * LANE)
        return jnp.concatenate([ref[1 - slot, :, cols], ref[slot, :, cols]], axis=0)

    def scores(gi):
        bi, d, blocks = groups[gi]
        for g, (q_start, k_start, n) in enumerate(blocks):
            if d == n_blocks:
                q = qw_ref[:, q_start * LANE:(q_start + 1) * LANE] * scale
                kb = residue_window(kw_ref, q_start)
            else:
                q = q_ref[0, pl.ds(q_start, q_blk, stride=d), :] * scale
                kb = window(kp_ref, kc_ref, k_start, d)
            q2 = jnp.concatenate([jnp.where(head0, q, 0.0), jnp.where(head0, 0.0, q)], axis=0)
            bias = bias_ref[first_bias] if n == 0 else bias_ref[0]
            s_bufs[gi % 2][g] = _dot_nt(q2, kb) + bias

    def softmax(gi):
        bi, d, blocks = groups[gi]
        for g, (q_start, k_start, n) in enumerate(blocks):
            s = s_bufs[gi % 2][g]
            m2 = jnp.max(s, axis=-1, keepdims=True)
            p_bufs[gi % 2][g] = jnp.exp2(s - m2).astype(BF16)
            m_ref[bi, pl.ds(q_start, q_blk, stride=d), :] = jnp.where(head0, m2[:q_blk], m2[q_blk:])

    def values(gi):
        bi, d, blocks = groups[gi]
        for g, (q_start, k_start, n) in enumerate(blocks):
            if d == n_blocks:
                vb = residue_window(vw_ref, q_start).astype(BF16)
            else:
                vb = window(vp_ref, vc_ref, k_start, d).astype(BF16)
            o2 = jnp.dot(p_bufs[gi % 2][g], jnp.concatenate([vb, jnp.ones_like(vb)], axis=1),
                         preferred_element_type=F32)
            rows = pl.ds(q_start, q_blk, stride=d)
            acc_ref[bi, rows, :] = jnp.where(head0, o2[:q_blk, :LANE], o2[q_blk:, :LANE])
            l_ref[bi, rows, :] = jnp.where(head0, o2[:q_blk, LANE:], o2[q_blk:, LANE:])

    n_narrow = sum(1 for _, d, _ in groups if d != n_blocks)
    for step in range(len(groups) + 2):
        if step < n_narrow and step % (n_narrow // n_blocks) == 0:
            relay(step // (n_narrow // n_blocks))
        if step < len(groups):
            scores(step)
        if 0 <= step - 1 < len(groups):
            softmax(step - 1)
        if step - 2 >= 0:
            values(step - 2)

    def merge(c, carry):
        rows = pl.ds(pl.multiple_of(c * q_blk, q_blk), q_blk)
        ms = [m_ref[bi, rows, :] for bi in range(len(DIL_DILATIONS))]
        m_all = functools.reduce(jnp.maximum, ms)
        ws = [jnp.exp2(m - m_all) for m in ms]
        num = functools.reduce(lambda a, b: a + b, [w * acc_ref[bi, rows, :] for bi, w in enumerate(ws)])
        den = functools.reduce(lambda a, b: a + b, [w * l_ref[bi, rows, :] for bi, w in enumerate(ws)])
        o_ref[0, rows, :] = (num / den).astype(o_ref.dtype)
        return carry

    lax.fori_loop(0, n_blocks, merge, 0)


def _dilated_bias():
    q_blk = DIL_BLOCK
    qi = np.arange(2 * q_blk)[:, None] % q_blk
    kj = np.arange(2 * q_blk)[None, :]
    band = (kj >= qi) & (kj <= qi + q_blk)
    return jnp.asarray(np.stack([np.where(band, 0.0, NEG), np.where(band & (kj >= q_blk), 0.0, NEG)]), F32)


def _dilated(slabs, b):
    n_slabs, t, _ = slabs.shape
    s = t // b
    h3 = slabs.reshape(n_slabs * b, s, LANE)
    sup = DIL_SUPER
    n_pairs = DIL_HEADS * DIL_HEAD_DIM // LANE
    n_br = len(DIL_DILATIONS)
    cq, ck, cv = 0, n_pairs, 2 * n_pairs
    cur = lambda c: (lambda bi, p, j: ((c + p) * b + bi, j, 0))
    prev = lambda c: (lambda bi, p, j: ((c + p) * b + bi, jnp.maximum(j - 1, 0), 0))
    blk = (1, sup, LANE)
    d_wide = sup // DIL_BLOCK
    assert d_wide == max(DIL_DILATIONS)
    wide = (DIL_BLOCK, d_wide * LANE)
    return pl.pallas_call(
        _dilated_kernel,
        grid=(b, n_pairs, s // sup),
        in_specs=[pl.BlockSpec(blk, cur(cq)),
                  pl.BlockSpec(blk, prev(ck)), pl.BlockSpec(blk, cur(ck)),
                  pl.BlockSpec(blk, prev(cv)), pl.BlockSpec(blk, cur(cv)),
                  pl.BlockSpec((2, 2 * DIL_BLOCK, 2 * DIL_BLOCK), lambda bi, p, j: (0, 0, 0))],
        out_specs=pl.BlockSpec(blk, lambda bi, p, j: (bi, j, p)),
        out_shape=jax.ShapeDtypeStruct((b, s, n_pairs * LANE), BF16),
        scratch_shapes=[pltpu.VMEM((n_br, sup, LANE), F32), pltpu.VMEM((n_br, sup, LANE), F32),
                        pltpu.VMEM((n_br, sup, LANE), F32),
                        pltpu.VMEM((DIL_GROUP, 2 * DIL_BLOCK, 2 * DIL_BLOCK), F32),
                        pltpu.VMEM((DIL_GROUP, 2 * DIL_BLOCK, 2 * DIL_BLOCK), F32),
                        pltpu.VMEM((DIL_GROUP, 2 * DIL_BLOCK, 2 * DIL_BLOCK), BF16),
                        pltpu.VMEM((DIL_GROUP, 2 * DIL_BLOCK, 2 * DIL_BLOCK), BF16),
                        pltpu.VMEM(wide, F32), pltpu.VMEM((2,) + wide, F32), pltpu.VMEM((2,) + wide, F32)],
        compiler_params=_params("parallel", "parallel", "arbitrary"),
        name="dilated",
    )(h3, h3, h3, h3, h3, _dilated_bias())


def _mem_kv(mem2d, w_bf16, tn=512):
    m, d = mem2d.shape
    n = w_bf16.shape[1]
    return pl.pallas_call(
        _proj_in_kernel,
        grid=(n // tn,),
        in_specs=[pl.BlockSpec((m, d), lambda i: (0, 0)),
                  pl.BlockSpec((d, tn), lambda i: (0, i))],
        out_specs=pl.BlockSpec((m, tn), lambda i: (0, i)),
        out_shape=jax.ShapeDtypeStruct((m, n), BF16),
        compiler_params=_params("parallel"),
        name="mem_kv",
    )(mem2d, w_bf16)


def _xattn_kernel(alpha, x_ref, yr_ref, yd_ref, wout_ref, g1_ref, b1_ref, k_ref, v_ref, wq_ref, wo_ref,
                  g_ref, b_ref, wr_ref, o_ref, ot_ref, lg_ref, x1_ref, q_ref, att_ref):
    tm, d = x_ref.shape
    dh = d // XATTN_HEADS
    wr = yr_ref.shape[1]
    scale = dh ** -0.5 * math.log2(math.e)
    part = tm // XATTN_PARTS

    def mix(rows):
        y = _dot(yr_ref[rows, :], wout_ref[0:wr, :]) + _dot(yd_ref[rows, :], wout_ref[wr:, :])
        x1_ref[rows, :] = _layer_norm(alpha * x_ref[rows, :] + y, g1_ref[...], b1_ref[...])

    def query(rows):
        q_ref[rows, :] = _dot(x1_ref[rows, :], wq_ref[...]).astype(BF16)

    def attend(rows):
        for h in range(XATTN_HEADS):
            sl = slice(h * dh, (h + 1) * dh)
            s = _dot_nt(q_ref[rows, sl], k_ref[0, :, sl]) * scale
            m = jnp.max(s, axis=-1, keepdims=True)
            e = jnp.exp2(s - m)
            p = e / jnp.sum(e, axis=-1, keepdims=True)
            att_ref[rows, sl] = _dot(p, v_ref[0, :, sl]).astype(BF16)

    def finish(rows, r0):
        y = _dot(att_ref[rows, :], wo_ref[...])
        x2 = _layer_norm(alpha * x1_ref[rows, :] + y, g_ref[...], b_ref[...])
        o_ref[rows, :] = x2
        for c in range(d // LANE):
            ot_ref[pl.ds(r0 * SUBLANE + c, part, stride=SUBLANE), :] = x2[:, c * LANE:(c + 1) * LANE]
        lg_ref[:, rows] = _dot_nt(wr_ref[...], x2)

    stages = (mix, query, attend, finish)
    for step in range(XATTN_PARTS + len(stages) - 1):
        for si, stage in enumerate(stages):
            pi = step - si
            if 0 <= pi < XATTN_PARTS:
                rows = slice(pi * part, (pi + 1) * part)
                if stage is finish:
                    stage(rows, pi * part)
                else:
                    stage(rows)


def _xattn(x2d, y_ret, y_dil, w_out, g1, b1, kv, wq, wo, g, bb, w_r, alpha, seq, tm=1024):
    t, d = x2d.shape
    mlen = kv.shape[1]
    tiles_per_seq = seq // tm
    row = lambda i: (i, 0)

    def whole(shape):
        return pl.BlockSpec(shape, lambda i: (0,) * len(shape), pipeline_mode=pl.Buffered(1))

    vec = whole((1, d))
    return pl.pallas_call(
        functools.partial(_xattn_kernel, alpha),
        grid=(t // tm,),
        in_specs=[pl.BlockSpec((tm, d), row),
                  pl.BlockSpec((tm, y_ret.shape[1]), row),
                  pl.BlockSpec((tm, y_dil.shape[1]), row),
                  whole(w_out.shape), vec, vec,
                  pl.BlockSpec((1, mlen, d), lambda i: (i // tiles_per_seq, 0, 0)),
                  pl.BlockSpec((1, mlen, d), lambda i: (i // tiles_per_seq, 0, 1)),
                  whole((d, d)), whole((d, d)), vec, vec,
                  whole((ROUTE_ROWS, d))],
        out_specs=[pl.BlockSpec((tm, d), row),
                   pl.BlockSpec((tm * SUBLANE, LANE), row),
                   pl.BlockSpec((ROUTE_ROWS, tm), lambda i: (0, i))],
        out_shape=[jax.ShapeDtypeStruct((t, d), F32),
                   jax.ShapeDtypeStruct((t * SUBLANE, LANE), F32),
                   jax.ShapeDtypeStruct((ROUTE_ROWS, t), F32)],
        scratch_shapes=[pltpu.VMEM((tm, d), F32), pltpu.VMEM((tm, d), BF16), pltpu.VMEM((tm, d), BF16)],
        compiler_params=_params("parallel"),
        name="xattn",
    )(x2d, y_ret, y_dil, w_out, g1, b1, kv, kv, wq, wo, g, bb, w_r)


def _route_kernel(lg_ref, before_ref, meta_ref, cnt_ref, carry_ref):
    i = pl.program_id(0)

    @pl.when(i == 0)
    def _():
        carry_ref[...] = jnp.zeros_like(carry_ref)

    rows, tt = lg_ref.shape
    sub = before_ref.shape[0]
    r = lax.broadcasted_iota(jnp.int32, (rows, sub), 0)
    r8 = lax.broadcasted_iota(jnp.int32, (SUBLANE, sub), 0)
    is_group = r < N_GROUPS

    def col_max(a):
        return jnp.max(a, axis=0, keepdims=True)

    def first_row_where(mask):
        return jnp.min(jnp.where(mask, r, rows), axis=0, keepdims=True)

    for c in range(tt // sub):
        lg = lg_ref[:, c * sub:(c + 1) * sub]
        mg = col_max(jnp.where(is_group, lg, NEG))
        eg = jnp.where(is_group, jnp.exp(lg - mg), 0.0)
        pg = eg / jnp.sum(eg, axis=0, keepdims=True)
        g1 = col_max(pg)
        gi = first_row_where(is_group & (pg == g1))
        lo = N_GROUPS + gi * EXPERTS_PER_GROUP
        in_grp = (r >= lo) & (r < lo + EXPERTS_PER_GROUP)
        v1 = col_max(jnp.where(in_grp, lg, NEG))
        i1 = first_row_where(in_grp & (lg == v1))
        rest = in_grp & (r != i1)
        v2 = col_max(jnp.where(rest, lg, NEG))
        i2 = first_row_where(rest & (lg == v2))
        e2 = jnp.exp(v2 - v1)
        den = 1.0 + e2
        gate1 = g1 * (1.0 / den)
        gate2 = g1 * (e2 / den)
        sel1 = r == i1
        sel2 = r == i2
        onehot = jnp.where(sel1 | sel2, 1.0, 0.0)
        rank = _dot(onehot, before_ref[...]) + carry_ref[:, 0:1]
        r1 = jnp.sum(jnp.where(sel1, rank, 0.0), axis=0, keepdims=True)
        r2 = jnp.sum(jnp.where(sel2, rank, 0.0), axis=0, keepdims=True)
        carry_ref[...] = carry_ref[...] + jnp.sum(onehot, axis=1, keepdims=True)
        meta = jnp.where(r8 == 0, (i1 - N_GROUPS).astype(F32), 0.0)
        meta = jnp.where(r8 == 1, (i2 - N_GROUPS).astype(F32), meta)
        meta = jnp.where(r8 == 2, r1, meta)
        meta = jnp.where(r8 == 3, r2, meta)
        meta = jnp.where(r8 == 4, gate1, meta)
        meta = jnp.where(r8 == 5, gate2, meta)
        meta_ref[:, c * sub:(c + 1) * sub] = meta
    cnt_ref[...] = carry_ref[...]


def _route(logits_t, tt=1024, sub=256):
    rows, t = logits_t.shape
    before = jnp.asarray(np.arange(sub)[:, None] < np.arange(sub)[None, :], BF16)
    return pl.pallas_call(
        _route_kernel,
        grid=(t // tt,),
        in_specs=[pl.BlockSpec((rows, tt), lambda i: (0, i)),
                  pl.BlockSpec((sub, sub), lambda i: (0, 0))],
        out_specs=[pl.BlockSpec((SUBLANE, tt), lambda i: (0, i)),
                   pl.BlockSpec((rows, LANE), lambda i: (0, 0))],
        out_shape=[jax.ShapeDtypeStruct((SUBLANE, t), F32),
                   jax.ShapeDtypeStruct((rows, LANE), F32)],
        scratch_shapes=[pltpu.VMEM((rows, LANE), F32)],
        compiler_params=_params("arbitrary"),
        name="route",
    )(logits_t, before)


def _plan_kernel(meta_ref, first_row_ref, dest_ref):
    n_tiles = dest_ref.shape[0]
    tt = dest_ref.shape[2] // 2
    rows = first_row_ref.shape[0]
    first_row = first_row_ref[:, 0:1]
    r = lax.broadcasted_iota(jnp.int32, (rows, tt), 0)
    for g in range(n_tiles):
        m = meta_ref[:, g * tt:(g + 1) * tt]

        def dest_of(k):
            e_row = m[k:k + 1, :].astype(jnp.int32) + N_GROUPS
            return jnp.sum(jnp.where(r == e_row, first_row, 0.0), axis=0, keepdims=True) + m[2 + k:3 + k, :]

        dest_ref[g] = jnp.concatenate([dest_of(0), dest_of(1)], axis=1).astype(jnp.int32)


def _plan(meta_t, first_row, tt, tiles_per_step=4):
    t = meta_t.shape[1]
    rows = first_row.shape[0]
    return pl.pallas_call(
        _plan_kernel,
        grid=(t // (tt * tiles_per_step),),
        in_specs=[pl.BlockSpec((SUBLANE, tt * tiles_per_step), lambda i: (0, i)),
                  pl.BlockSpec((rows, LANE), lambda i: (0, 0))],
        out_specs=pl.BlockSpec((tiles_per_step, 1, 2 * tt), lambda i: (i, 0, 0)),
        out_shape=jax.ShapeDtypeStruct((t // tt, 1, 2 * tt), jnp.int32),
        compiler_params=_params("parallel"),
        name="plan",
    )(meta_t, first_row)


def _row_copy(src, dst, s_row, d_row, sem):
    return pltpu.make_async_copy(src.at[pl.ds(pl.multiple_of(s_row * SUBLANE, SUBLANE), SUBLANE), :],
                                 dst.at[pl.ds(pl.multiple_of(d_row * SUBLANE, SUBLANE), SUBLANE), :], sem)


def _dispatch_kernel(dest_ref, xt_ref, xs_ref, sem):
    tq = dest_ref.shape[2] // 2

    def issue(t, c):
        _row_copy(xt_ref, xs_ref, t, dest_ref[0, 0, t], sem).start(priority=0)
        _row_copy(xt_ref, xs_ref, t, dest_ref[0, 0, tq + t], sem).start(priority=1)
        return c

    lax.fori_loop(0, tq, issue, 0, unroll=8)
    for _ in range(2):
        pltpu.make_async_copy(xt_ref, xs_ref.at[pl.ds(0, tq * SUBLANE), :], sem).wait()


def _dispatch(dest3, x_tiled, n_rows, tq=4096):
    n_tiles, _, two_tt = dest3.shape
    tt = two_tt // 2
    tiles_per_step = tq // tt
    n_steps = n_tiles // tiles_per_step
    dest3 = dest3.reshape(n_steps, tiles_per_step, 2, tt).transpose(0, 2, 1, 3).reshape(n_steps, 1, 2 * tq)
    return pl.pallas_call(
        _dispatch_kernel,
        grid=(n_steps,),
        in_specs=[pl.BlockSpec((1, 1, 2 * tq), lambda i: (i, 0, 0), memory_space=pltpu.SMEM),
                  pl.BlockSpec((tq * SUBLANE, LANE), lambda i: (i, 0))],
        out_specs=pl.BlockSpec(memory_space=pl.ANY),
        out_shape=jax.ShapeDtypeStruct((n_rows * SUBLANE, LANE), F32),
        scratch_shapes=[pltpu.SemaphoreType.DMA(())],
        compiler_params=_params("arbitrary"),
        name="dispatch",
    )(dest3, x_tiled)


def _experts_kernel(blk_exp_ref, blk_rows_ref, n_used_ref, blk_first_ref, blk_slot_ref, blk_next_ref,
                    xs_ref, wg_hbm, wu_hbm, wd_hbm, ys_ref, x_scr, y_scr, wg_buf, wu_buf, wd_buf, sems):
    b = pl.program_id(0)

    def weight_copies(e, slot):
        return [pltpu.make_async_copy(hbm.at[e], buf.at[slot], sems.at[k, slot])
                for k, (hbm, buf) in enumerate(((wg_hbm, wg_buf), (wu_hbm, wu_buf), (wd_hbm, wd_buf)))]

    @pl.when(b < n_used_ref[0])
    def _():
        slot = blk_slot_ref[b]

        @pl.when(b == 0)
        def _():
            for cp in weight_copies(blk_exp_ref[0], slot):
                cp.start()

        @pl.when(blk_first_ref[b] == 1)
        def _():
            @pl.when(blk_next_ref[b] >= 0)
            def _():
                for cp in weight_copies(blk_next_ref[b], 1 - slot):
                    cp.start()

            for cp in weight_copies(blk_exp_ref[b], slot):
                cp.wait()

        rows, d = x_scr.shape
        n_valid = blk_rows_ref[b]

        def run(m):
            ri = lax.broadcasted_iota(jnp.int32, (m, 1), 0)
            for c in range(d // LANE):
                x = xs_ref[pl.ds(c, m, stride=SUBLANE), :]
                x_scr[0:m, c * LANE:(c + 1) * LANE] = jnp.where(ri < n_valid, x, 0.0).astype(BF16)
            x = x_scr[0:m, :]
            hg = _dot(x, wg_buf[slot])
            hu = _dot(x, wu_buf[slot])
            y_scr[0:m, :] = _dot(hg * jax.nn.sigmoid(hg) * hu, wd_buf[slot])
            for c in range(d // LANE):
                ys_ref[pl.ds(c, m, stride=SUBLANE), :] = y_scr[0:m, c * LANE:(c + 1) * LANE]

        n_quanta = rows // EXPERT_QUANTUM
        for k in range(1, n_quanta + 1):
            lo = (k - 1) * EXPERT_QUANTUM
            pl.when((n_valid > lo) & (n_valid <= k * EXPERT_QUANTUM))(functools.partial(run, k * EXPERT_QUANTUM))


def _experts(blk_exp, blk_rows, n_used, blk_first, blk_slot, blk_next, xs_tiled, w_g, w_u, w_d):
    n_rows = xs_tiled.shape[0] // SUBLANE
    rows = EXPERT_ROWS
    nblk = n_rows // rows
    _, d, ff = w_g.shape
    used = lambda b, be, br, nu, *_: (jnp.minimum(b, nu[0] - 1), 0)
    grid_spec = pltpu.PrefetchScalarGridSpec(
        num_scalar_prefetch=6,
        grid=(nblk,),
        in_specs=[pl.BlockSpec((rows * SUBLANE, LANE), used),
                  pl.BlockSpec(memory_space=pl.ANY), pl.BlockSpec(memory_space=pl.ANY),
                  pl.BlockSpec(memory_space=pl.ANY)],
        out_specs=pl.BlockSpec((rows * SUBLANE, LANE), used),
        scratch_shapes=[pltpu.VMEM((rows, d), BF16), pltpu.VMEM((rows, d), F32),
                        pltpu.VMEM((2, d, ff), F32), pltpu.VMEM((2, d, ff), F32), pltpu.VMEM((2, ff, d), F32),
                        pltpu.SemaphoreType.DMA((3, 2))],
    )
    return pl.pallas_call(
        _experts_kernel,
        grid_spec=grid_spec,
        out_shape=jax.ShapeDtypeStruct((n_rows * SUBLANE, LANE), F32),
        compiler_params=_params("arbitrary"),
        name="experts",
    )(blk_exp, blk_rows, n_used, blk_first, blk_slot, blk_next, xs_tiled, w_g, w_u, w_d)


def _combine_kernel(alpha, dcur_ref, dnext_ref, meta_ref, x_ref, g_ref, b_ref, ys_ref, o_ref, buf_ref, sems):
    i = pl.program_id(0)
    tq, d = x_ref.shape
    slot_rows = 2 * tq * SUBLANE

    def issue(d_ref, slot, t):
        _row_copy(ys_ref, buf_ref, d_ref[0, 0, t], slot * (2 * tq) + t, sems.at[slot]).start(priority=0)
        _row_copy(ys_ref, buf_ref, d_ref[0, 0, tq + t], slot * (2 * tq) + tq + t, sems.at[slot]).start(priority=1)

    def wait_slot(slot):
        off = pl.multiple_of(slot * slot_rows, slot_rows)
        pltpu.make_async_copy(ys_ref.at[pl.ds(0, slot_rows), :], buf_ref.at[pl.ds(off, slot_rows), :],
                              sems.at[slot]).wait()
        return off

    def issue_all(d_ref, slot):
        lax.fori_loop(0, tq, lambda t, c: (issue(d_ref, slot, t), c)[1], 0, unroll=4)

    slot = i % 2

    @pl.when(i == 0)
    def _():
        issue_all(dcur_ref, 0)

    @pl.when(i + 1 < pl.num_programs(0))
    def _():
        issue_all(dnext_ref, 1 - slot)

    off = wait_slot(slot)
    meta_rows = jnp.transpose(jnp.concatenate([meta_ref[...], jnp.zeros((LANE - SUBLANE, tq), F32)], axis=0))
    gate1 = meta_rows[:, 4:5]
    gate2 = meta_rows[:, 5:6]
    for c in range(d // LANE):
        sl = slice(c * LANE, (c + 1) * LANE)
        y1 = buf_ref[pl.ds(off + c, tq, stride=SUBLANE), :]
        y2 = buf_ref[pl.ds(off + tq * SUBLANE + c, tq, stride=SUBLANE), :]
        o_ref[:, sl] = alpha * x_ref[:, sl] + (y1 * gate1 + y2 * gate2)
    o_ref[...] = _layer_norm(o_ref[...], g_ref[...], b_ref[...])


def _combine(dest3, meta, x2, g, bb, ys_tiled, alpha):
    t, d = x2.shape
    n = dest3.shape[0]
    tq = t // n
    row = lambda i: (i, 0)
    const = lambda i: (0, 0)
    return pl.pallas_call(
        functools.partial(_combine_kernel, alpha),
        grid=(n,),
        in_specs=[pl.BlockSpec((1, 1, 2 * tq), lambda i: (i, 0, 0), memory_space=pltpu.SMEM),
                  pl.BlockSpec((1, 1, 2 * tq), lambda i: (jnp.minimum(i + 1, n - 1), 0, 0), memory_space=pltpu.SMEM),
                  pl.BlockSpec((SUBLANE, tq), lambda i: (0, i)),
                  pl.BlockSpec((tq, d), row),
                  pl.BlockSpec((1, d), const), pl.BlockSpec((1, d), const),
                  pl.BlockSpec(memory_space=pl.ANY)],
        out_specs=pl.BlockSpec((tq, d), row),
        out_shape=jax.ShapeDtypeStruct((t, d), F32),
        scratch_shapes=[pltpu.VMEM((2 * 2 * tq * SUBLANE, LANE), F32), pltpu.SemaphoreType.DMA((2,))],
        compiler_params=_params("arbitrary"),
        name="combine",
    )(dest3, dest3, meta, x2, g, bb, ys_tiled)


def _moe(x2, x2_tiled, logits, w_g, w_u, w_d, g, bb, alpha):
    t, d = x2.shape
    meta, cnt = _route(logits)
    counts = cnt[N_GROUPS:N_GROUPS + N_EXPERTS, 0].astype(jnp.int32)
    padded = ((counts + EXPERT_ROWS - 1) // EXPERT_ROWS) * EXPERT_ROWS
    pend = jnp.cumsum(padded)
    poff = pend - padded
    first_row = jnp.pad(poff.astype(F32), (N_GROUPS, ROUTE_ROWS - N_GROUPS - N_EXPERTS))
    dest3 = _plan(meta, jnp.broadcast_to(first_row[:, None], (ROUTE_ROWS, LANE)), ROUTE_TILE)
    n_rows = t * 2 + N_EXPERTS * EXPERT_ROWS
    nblk = n_rows // EXPERT_ROWS
    blk_start = jnp.arange(nblk, dtype=jnp.int32) * EXPERT_ROWS
    blk_exp = jnp.sum((pend[None, :] <= blk_start[:, None]).astype(jnp.int32), axis=1)
    blk_exp = jnp.minimum(blk_exp, N_EXPERTS - 1)
    e_idx = jnp.arange(N_EXPERTS, dtype=jnp.int32)
    owner = blk_exp[:, None] == e_idx[None, :]

    def of_block(per_expert):
        return jnp.sum(jnp.where(owner, per_expert[None, :], 0), axis=1).astype(jnp.int32)

    blk_rows = jnp.clip(of_block(poff + counts) - blk_start, 0, EXPERT_ROWS).astype(jnp.int32)
    n_used = (pend[-1:] // EXPERT_ROWS).astype(jnp.int32)
    has_rows = counts > 0
    ordinal = jnp.cumsum(has_rows.astype(jnp.int32)) - 1
    later = has_rows[None, :] & (e_idx[None, :] > e_idx[:, None])
    next_used = jnp.min(jnp.where(later, e_idx[None, :], N_EXPERTS), axis=1)
    next_used = jnp.where(next_used == N_EXPERTS, -1, next_used).astype(jnp.int32)
    blk_first = (blk_start == of_block(poff)).astype(jnp.int32)
    blk_slot = of_block(ordinal % 2)
    blk_next = of_block(next_used)
    xs_tiled = _dispatch(dest3, x2_tiled, n_rows)
    ys_tiled = _experts(blk_exp, blk_rows, n_used, blk_first, blk_slot, blk_next, xs_tiled, w_g, w_u, w_d)
    return _combine(dest3, meta, x2, g, bb, ys_tiled, alpha)


def kernel(x, mem, w_in, w_out, ln_mix_g, ln_mix_b, w_xq, w_xkv, w_xo, ln_x_g, ln_x_b, w_route_group,
           w_route_expert, w_exp_gate, w_exp_up, w_exp_down, ln_moe_g, ln_moe_b):
    b, s, d = x.shape
    depth = w_in.shape[0]
    alpha = (2.0 * depth) ** 0.25
    t = b * s
    ret_cols = (2 * RET_HEADS * RET_QK_DIM + 2 * RET_HEADS * RET_V_DIM)
    assert s % DIL_SUPER == 0 and s % (RET_STEP_CHUNKS * RET_CHUNK) == 0, "sequence length not tileable"
    assert w_in.shape[2] == ret_cols + 3 * DIL_HEADS * DIL_HEAD_DIM and d % (XATTN_HEADS * LANE) == 0
    assert w_route_group.shape[2] == N_GROUPS and w_route_expert.shape[2] == N_EXPERTS
    xc = x.reshape(t, d)
    for l in range(depth):
        n_qk = 2 * RET_HEADS * RET_QK_DIM
        h_ret, h_dil = _proj_in(xc, w_in[l], _retention_weight_layout(w_in[l][:, :n_qk]).astype(BF16), ret_cols)
        y_ret = _retention(h_ret.reshape(b, s, ret_cols)).reshape(t, -1)
        y_dil = _dilated(h_dil, b).reshape(t, -1)
        kv = _mem_kv(mem.reshape(b * mem.shape[1], d), w_xkv[l]).reshape(b, mem.shape[1], 2 * d)
        w_r = jnp.concatenate([w_route_group[l], w_route_expert[l]], axis=-1)
        w_r = jnp.pad(w_r.T, ((0, ROUTE_ROWS - w_r.shape[1]), (0, 0))).astype(BF16)
        x2, x2_tiled, logits = _xattn(xc, y_ret, y_dil, w_out[l].astype(BF16), ln_mix_g[l][None], ln_mix_b[l][None],
                                      kv, w_xq[l].astype(BF16), w_xo[l].astype(BF16),
                                      ln_x_g[l][None], ln_x_b[l][None], w_r, alpha, s)
        xc = _moe(x2, x2_tiled, logits, w_exp_gate[l], w_exp_up[l], w_exp_down[l],
                  ln_moe_g[l][None], ln_moe_b[l][None], alpha)
    return xc.reshape(b, s, d)
```

```python
import functools
import math

import jax
import jax.numpy as jnp
import numpy as np
from jax import lax
from jax.experimental import pallas as pl
from jax.experimental.pallas import tpu as pltpu

BF16 = jnp.bfloat16
F32 = jnp.float32

LANE = 128
SUBLANE = 8
VMEM_LIMIT = 56 * 1024 * 1024

RET_HEADS = 4
RET_QK_DIM = 64
RET_V_DIM = 128
RET_CHUNK = 128
RET_STEP_CHUNKS = 8
ROPE_BASE = 10000.0
DIL_HEADS = 8
DIL_HEAD_DIM = 64
DIL_DILATIONS = (1, 4, 16)
DIL_BLOCK = 128
DIL_SUPER = DIL_BLOCK * max(DIL_DILATIONS)
DIL_GROUP = 1
XATTN_HEADS = 4
XATTN_PARTS = 2
N_GROUPS = 4
EXPERTS_PER_GROUP = 8
N_EXPERTS = N_GROUPS * EXPERTS_PER_GROUP
ROUTE_ROWS = -(-(N_GROUPS + N_EXPERTS) // SUBLANE) * SUBLANE
EXPERT_ROWS = 512
EXPERT_QUANTUM = 256
ROUTE_TILE = 512
LN_EPS = 1e-5
GN_EPS = 1e-6
NEG = -1e30


def _params(*sem):
    return pltpu.CompilerParams(dimension_semantics=sem, vmem_limit_bytes=VMEM_LIMIT)


def _layer_norm(z, g, b):
    mu = jnp.mean(z, axis=-1, keepdims=True)
    zc = z - mu
    var = jnp.mean(zc * zc, axis=-1, keepdims=True)
    return zc * lax.rsqrt(var + LN_EPS) * g + b


def _dot(a, b):
    return jnp.dot(a.astype(BF16), b.astype(BF16), preferred_element_type=F32)


def _dot_nt(a, b):
    return lax.dot_general(a.astype(BF16), b.astype(BF16), (((1,), (1,)), ((), ())),
                           preferred_element_type=F32)


def _dot_tn(a, b):
    return lax.dot_general(a.astype(BF16), b.astype(BF16), (((0,), (0,)), ((), ())),
                           preferred_element_type=F32)


def _proj_in_kernel(x_ref, w_ref, o_ref):
    o_ref[...] = _dot(x_ref[...], w_ref[...]).astype(o_ref.dtype)


def _proj_in2_kernel(x_ref, w_ref, w_head_ref, o_ref, slab_ref, w_bf16):
    @pl.when(pl.program_id(0) == 0)
    def _():
        w_bf16[...] = w_ref[...].astype(BF16)
        w_bf16[:, 0:w_head_ref.shape[1]] = w_head_ref[...]

    n_row = o_ref.shape[1]
    x = x_ref[...].astype(BF16)
    o_ref[...] = jnp.dot(x, w_bf16[:, 0:n_row], preferred_element_type=F32)
    rest = jnp.dot(x, w_bf16[:, n_row:], preferred_element_type=F32)
    for c in range(slab_ref.shape[0]):
        slab_ref[c] = rest[:, c * LANE:(c + 1) * LANE]


def _proj_in(x2d, w_f32, w_head_bf16, n_row, tm=1024):
    t, d = x2d.shape
    n = w_f32.shape[1]
    n_slabs = (n - n_row) // LANE
    whole = lambda shape: pl.BlockSpec(shape, lambda i: (0, 0), pipeline_mode=pl.Buffered(1))
    return pl.pallas_call(
        _proj_in2_kernel,
        grid=(t // tm,),
        in_specs=[pl.BlockSpec((tm, d), lambda i: (i, 0)), whole((d, n)), whole(w_head_bf16.shape)],
        out_specs=[pl.BlockSpec((tm, n_row), lambda i: (i, 0)),
                   pl.BlockSpec((n_slabs, tm, LANE), lambda i: (0, i, 0))],
        out_shape=[jax.ShapeDtypeStruct((t, n_row), F32),
                   jax.ShapeDtypeStruct((n_slabs, t, LANE), F32)],
        scratch_shapes=[pltpu.VMEM((d, n), BF16)],
        compiler_params=_params("arbitrary"),
        name="proj_in",
    )(x2d, w_f32, w_head_bf16)


def _retention_kernel(qk_ref, v_ref, g_ref, cos_ref, sin_ref, decay_ref, zeta_ref, xi_ref, gam_ref,
                      o_ref, state_ref, y_ref):
    n = pl.program_id(1)

    @pl.when(n == 0)
    def _():
        state_ref[...] = jnp.zeros_like(state_ref)

    c = RET_CHUNK
    n_sub = qk_ref.shape[1] // c
    half = RET_QK_DIM // 2
    lane = lax.broadcasted_iota(jnp.int32, (c, LANE), 1)

    rotated = []
    for j in range(n_sub):
        rows = slice(j * c, (j + 1) * c)
        cos = cos_ref[rows, :]
        sin = sin_ref[rows, :]

        def rot(col, rows=rows, cos=cos, sin=sin):
            t1 = qk_ref[0, rows, col * LANE:(col + 1) * LANE]
            t2 = qk_ref[0, rows, (col + 1) * LANE:(col + 2) * LANE]
            return t1 * cos - t2 * sin, t1 * sin + t2 * cos

        q1, q2 = rot(0)
        k1, k2 = (t * (RET_QK_DIM ** -0.5) for t in rot(2))
        rotated.append((q1, q2, k1, k2, jnp.concatenate([k1, k2], axis=1)))

    for h in range(RET_HEADS):
        cols = slice(h * RET_V_DIM, (h + 1) * RET_V_DIM)
        mine = (lane >= h * half) & (lane < (h + 1) * half)
        zeta = zeta_ref[h]
        st = state_ref[h]
        pad_lo = [jnp.zeros((h * half, RET_V_DIM), BF16)] if h > 0 else []
        pad_hi = [jnp.zeros(((RET_HEADS - 1 - h) * half, RET_V_DIM), BF16)] if h < RET_HEADS - 1 else []
        for j in range(n_sub):
            rows = slice(j * c, (j + 1) * c)
            q1, q2, k1, k2, k_all = rotated[j]
            qm = jnp.concatenate([jnp.where(mine, q1, 0.0), jnp.where(mine, q2, 0.0)], axis=1)
            v = v_ref[0, rows, cols]
            s = _dot_nt(qm, k_all) * decay_ref[h]
            sb = st.astype(BF16)
            st_rows = jnp.concatenate(pad_lo + [sb[:half]] + pad_hi + pad_lo + [sb[half:]] + pad_hi, axis=0)
            y_ref[rows, cols] = _dot(s, v) + _dot(qm, st_rows) * xi_ref[h]
            kv = _dot_tn(jnp.concatenate([k1 * zeta, k2 * zeta], axis=1), v)
            lo = h * half
            hi = RET_HEADS * half + h * half
            st = gam_ref[h, :, :] * st + jnp.concatenate([kv[lo:lo + half], kv[hi:hi + half]], axis=0)
        state_ref[h] = st

    for h in range(RET_HEADS):
        cols = slice(h * RET_V_DIM, (h + 1) * RET_V_DIM)
        y = y_ref[:, cols]
        mu = jnp.mean(y, axis=-1, keepdims=True)
        yc = y - mu
        var = jnp.mean(yc * yc, axis=-1, keepdims=True)
        yn = yc * lax.rsqrt(var + GN_EPS)
        gate = g_ref[0, :, cols]
        o_ref[0, :, cols] = (gate * jax.nn.sigmoid(gate) * yn).astype(o_ref.dtype)


def _retention_tables(s):
    half = RET_QK_DIM // 2
    inv = 1.0 / (ROPE_BASE ** (np.arange(half, dtype=np.float64) / half))
    ang = np.arange(s, dtype=np.float64)[:, None] * inv[None, :]
    cos_t = np.tile(np.cos(ang), (1, RET_HEADS))
    sin_t = np.tile(np.sin(ang), (1, RET_HEADS))
    c = RET_CHUNK
    lg = np.log(1.0 - np.exp2(-5.0 - np.arange(RET_HEADS, dtype=np.float64)))
    idx = np.arange(c, dtype=np.float64)
    diff = idx[:, None] - idx[None, :]
    decay = np.where(diff >= 0, np.exp(lg[:, None, None] * np.maximum(diff, 0.0)), 0.0)
    lanes = (RET_HEADS, c, LANE)
    zeta = np.broadcast_to(np.exp(lg[:, None] * (c - 1.0 - idx))[:, :, None], lanes)
    xi = np.broadcast_to(np.exp(lg[:, None] * (idx + 1.0))[:, :, None], lanes)
    gam = np.broadcast_to(np.exp(lg * c)[:, None, None], (RET_HEADS, 1, LANE))
    return tuple(jnp.asarray(np.ascontiguousarray(a), F32) for a in (cos_t, sin_t, decay, zeta, xi, gam))


def _retention_weight_layout(w_qk):
    d = w_qk.shape[0]
    half = RET_QK_DIM // 2
    return w_qk.reshape(d, 2, RET_HEADS, 2, half).transpose(0, 1, 3, 2, 4).reshape(d, -1)


def _retention(h3):
    b, s, _ = h3.shape
    c = RET_CHUNK
    qk_w = 2 * RET_HEADS * RET_QK_DIM
    v_w = RET_HEADS * RET_V_DIM
    assert qk_w == v_w
    cos_t, sin_t, decay, zeta, xi, gam = _retention_tables(s)
    const3 = lambda bi, n: (0, 0, 0)
    rows = RET_STEP_CHUNKS * c
    return pl.pallas_call(
        _retention_kernel,
        grid=(b, s // rows),
        in_specs=[pl.BlockSpec((1, rows, qk_w), lambda bi, n: (bi, n, 0)),
                  pl.BlockSpec((1, rows, v_w), lambda bi, n: (bi, n, 1)),
                  pl.BlockSpec((1, rows, v_w), lambda bi, n: (bi, n, 2)),
                  pl.BlockSpec((rows, LANE), lambda bi, n: (n, 0)),
                  pl.BlockSpec((rows, LANE), lambda bi, n: (n, 0)),
                  pl.BlockSpec((RET_HEADS, c, c), const3),
                  pl.BlockSpec((RET_HEADS, c, LANE), const3),
                  pl.BlockSpec((RET_HEADS, c, LANE), const3),
                  pl.BlockSpec((RET_HEADS, 1, LANE), const3)],
        out_specs=pl.BlockSpec((1, rows, v_w), lambda bi, n: (bi, n, 0)),
        out_shape=jax.ShapeDtypeStruct((b, s, v_w), BF16),
        scratch_shapes=[pltpu.VMEM((RET_HEADS, RET_QK_DIM, RET_V_DIM), F32),
                        pltpu.VMEM((rows, v_w), F32)],
        compiler_params=_params("parallel", "arbitrary"),
        name="retention",
    )(h3, h3, h3, cos_t, sin_t, decay, zeta, xi, gam)


def _dilated_kernel(q_ref, kp_ref, kc_ref, vp_ref, vc_ref, bias_ref,
                    o_ref, acc_ref, m_ref, l_ref, s0_ref, s1_ref, p0_ref, p1_ref, qw_ref, kw_ref, vw_ref):
    j = pl.program_id(2)
    sup = DIL_SUPER
    q_blk = DIL_BLOCK
    scale = DIL_HEAD_DIM ** -0.5 * math.log2(math.e)
    n_blocks = sup // q_blk
    lane = lax.broadcasted_iota(jnp.int32, (q_blk, LANE), 1)
    head0 = lane < DIL_HEAD_DIM
    first_bias = jnp.where(j == 0, 1, 0)

    slot = j % 2

    def relay(r):
        cols = slice(r * LANE, (r + 1) * LANE)
        qw_ref[:, cols] = q_ref[0, pl.ds(r, q_blk, stride=n_blocks), :]
        kw_ref[slot, :, cols] = kc_ref[0, pl.ds(r, q_blk, stride=n_blocks), :]
        vw_ref[slot, :, cols] = vc_ref[0, pl.ds(r, q_blk, stride=n_blocks), :]

    @pl.when(j == 0)
    def _():
        kw_ref[1] = jnp.zeros(kw_ref.shape[1:], F32)
        vw_ref[1] = jnp.zeros(vw_ref.shape[1:], F32)

    groups = []
    for bi, d in enumerate(DIL_DILATIONS):
        n_per_r = sup // (q_blk * d)
        for t0 in range(0, n_blocks, DIL_GROUP):
            blocks = []
            for t in range(t0, t0 + DIL_GROUP):
                r, n = divmod(t, n_per_r)
                blocks.append((n * (q_blk * d) + r, (n - 1) * (q_blk * d) + r, n))
            groups.append((bi, d, blocks))
    s_bufs = (s0_ref, s1_ref)
    p_bufs = (p0_ref, p1_ref)

    def window(prev_ref, cur_ref, k_start, d):
        if k_start >= 0:
            return cur_ref[0, pl.ds(k_start, 2 * q_blk, stride=d), :]
        return jnp.concatenate([prev_ref[0, pl.ds(sup + k_start, q_blk, stride=d), :],
                                cur_ref[0, pl.ds(k_start + q_blk * d, q_blk, stride=d), :]], axis=0)

    def residue_window(ref, q_start):
        cols = slice(q_start * LANE, (q_start + 1) * LANE)
        return jnp.concatenate([ref[1 - slot, :, cols], ref[slot, :, cols]], axis=0)

    def scores(gi):
        bi, d, blocks = groups[gi]
        for g, (q_start, k_start, n) in enumerate(blocks):
            if d == n_blocks:
                q = qw_ref[:, q_start * LANE:(q_start + 1) * LANE] * scale
                kb = residue_window(kw_ref, q_start)
            else:
                q = q_ref[0, pl.ds(q_start, q_blk, stride=d), :] * scale
                kb = window(kp_ref, kc_ref, k_start, d)
            q2 = jnp.concatenate([jnp.where(head0, q, 0.0), jnp.where(head0, 0.0, q)], axis=0)
            bias = bias_ref[first_bias] if n == 0 else bias_ref[0]
            s_bufs[gi % 2][g] = _dot_nt(q2, kb) + bias

    def softmax(gi):
        bi, d, blocks = groups[gi]
        for g, (q_start, k_start, n) in enumerate(blocks):
            s = s_bufs[gi % 2][g]
            m2 = jnp.max(s, axis=-1, keepdims=True)
            p_bufs[gi % 2][g] = jnp.exp2(s - m2).astype(BF16)
            m_ref[bi, pl.ds(q_start, q_blk, stride=d), :] = jnp.where(head0, m2[:q_blk], m2[q_blk:])

    def values(gi):
        bi, d, blocks = groups[gi]
        for g, (q_start, k_start, n) in enumerate(blocks):
            if d == n_blocks:
                vb = residue_window(vw_ref, q_start).astype(BF16)
            else:
                vb = window(vp_ref, vc_ref, k_start, d).astype(BF16)
            o2 = jnp.dot(p_bufs[gi % 2][g], jnp.concatenate([vb, jnp.ones_like(vb)], axis=1),
                         preferred_element_type=F32)
            rows = pl.ds(q_start, q_blk, stride=d)
            acc_ref[bi, rows, :] = jnp.where(head0, o2[:q_blk, :LANE], o2[q_blk:, :LANE])
            l_ref[bi, rows, :] = jnp.where(head0, o2[:q_blk, LANE:], o2[q_blk:, LANE:])

    n_narrow = sum(1 for _, d, _ in groups if d != n_blocks)
    for step in range(len(groups) + 2):
        if step < n_narrow and step % (n_narrow // n_blocks) == 0:
            relay(step // (n_narrow // n_blocks))
        if step < len(groups):
            scores(step)
        if 0 <= step - 1 < len(groups):
            softmax(step - 1)
        if step - 2 >= 0:
            values(step - 2)

    def merge(c, carry):
        rows = pl.ds(pl.multiple_of(c * q_blk, q_blk), q_blk)
        ms = [m_ref[bi, rows, :] for bi in range(len(DIL_DILATIONS))]
        m_all = functools.reduce(jnp.maximum, ms)
        ws = [jnp.exp2(m - m_all) for m in ms]
        num = functools.reduce(lambda a, b: a + b, [w * acc_ref[bi, rows, :] for bi, w in enumerate(ws)])
        den = functools.reduce(lambda a, b: a + b, [w * l_ref[bi, rows, :] for bi, w in enumerate(ws)])
        o_ref[0, rows, :] = (num / den).astype(o_ref.dtype)
        return carry

    lax.fori_loop(0, n_blocks, merge, 0)


def _dilated_bias():
    q_blk = DIL_BLOCK
    qi = np.arange(2 * q_blk)[:, None] % q_blk
    kj = np.arange(2 * q_blk)[None, :]
    band = (kj >= qi) & (kj <= qi + q_blk)
    return jnp.asarray(np.stack([np.where(band, 0.0, NEG), np.where(band & (kj >= q_blk), 0.0, NEG)]), F32)


def _dilated(slabs, b):
    n_slabs, t, _ = slabs.shape
    s = t // b
    h3 = slabs.reshape(n_slabs * b, s, LANE)
    sup = DIL_SUPER
    n_pairs = DIL_HEADS * DIL_HEAD_DIM // LANE
    n_br = len(DIL_DILATIONS)
    cq, ck, cv = 0, n_pairs, 2 * n_pairs
    cur = lambda c: (lambda bi, p, j: ((c + p) * b + bi, j, 0))
    prev = lambda c: (lambda bi, p, j: ((c + p) * b + bi, jnp.maximum(j - 1, 0), 0))
    blk = (1, sup, LANE)
    d_wide = sup // DIL_BLOCK
    assert d_wide == max(DIL_DILATIONS)
    wide = (DIL_BLOCK, d_wide * LANE)
    return pl.pallas_call(
        _dilated_kernel,
        grid=(b, n_pairs, s // sup),
        in_specs=[pl.BlockSpec(blk, cur(cq)),
                  pl.BlockSpec(blk, prev(ck)), pl.BlockSpec(blk, cur(ck)),
                  pl.BlockSpec(blk, prev(cv)), pl.BlockSpec(blk, cur(cv)),
                  pl.BlockSpec((2, 2 * DIL_BLOCK, 2 * DIL_BLOCK), lambda bi, p, j: (0, 0, 0))],
        out_specs=pl.BlockSpec(blk, lambda bi, p, j: (bi, j, p)),
        out_shape=jax.ShapeDtypeStruct((b, s, n_pairs * LANE), BF16),
        scratch_shapes=[pltpu.VMEM((n_br, sup, LANE), F32), pltpu.VMEM((n_br, sup, LANE), F32),
                        pltpu.VMEM((n_br, sup, LANE), F32),
                        pltpu.VMEM((DIL_GROUP, 2 * DIL_BLOCK, 2 * DIL_BLOCK), F32),
                        pltpu.VMEM((DIL_GROUP, 2 * DIL_BLOCK, 2 * DIL_BLOCK), F32),
                        pltpu.VMEM((DIL_GROUP, 2 * DIL_BLOCK, 2 * DIL_BLOCK), BF16),
                        pltpu.VMEM((DIL_GROUP, 2 * DIL_BLOCK, 2 * DIL_BLOCK), BF16),
                        pltpu.VMEM(wide, F32), pltpu.VMEM((2,) + wide, F32), pltpu.VMEM((2,) + wide, F32)],
        compiler_params=_params("parallel", "parallel", "arbitrary"),
        name="dilated",
    )(h3, h3, h3, h3, h3, _dilated_bias())


def _mem_kv(mem2d, w_bf16, tn=512):
    m, d = mem2d.shape
    n = w_bf16.shape[1]
    return pl.pallas_call(
        _proj_in_kernel,
        grid=(n // tn,),
        in_specs=[pl.BlockSpec((m, d), lambda i: (0, 0)),
                  pl.BlockSpec((d, tn), lambda i: (0, i))],
        out_specs=pl.BlockSpec((m, tn), lambda i: (0, i)),
        out_shape=jax.ShapeDtypeStruct((m, n), BF16),
        compiler_params=_params("parallel"),
        name="mem_kv",
    )(mem2d, w_bf16)


def _xattn_kernel(alpha, x_ref, yr_ref, yd_ref, wout_ref, g1_ref, b1_ref, k_ref, v_ref, wq_ref, wo_ref,
                  g_ref, b_ref, wr_ref, o_ref, ot_ref, lg_ref, x1_ref, q_ref, att_ref):
    tm, d = x_ref.shape
    dh = d // XATTN_HEADS
    wr = yr_ref.shape[1]
    scale = dh ** -0.5 * math.log2(math.e)
    part = tm // XATTN_PARTS

    def mix(rows):
        y = _dot(yr_ref[rows, :], wout_ref[0:wr, :]) + _dot(yd_ref[rows, :], wout_ref[wr:, :])
        x1_ref[rows, :] = _layer_norm(alpha * x_ref[rows, :] + y, g1_ref[...], b1_ref[...])

    def query(rows):
        q_ref[rows, :] = _dot(x1_ref[rows, :], wq_ref[...]).astype(BF16)

    def attend(rows):
        for h in range(XATTN_HEADS):
            sl = slice(h * dh, (h + 1) * dh)
            s = _dot_nt(q_ref[rows, sl], k_ref[0, :, sl]) * scale
            m = jnp.max(s, axis=-1, keepdims=True)
            e = jnp.exp2(s - m)
            p = e / jnp.sum(e, axis=-1, keepdims=True)
            att_ref[rows, sl] = _dot(p, v_ref[0, :, sl]).astype(BF16)

    def finish(rows, r0):
        y = _dot(att_ref[rows, :], wo_ref[...])
        x2 = _layer_norm(alpha * x1_ref[rows, :] + y, g_ref[...], b_ref[...])
        o_ref[rows, :] = x2
        for c in range(d // LANE):
            ot_ref[pl.ds(r0 * SUBLANE + c, part, stride=SUBLANE), :] = x2[:, c * LANE:(c + 1) * LANE]
        lg_ref[:, rows] = _dot_nt(wr_ref[...], x2)

    stages = (mix, query, attend, finish)
    for step in range(XATTN_PARTS + len(stages) - 1):
        for si, stage in enumerate(stages):
            pi = step - si
            if 0 <= pi < XATTN_PARTS:
                rows = slice(pi * part, (pi + 1) * part)
                if stage is finish:
                    stage(rows, pi * part)
                else:
                    stage(rows)


def _xattn(x2d, y_ret, y_dil, w_out, g1, b1, kv, wq, wo, g, bb, w_r, alpha, seq, tm=1024):
    t, d = x2d.shape
    mlen = kv.shape[1]
    tiles_per_seq = seq // tm
    row = lambda i: (i, 0)

    def whole(shape):
        return pl.BlockSpec(shape, lambda i: (0,) * len(shape), pipeline_mode=pl.Buffered(1))

    vec = whole((1, d))
    return pl.pallas_call(
        functools.partial(_xattn_kernel, alpha),
        grid=(t // tm,),
        in_specs=[pl.BlockSpec((tm, d), row),
                  pl.BlockSpec((tm, y_ret.shape[1]), row),
                  pl.BlockSpec((tm, y_dil.shape[1]), row),
                  whole(w_out.shape), vec, vec,
                  pl.BlockSpec((1, mlen, d), lambda i: (i // tiles_per_seq, 0, 0)),
                  pl.BlockSpec((1, mlen, d), lambda i: (i // tiles_per_seq, 0, 1)),
                  whole((d, d)), whole((d, d)), vec, vec,
                  whole((ROUTE_ROWS, d))],
        out_specs=[pl.BlockSpec((tm, d), row),
                   pl.BlockSpec((tm * SUBLANE, LANE), row),
                   pl.BlockSpec((ROUTE_ROWS, tm), lambda i: (0, i))],
        out_shape=[jax.ShapeDtypeStruct((t, d), F32),
                   jax.ShapeDtypeStruct((t * SUBLANE, LANE), F32),
                   jax.ShapeDtypeStruct((ROUTE_ROWS, t), F32)],
        scratch_shapes=[pltpu.VMEM((tm, d), F32), pltpu.VMEM((tm, d), BF16), pltpu.VMEM((tm, d), BF16)],
        compiler_params=_params("parallel"),
        name="xattn",
    )(x2d, y_ret, y_dil, w_out, g1, b1, kv, kv, wq, wo, g, bb, w_r)


def _route_kernel(lg_ref, before_ref, meta_ref, cnt_ref, carry_ref):
    i = pl.program_id(0)

    @pl.when(i == 0)
    def _():
        carry_ref[...] = jnp.zeros_like(carry_ref)

    rows, tt = lg_ref.shape
    sub = before_ref.shape[0]
    r = lax.broadcasted_iota(jnp.int32, (rows, sub), 0)
    r8 = lax.broadcasted_iota(jnp.int32, (SUBLANE, sub), 0)
    is_group = r < N_GROUPS

    def col_max(a):
        return jnp.max(a, axis=0, keepdims=True)

    def first_row_where(mask):
        return jnp.min(jnp.where(mask, r, rows), axis=0, keepdims=True)

    for c in range(tt // sub):
        lg = lg_ref[:, c * sub:(c + 1) * sub]
        mg = col_max(jnp.where(is_group, lg, NEG))
        eg = jnp.where(is_group, jnp.exp(lg - mg), 0.0)
        pg = eg / jnp.sum(eg, axis=0, keepdims=True)
        g1 = col_max(pg)
        gi = first_row_where(is_group & (pg == g1))
        lo = N_GROUPS + gi * EXPERTS_PER_GROUP
        in_grp = (r >= lo) & (r < lo + EXPERTS_PER_GROUP)
        v1 = col_max(jnp.where(in_grp, lg, NEG))
        i1 = first_row_where(in_grp & (lg == v1))
        rest = in_grp & (r != i1)
        v2 = col_max(jnp.where(rest, lg, NEG))
        i2 = first_row_where(rest & (lg == v2))
        e2 = jnp.exp(v2 - v1)
        den = 1.0 + e2
        gate1 = g1 * (1.0 / den)
        gate2 = g1 * (e2 / den)
        sel1 = r == i1
        sel2 = r == i2
        onehot = jnp.where(sel1 | sel2, 1.0, 0.0)
        rank = _dot(onehot, before_ref[...]) + carry_ref[:, 0:1]
        r1 = jnp.sum(jnp.where(sel1, rank, 0.0), axis=0, keepdims=True)
        r2 = jnp.sum(jnp.where(sel2, rank, 0.0), axis=0, keepdims=True)
        carry_ref[...] = carry_ref[...] + jnp.sum(onehot, axis=1, keepdims=True)
        meta = jnp.where(r8 == 0, (i1 - N_GROUPS).astype(F32), 0.0)
        meta = jnp.where(r8 == 1, (i2 - N_GROUPS).astype(F32), meta)
        meta = jnp.where(r8 == 2, r1, meta)
        meta = jnp.where(r8 == 3, r2, meta)
        meta = jnp.where(r8 == 4, gate1, meta)
        meta = jnp.where(r8 == 5, gate2, meta)
        meta_ref[:, c * sub:(c + 1) * sub] = meta
    cnt_ref[...] = carry_ref[...]


def _route(logits_t, tt=1024, sub=256):
    rows, t = logits_t.shape
    before = jnp.asarray(np.arange(sub)[:, None] < np.arange(sub)[None, :], BF16)
    return pl.pallas_call(
        _route_kernel,
        grid=(t // tt,),
        in_specs=[pl.BlockSpec((rows, tt), lambda i: (0, i)),
                  pl.BlockSpec((sub, sub), lambda i: (0, 0))],
        out_specs=[pl.BlockSpec((SUBLANE, tt), lambda i: (0, i)),
                   pl.BlockSpec((rows, LANE), lambda i: (0, 0))],
        out_shape=[jax.ShapeDtypeStruct((SUBLANE, t), F32),
                   jax.ShapeDtypeStruct((rows, LANE), F32)],
        scratch_shapes=[pltpu.VMEM((rows, LANE), F32)],
        compiler_params=_params("arbitrary"),
        name="route",
    )(logits_t, before)


def _plan_kernel(meta_ref, first_row_ref, dest_ref):
    n_tiles = dest_ref.shape[0]
    tt = dest_ref.shape[2] // 2
    rows = first_row_ref.shape[0]
    first_row = first_row_ref[:, 0:1]
    r = lax.broadcasted_iota(jnp.int32, (rows, tt), 0)
    for g in range(n_tiles):
        m = meta_ref[:, g * tt:(g + 1) * tt]

        def dest_of(k):
            e_row = m[k:k + 1, :].astype(jnp.int32) + N_GROUPS
            return jnp.sum(jnp.where(r == e_row, first_row, 0.0), axis=0, keepdims=True) + m[2 + k:3 + k, :]

        dest_ref[g] = jnp.concatenate([dest_of(0), dest_of(1)], axis=1).astype(jnp.int32)


def _plan(meta_t, first_row, tt, tiles_per_step=4):
    t = meta_t.shape[1]
    rows = first_row.shape[0]
    return pl.pallas_call(
        _plan_kernel,
        grid=(t // (tt * tiles_per_step),),
        in_specs=[pl.BlockSpec((SUBLANE, tt * tiles_per_step), lambda i: (0, i)),
                  pl.BlockSpec((rows, LANE), lambda i: (0, 0))],
        out_specs=pl.BlockSpec((tiles_per_step, 1, 2 * tt), lambda i: (i, 0, 0)),
        out_shape=jax.ShapeDtypeStruct((t // tt, 1, 2 * tt), jnp.int32),
        compiler_params=_params("parallel"),
        name="plan",
    )(meta_t, first_row)


def _row_copy(src, dst, s_row, d_row, sem):
    return pltpu.make_async_copy(src.at[pl.ds(pl.multiple_of(s_row * SUBLANE, SUBLANE), SUBLANE), :],
                                 dst.at[pl.ds(pl.multiple_of(d_row * SUBLANE, SUBLANE), SUBLANE), :], sem)


def _dispatch_kernel(dest_ref, xt_ref, xs_ref, sem):
    tq = dest_ref.shape[2] // 2

    def issue(t, c):
        _row_copy(xt_ref, xs_ref, t, dest_ref[0, 0, t], sem).start(priority=0)
        _row_copy(xt_ref, xs_ref, t, dest_ref[0, 0, tq + t], sem).start(priority=1)
        return c

    lax.fori_loop(0, tq, issue, 0, unroll=8)
    for _ in range(2):
        pltpu.make_async_copy(xt_ref, xs_ref.at[pl.ds(0, tq * SUBLANE), :], sem).wait()


def _dispatch(dest3, x_tiled, n_rows, tq=4096):
    n_tiles, _, two_tt = dest3.shape
    tt = two_tt // 2
    tiles_per_step = tq // tt
    n_steps = n_tiles // tiles_per_step
    dest3 = dest3.reshape(n_steps, tiles_per_step, 2, tt).transpose(0, 2, 1, 3).reshape(n_steps, 1, 2 * tq)
    return pl.pallas_call(
        _dispatch_kernel,
        grid=(n_steps,),
        in_specs=[pl.BlockSpec((1, 1, 2 * tq), lambda i: (i, 0, 0), memory_space=pltpu.SMEM),
                  pl.BlockSpec((tq * SUBLANE, LANE), lambda i: (i, 0))],
        out_specs=pl.BlockSpec(memory_space=pl.ANY),
        out_shape=jax.ShapeDtypeStruct((n_rows * SUBLANE, LANE), F32),
        scratch_shapes=[pltpu.SemaphoreType.DMA(())],
        compiler_params=_params("arbitrary"),
        name="dispatch",
    )(dest3, x_tiled)


def _experts_kernel(blk_exp_ref, blk_rows_ref, n_used_ref, blk_first_ref, blk_slot_ref, blk_next_ref,
                    xs_ref, wg_hbm, wu_hbm, wd_hbm, ys_ref, x_scr, y_scr, wg_buf, wu_buf, wd_buf, sems):
    b = pl.program_id(0)

    def weight_copies(e, slot):
        return [pltpu.make_async_copy(hbm.at[e], buf.at[slot], sems.at[k, slot])
                for k, (hbm, buf) in enumerate(((wg_hbm, wg_buf), (wu_hbm, wu_buf), (wd_hbm, wd_buf)))]

    @pl.when(b < n_used_ref[0])
    def _():
        slot = blk_slot_ref[b]

        @pl.when(b == 0)
        def _():
            for cp in weight_copies(blk_exp_ref[0], slot):
                cp.start()

        @pl.when(blk_first_ref[b] == 1)
        def _():
            @pl.when(blk_next_ref[b] >= 0)
            def _():
                for cp in weight_copies(blk_next_ref[b], 1 - slot):
                    cp.start()

            for cp in weight_copies(blk_exp_ref[b], slot):
                cp.wait()

        rows, d = x_scr.shape
        n_valid = blk_rows_ref[b]

        def run(m):
            ri = lax.broadcasted_iota(jnp.int32, (m, 1), 0)
            for c in range(d // LANE):
                x = xs_ref[pl.ds(c, m, stride=SUBLANE), :]
                x_scr[0:m, c * LANE:(c + 1) * LANE] = jnp.where(ri < n_valid, x, 0.0).astype(BF16)
            x = x_scr[0:m, :]
            hg = _dot(x, wg_buf[slot])
            hu = _dot(x, wu_buf[slot])
            y_scr[0:m, :] = _dot(hg * jax.nn.sigmoid(hg) * hu, wd_buf[slot])
            for c in range(d // LANE):
                ys_ref[pl.ds(c, m, stride=SUBLANE), :] = y_scr[0:m, c * LANE:(c + 1) * LANE]

        n_quanta = rows // EXPERT_QUANTUM
        for k in range(1, n_quanta + 1):
            lo = (k - 1) * EXPERT_QUANTUM
            pl.when((n_valid > lo) & (n_valid <= k * EXPERT_QUANTUM))(functools.partial(run, k * EXPERT_QUANTUM))


def _experts(blk_exp, blk_rows, n_used, blk_first, blk_slot, blk_next, xs_tiled, w_g, w_u, w_d):
    n_rows = xs_tiled.shape[0] // SUBLANE
    rows = EXPERT_ROWS
    nblk = n_rows // rows
    _, d, ff = w_g.shape
    used = lambda b, be, br, nu, *_: (jnp.minimum(b, nu[0] - 1), 0)
    grid_spec = pltpu.PrefetchScalarGridSpec(
        num_scalar_prefetch=6,
        grid=(nblk,),
        in_specs=[pl.BlockSpec((rows * SUBLANE, LANE), used),
                  pl.BlockSpec(memory_space=pl.ANY), pl.BlockSpec(memory_space=pl.ANY),
                  pl.BlockSpec(memory_space=pl.ANY)],
        out_specs=pl.BlockSpec((rows * SUBLANE, LANE), used),
        scratch_shapes=[pltpu.VMEM((rows, d), BF16), pltpu.VMEM((rows, d), F32),
                        pltpu.VMEM((2, d, ff), F32), pltpu.VMEM((2, d, ff), F32), pltpu.VMEM((2, ff, d), F32),
                        pltpu.SemaphoreType.DMA((3, 2))],
    )
    return pl.pallas_call(
        _experts_kernel,
        grid_spec=grid_spec,
        out_shape=jax.ShapeDtypeStruct((n_rows * SUBLANE, LANE), F32),
        compiler_params=_params("arbitrary"),
        name="experts",
    )(blk_exp, blk_rows, n_used, blk_first, blk_slot, blk_next, xs_tiled, w_g, w_u, w_d)


def _combine_kernel(alpha, dcur_ref, dnext_ref, meta_ref, x_ref, g_ref, b_ref, ys_ref, o_ref, buf_ref, sems):
    i = pl.program_id(0)
    tq, d = x_ref.shape
    slot_rows = 2 * tq * SUBLANE

    def issue(d_ref, slot, t):
        _row_copy(ys_ref, buf_ref, d_ref[0, 0, t], slot * (2 * tq) + t, sems.at[slot]).start(priority=0)
        _row_copy(ys_ref, buf_ref, d_ref[0, 0, tq + t], slot * (2 * tq) + tq + t, sems.at[slot]).start(priority=1)

    def wait_slot(slot):
        off = pl.multiple_of(slot * slot_rows, slot_rows)
        pltpu.make_async_copy(ys_ref.at[pl.ds(0, slot_rows), :], buf_ref.at[pl.ds(off, slot_rows), :],
                              sems.at[slot]).wait()
        return off

    def issue_all(d_ref, slot):
        lax.fori_loop(0, tq, lambda t, c: (issue(d_ref, slot, t), c)[1], 0, unroll=4)

    slot = i % 2

    @pl.when(i == 0)
    def _():
        issue_all(dcur_ref, 0)

    @pl.when(i + 1 < pl.num_programs(0))
    def _():
        issue_all(dnext_ref, 1 - slot)

    off = wait_slot(slot)
    meta_rows = jnp.transpose(jnp.concatenate([meta_ref[...], jnp.zeros((LANE - SUBLANE, tq), F32)], axis=0))
    gate1 = meta_rows[:, 4:5]
    gate2 = meta_rows[:, 5:6]
    for c in range(d // LANE):
        sl = slice(c * LANE, (c + 1) * LANE)
        y1 = buf_ref[pl.ds(off + c, tq, stride=SUBLANE), :]
        y2 = buf_ref[pl.ds(off + tq * SUBLANE + c, tq, stride=SUBLANE), :]
        o_ref[:, sl] = alpha * x_ref[:, sl] + (y1 * gate1 + y2 * gate2)
    o_ref[...] = _layer_norm(o_ref[...], g_ref[...], b_ref[...])


def _combine(dest3, meta, x2, g, bb, ys_tiled, alpha):
    t, d = x2.shape
    n = dest3.shape[0]
    tq = t // n
    row = lambda i: (i, 0)
    const = lambda i: (0, 0)
    return pl.pallas_call(
        functools.partial(_combine_kernel, alpha),
        grid=(n,),
        in_specs=[pl.BlockSpec((1, 1, 2 * tq), lambda i: (i, 0, 0), memory_space=pltpu.SMEM),
                  pl.BlockSpec((1, 1, 2 * tq), lambda i: (jnp.minimum(i + 1, n - 1), 0, 0), memory_space=pltpu.SMEM),
                  pl.BlockSpec((SUBLANE, tq), lambda i: (0, i)),
                  pl.BlockSpec((tq, d), row),
                  pl.BlockSpec((1, d), const), pl.BlockSpec((1, d), const),
                  pl.BlockSpec(memory_space=pl.ANY)],
        out_specs=pl.BlockSpec((tq, d), row),
        out_shape=jax.ShapeDtypeStruct((t, d), F32),
        scratch_shapes=[pltpu.VMEM((2 * 2 * tq * SUBLANE, LANE), F32), pltpu.SemaphoreType.DMA((2,))],
        compiler_params=_params("arbitrary"),
        name="combine",
    )(dest3, dest3, meta, x2, g, bb, ys_tiled)


def _moe(x2, x2_tiled, logits, w_g, w_u, w_d, g, bb, alpha):
    t, d = x2.shape
    meta, cnt = _route(logits)
    counts = cnt[N_GROUPS:N_GROUPS + N_EXPERTS, 0].astype(jnp.int32)
    padded = ((counts + EXPERT_ROWS - 1) // EXPERT_ROWS) * EXPERT_ROWS
    pend = jnp.cumsum(padded)
    poff = pend - padded
    first_row = jnp.pad(poff.astype(F32), (N_GROUPS, ROUTE_ROWS - N_GROUPS - N_EXPERTS))
    dest3 = _plan(meta, jnp.broadcast_to(first_row[:, None], (ROUTE_ROWS, LANE)), ROUTE_TILE)
    n_rows = t * 2 + N_EXPERTS * EXPERT_ROWS
    nblk = n_rows // EXPERT_ROWS
    blk_start = jnp.arange(nblk, dtype=jnp.int32) * EXPERT_ROWS
    blk_exp = jnp.sum((pend[None, :] <= blk_start[:, None]).astype(jnp.int32), axis=1)
    blk_exp = jnp.minimum(blk_exp, N_EXPERTS - 1)
    e_idx = jnp.arange(N_EXPERTS, dtype=jnp.int32)
    owner = blk_exp[:, None] == e_idx[None, :]

    def of_block(per_expert):
        return jnp.sum(jnp.where(owner, per_expert[None, :], 0), axis=1).astype(jnp.int32)

    blk_rows = jnp.clip(of_block(poff + counts) - blk_start, 0, EXPERT_ROWS).astype(jnp.int32)
    n_used = (pend[-1:] // EXPERT_ROWS).astype(jnp.int32)
    has_rows = counts > 0
    ordinal = jnp.cumsum(has_rows.astype(jnp.int32)) - 1
    later = has_rows[None, :] & (e_idx[None, :] > e_idx[:, None])
    next_used = jnp.min(jnp.where(later, e_idx[None, :], N_EXPERTS), axis=1)
    next_used = jnp.where(next_used == N_EXPERTS, -1, next_used).astype(jnp.int32)
    blk_first = (blk_start == of_block(poff)).astype(jnp.int32)
    blk_slot = of_block(ordinal % 2)
    blk_next = of_block(next_used)
    xs_tiled = _dispatch(dest3, x2_tiled, n_rows)
    ys_tiled = _experts(blk_exp, blk_rows, n_used, blk_first, blk_slot, blk_next, xs_tiled, w_g, w_u, w_d)
    return _combine(dest3, meta, x2, g, bb, ys_tiled, alpha)


def kernel(x, mem, w_in, w_out, ln_mix_g, ln_mix_b, w_xq, w_xkv, w_xo, ln_x_g, ln_x_b, w_route_group,
           w_route_expert, w_exp_gate, w_exp_up, w_exp_down, ln_moe_g, ln_moe_b):
    b, s, d = x.shape
    depth = w_in.shape[0]
    alpha = (2.0 * depth) ** 0.25
    t = b * s
    ret_cols = (2 * RET_HEADS * RET_QK_DIM + 2 * RET_HEADS * RET_V_DIM)
    assert s % DIL_SUPER == 0 and s % (RET_STEP_CHUNKS * RET_CHUNK) == 0, "sequence length not tileable"
    assert w_in.shape[2] == ret_cols + 3 * DIL_HEADS * DIL_HEAD_DIM and d % (XATTN_HEADS * LANE) == 0
    assert w_route_group.shape[2] == N_GROUPS and w_route_expert.shape[2] == N_EXPERTS
    xc = x.reshape(t, d)
    for l in range(depth):
        n_qk = 2 * RET_HEADS * RET_QK_DIM
        h_ret, h_dil = _proj_in(xc, w_in[l], _retention_weight_layout(w_in[l][:, :n_qk]).astype(BF16), ret_cols)
        y_ret = _retention(h_ret.reshape(b, s, ret_cols)).reshape(t, -1)
        y_dil = _dilated(h_dil, b).reshape(t, -1)
        kv = _mem_kv(mem.reshape(b * mem.shape[1], d), w_xkv[l]).reshape(b, mem.shape[1], 2 * d)
        w_r = jnp.concatenate([w_route_group[l], w_route_expert[l]], axis=-1)
        w_r = jnp.pad(w_r.T, ((0, ROUTE_ROWS - w_r.shape[1]), (0, 0))).astype(BF16)
        x2, x2_tiled, logits = _xattn(xc, y_ret, y_dil, w_out[l].astype(BF16), ln_mix_g[l][None], ln_mix_b[l][None],
                                      kv, w_xq[l].astype(BF16), w_xo[l].astype(BF16),
                                      ln_x_g[l][None], ln_x_b[l][None], w_r, alpha, s)
        xc = _moe(x2, x2_tiled, logits, w_exp_gate[l], w_exp_up[l], w_exp_down[l],
                  ln_moe_g[l][None], ln_moe_b[l][None], alpha)
    return xc.reshape(b, s, d)
```

```python
import functools
import math

import jax
import jax.numpy as jnp
import numpy as np
from jax import lax
from jax.experimental import pallas as pl
from jax.experimental.pallas import tpu as pltpu

BF16 = jnp.bfloat16
F32 = jnp.float32

LANE = 128
SUBLANE = 8
VMEM_LIMIT = 56 * 1024 * 1024

RET_HEADS = 4
RET_QK_DIM = 64
RET_V_DIM = 128
RET_CHUNK = 128
RET_STEP_CHUNKS = 4
ROPE_BASE = 10000.0
DIL_HEADS = 8
DIL_HEAD_DIM = 64
DIL_DILATIONS = (1, 4, 16)
DIL_BLOCK = 128
DIL_SUPER = DIL_BLOCK * max(DIL_DILATIONS)
DIL_GROUP = 1
XATTN_HEADS = 4
XATTN_PARTS = 2
N_GROUPS = 4
EXPERTS_PER_GROUP = 8
N_EXPERTS = N_GROUPS * EXPERTS_PER_GROUP
ROUTE_ROWS = -(-(N_GROUPS + N_EXPERTS) // SUBLANE) * SUBLANE
EXPERT_ROWS = 512
EXPERT_QUANTUM = 256
ROUTE_TILE = 512
LN_EPS = 1e-5
GN_EPS = 1e-6
NEG = -1e30


def _params(*sem):
    return pltpu.CompilerParams(dimension_semantics=sem, vmem_limit_bytes=VMEM_LIMIT)


def _layer_norm(z, g, b):
    mu = jnp.mean(z, axis=-1, keepdims=True)
    zc = z - mu
    var = jnp.mean(zc * zc, axis=-1, keepdims=True)
    return zc * lax.rsqrt(var + LN_EPS) * g + b


def _dot(a, b):
    return jnp.dot(a.astype(BF16), b.astype(BF16), preferred_element_type=F32)


def _dot_nt(a, b):
    return lax.dot_general(a.astype(BF16), b.astype(BF16), (((1,), (1,)), ((), ())),
                           preferred_element_type=F32)


def _dot_tn(a, b):
    return lax.dot_general(a.astype(BF16), b.astype(BF16), (((0,), (0,)), ((), ())),
                           preferred_element_type=F32)


def _proj_in_kernel(x_ref, w_ref, o_ref):
    o_ref[...] = _dot(x_ref[...], w_ref[...]).astype(o_ref.dtype)


def _proj_in2_kernel(x_ref, w_ref, w_head_ref, o_ref, slab_ref, w_bf16):
    @pl.when(pl.program_id(0) == 0)
    def _():
        w_bf16[...] = w_ref[...].astype(BF16)
        w_bf16[:, 0:w_head_ref.shape[1]] = w_head_ref[...]

    n_row = o_ref.shape[1]
    x = x_ref[...].astype(BF16)
    o_ref[...] = jnp.dot(x, w_bf16[:, 0:n_row], preferred_element_type=F32)
    rest = jnp.dot(x, w_bf16[:, n_row:], preferred_element_type=F32)
    for c in range(slab_ref.shape[0]):
        slab_ref[c] = rest[:, c * LANE:(c + 1) * LANE]


def _proj_in(x2d, w_f32, w_head_bf16, n_row, tm=1024):
    t, d = x2d.shape
    n = w_f32.shape[1]
    n_slabs = (n - n_row) // LANE
    whole = lambda shape: pl.BlockSpec(shape, lambda i: (0, 0), pipeline_mode=pl.Buffered(1))
    return pl.pallas_call(
        _proj_in2_kernel,
        grid=(t // tm,),
        in_specs=[pl.BlockSpec((tm, d), lambda i: (i, 0)), whole((d, n)), whole(w_head_bf16.shape)],
        out_specs=[pl.BlockSpec((tm, n_row), lambda i: (i, 0)),
                   pl.BlockSpec((n_slabs, tm, LANE), lambda i: (0, i, 0))],
        out_shape=[jax.ShapeDtypeStruct((t, n_row), F32),
                   jax.ShapeDtypeStruct((n_slabs, t, LANE), F32)],
        scratch_shapes=[pltpu.VMEM((d, n), BF16)],
        compiler_params=_params("arbitrary"),
        name="proj_in",
    )(x2d, w_f32, w_head_bf16)


def _retention_kernel(qk_ref, v_ref, g_ref, cos_ref, sin_ref, decay_ref, zeta_ref, xi_ref, gam_ref,
                      o_ref, state_ref, y_ref):
    n = pl.program_id(1)

    @pl.when(n == 0)
    def _():
        state_ref[...] = jnp.zeros_like(state_ref)

    c = RET_CHUNK
    n_sub = qk_ref.shape[1] // c
    half = RET_QK_DIM // 2
    lane = lax.broadcasted_iota(jnp.int32, (c, LANE), 1)

    rotated = []
    for j in range(n_sub):
        rows = slice(j * c, (j + 1) * c)
        cos = cos_ref[rows, :]
        sin = sin_ref[rows, :]

        def rot(col, rows=rows, cos=cos, sin=sin):
            t1 = qk_ref[0, rows, col * LANE:(col + 1) * LANE]
            t2 = qk_ref[0, rows, (col + 1) * LANE:(col + 2) * LANE]
            return t1 * cos - t2 * sin, t1 * sin + t2 * cos

        q1, q2 = rot(0)
        k1, k2 = (t * (RET_QK_DIM ** -0.5) for t in rot(2))
        rotated.append((q1, q2, k1, k2, jnp.concatenate([k1, k2], axis=1)))

    for h in range(RET_HEADS):
        cols = slice(h * RET_V_DIM, (h + 1) * RET_V_DIM)
        mine = (lane >= h * half) & (lane < (h + 1) * half)
        zeta = zeta_ref[h]
        st = state_ref[h]
        pad_lo = [jnp.zeros((h * half, RET_V_DIM), BF16)] if h > 0 else []
        pad_hi = [jnp.zeros(((RET_HEADS - 1 - h) * half, RET_V_DIM), BF16)] if h < RET_HEADS - 1 else []
        for j in range(n_sub):
            rows = slice(j * c, (j + 1) * c)
            q1, q2, k1, k2, k_all = rotated[j]
            qm = jnp.concatenate([jnp.where(mine, q1, 0.0), jnp.where(mine, q2, 0.0)], axis=1)
            v = v_ref[0, rows, cols]
            s = _dot_nt(qm, k_all) * decay_ref[h]
            sb = st.astype(BF16)
            st_rows = jnp.concatenate(pad_lo + [sb[:half]] + pad_hi + pad_lo + [sb[half:]] + pad_hi, axis=0)
            y_ref[rows, cols] = _dot(s, v) + _dot(qm, st_rows) * xi_ref[h]
            kv = _dot_tn(jnp.concatenate([k1 * zeta, k2 * zeta], axis=1), v)
            lo = h * half
            hi = RET_HEADS * half + h * half
            st = gam_ref[h, :, :] * st + jnp.concatenate([kv[lo:lo + half], kv[hi:hi + half]], axis=0)
        state_ref[h] = st

    for h in range(RET_HEADS):
        cols = slice(h * RET_V_DIM, (h + 1) * RET_V_DIM)
        y = y_ref[:, cols]
        mu = jnp.mean(y, axis=-1, keepdims=True)
        yc = y - mu
        var = jnp.mean(yc * yc, axis=-1, keepdims=True)
        yn = yc * lax.rsqrt(var + GN_EPS)
        gate = g_ref[0, :, cols]
        o_ref[0, :, cols] = (gate * jax.nn.sigmoid(gate) * yn).astype(o_ref.dtype)


def _retention_tables(s):
    half = RET_QK_DIM // 2
    inv = 1.0 / (ROPE_BASE ** (np.arange(half, dtype=np.float64) / half))
    ang = np.arange(s, dtype=np.float64)[:, None] * inv[None, :]
    cos_t = np.tile(np.cos(ang), (1, RET_HEADS))
    sin_t = np.tile(np.sin(ang), (1, RET_HEADS))
    c = RET_CHUNK
    lg = np.log(1.0 - np.exp2(-5.0 - np.arange(RET_HEADS, dtype=np.float64)))
    idx = np.arange(c, dtype=np.float64)
    diff = idx[:, None] - idx[None, :]
    decay = np.where(diff >= 0, np.exp(lg[:, None, None] * np.maximum(diff, 0.0)), 0.0)
    lanes = (RET_HEADS, c, LANE)
    zeta = np.broadcast_to(np.exp(lg[:, None] * (c - 1.0 - idx))[:, :, None], lanes)
    xi = np.broadcast_to(np.exp(lg[:, None] * (idx + 1.0))[:, :, None], lanes)
    gam = np.broadcast_to(np.exp(lg * c)[:, None, None], (RET_HEADS, 1, LANE))
    return tuple(jnp.asarray(np.ascontiguousarray(a), F32) for a in (cos_t, sin_t, decay, zeta, xi, gam))


def _retention_weight_layout(w_qk):
    d = w_qk.shape[0]
    half = RET_QK_DIM // 2
    return w_qk.reshape(d, 2, RET_HEADS, 2, half).transpose(0, 1, 3, 2, 4).reshape(d, -1)


def _retention(h3):
    b, s, _ = h3.shape
    c = RET_CHUNK
    qk_w = 2 * RET_HEADS * RET_QK_DIM
    v_w = RET_HEADS * RET_V_DIM
    assert qk_w == v_w
    cos_t, sin_t, decay, zeta, xi, gam = _retention_tables(s)
    const3 = lambda bi, n: (0, 0, 0)
    rows = RET_STEP_CHUNKS * c
    return pl.pallas_call(
        _retention_kernel,
        grid=(b, s // rows),
        in_specs=[pl.BlockSpec((1, rows, qk_w), lambda bi, n: (bi, n, 0)),
                  pl.BlockSpec((1, rows, v_w), lambda bi, n: (bi, n, 1)),
                  pl.BlockSpec((1, rows, v_w), lambda bi, n: (bi, n, 2)),
                  pl.BlockSpec((rows, LANE), lambda bi, n: (n, 0)),
                  pl.BlockSpec((rows, LANE), lambda bi, n: (n, 0)),
                  pl.BlockSpec((RET_HEADS, c, c), const3),
                  pl.BlockSpec((RET_HEADS, c, LANE), const3),
                  pl.BlockSpec((RET_HEADS, c, LANE), const3),
                  pl.BlockSpec((RET_HEADS, 1, LANE), const3)],
        out_specs=pl.BlockSpec((1, rows, v_w), lambda bi, n: (bi, n, 0)),
        out_shape=jax.ShapeDtypeStruct((b, s, v_w), BF16),
        scratch_shapes=[pltpu.VMEM((RET_HEADS, RET_QK_DIM, RET_V_DIM), F32),
                        pltpu.VMEM((rows, v_w), F32)],
        compiler_params=_params("parallel", "arbitrary"),
        name="retention",
    )(h3, h3, h3, cos_t, sin_t, decay, zeta, xi, gam)


def _dilated_kernel(q_ref, kp_ref, kc_ref, vp_ref, vc_ref, bias_ref,
                    o_ref, acc_ref, m_ref, l_ref, s0_ref, s1_ref, p0_ref, p1_ref, qw_ref, kw_ref, vw_ref):
    j = pl.program_id(2)
    sup = DIL_SUPER
    q_blk = DIL_BLOCK
    scale = DIL_HEAD_DIM ** -0.5 * math.log2(math.e)
    n_blocks = sup // q_blk
    lane = lax.broadcasted_iota(jnp.int32, (q_blk, LANE), 1)
    head0 = lane < DIL_HEAD_DIM
    first_bias = jnp.where(j == 0, 1, 0)

    slot = j % 2

    def relay(r):
        cols = slice(r * LANE, (r + 1) * LANE)
        qw_ref[:, cols] = q_ref[0, pl.ds(r, q_blk, stride=n_blocks), :]
        kw_ref[slot, :, cols] = kc_ref[0, pl.ds(r, q_blk, stride=n_blocks), :]
        vw_ref[slot, :, cols] = vc_ref[0, pl.ds(r, q_blk, stride=n_blocks), :]

    @pl.when(j == 0)
    def _():
        kw_ref[1] = jnp.zeros(kw_ref.shape[1:], F32)
        vw_ref[1] = jnp.zeros(vw_ref.shape[1:], F32)

    groups = []
    for bi, d in enumerate(DIL_DILATIONS):
        n_per_r = sup // (q_blk * d)
        for t0 in range(0, n_blocks, DIL_GROUP):
            blocks = []
            for t in range(t0, t0 + DIL_GROUP):
                r, n = divmod(t, n_per_r)
                blocks.append((n * (q_blk * d) + r, (n - 1) * (q_blk * d) + r, n))
            groups.append((bi, d, blocks))
    s_bufs = (s0_ref, s1_ref)
    p_bufs = (p0_ref, p1_ref)

    def window(prev_ref, cur_ref, k_start, d):
        if k_start >= 0:
            return cur_ref[0, pl.ds(k_start, 2 * q_blk, stride=d), :]
        return jnp.concatenate([prev_ref[0, pl.ds(sup + k_start, q_blk, stride=d), :],
                                cur_ref[0, pl.ds(k_start + q_blk * d, q_blk, stride=d), :]], axis=0)

    def residue_window(ref, q_start):
        cols = slice(q_start * LANE, (q_start + 1) * LANE)
        return jnp.concatenate([ref[1 - slot, :, cols], ref[slot, :, cols]], axis=0)

    def scores(gi):
        bi, d, blocks = groups[gi]
        for g, (q_start, k_start, n) in enumerate(blocks):
            if d == n_blocks:
                q = qw_ref[:, q_start * LANE:(q_start + 1) * LANE] * scale
                kb = residue_window(kw_ref, q_start)
            else:
                q = q_ref[0, pl.ds(q_start, q_blk, stride=d), :] * scale
                kb = window(kp_ref, kc_ref, k_start, d)
            q2 = jnp.concatenate([jnp.where(head0, q, 0.0), jnp.where(head0, 0.0, q)], axis=0)
            bias = bias_ref[first_bias] if n == 0 else bias_ref[0]
            s_bufs[gi % 2][g] = _dot_nt(q2, kb) + bias

    def softmax(gi):
        bi, d, blocks = groups[gi]
        for g, (q_start, k_start, n) in enumerate(blocks):
            s = s_bufs[gi % 2][g]
            m2 = jnp.max(s, axis=-1, keepdims=True)
            p_bufs[gi % 2][g] = jnp.exp2(s - m2).astype(BF16)
            m_ref[bi, pl.ds(q_start, q_blk, stride=d), :] = jnp.where(head0, m2[:q_blk], m2[q_blk:])

    def values(gi):
        bi, d, blocks = groups[gi]
        for g, (q_start, k_start, n) in enumerate(blocks):
            if d == n_blocks:
                vb = residue_window(vw_ref, q_start).astype(BF16)
            else:
                vb = window(vp_ref, vc_ref, k_start, d).astype(BF16)
            o2 = jnp.dot(p_bufs[gi % 2][g], jnp.concatenate([vb, jnp.ones_like(vb)], axis=1),
                         preferred_element_type=F32)
            rows = pl.ds(q_start, q_blk, stride=d)
            acc_ref[bi, rows, :] = jnp.where(head0, o2[:q_blk, :LANE], o2[q_blk:, :LANE])
            l_ref[bi, rows, :] = jnp.where(head0, o2[:q_blk, LANE:], o2[q_blk:, LANE:])

    n_narrow = sum(1 for _, d, _ in groups if d != n_blocks)
    for step in range(len(groups) + 2):
        if step < n_narrow and step % (n_narrow // n_blocks) == 0:
            relay(step // (n_narrow // n_blocks))
        if step < len(groups):
            scores(step)
        if 0 <= step - 1 < len(groups):
            softmax(step - 1)
        if step - 2 >= 0:
            values(step - 2)

    def merge(c, carry):
        rows = pl.ds(pl.multiple_of(c * q_blk, q_blk), q_blk)
        ms = [m_ref[bi, rows, :] for bi in range(len(DIL_DILATIONS))]
        m_all = functools.reduce(jnp.maximum, ms)
        ws = [jnp.exp2(m - m_all) for m in ms]
        num = functools.reduce(lambda a, b: a + b, [w * acc_ref[bi, rows, :] for bi, w in enumerate(ws)])
        den = functools.reduce(lambda a, b: a + b, [w * l_ref[bi, rows, :] for bi, w in enumerate(ws)])
        o_ref[0, rows, :] = (num / den).astype(o_ref.dtype)
        return carry

    lax.fori_loop(0, n_blocks, merge, 0)


def _dilated_bias():
    q_blk = DIL_BLOCK
    qi = np.arange(2 * q_blk)[:, None] % q_blk
    kj = np.arange(2 * q_blk)[None, :]
    band = (kj >= qi) & (kj <= qi + q_blk)
    return jnp.asarray(np.stack([np.where(band, 0.0, NEG), np.where(band & (kj >= q_blk), 0.0, NEG)]), F32)


def _dilated(slabs, b):
    n_slabs, t, _ = slabs.shape
    s = t // b
    h3 = slabs.reshape(n_slabs * b, s, LANE)
    sup = DIL_SUPER
    n_pairs = DIL_HEADS * DIL_HEAD_DIM // LANE
    n_br = len(DIL_DILATIONS)
    cq, ck, cv = 0, n_pairs, 2 * n_pairs
    cur = lambda c: (lambda bi, p, j: ((c + p) * b + bi, j, 0))
    prev = lambda c: (lambda bi, p, j: ((c + p) * b + bi, jnp.maximum(j - 1, 0), 0))
    blk = (1, sup, LANE)
    d_wide = sup // DIL_BLOCK
    assert d_wide == max(DIL_DILATIONS)
    wide = (DIL_BLOCK, d_wide * LANE)
    return pl.pallas_call(
        _dilated_kernel,
        grid=(b, n_pairs, s // sup),
        in_specs=[pl.BlockSpec(blk, cur(cq)),
                  pl.BlockSpec(blk, prev(ck)), pl.BlockSpec(blk, cur(ck)),
                  pl.BlockSpec(blk, prev(cv)), pl.BlockSpec(blk, cur(cv)),
                  pl.BlockSpec((2, 2 * DIL_BLOCK, 2 * DIL_BLOCK), lambda bi, p, j: (0, 0, 0))],
        out_specs=pl.BlockSpec(blk, lambda bi, p, j: (bi, j, p)),
        out_shape=jax.ShapeDtypeStruct((b, s, n_pairs * LANE), BF16),
        scratch_shapes=[pltpu.VMEM((n_br, sup, LANE), F32), pltpu.VMEM((n_br, sup, LANE), F32),
                        pltpu.VMEM((n_br, sup, LANE), F32),
                        pltpu.VMEM((DIL_GROUP, 2 * DIL_BLOCK, 2 * DIL_BLOCK), F32),
                        pltpu.VMEM((DIL_GROUP, 2 * DIL_BLOCK, 2 * DIL_BLOCK), F32),
                        pltpu.VMEM((DIL_GROUP, 2 * DIL_BLOCK, 2 * DIL_BLOCK), BF16),
                        pltpu.VMEM((DIL_GROUP, 2 * DIL_BLOCK, 2 * DIL_BLOCK), BF16),
                        pltpu.VMEM(wide, F32), pltpu.VMEM((2,) + wide, F32), pltpu.VMEM((2,) + wide, F32)],
        compiler_params=_params("parallel", "parallel", "arbitrary"),
        name="dilated",
    )(h3, h3, h3, h3, h3, _dilated_bias())


def _mem_kv(mem2d, w_bf16, tn=512):
    m, d = mem2d.shape
    n = w_bf16.shape[1]
    return pl.pallas_call(
        _proj_in_kernel,
        grid=(n // tn,),
        in_specs=[pl.BlockSpec((m, d), lambda i: (0, 0)),
                  pl.BlockSpec((d, tn), lambda i: (0, i))],
        out_specs=pl.BlockSpec((m, tn), lambda i: (0, i)),
        out_shape=jax.ShapeDtypeStruct((m, n), BF16),
        compiler_params=_params("parallel"),
        name="mem_kv",
    )(mem2d, w_bf16)


def _xattn_kernel(alpha, x_ref, yr_ref, yd_ref, wout_ref, g1_ref, b1_ref, k_ref, v_ref, wq_ref, wo_ref,
                  g_ref, b_ref, wr_ref, o_ref, ot_ref, lg_ref, x1_ref, q_ref, att_ref):
    tm, d = x_ref.shape
    dh = d // XATTN_HEADS
    wr = yr_ref.shape[1]
    scale = dh ** -0.5 * math.log2(math.e)
    part = tm // XATTN_PARTS

    def mix(rows):
        y = _dot(yr_ref[rows, :], wout_ref[0:wr, :]) + _dot(yd_ref[rows, :], wout_ref[wr:, :])
        x1_ref[rows, :] = _layer_norm(alpha * x_ref[rows, :] + y, g1_ref[...], b1_ref[...])

    def query(rows):
        q_ref[rows, :] = _dot(x1_ref[rows, :], wq_ref[...]).astype(BF16)

    def attend(rows):
        for h in range(XATTN_HEADS):
            sl = slice(h * dh, (h + 1) * dh)
            s = _dot_nt(q_ref[rows, sl], k_ref[0, :, sl]) * scale
            m = jnp.max(s, axis=-1, keepdims=True)
            e = jnp.exp2(s - m)
            p = e / jnp.sum(e, axis=-1, keepdims=True)
            att_ref[rows, sl] = _dot(p, v_ref[0, :, sl]).astype(BF16)

    def finish(rows, r0):
        y = _dot(att_ref[rows, :], wo_ref[...])
        x2 = _layer_norm(alpha * x1_ref[rows, :] + y, g_ref[...], b_ref[...])
        o_ref[rows, :] = x2
        for c in range(d // LANE):
            ot_ref[pl.ds(r0 * SUBLANE + c, part, stride=SUBLANE), :] = x2[:, c * LANE:(c + 1) * LANE]
        lg_ref[:, rows] = _dot_nt(wr_ref[...], x2)

    stages = (mix, query, attend, finish)
    for step in range(XATTN_PARTS + len(stages) - 1):
        for si, stage in enumerate(stages):
            pi = step - si
            if 0 <= pi < XATTN_PARTS:
                rows = slice(pi * part, (pi + 1) * part)
                if stage is finish:
                    stage(rows, pi * part)
                else:
                    stage(rows)


def _xattn(x2d, y_ret, y_dil, w_out, g1, b1, kv, wq, wo, g, bb, w_r, alpha, seq, tm=1024):
    t, d = x2d.shape
    mlen = kv.shape[1]
    tiles_per_seq = seq // tm
    row = lambda i: (i, 0)

    def whole(shape):
        return pl.BlockSpec(shape, lambda i: (0,) * len(shape), pipeline_mode=pl.Buffered(1))

    vec = whole((1, d))
    return pl.pallas_call(
        functools.partial(_xattn_kernel, alpha),
        grid=(t // tm,),
        in_specs=[pl.BlockSpec((tm, d), row),
                  pl.BlockSpec((tm, y_ret.shape[1]), row),
                  pl.BlockSpec((tm, y_dil.shape[1]), row),
                  whole(w_out.shape), vec, vec,
                  pl.BlockSpec((1, mlen, d), lambda i: (i // tiles_per_seq, 0, 0)),
                  pl.BlockSpec((1, mlen, d), lambda i: (i // tiles_per_seq, 0, 1)),
                  whole((d, d)), whole((d, d)), vec, vec,
                  whole((ROUTE_ROWS, d))],
        out_specs=[pl.BlockSpec((tm, d), row),
                   pl.BlockSpec((tm * SUBLANE, LANE), row),
                   pl.BlockSpec((ROUTE_ROWS, tm), lambda i: (0, i))],
        out_shape=[jax.ShapeDtypeStruct((t, d), F32),
                   jax.ShapeDtypeStruct((t * SUBLANE, LANE), F32),
                   jax.ShapeDtypeStruct((ROUTE_ROWS, t), F32)],
        scratch_shapes=[pltpu.VMEM((tm, d), F32), pltpu.VMEM((tm, d), BF16), pltpu.VMEM((tm, d), BF16)],
        compiler_params=_params("parallel"),
        name="xattn",
    )(x2d, y_ret, y_dil, w_out, g1, b1, kv, kv, wq, wo, g, bb, w_r)


def _route_kernel(lg_ref, before_ref, meta_ref, cnt_ref, carry_ref):
    i = pl.program_id(0)

    @pl.when(i == 0)
    def _():
        carry_ref[...] = jnp.zeros_like(carry_ref)

    rows, tt = lg_ref.shape
    sub = before_ref.shape[0]
    r = lax.broadcasted_iota(jnp.int32, (rows, sub), 0)
    r8 = lax.broadcasted_iota(jnp.int32, (SUBLANE, sub), 0)
    is_group = r < N_GROUPS

    def col_max(a):
        return jnp.max(a, axis=0, keepdims=True)

    def first_row_where(mask):
        return jnp.min(jnp.where(mask, r, rows), axis=0, keepdims=True)

    for c in range(tt // sub):
        lg = lg_ref[:, c * sub:(c + 1) * sub]
        mg = col_max(jnp.where(is_group, lg, NEG))
        eg = jnp.where(is_group, jnp.exp(lg - mg), 0.0)
        pg = eg / jnp.sum(eg, axis=0, keepdims=True)
        g1 = col_max(pg)
        gi = first_row_where(is_group & (pg == g1))
        lo = N_GROUPS + gi * EXPERTS_PER_GROUP
        in_grp = (r >= lo) & (r < lo + EXPERTS_PER_GROUP)
        v1 = col_max(jnp.where(in_grp, lg, NEG))
        i1 = first_row_where(in_grp & (lg == v1))
        rest = in_grp & (r != i1)
        v2 = col_max(jnp.where(rest, lg, NEG))
        i2 = first_row_where(rest & (lg == v2))
        e2 = jnp.exp(v2 - v1)
        den = 1.0 + e2
        gate1 = g1 * (1.0 / den)
        gate2 = g1 * (e2 / den)
        sel1 = r == i1
        sel2 = r == i2
        onehot = jnp.where(sel1 | sel2, 1.0, 0.0)
        rank = _dot(onehot, before_ref[...]) + carry_ref[:, 0:1]
        r1 = jnp.sum(jnp.where(sel1, rank, 0.0), axis=0, keepdims=True)
        r2 = jnp.sum(jnp.where(sel2, rank, 0.0), axis=0, keepdims=True)
        carry_ref[...] = carry_ref[...] + jnp.sum(onehot, axis=1, keepdims=True)
        meta = jnp.where(r8 == 0, (i1 - N_GROUPS).astype(F32), 0.0)
        meta = jnp.where(r8 == 1, (i2 - N_GROUPS).astype(F32), meta)
        meta = jnp.where(r8 == 2, r1, meta)
        meta = jnp.where(r8 == 3, r2, meta)
        meta = jnp.where(r8 == 4, gate1, meta)
        meta = jnp.where(r8 == 5, gate2, meta)
        meta_ref[:, c * sub:(c + 1) * sub] = meta
    cnt_ref[...] = carry_ref[...]


def _route(logits_t, tt=1024, sub=256):
    rows, t = logits_t.shape
    before = jnp.asarray(np.arange(sub)[:, None] < np.arange(sub)[None, :], BF16)
    return pl.pallas_call(
        _route_kernel,
        grid=(t // tt,),
        in_specs=[pl.BlockSpec((rows, tt), lambda i: (0, i)),
                  pl.BlockSpec((sub, sub), lambda i: (0, 0))],
        out_specs=[pl.BlockSpec((SUBLANE, tt), lambda i: (0, i)),
                   pl.BlockSpec((rows, LANE), lambda i: (0, 0))],
        out_shape=[jax.ShapeDtypeStruct((SUBLANE, t), F32),
                   jax.ShapeDtypeStruct((rows, LANE), F32)],
        scratch_shapes=[pltpu.VMEM((rows, LANE), F32)],
        compiler_params=_params("arbitrary"),
        name="route",
    )(logits_t, before)


def _plan_kernel(meta_ref, first_row_ref, dest_ref):
    n_tiles = dest_ref.shape[0]
    tt = dest_ref.shape[2] // 2
    rows = first_row_ref.shape[0]
    first_row = first_row_ref[:, 0:1]
    r = lax.broadcasted_iota(jnp.int32, (rows, tt), 0)
    for g in range(n_tiles):
        m = meta_ref[:, g * tt:(g + 1) * tt]

        def dest_of(k):
            e_row = m[k:k + 1, :].astype(jnp.int32) + N_GROUPS
            return jnp.sum(jnp.where(r == e_row, first_row, 0.0), axis=0, keepdims=True) + m[2 + k:3 + k, :]

        dest_ref[g] = jnp.concatenate([dest_of(0), dest_of(1)], axis=1).astype(jnp.int32)


def _plan(meta_t, first_row, tt, tiles_per_step=4):
    t = meta_t.shape[1]
    rows = first_row.shape[0]
    return pl.pallas_call(
        _plan_kernel,
        grid=(t // (tt * tiles_per_step),),
        in_specs=[pl.BlockSpec((SUBLANE, tt * tiles_per_step), lambda i: (0, i)),
                  pl.BlockSpec((rows, LANE), lambda i: (0, 0))],
        out_specs=pl.BlockSpec((tiles_per_step, 1, 2 * tt), lambda i: (i, 0, 0)),
        out_shape=jax.ShapeDtypeStruct((t // tt, 1, 2 * tt), jnp.int32),
        compiler_params=_params("parallel"),
        name="plan",
    )(meta_t, first_row)


def _row_copy(src, dst, s_row, d_row, sem):
    return pltpu.make_async_copy(src.at[pl.ds(pl.multiple_of(s_row * SUBLANE, SUBLANE), SUBLANE), :],
                                 dst.at[pl.ds(pl.multiple_of(d_row * SUBLANE, SUBLANE), SUBLANE), :], sem)


def _dispatch_kernel(dest_ref, xt_ref, xs_ref, sem):
    tq = dest_ref.shape[2] // 2

    def issue(t, c):
        _row_copy(xt_ref, xs_ref, t, dest_ref[0, 0, t], sem).start(priority=0)
        _row_copy(xt_ref, xs_ref, t, dest_ref[0, 0, tq + t], sem).start(priority=1)
        return c

    lax.fori_loop(0, tq, issue, 0, unroll=8)
    for _ in range(2):
        pltpu.make_async_copy(xt_ref, xs_ref.at[pl.ds(0, tq * SUBLANE), :], sem).wait()


def _dispatch(dest3, x_tiled, n_rows, tq=4096):
    n_tiles, _, two_tt = dest3.shape
    tt = two_tt // 2
    tiles_per_step = tq // tt
    n_steps = n_tiles // tiles_per_step
    dest3 = dest3.reshape(n_steps, tiles_per_step, 2, tt).transpose(0, 2, 1, 3).reshape(n_steps, 1, 2 * tq)
    return pl.pallas_call(
        _dispatch_kernel,
        grid=(n_steps,),
        in_specs=[pl.BlockSpec((1, 1, 2 * tq), lambda i: (i, 0, 0), memory_space=pltpu.SMEM),
                  pl.BlockSpec((tq * SUBLANE, LANE), lambda i: (i, 0))],
        out_specs=pl.BlockSpec(memory_space=pl.ANY),
        out_shape=jax.ShapeDtypeStruct((n_rows * SUBLANE, LANE), F32),
        scratch_shapes=[pltpu.SemaphoreType.DMA(())],
        compiler_params=_params("arbitrary"),
        name="dispatch",
    )(dest3, x_tiled)


def _experts_kernel(blk_exp_ref, blk_rows_ref, n_used_ref, blk_first_ref, blk_slot_ref, blk_next_ref,
                    xs_ref, wg_hbm, wu_hbm, wd_hbm, ys_ref, x_scr, y_scr, wg_buf, wu_buf, wd_buf, wg_bf, wu_bf, wd_bf,
                    sems):
    b = pl.program_id(0)

    def weight_copies(e, slot):
        return [pltpu.make_async_copy(hbm.at[e], buf.at[slot], sems.at[k, slot])
                for k, (hbm, buf) in enumerate(((wg_hbm, wg_buf), (wu_hbm, wu_buf), (wd_hbm, wd_buf)))]

    @pl.when(b < n_used_ref[0])
    def _():
        slot = blk_slot_ref[b]

        @pl.when(b == 0)
        def _():
            for cp in weight_copies(blk_exp_ref[0], slot):
                cp.start()

        @pl.when(blk_first_ref[b] == 1)
        def _():
            @pl.when(blk_next_ref[b] >= 0)
            def _():
                for cp in weight_copies(blk_next_ref[b], 1 - slot):
                    cp.start()

            for cp in weight_copies(blk_exp_ref[b], slot):
                cp.wait()
            wg_bf[...] = wg_buf[slot].astype(BF16)
            wu_bf[...] = wu_buf[slot].astype(BF16)
            wd_bf[...] = wd_buf[slot].astype(BF16)

        rows, d = x_scr.shape
        n_valid = blk_rows_ref[b]

        def run(m):
            ri = lax.broadcasted_iota(jnp.int32, (m, 1), 0)
            for c in range(d // LANE):
                x = xs_ref[pl.ds(c, m, stride=SUBLANE), :]
                x_scr[0:m, c * LANE:(c + 1) * LANE] = jnp.where(ri < n_valid, x, 0.0).astype(BF16)
            x = x_scr[0:m, :]
            hg = _dot(x, wg_bf[...])
            hu = _dot(x, wu_bf[...])
            y_scr[0:m, :] = _dot(hg * jax.nn.sigmoid(hg) * hu, wd_bf[...])
            for c in range(d // LANE):
                ys_ref[pl.ds(c, m, stride=SUBLANE), :] = y_scr[0:m, c * LANE:(c + 1) * LANE]

        n_quanta = rows // EXPERT_QUANTUM
        for k in range(1, n_quanta + 1):
            lo = (k - 1) * EXPERT_QUANTUM
            pl.when((n_valid > lo) & (n_valid <= k * EXPERT_QUANTUM))(functools.partial(run, k * EXPERT_QUANTUM))


def _experts(blk_exp, blk_rows, n_used, blk_first, blk_slot, blk_next, xs_tiled, w_g, w_u, w_d):
    n_rows = xs_tiled.shape[0] // SUBLANE
    rows = EXPERT_ROWS
    nblk = n_rows // rows
    _, d, ff = w_g.shape
    used = lambda b, be, br, nu, *_: (jnp.minimum(b, nu[0] - 1), 0)
    grid_spec = pltpu.PrefetchScalarGridSpec(
        num_scalar_prefetch=6,
        grid=(nblk,),
        in_specs=[pl.BlockSpec((rows * SUBLANE, LANE), used),
                  pl.BlockSpec(memory_space=pl.ANY), pl.BlockSpec(memory_space=pl.ANY),
                  pl.BlockSpec(memory_space=pl.ANY)],
        out_specs=pl.BlockSpec((rows * SUBLANE, LANE), used),
        scratch_shapes=[pltpu.VMEM((rows, d), BF16), pltpu.VMEM((rows, d), F32),
                        pltpu.VMEM((2, d, ff), F32), pltpu.VMEM((2, d, ff), F32), pltpu.VMEM((2, ff, d), F32),
                        pltpu.VMEM((d, ff), BF16), pltpu.VMEM((d, ff), BF16), pltpu.VMEM((ff, d), BF16),
                        pltpu.SemaphoreType.DMA((3, 2))],
    )
    return pl.pallas_call(
        _experts_kernel,
        grid_spec=grid_spec,
        out_shape=jax.ShapeDtypeStruct((n_rows * SUBLANE, LANE), F32),
        compiler_params=_params("arbitrary"),
        name="experts",
    )(blk_exp, blk_rows, n_used, blk_first, blk_slot, blk_next, xs_tiled, w_g, w_u, w_d)


def _combine_kernel(alpha, dcur_ref, dnext_ref, meta_ref, x_ref, g_ref, b_ref, ys_ref, o_ref, buf_ref, sems):
    i = pl.program_id(0)
    tq, d = x_ref.shape
    slot_rows = 2 * tq * SUBLANE

    def issue(d_ref, slot, t):
        _row_copy(ys_ref, buf_ref, d_ref[0, 0, t], slot * (2 * tq) + t, sems.at[slot]).start(priority=0)
        _row_copy(ys_ref, buf_ref, d_ref[0, 0, tq + t], slot * (2 * tq) + tq + t, sems.at[slot]).start(priority=1)

    def wait_slot(slot):
        off = pl.multiple_of(slot * slot_rows, slot_rows)
        pltpu.make_async_copy(ys_ref.at[pl.ds(0, slot_rows), :], buf_ref.at[pl.ds(off, slot_rows), :],
                              sems.at[slot]).wait()
        return off

    def issue_all(d_ref, slot):
        lax.fori_loop(0, tq, lambda t, c: (issue(d_ref, slot, t), c)[1], 0, unroll=4)

    slot = i % 2

    @pl.when(i == 0)
    def _():
        issue_all(dcur_ref, 0)

    @pl.when(i + 1 < pl.num_programs(0))
    def _():
        issue_all(dnext_ref, 1 - slot)

    off = wait_slot(slot)
    meta_rows = jnp.transpose(jnp.concatenate([meta_ref[...], jnp.zeros((LANE - SUBLANE, tq), F32)], axis=0))
    gate1 = meta_rows[:, 4:5]
    gate2 = meta_rows[:, 5:6]
    for c in range(d // LANE):
        sl = slice(c * LANE, (c + 1) * LANE)
        y1 = buf_ref[pl.ds(off + c, tq, stride=SUBLANE), :]
        y2 = buf_ref[pl.ds(off + tq * SUBLANE + c, tq, stride=SUBLANE), :]
        o_ref[:, sl] = alpha * x_ref[:, sl] + (y1 * gate1 + y2 * gate2)
    o_ref[...] = _layer_norm(o_ref[...], g_ref[...], b_ref[...])


def _combine(dest3, meta, x2, g, bb, ys_tiled, alpha):
    t, d = x2.shape
    n = dest3.shape[0]
    tq = t // n
    row = lambda i: (i, 0)
    const = lambda i: (0, 0)
    return pl.pallas_call(
        functools.partial(_combine_kernel, alpha),
        grid=(n,),
        in_specs=[pl.BlockSpec((1, 1, 2 * tq), lambda i: (i, 0, 0), memory_space=pltpu.SMEM),
                  pl.BlockSpec((1, 1, 2 * tq), lambda i: (jnp.minimum(i + 1, n - 1), 0, 0), memory_space=pltpu.SMEM),
                  pl.BlockSpec((SUBLANE, tq), lambda i: (0, i)),
                  pl.BlockSpec((tq, d), row),
                  pl.BlockSpec((1, d), const), pl.BlockSpec((1, d), const),
                  pl.BlockSpec(memory_space=pl.ANY)],
        out_specs=pl.BlockSpec((tq, d), row),
        out_shape=jax.ShapeDtypeStruct((t, d), F32),
        scratch_shapes=[pltpu.VMEM((2 * 2 * tq * SUBLANE, LANE), F32), pltpu.SemaphoreType.DMA((2,))],
        compiler_params=_params("arbitrary"),
        name="combine",
    )(dest3, dest3, meta, x2, g, bb, ys_tiled)


def _moe(x2, x2_tiled, logits, w_g, w_u, w_d, g, bb, alpha):
    t, d = x2.shape
    meta, cnt = _route(logits)
    counts = cnt[N_GROUPS:N_GROUPS + N_EXPERTS, 0].astype(jnp.int32)
    padded = ((counts + EXPERT_ROWS - 1) // EXPERT_ROWS) * EXPERT_ROWS
    pend = jnp.cumsum(padded)
    poff = pend - padded
    first_row = jnp.pad(poff.astype(F32), (N_GROUPS, ROUTE_ROWS - N_GROUPS - N_EXPERTS))
    dest3 = _plan(meta, jnp.broadcast_to(first_row[:, None], (ROUTE_ROWS, LANE)), ROUTE_TILE)
    n_rows = t * 2 + N_EXPERTS * EXPERT_ROWS
    nblk = n_rows // EXPERT_ROWS
    blk_start = jnp.arange(nblk, dtype=jnp.int32) * EXPERT_ROWS
    blk_exp = jnp.sum((pend[None, :] <= blk_start[:, None]).astype(jnp.int32), axis=1)
    blk_exp = jnp.minimum(blk_exp, N_EXPERTS - 1)
    e_idx = jnp.arange(N_EXPERTS, dtype=jnp.int32)
    owner = blk_exp[:, None] == e_idx[None, :]

    def of_block(per_expert):
        return jnp.sum(jnp.where(owner, per_expert[None, :], 0), axis=1).astype(jnp.int32)

    blk_rows = jnp.clip(of_block(poff + counts) - blk_start, 0, EXPERT_ROWS).astype(jnp.int32)
    n_used = (pend[-1:] // EXPERT_ROWS).astype(jnp.int32)
    has_rows = counts > 0
    ordinal = jnp.cumsum(has_rows.astype(jnp.int32)) - 1
    later = has_rows[None, :] & (e_idx[None, :] > e_idx[:, None])
    next_used = jnp.min(jnp.where(later, e_idx[None, :], N_EXPERTS), axis=1)
    next_used = jnp.where(next_used == N_EXPERTS, -1, next_used).astype(jnp.int32)
    blk_first = (blk_start == of_block(poff)).astype(jnp.int32)
    blk_slot = of_block(ordinal % 2)
    blk_next = of_block(next_used)
    xs_tiled = _dispatch(dest3, x2_tiled, n_rows)
    ys_tiled = _experts(blk_exp, blk_rows, n_used, blk_first, blk_slot, blk_next, xs_tiled, w_g, w_u, w_d)
    return _combine(dest3, meta, x2, g, bb, ys_tiled, alpha)


def kernel(x, mem, w_in, w_out, ln_mix_g, ln_mix_b, w_xq, w_xkv, w_xo, ln_x_g, ln_x_b, w_route_group,
           w_route_expert, w_exp_gate, w_exp_up, w_exp_down, ln_moe_g, ln_moe_b):
    b, s, d = x.shape
    depth = w_in.shape[0]
    alpha = (2.0 * depth) ** 0.25
    t = b * s
    ret_cols = (2 * RET_HEADS * RET_QK_DIM + 2 * RET_HEADS * RET_V_DIM)
    assert s % DIL_SUPER == 0 and s % (RET_STEP_CHUNKS * RET_CHUNK) == 0, "sequence length not tileable"
    assert w_in.shape[2] == ret_cols + 3 * DIL_HEADS * DIL_HEAD_DIM and d % (XATTN_HEADS * LANE) == 0
    assert w_route_group.shape[2] == N_GROUPS and w_route_expert.shape[2] == N_EXPERTS
    xc = x.reshape(t, d)
    for l in range(depth):
        n_qk = 2 * RET_HEADS * RET_QK_DIM
        h_ret, h_dil = _proj_in(xc, w_in[l], _retention_weight_layout(w_in[l][:, :n_qk]).astype(BF16), ret_cols)
        y_ret = _retention(h_ret.reshape(b, s, ret_cols)).reshape(t, -1)
        y_dil = _dilated(h_dil, b).reshape(t, -1)
        kv = _mem_kv(mem.reshape(b * mem.shape[1], d), w_xkv[l]).reshape(b, mem.shape[1], 2 * d)
        w_r = jnp.concatenate([w_route_group[l], w_route_expert[l]], axis=-1)
        w_r = jnp.pad(w_r.T, ((0, ROUTE_ROWS - w_r.shape[1]), (0, 0))).astype(BF16)
        x2, x2_tiled, logits = _xattn(xc, y_ret, y_dil, w_out[l].astype(BF16), ln_mix_g[l][None], ln_mix_b[l][None],
                                      kv, w_xq[l].astype(BF16), w_xo[l].astype(BF16),
                                      ln_x_g[l][None], ln_x_b[l][None], w_r, alpha, s)
        xc = _moe(x2, x2_tiled, logits, w_exp_gate[l], w_exp_up[l], w_exp_down[l],
                  ln_moe_g[l][None], ln_moe_b[l][None], alpha)
    return xc.reshape(b, s, d)
```

```python
import functools
import math

import jax
import jax.numpy as jnp
import numpy as np
from jax import lax
from jax.experimental import pallas as pl
from jax.experimental.pallas import tpu as pltpu

BF16 = jnp.bfloat16
F32 = jnp.float32

LANE = 128
SUBLANE = 8
VMEM_LIMIT = 56 * 1024 * 1024

RET_HEADS = 4
RET_QK_DIM = 64
RET_V_DIM = 128
RET_CHUNK = 128
RET_STEP_CHUNKS = 4
ROPE_BASE = 10000.0
DIL_HEADS = 8
DIL_HEAD_DIM = 64
DIL_DILATIONS = (1, 4, 16)
DIL_BLOCK = 128
DIL_SUPER = DIL_BLOCK * max(DIL_DILATIONS)
DIL_GROUP = 1
XATTN_HEADS = 4
XATTN_PARTS = 2
N_GROUPS = 4
EXPERTS_PER_GROUP = 8
N_EXPERTS = N_GROUPS * EXPERTS_PER_GROUP
ROUTE_ROWS = -(-(N_GROUPS + N_EXPERTS) // SUBLANE) * SUBLANE
EXPERT_ROWS = 512
EXPERT_QUANTUM = 256
ROUTE_TILE = 1024
LN_EPS = 1e-5
GN_EPS = 1e-6
NEG = -1e30


def _params(*sem):
    return pltpu.CompilerParams(dimension_semantics=sem, vmem_limit_bytes=VMEM_LIMIT)


def _layer_norm(z, g, b):
    mu = jnp.mean(z, axis=-1, keepdims=True)
    zc = z - mu
    var = jnp.mean(zc * zc, axis=-1, keepdims=True)
    return zc * lax.rsqrt(var + LN_EPS) * g + b


def _dot(a, b):
    return jnp.dot(a.astype(BF16), b.astype(BF16), preferred_element_type=F32)


def _dot_nt(a, b):
    return lax.dot_general(a.astype(BF16), b.astype(BF16), (((1,), (1,)), ((), ())),
                           preferred_element_type=F32)


def _dot_tn(a, b):
    return lax.dot_general(a.astype(BF16), b.astype(BF16), (((0,), (0,)), ((), ())),
                           preferred_element_type=F32)


def _proj_in_kernel(x_ref, w_ref, o_ref):
    o_ref[...] = _dot(x_ref[...], w_ref[...]).astype(o_ref.dtype)


def _proj_in2_kernel(x_ref, w_ref, w_head_ref, o_ref, slab_ref, w_bf16):
    @pl.when(pl.program_id(0) == 0)
    def _():
        w_bf16[...] = w_ref[...].astype(BF16)
        w_bf16[:, 0:w_head_ref.shape[1]] = w_head_ref[...]

    n_row = o_ref.shape[1]
    x = x_ref[...].astype(BF16)
    o_ref[...] = jnp.dot(x, w_bf16[:, 0:n_row], preferred_element_type=F32)
    rest = jnp.dot(x, w_bf16[:, n_row:], preferred_element_type=F32)
    for c in range(slab_ref.shape[0]):
        slab_ref[c] = rest[:, c * LANE:(c + 1) * LANE]


def _proj_in(x2d, w_f32, w_head_bf16, n_row, tm=1024):
    t, d = x2d.shape
    n = w_f32.shape[1]
    n_slabs = (n - n_row) // LANE
    whole = lambda shape: pl.BlockSpec(shape, lambda i: (0, 0), pipeline_mode=pl.Buffered(1))
    return pl.pallas_call(
        _proj_in2_kernel,
        grid=(t // tm,),
        in_specs=[pl.BlockSpec((tm, d), lambda i: (i, 0)), whole((d, n)), whole(w_head_bf16.shape)],
        out_specs=[pl.BlockSpec((tm, n_row), lambda i: (i, 0)),
                   pl.BlockSpec((n_slabs, tm, LANE), lambda i: (0, i, 0))],
        out_shape=[jax.ShapeDtypeStruct((t, n_row), F32),
                   jax.ShapeDtypeStruct((n_slabs, t, LANE), F32)],
        scratch_shapes=[pltpu.VMEM((d, n), BF16)],
        compiler_params=_params("arbitrary"),
        name="proj_in",
    )(x2d, w_f32, w_head_bf16)


def _retention_kernel(qk_ref, v_ref, g_ref, cos_ref, sin_ref, decay_ref, zeta_ref, xi_ref, gam_ref,
                      o_ref, state_ref, y_ref):
    n = pl.program_id(1)

    @pl.when(n == 0)
    def _():
        state_ref[...] = jnp.zeros_like(state_ref)

    c = RET_CHUNK
    n_sub = qk_ref.shape[1] // c
    half = RET_QK_DIM // 2
    lane = lax.broadcasted_iota(jnp.int32, (c, LANE), 1)

    rotated = []
    for j in range(n_sub):
        rows = slice(j * c, (j + 1) * c)
        cos = cos_ref[rows, :]
        sin = sin_ref[rows, :]

        def rot(col, rows=rows, cos=cos, sin=sin):
            t1 = qk_ref[0, rows, col * LANE:(col + 1) * LANE]
            t2 = qk_ref[0, rows, (col + 1) * LANE:(col + 2) * LANE]
            return t1 * cos - t2 * sin, t1 * sin + t2 * cos

        q1, q2 = rot(0)
        k1, k2 = (t * (RET_QK_DIM ** -0.5) for t in rot(2))
        rotated.append((q1, q2, k1, k2, jnp.concatenate([k1, k2], axis=1)))

    for h in range(RET_HEADS):
        cols = slice(h * RET_V_DIM, (h + 1) * RET_V_DIM)
        mine = (lane >= h * half) & (lane < (h + 1) * half)
        zeta = zeta_ref[h]
        st = state_ref[h]
        pad_lo = [jnp.zeros((h * half, RET_V_DIM), BF16)] if h > 0 else []
        pad_hi = [jnp.zeros(((RET_HEADS - 1 - h) * half, RET_V_DIM), BF16)] if h < RET_HEADS - 1 else []
        for j in range(n_sub):
            rows = slice(j * c, (j + 1) * c)
            q1, q2, k1, k2, k_all = rotated[j]
            qm = jnp.concatenate([jnp.where(mine, q1, 0.0), jnp.where(mine, q2, 0.0)], axis=1)
            v = v_ref[0, rows, cols]
            s = _dot_nt(qm, k_all) * decay_ref[h]
            sb = st.astype(BF16)
            st_rows = jnp.concatenate(pad_lo + [sb[:half]] + pad_hi + pad_lo + [sb[half:]] + pad_hi, axis=0)
            y_ref[rows, cols] = _dot(s, v) + _dot(qm, st_rows) * xi_ref[h]
            kv = _dot_tn(jnp.concatenate([k1 * zeta, k2 * zeta], axis=1), v)
            lo = h * half
            hi = RET_HEADS * half + h * half
            st = gam_ref[h, :, :] * st + jnp.concatenate([kv[lo:lo + half], kv[hi:hi + half]], axis=0)
        state_ref[h] = st

    for h in range(RET_HEADS):
        cols = slice(h * RET_V_DIM, (h + 1) * RET_V_DIM)
        y = y_ref[:, cols]
        mu = jnp.mean(y, axis=-1, keepdims=True)
        yc = y - mu
        var = jnp.mean(yc * yc, axis=-1, keepdims=True)
        yn = yc * lax.rsqrt(var + GN_EPS)
        gate = g_ref[0, :, cols]
        o_ref[0, :, cols] = (gate * jax.nn.sigmoid(gate) * yn).astype(o_ref.dtype)


def _retention_tables(s):
    half = RET_QK_DIM // 2
    inv = 1.0 / (ROPE_BASE ** (np.arange(half, dtype=np.float64) / half))
    ang = np.arange(s, dtype=np.float64)[:, None] * inv[None, :]
    cos_t = np.tile(np.cos(ang), (1, RET_HEADS))
    sin_t = np.tile(np.sin(ang), (1, RET_HEADS))
    c = RET_CHUNK
    lg = np.log(1.0 - np.exp2(-5.0 - np.arange(RET_HEADS, dtype=np.float64)))
    idx = np.arange(c, dtype=np.float64)
    diff = idx[:, None] - idx[None, :]
    decay = np.where(diff >= 0, np.exp(lg[:, None, None] * np.maximum(diff, 0.0)), 0.0)
    lanes = (RET_HEADS, c, LANE)
    zeta = np.broadcast_to(np.exp(lg[:, None] * (c - 1.0 - idx))[:, :, None], lanes)
    xi = np.broadcast_to(np.exp(lg[:, None] * (idx + 1.0))[:, :, None], lanes)
    gam = np.broadcast_to(np.exp(lg * c)[:, None, None], (RET_HEADS, 1, LANE))
    return tuple(jnp.asarray(np.ascontiguousarray(a), F32) for a in (cos_t, sin_t, decay, zeta, xi, gam))


def _retention_weight_layout(w_qk):
    d = w_qk.shape[0]
    half = RET_QK_DIM // 2
    return w_qk.reshape(d, 2, RET_HEADS, 2, half).transpose(0, 1, 3, 2, 4).reshape(d, -1)


def _retention(h3):
    b, s, _ = h3.shape
    c = RET_CHUNK
    qk_w = 2 * RET_HEADS * RET_QK_DIM
    v_w = RET_HEADS * RET_V_DIM
    assert qk_w == v_w
    cos_t, sin_t, decay, zeta, xi, gam = _retention_tables(s)
    const3 = lambda bi, n: (0, 0, 0)
    rows = RET_STEP_CHUNKS * c
    return pl.pallas_call(
        _retention_kernel,
        grid=(b, s // rows),
        in_specs=[pl.BlockSpec((1, rows, qk_w), lambda bi, n: (bi, n, 0)),
                  pl.BlockSpec((1, rows, v_w), lambda bi, n: (bi, n, 1)),
                  pl.BlockSpec((1, rows, v_w), lambda bi, n: (bi, n, 2)),
                  pl.BlockSpec((rows, LANE), lambda bi, n: (n, 0)),
                  pl.BlockSpec((rows, LANE), lambda bi, n: (n, 0)),
                  pl.BlockSpec((RET_HEADS, c, c), const3),
                  pl.BlockSpec((RET_HEADS, c, LANE), const3),
                  pl.BlockSpec((RET_HEADS, c, LANE), const3),
                  pl.BlockSpec((RET_HEADS, 1, LANE), const3)],
        out_specs=pl.BlockSpec((1, rows, v_w), lambda bi, n: (bi, n, 0)),
        out_shape=jax.ShapeDtypeStruct((b, s, v_w), BF16),
        scratch_shapes=[pltpu.VMEM((RET_HEADS, RET_QK_DIM, RET_V_DIM), F32),
                        pltpu.VMEM((rows, v_w), F32)],
        compiler_params=_params("parallel", "arbitrary"),
        name="retention",
    )(h3, h3, h3, cos_t, sin_t, decay, zeta, xi, gam)


def _dilated_kernel(q_ref, kp_ref, kc_ref, vp_ref, vc_ref, bias_ref,
                    o_ref, acc_ref, m_ref, l_ref, s0_ref, s1_ref, p0_ref, p1_ref, qw_ref, kw_ref, vw_ref):
    j = pl.program_id(2)
    sup = DIL_SUPER
    q_blk = DIL_BLOCK
    scale = DIL_HEAD_DIM ** -0.5 * math.log2(math.e)
    n_blocks = sup // q_blk
    lane = lax.broadcasted_iota(jnp.int32, (q_blk, LANE), 1)
    head0 = lane < DIL_HEAD_DIM
    first_bias = jnp.where(j == 0, 1, 0)

    slot = j % 2

    def relay(r):
        cols = slice(r * LANE, (r + 1) * LANE)
        qw_ref[:, cols] = q_ref[0, pl.ds(r, q_blk, stride=n_blocks), :]
        kw_ref[slot, :, cols] = kc_ref[0, pl.ds(r, q_blk, stride=n_blocks), :]
        vw_ref[slot, :, cols] = vc_ref[0, pl.ds(r, q_blk, stride=n_blocks), :]

    @pl.when(j == 0)
    def _():
        kw_ref[1] = jnp.zeros(kw_ref.shape[1:], F32)
        vw_ref[1] = jnp.zeros(vw_ref.shape[1:], F32)

    groups = []
    for bi, d in enumerate(DIL_DILATIONS):
        n_per_r = sup // (q_blk * d)
        for t0 in range(0, n_blocks, DIL_GROUP):
            blocks = []
            for t in range(t0, t0 + DIL_GROUP):
                r, n = divmod(t, n_per_r)
                blocks.append((n * (q_blk * d) + r, (n - 1) * (q_blk * d) + r, n))
            groups.append((bi, d, blocks))
    s_bufs = (s0_ref, s1_ref)
    p_bufs = (p0_ref, p1_ref)

    def window(prev_ref, cur_ref, k_start, d):
        if k_start >= 0:
            return cur_ref[0, pl.ds(k_start, 2 * q_blk, stride=d), :]
        return jnp.concatenate([prev_ref[0, pl.ds(sup + k_start, q_blk, stride=d), :],
                                cur_ref[0, pl.ds(k_start + q_blk * d, q_blk, stride=d), :]], axis=0)

    def residue_window(ref, q_start):
        cols = slice(q_start * LANE, (q_start + 1) * LANE)
        return jnp.concatenate([ref[1 - slot, :, cols], ref[slot, :, cols]], axis=0)

    def scores(gi):
        bi, d, blocks = groups[gi]
        for g, (q_start, k_start, n) in enumerate(blocks):
            if d == n_blocks:
                q = qw_ref[:, q_start * LANE:(q_start + 1) * LANE] * scale
                kb = residue_window(kw_ref, q_start)
            else:
                q = q_ref[0, pl.ds(q_start, q_blk, stride=d), :] * scale
                kb = window(kp_ref, kc_ref, k_start, d)
            q2 = jnp.concatenate([jnp.where(head0, q, 0.0), jnp.where(head0, 0.0, q)], axis=0)
            bias = bias_ref[first_bias] if n == 0 else bias_ref[0]
            s_bufs[gi % 2][g] = _dot_nt(q2, kb) + bias

    def softmax(gi):
        bi, d, blocks = groups[gi]
        for g, (q_start, k_start, n) in enumerate(blocks):
            s = s_bufs[gi % 2][g]
            m2 = jnp.max(s, axis=-1, keepdims=True)
            p_bufs[gi % 2][g] = jnp.exp2(s - m2).astype(BF16)
            m_ref[bi, pl.ds(q_start, q_blk, stride=d), :] = jnp.where(head0, m2[:q_blk], m2[q_blk:])

    def values(gi):
        bi, d, blocks = groups[gi]
        for g, (q_start, k_start, n) in enumerate(blocks):
            if d == n_blocks:
                vb = residue_window(vw_ref, q_start).astype(BF16)
            else:
                vb = window(vp_ref, vc_ref, k_start, d).astype(BF16)
            o2 = jnp.dot(p_bufs[gi % 2][g], jnp.concatenate([vb, jnp.ones_like(vb)], axis=1),
                         preferred_element_type=F32)
            rows = pl.ds(q_start, q_blk, stride=d)
            acc_ref[bi, rows, :] = jnp.where(head0, o2[:q_blk, :LANE], o2[q_blk:, :LANE])
            l_ref[bi, rows, :] = jnp.where(head0, o2[:q_blk, LANE:], o2[q_blk:, LANE:])

    n_narrow = sum(1 for _, d, _ in groups if d != n_blocks)
    for step in range(len(groups) + 2):
        if step < n_narrow and step % (n_narrow // n_blocks) == 0:
            relay(step // (n_narrow // n_blocks))
        if step < len(groups):
            scores(step)
        if 0 <= step - 1 < len(groups):
            softmax(step - 1)
        if step - 2 >= 0:
            values(step - 2)

    def merge(c, carry):
        rows = pl.ds(pl.multiple_of(c * q_blk, q_blk), q_blk)
        ms = [m_ref[bi, rows, :] for bi in range(len(DIL_DILATIONS))]
        m_all = functools.reduce(jnp.maximum, ms)
        ws = [jnp.exp2(m - m_all) for m in ms]
        num = functools.reduce(lambda a, b: a + b, [w * acc_ref[bi, rows, :] for bi, w in enumerate(ws)])
        den = functools.reduce(lambda a, b: a + b, [w * l_ref[bi, rows, :] for bi, w in enumerate(ws)])
        o_ref[0, rows, :] = (num / den).astype(o_ref.dtype)
        return carry

    lax.fori_loop(0, n_blocks, merge, 0)


def _dilated_bias():
    q_blk = DIL_BLOCK
    qi = np.arange(2 * q_blk)[:, None] % q_blk
    kj = np.arange(2 * q_blk)[None, :]
    band = (kj >= qi) & (kj <= qi + q_blk)
    return jnp.asarray(np.stack([np.where(band, 0.0, NEG), np.where(band & (kj >= q_blk), 0.0, NEG)]), F32)


def _dilated(slabs, b):
    n_slabs, t, _ = slabs.shape
    s = t // b
    h3 = slabs.reshape(n_slabs * b, s, LANE)
    sup = DIL_SUPER
    n_pairs = DIL_HEADS * DIL_HEAD_DIM // LANE
    n_br = len(DIL_DILATIONS)
    cq, ck, cv = 0, n_pairs, 2 * n_pairs
    cur = lambda c: (lambda bi, p, j: ((c + p) * b + bi, j, 0))
    prev = lambda c: (lambda bi, p, j: ((c + p) * b + bi, jnp.maximum(j - 1, 0), 0))
    blk = (1, sup, LANE)
    d_wide = sup // DIL_BLOCK
    assert d_wide == max(DIL_DILATIONS)
    wide = (DIL_BLOCK, d_wide * LANE)
    return pl.pallas_call(
        _dilated_kernel,
        grid=(b, n_pairs, s // sup),
        in_specs=[pl.BlockSpec(blk, cur(cq)),
                  pl.BlockSpec(blk, prev(ck)), pl.BlockSpec(blk, cur(ck)),
                  pl.BlockSpec(blk, prev(cv)), pl.BlockSpec(blk, cur(cv)),
                  pl.BlockSpec((2, 2 * DIL_BLOCK, 2 * DIL_BLOCK), lambda bi, p, j: (0, 0, 0))],
        out_specs=pl.BlockSpec(blk, lambda bi, p, j: (bi, j, p)),
        out_shape=jax.ShapeDtypeStruct((b, s, n_pairs * LANE), BF16),
        scratch_shapes=[pltpu.VMEM((n_br, sup, LANE), F32), pltpu.VMEM((n_br, sup, LANE), F32),
                        pltpu.VMEM((n_br, sup, LANE), F32),
                        pltpu.VMEM((DIL_GROUP, 2 * DIL_BLOCK, 2 * DIL_BLOCK), F32),
                        pltpu.VMEM((DIL_GROUP, 2 * DIL_BLOCK, 2 * DIL_BLOCK), F32),
                        pltpu.VMEM((DIL_GROUP, 2 * DIL_BLOCK, 2 * DIL_BLOCK), BF16),
                        pltpu.VMEM((DIL_GROUP, 2 * DIL_BLOCK, 2 * DIL_BLOCK), BF16),
                        pltpu.VMEM(wide, F32), pltpu.VMEM((2,) + wide, F32), pltpu.VMEM((2,) + wide, F32)],
        compiler_params=_params("parallel", "parallel", "arbitrary"),
        name="dilated",
    )(h3, h3, h3, h3, h3, _dilated_bias())


def _mem_kv(mem2d, w_bf16, tn=512):
    m, d = mem2d.shape
    n = w_bf16.shape[1]
    return pl.pallas_call(
        _proj_in_kernel,
        grid=(n // tn,),
        in_specs=[pl.BlockSpec((m, d), lambda i: (0, 0)),
                  pl.BlockSpec((d, tn), lambda i: (0, i))],
        out_specs=pl.BlockSpec((m, tn), lambda i: (0, i)),
        out_shape=jax.ShapeDtypeStruct((m, n), BF16),
        compiler_params=_params("parallel"),
        name="mem_kv",
    )(mem2d, w_bf16)


def _xattn_kernel(alpha, x_ref, yr_ref, yd_ref, wout_ref, g1_ref, b1_ref, k_ref, v_ref, wq_ref, wo_ref,
                  g_ref, b_ref, wr_ref, o_ref, ot_ref, lg_ref, x1_ref, q_ref, att_ref):
    tm, d = x_ref.shape
    dh = d // XATTN_HEADS
    wr = yr_ref.shape[1]
    scale = dh ** -0.5 * math.log2(math.e)
    part = tm // XATTN_PARTS

    def mix(rows):
        y = _dot(yr_ref[rows, :], wout_ref[0:wr, :]) + _dot(yd_ref[rows, :], wout_ref[wr:, :])
        x1_ref[rows, :] = _layer_norm(alpha * x_ref[rows, :] + y, g1_ref[...], b1_ref[...])

    def query(rows):
        q_ref[rows, :] = _dot(x1_ref[rows, :], wq_ref[...]).astype(BF16)

    def attend(rows):
        for h in range(XATTN_HEADS):
            sl = slice(h * dh, (h + 1) * dh)
            s = _dot_nt(q_ref[rows, sl], k_ref[0, :, sl]) * scale
            m = jnp.max(s, axis=-1, keepdims=True)
            e = jnp.exp2(s - m)
            p = e / jnp.sum(e, axis=-1, keepdims=True)
            att_ref[rows, sl] = _dot(p, v_ref[0, :, sl]).astype(BF16)

    def finish(rows, r0):
        y = _dot(att_ref[rows, :], wo_ref[...])
        x2 = _layer_norm(alpha * x1_ref[rows, :] + y, g_ref[...], b_ref[...])
        o_ref[rows, :] = x2
        for c in range(d // LANE):
            ot_ref[pl.ds(r0 * SUBLANE + c, part, stride=SUBLANE), :] = x2[:, c * LANE:(c + 1) * LANE]
        lg_ref[:, rows] = _dot_nt(wr_ref[...], x2)

    stages = (mix, query, attend, finish)
    for step in range(XATTN_PARTS + len(stages) - 1):
        for si, stage in enumerate(stages):
            pi = step - si
            if 0 <= pi < XATTN_PARTS:
                rows = slice(pi * part, (pi + 1) * part)
                if stage is finish:
                    stage(rows, pi * part)
                else:
                    stage(rows)


def _xattn(x2d, y_ret, y_dil, w_out, g1, b1, kv, wq, wo, g, bb, w_r, alpha, seq, tm=1024):
    t, d = x2d.shape
    mlen = kv.shape[1]
    tiles_per_seq = seq // tm
    row = lambda i: (i, 0)

    def whole(shape):
        return pl.BlockSpec(shape, lambda i: (0,) * len(shape), pipeline_mode=pl.Buffered(1))

    vec = whole((1, d))
    return pl.pallas_call(
        functools.partial(_xattn_kernel, alpha),
        grid=(t // tm,),
        in_specs=[pl.BlockSpec((tm, d), row),
                  pl.BlockSpec((tm, y_ret.shape[1]), row),
                  pl.BlockSpec((tm, y_dil.shape[1]), row),
                  whole(w_out.shape), vec, vec,
                  pl.BlockSpec((1, mlen, d), lambda i: (i // tiles_per_seq, 0, 0)),
                  pl.BlockSpec((1, mlen, d), lambda i: (i // tiles_per_seq, 0, 1)),
                  whole((d, d)), whole((d, d)), vec, vec,
                  whole((ROUTE_ROWS, d))],
        out_specs=[pl.BlockSpec((tm, d), row),
                   pl.BlockSpec((tm * SUBLANE, LANE), row),
                   pl.BlockSpec((ROUTE_ROWS, tm), lambda i: (0, i))],
        out_shape=[jax.ShapeDtypeStruct((t, d), F32),
                   jax.ShapeDtypeStruct((t * SUBLANE, LANE), F32),
                   jax.ShapeDtypeStruct((ROUTE_ROWS, t), F32)],
        scratch_shapes=[pltpu.VMEM((tm, d), F32), pltpu.VMEM((tm, d), BF16), pltpu.VMEM((tm, d), BF16)],
        compiler_params=_params("parallel"),
        name="xattn",
    )(x2d, y_ret, y_dil, w_out, g1, b1, kv, kv, wq, wo, g, bb, w_r)


def _route_kernel(lg_ref, before_ref, meta_ref, cnt_ref, carry_ref):
    i = pl.program_id(0)

    @pl.when(i == 0)
    def _():
        carry_ref[...] = jnp.zeros_like(carry_ref)

    rows, tt = lg_ref.shape
    sub = before_ref.shape[0]
    r = lax.broadcasted_iota(jnp.int32, (rows, sub), 0)
    r8 = lax.broadcasted_iota(jnp.int32, (SUBLANE, sub), 0)
    is_group = r < N_GROUPS

    def col_max(a):
        return jnp.max(a, axis=0, keepdims=True)

    def first_row_where(mask):
        return jnp.min(jnp.where(mask, r, rows), axis=0, keepdims=True)

    for c in range(tt // sub):
        lg = lg_ref[:, c * sub:(c + 1) * sub]
        mg = col_max(jnp.where(is_group, lg, NEG))
        eg = jnp.where(is_group, jnp.exp(lg - mg), 0.0)
        pg = eg / jnp.sum(eg, axis=0, keepdims=True)
        g1 = col_max(pg)
        gi = first_row_where(is_group & (pg == g1))
        lo = N_GROUPS + gi * EXPERTS_PER_GROUP
        in_grp = (r >= lo) & (r < lo + EXPERTS_PER_GROUP)
        v1 = col_max(jnp.where(in_grp, lg, NEG))
        i1 = first_row_where(in_grp & (lg == v1))
        rest = in_grp & (r != i1)
        v2 = col_max(jnp.where(rest, lg, NEG))
        i2 = first_row_where(rest & (lg == v2))
        e2 = jnp.exp(v2 - v1)
        den = 1.0 + e2
        gate1 = g1 * (1.0 / den)
        gate2 = g1 * (e2 / den)
        sel1 = r == i1
        sel2 = r == i2
        onehot = jnp.where(sel1 | sel2, 1.0, 0.0)
        rank = _dot(onehot, before_ref[...]) + carry_ref[:, 0:1]
        r1 = jnp.sum(jnp.where(sel1, rank, 0.0), axis=0, keepdims=True)
        r2 = jnp.sum(jnp.where(sel2, rank, 0.0), axis=0, keepdims=True)
        carry_ref[...] = carry_ref[...] + jnp.sum(onehot, axis=1, keepdims=True)
        meta = jnp.where(r8 == 0, (i1 - N_GROUPS).astype(F32), 0.0)
        meta = jnp.where(r8 == 1, (i2 - N_GROUPS).astype(F32), meta)
        meta = jnp.where(r8 == 2, r1, meta)
        meta = jnp.where(r8 == 3, r2, meta)
        meta = jnp.where(r8 == 4, gate1, meta)
        meta = jnp.where(r8 == 5, gate2, meta)
        meta_ref[:, c * sub:(c + 1) * sub] = meta
    cnt_ref[...] = carry_ref[...]


def _route(logits_t, tt=1024, sub=256):
    rows, t = logits_t.shape
    before = jnp.asarray(np.arange(sub)[:, None] < np.arange(sub)[None, :], BF16)
    return pl.pallas_call(
        _route_kernel,
        grid=(t // tt,),
        in_specs=[pl.BlockSpec((rows, tt), lambda i: (0, i)),
                  pl.BlockSpec((sub, sub), lambda i: (0, 0))],
        out_specs=[pl.BlockSpec((SUBLANE, tt), lambda i: (0, i)),
                   pl.BlockSpec((rows, LANE), lambda i: (0, 0))],
        out_shape=[jax.ShapeDtypeStruct((SUBLANE, t), F32),
                   jax.ShapeDtypeStruct((rows, LANE), F32)],
        scratch_shapes=[pltpu.VMEM((rows, LANE), F32)],
        compiler_params=_params("arbitrary"),
        name="route",
    )(logits_t, before)


def _plan_kernel(meta_ref, first_row_ref, dest_ref):
    n_tiles = dest_ref.shape[0]
    tt = dest_ref.shape[2] // 2
    rows = first_row_ref.shape[0]
    first_row = first_row_ref[:, 0:1]
    r = lax.broadcasted_iota(jnp.int32, (rows, tt), 0)
    for g in range(n_tiles):
        m = meta_ref[:, g * tt:(g + 1) * tt]

        def dest_of(k):
            e_row = m[k:k + 1, :].astype(jnp.int32) + N_GROUPS
            return jnp.sum(jnp.where(r == e_row, first_row, 0.0), axis=0, keepdims=True) + m[2 + k:3 + k, :]

        dest_ref[g] = jnp.concatenate([dest_of(0), dest_of(1)], axis=1).astype(jnp.int32)


def _plan(meta_t, first_row, tt, tiles_per_step=4):
    t = meta_t.shape[1]
    rows = first_row.shape[0]
    return pl.pallas_call(
        _plan_kernel,
        grid=(t // (tt * tiles_per_step),),
        in_specs=[pl.BlockSpec((SUBLANE, tt * tiles_per_step), lambda i: (0, i)),
                  pl.BlockSpec((rows, LANE), lambda i: (0, 0))],
        out_specs=pl.BlockSpec((tiles_per_step, 1, 2 * tt), lambda i: (i, 0, 0)),
        out_shape=jax.ShapeDtypeStruct((t // tt, 1, 2 * tt), jnp.int32),
        compiler_params=_params("parallel"),
        name="plan",
    )(meta_t, first_row)


def _row_copy(src, dst, s_row, d_row, sem):
    return pltpu.make_async_copy(src.at[pl.ds(pl.multiple_of(s_row * SUBLANE, SUBLANE), SUBLANE), :],
                                 dst.at[pl.ds(pl.multiple_of(d_row * SUBLANE, SUBLANE), SUBLANE), :], sem)


def _dispatch_kernel(dest_ref, xt_ref, xs_ref, sem):
    tq = dest_ref.shape[2] // 2

    def issue(t, c):
        _row_copy(xt_ref, xs_ref, t, dest_ref[0, 0, t], sem).start(priority=0)
        _row_copy(xt_ref, xs_ref, t, dest_ref[0, 0, tq + t], sem).start(priority=1)
        return c

    lax.fori_loop(0, tq, issue, 0, unroll=8)
    for _ in range(2):
        pltpu.make_async_copy(xt_ref, xs_ref.at[pl.ds(0, tq * SUBLANE), :], sem).wait()


def _dispatch(dest3, x_tiled, n_rows, tq=4096):
    n_tiles, _, two_tt = dest3.shape
    tt = two_tt // 2
    tiles_per_step = tq // tt
    n_steps = n_tiles // tiles_per_step
    dest3 = dest3.reshape(n_steps, tiles_per_step, 2, tt).transpose(0, 2, 1, 3).reshape(n_steps, 1, 2 * tq)
    return pl.pallas_call(
        _dispatch_kernel,
        grid=(n_steps,),
        in_specs=[pl.BlockSpec((1, 1, 2 * tq), lambda i: (i, 0, 0), memory_space=pltpu.SMEM),
                  pl.BlockSpec((tq * SUBLANE, LANE), lambda i: (i, 0))],
        out_specs=pl.BlockSpec(memory_space=pl.ANY),
        out_shape=jax.ShapeDtypeStruct((n_rows * SUBLANE, LANE), F32),
        scratch_shapes=[pltpu.SemaphoreType.DMA(())],
        compiler_params=_params("arbitrary"),
        name="dispatch",
    )(dest3, x_tiled)


def _experts_kernel(blk_exp_ref, blk_rows_ref, n_used_ref, blk_first_ref, blk_slot_ref, blk_next_ref,
                    xs_ref, wg_hbm, wu_hbm, wd_hbm, ys_ref, x_scr, y_scr, wg_buf, wu_buf, wd_buf, sems):
    b = pl.program_id(0)

    def weight_copies(e, slot):
        return [pltpu.make_async_copy(hbm.at[e], buf.at[slot], sems.at[k, slot])
                for k, (hbm, buf) in enumerate(((wg_hbm, wg_buf), (wu_hbm, wu_buf), (wd_hbm, wd_buf)))]

    @pl.when(b < n_used_ref[0])
    def _():
        slot = blk_slot_ref[b]

        @pl.when(b == 0)
        def _():
            for cp in weight_copies(blk_exp_ref[0], slot):
                cp.start()

        @pl.when(blk_first_ref[b] == 1)
        def _():
            @pl.when(blk_next_ref[b] >= 0)
            def _():
                for cp in weight_copies(blk_next_ref[b], 1 - slot):
                    cp.start()

            for cp in weight_copies(blk_exp_ref[b], slot):
                cp.wait()

        rows, d = x_scr.shape
        n_valid = blk_rows_ref[b]

        def run(m):
            ri = lax.broadcasted_iota(jnp.int32, (m, 1), 0)
            for c in range(d // LANE):
                x = xs_ref[pl.ds(c, m, stride=SUBLANE), :]
                x_scr[0:m, c * LANE:(c + 1) * LANE] = jnp.where(ri < n_valid, x, 0.0).astype(BF16)
            x = x_scr[0:m, :]
            hg = _dot(x, wg_buf[slot])
            hu = _dot(x, wu_buf[slot])
            y_scr[0:m, :] = _dot(hg * jax.nn.sigmoid(hg) * hu, wd_buf[slot])
            for c in range(d // LANE):
                ys_ref[pl.ds(c, m, stride=SUBLANE), :] = y_scr[0:m, c * LANE:(c + 1) * LANE]

        n_quanta = rows // EXPERT_QUANTUM
        for k in range(1, n_quanta + 1):
            lo = (k - 1) * EXPERT_QUANTUM
            pl.when((n_valid > lo) & (n_valid <= k * EXPERT_QUANTUM))(functools.partial(run, k * EXPERT_QUANTUM))


def _experts(blk_exp, blk_rows, n_used, blk_first, blk_slot, blk_next, xs_tiled, w_g, w_u, w_d):
    n_rows = xs_tiled.shape[0] // SUBLANE
    rows = EXPERT_ROWS
    nblk = n_rows // rows
    _, d, ff = w_g.shape
    used = lambda b, be, br, nu, *_: (jnp.minimum(b, nu[0] - 1), 0)
    grid_spec = pltpu.PrefetchScalarGridSpec(
        num_scalar_prefetch=6,
        grid=(nblk,),
        in_specs=[pl.BlockSpec((rows * SUBLANE, LANE), used),
                  pl.BlockSpec(memory_space=pl.ANY), pl.BlockSpec(memory_space=pl.ANY),
                  pl.BlockSpec(memory_space=pl.ANY)],
        out_specs=pl.BlockSpec((rows * SUBLANE, LANE), used),
        scratch_shapes=[pltpu.VMEM((rows, d), BF16), pltpu.VMEM((rows, d), F32),
                        pltpu.VMEM((2, d, ff), F32), pltpu.VMEM((2, d, ff), F32), pltpu.VMEM((2, ff, d), F32),
                        pltpu.SemaphoreType.DMA((3, 2))],
    )
    return pl.pallas_call(
        _experts_kernel,
        grid_spec=grid_spec,
        out_shape=jax.ShapeDtypeStruct((n_rows * SUBLANE, LANE), F32),
        compiler_params=_params("arbitrary"),
        name="experts",
    )(blk_exp, blk_rows, n_used, blk_first, blk_slot, blk_next, xs_tiled, w_g, w_u, w_d)


def _combine_kernel(alpha, dcur_ref, dnext_ref, meta_ref, x_ref, g_ref, b_ref, ys_ref, o_ref, buf_ref, sems):
    i = pl.program_id(0)
    tq, d = x_ref.shape
    slot_rows = 2 * tq * SUBLANE

    def issue(d_ref, slot, t):
        _row_copy(ys_ref, buf_ref, d_ref[0, 0, t], slot * (2 * tq) + t, sems.at[slot]).start(priority=0)
        _row_copy(ys_ref, buf_ref, d_ref[0, 0, tq + t], slot * (2 * tq) + tq + t, sems.at[slot]).start(priority=1)

    def wait_slot(slot):
        off = pl.multiple_of(slot * slot_rows, slot_rows)
        pltpu.make_async_copy(ys_ref.at[pl.ds(0, slot_rows), :], buf_ref.at[pl.ds(off, slot_rows), :],
                              sems.at[slot]).wait()
        return off

    def issue_all(d_ref, slot):
        lax.fori_loop(0, tq, lambda t, c: (issue(d_ref, slot, t), c)[1], 0, unroll=4)

    slot = i % 2

    @pl.when(i == 0)
    def _():
        issue_all(dcur_ref, 0)

    @pl.when(i + 1 < pl.num_programs(0))
    def _():
        issue_all(dnext_ref, 1 - slot)

    off = wait_slot(slot)
    meta_rows = jnp.transpose(jnp.concatenate([meta_ref[...], jnp.zeros((LANE - SUBLANE, tq), F32)], axis=0))
    gate1 = meta_rows[:, 4:5]
    gate2 = meta_rows[:, 5:6]
    for c in range(d // LANE):
        sl = slice(c * LANE, (c + 1) * LANE)
        y1 = buf_ref[pl.ds(off + c, tq, stride=SUBLANE), :]
        y2 = buf_ref[pl.ds(off + tq * SUBLANE + c, tq, stride=SUBLANE), :]
        o_ref[:, sl] = alpha * x_ref[:, sl] + (y1 * gate1 + y2 * gate2)
    o_ref[...] = _layer_norm(o_ref[...], g_ref[...], b_ref[...])


def _combine(dest3, meta, x2, g, bb, ys_tiled, alpha):
    t, d = x2.shape
    n = dest3.shape[0]
    tq = t // n
    row = lambda i: (i, 0)
    const = lambda i: (0, 0)
    return pl.pallas_call(
        functools.partial(_combine_kernel, alpha),
        grid=(n,),
        in_specs=[pl.BlockSpec((1, 1, 2 * tq), lambda i: (i, 0, 0), memory_space=pltpu.SMEM),
                  pl.BlockSpec((1, 1, 2 * tq), lambda i: (jnp.minimum(i + 1, n - 1), 0, 0), memory_space=pltpu.SMEM),
                  pl.BlockSpec((SUBLANE, tq), lambda i: (0, i)),
                  pl.BlockSpec((tq, d), row),
                  pl.BlockSpec((1, d), const), pl.BlockSpec((1, d), const),
                  pl.BlockSpec(memory_space=pl.ANY)],
        out_specs=pl.BlockSpec((tq, d), row),
        out_shape=jax.ShapeDtypeStruct((t, d), F32),
        scratch_shapes=[pltpu.VMEM((2 * 2 * tq * SUBLANE, LANE), F32), pltpu.SemaphoreType.DMA((2,))],
        compiler_params=_params("arbitrary"),
        name="combine",
    )(dest3, dest3, meta, x2, g, bb, ys_tiled)


def _moe(x2, x2_tiled, logits, w_g, w_u, w_d, g, bb, alpha):
    t, d = x2.shape
    meta, cnt = _route(logits)
    counts = cnt[N_GROUPS:N_GROUPS + N_EXPERTS, 0].astype(jnp.int32)
    padded = ((counts + EXPERT_ROWS - 1) // EXPERT_ROWS) * EXPERT_ROWS
    pend = jnp.cumsum(padded)
    poff = pend - padded
    first_row = jnp.pad(poff.astype(F32), (N_GROUPS, ROUTE_ROWS - N_GROUPS - N_EXPERTS))
    dest3 = _plan(meta, jnp.broadcast_to(first_row[:, None], (ROUTE_ROWS, LANE)), ROUTE_TILE)
    n_rows = t * 2 + N_EXPERTS * EXPERT_ROWS
    nblk = n_rows // EXPERT_ROWS
    blk_start = jnp.arange(nblk, dtype=jnp.int32) * EXPERT_ROWS
    blk_exp = jnp.sum((pend[None, :] <= blk_start[:, None]).astype(jnp.int32), axis=1)
    blk_exp = jnp.minimum(blk_exp, N_EXPERTS - 1)
    e_idx = jnp.arange(N_EXPERTS, dtype=jnp.int32)
    owner = blk_exp[:, None] == e_idx[None, :]

    def of_block(per_expert):
        return jnp.sum(jnp.where(owner, per_expert[None, :], 0), axis=1).astype(jnp.int32)

    blk_rows = jnp.clip(of_block(poff + counts) - blk_start, 0, EXPERT_ROWS).astype(jnp.int32)
    n_used = (pend[-1:] // EXPERT_ROWS).astype(jnp.int32)
    has_rows = counts > 0
    ordinal = jnp.cumsum(has_rows.astype(jnp.int32)) - 1
    later = has_rows[None, :] & (e_idx[None, :] > e_idx[:, None])
    next_used = jnp.min(jnp.where(later, e_idx[None, :], N_EXPERTS), axis=1)
    next_used = jnp.where(next_used == N_EXPERTS, -1, next_used).astype(jnp.int32)
    blk_first = (blk_start == of_block(poff)).astype(jnp.int32)
    blk_slot = of_block(ordinal % 2)
    blk_next = of_block(next_used)
    xs_tiled = _dispatch(dest3, x2_tiled, n_rows)
    ys_tiled = _experts(blk_exp, blk_rows, n_used, blk_first, blk_slot, blk_next, xs_tiled, w_g, w_u, w_d)
    return _combine(dest3, meta, x2, g, bb, ys_tiled, alpha)


def kernel(x, mem, w_in, w_out, ln_mix_g, ln_mix_b, w_xq, w_xkv, w_xo, ln_x_g, ln_x_b, w_route_group,
           w_route_expert, w_exp_gate, w_exp_up, w_exp_down, ln_moe_g, ln_moe_b):
    b, s, d = x.shape
    depth = w_in.shape[0]
    alpha = (2.0 * depth) ** 0.25
    t = b * s
    ret_cols = (2 * RET_HEADS * RET_QK_DIM + 2 * RET_HEADS * RET_V_DIM)
    assert s % DIL_SUPER == 0 and s % (RET_STEP_CHUNKS * RET_CHUNK) == 0, "sequence length not tileable"
    assert w_in.shape[2] == ret_cols + 3 * DIL_HEADS * DIL_HEAD_DIM and d % (XATTN_HEADS * LANE) == 0
    assert w_route_group.shape[2] == N_GROUPS and w_route_expert.shape[2] == N_EXPERTS
    xc = x.reshape(t, d)
    for l in range(depth):
        n_qk = 2 * RET_HEADS * RET_QK_DIM
        h_ret, h_dil = _proj_in(xc, w_in[l], _retention_weight_layout(w_in[l][:, :n_qk]).astype(BF16), ret_cols)
        y_ret = _retention(h_ret.reshape(b, s, ret_cols)).reshape(t, -1)
        y_dil = _dilated(h_dil, b).reshape(t, -1)
        kv = _mem_kv(mem.reshape(b * mem.shape[1], d), w_xkv[l]).reshape(b, mem.shape[1], 2 * d)
        w_r = jnp.concatenate([w_route_group[l], w_route_expert[l]], axis=-1)
        w_r = jnp.pad(w_r.T, ((0, ROUTE_ROWS - w_r.shape[1]), (0, 0))).astype(BF16)
        x2, x2_tiled, logits = _xattn(xc, y_ret, y_dil, w_out[l].astype(BF16), ln_mix_g[l][None], ln_mix_b[l][None],
                                      kv, w_xq[l].astype(BF16), w_xo[l].astype(BF16),
                                      ln_x_g[l][None], ln_x_b[l][None], w_r, alpha, s)
        xc = _moe(x2, x2_tiled, logits, w_exp_gate[l], w_exp_up[l], w_exp_down[l],
                  ln_moe_g[l][None], ln_moe_b[l][None], alpha)
    return xc.reshape(b, s, d)
```
